```python
import math
import jax, jax.numpy as jnp
from jax import lax
import numpy as np

D_MODEL = 2048
BATCH = 1
SEQ = 16384
DEPTH = 2

D_SHORTCONV = 3 * D_MODEL // 4
SHORTCONV_WIDTH = 3
D_SSM = D_MODEL // 4
SSM_GROUP = 16
SSM_GROUPS = D_SSM // SSM_GROUP
SSM_STATE = 64
DT_MIN = 1e-3
DT_MAX = 1e-1
D_CONFORMER = D_MODEL // 2
CONFORMER_WIDTH = 31
D_ATTN = D_MODEL // 2
HEAD_DIM = 128
N_HEADS = D_ATTN // HEAD_DIM
DILATED_PATTERNS = ((128, 1), (512, 4), (2048, 16))
ATTN_BLOCK = 128
REL_BUCKETS = 32
REL_MAX_DISTANCE = 2048
NORM_EPS = 1e-6
NEG_INF = -1e30

kernel_name = "hybrid_shortconv_s5_conformer_dilated_attn"


def _split(t, sizes):
    idx = [int(i) for i in np.cumsum(sizes)[:-1]]
    return jnp.split(t, idx, axis=-1)


def _rmsnorm(x, g):
    xf = x.astype(jnp.float32)
    y = xf * lax.rsqrt(jnp.mean(xf * xf, axis=-1, keepdims=True) + NORM_EPS)
    return (y * g.astype(jnp.float32)).astype(x.dtype)


def _layernorm(x, g, b):
    xf = x.astype(jnp.float32)
    mu = jnp.mean(xf, axis=-1, keepdims=True)
    xc = xf - mu
    y = xc * lax.rsqrt(jnp.mean(xc * xc, axis=-1, keepdims=True) + NORM_EPS)
    return (y * g.astype(jnp.float32) + b.astype(jnp.float32)).astype(x.dtype)


def _causal_depthwise_conv(x, w):
    width, ch = w.shape
    return lax.conv_general_dilated(
        x, w[:, None, :].astype(x.dtype), window_strides=(1,),
        padding=((width - 1, 0),), dimension_numbers=('NWC', 'WIO', 'NWC'),
        feature_group_count=ch)


def _t5_causal_bucket(dist):
    max_exact = REL_BUCKETS // 2
    d_f = jnp.maximum(dist, 1).astype(jnp.float32)
    large = max_exact + (jnp.log(d_f / max_exact) / math.log(REL_MAX_DISTANCE / max_exact)
                         * (REL_BUCKETS - max_exact)).astype(jnp.int32)
    large = jnp.minimum(large, REL_BUCKETS - 1)
    return jnp.where(dist < max_exact, dist, large)


def _linear_recurrence_op(e1, e2):
    a1, b1 = e1
    a2, b2 = e2
    return a1 * a2, a2 * b1 + b2


def _s5_ssm(u, lam_re, lam_im, log_dt, b_re, b_im, c_re, c_im, d_skip, w_glu):
    bsz, seq, _ = u.shape
    f32 = jnp.float32
    uf = u.astype(f32).reshape(bsz, seq, SSM_GROUPS, SSM_GROUP)
    lam = lax.complex(lam_re.astype(f32), lam_im.astype(f32))
    dt = jnp.exp(log_dt.astype(f32))[:, None]
    a_bar = jnp.exp(lam * dt)
    b_bar = ((a_bar - 1.0) / lam)[..., None] * lax.complex(b_re.astype(f32), b_im.astype(f32))
    bu = jnp.einsum('blgp,gnp->blgn', uf.astype(jnp.complex64), b_bar)
    a_seq = jnp.broadcast_to(a_bar, bu.shape)
    _, states = lax.associative_scan(_linear_recurrence_op, (a_seq, bu), axis=1)
    c = lax.complex(c_re.astype(f32), c_im.astype(f32))
    y = jnp.real(jnp.einsum('blgn,gpn->blgp', states, c))
    y = y + d_skip.astype(f32).reshape(SSM_GROUPS, SSM_GROUP) * uf
    y = y.reshape(bsz, seq, D_SSM)
    g = jax.nn.gelu(y)
    y = g * jax.nn.sigmoid(g @ w_glu.astype(f32))
    return y.astype(u.dtype)


def _even_mixer(h, w_in, conv_w, lam_re, lam_im, log_dt, b_re, b_im, c_re, c_im, d_skip, w_glu, w_out):
    proj = h @ w_in
    xa, gate_c, gate_b, za, u, zb = _split(
        proj, [D_SHORTCONV] * 4 + [D_SSM] * 2)
    ya = gate_b * _causal_depthwise_conv(gate_c * xa, conv_w)
    ya = ya * jax.nn.silu(za)
    yb = _s5_ssm(u, lam_re, lam_im, log_dt, b_re, b_im, c_re, c_im, d_skip, w_glu)
    yb = yb * jax.nn.silu(zb)
    return jnp.concatenate([ya, yb], axis=-1) @ w_out


def _dilated_window_pattern(q, k, v, rel_bias, window, dilation):
    bsz, seq, nh, hd = q.shape
    f32 = jnp.float32
    steps = window // dilation
    qb_sz = ATTN_BLOCK
    sub = seq // dilation
    nb = -(-sub // qb_sz)
    sp = nb * qb_sz

    def streams(t):
        return t.reshape(bsz, sub, dilation, nh, hd)

    pad_tail = ((0, 0), (0, sp - sub), (0, 0), (0, 0), (0, 0))
    pad_both = ((0, 0), (qb_sz, sp - sub), (0, 0), (0, 0), (0, 0))
    qs = jnp.pad(streams(q) * (HEAD_DIM ** -0.5), pad_tail).reshape(bsz, nb, qb_sz, dilation, nh, hd)

    def key_blocks(t):
        tp = jnp.pad(streams(t), pad_both)
        prev = tp[:, :sp].reshape(bsz, nb, qb_sz, dilation, nh, hd)
        cur = tp[:, qb_sz:].reshape(bsz, nb, qb_sz, dilation, nh, hd)
        return jnp.concatenate([prev, cur], axis=2)

    kb = key_blocks(k)
    vb = key_blocks(v)

    qi = np.arange(qb_sz)[:, None]
    kj = np.arange(2 * qb_sz)[None, :]
    rel = qb_sz + qi - kj
    blk = np.arange(nb)[:, None, None]
    valid = (rel >= 0) & (rel <= steps) & (blk * qb_sz - qb_sz + kj >= 0)
    bucket = _t5_causal_bucket(jnp.asarray(np.clip(rel, 0, None) * dilation, dtype=jnp.int32))
    bias = jnp.transpose(rel_bias.astype(f32)[bucket], (2, 0, 1))

    s = jnp.einsum('bnqrhd,bnkrhd->bnrhqk', qs, kb, preferred_element_type=f32)
    s = s + bias[None, None, None]
    s = jnp.where(valid[None, :, None, None], s, NEG_INF)
    mx = jnp.max(s, axis=-1, keepdims=True)
    p = jnp.exp(s - mx)
    den = jnp.sum(p, axis=-1)
    num = jnp.einsum('bnrhqk,bnkrhd->bnqrhd', p, vb.astype(f32))
    num = num.reshape(bsz, sp, dilation, nh, hd)[:, :sub].reshape(bsz, seq, nh, hd)

    def per_query(t):
        t = jnp.transpose(t, (0, 1, 4, 2, 3)).reshape(bsz, sp, dilation, nh)
        return t[:, :sub].reshape(bsz, seq, nh)

    return num, per_query(mx[..., 0]), per_query(den)


def _dilated_attention(q, k, v, rel_bias):
    nums, maxs, dens = [], [], []
    for window, dilation in DILATED_PATTERNS:
        num, mx, den = _dilated_window_pattern(q, k, v, rel_bias, window, dilation)
        nums.append(num)
        maxs.append(mx)
        dens.append(den)
    mx_all = jnp.stack(maxs)
    w = jnp.exp(mx_all - jnp.max(mx_all, axis=0, keepdims=True))
    numer = jnp.sum(w[..., None] * jnp.stack(nums), axis=0)
    denom = jnp.sum(w * jnp.stack(dens), axis=0)
    return (numer / denom[..., None]).astype(q.dtype)


def _odd_mixer(h, w_in, conv_w, conv_b, ln_g, ln_b, w_pw, rel_bias, w_out):
    bsz, seq, _ = h.shape
    proj = h @ w_in
    ga, gb, zc, q, k, v, zd = _split(
        proj, [D_CONFORMER] * 3 + [D_ATTN] * 4)
    yc = ga * jax.nn.sigmoid(gb)
    yc = _causal_depthwise_conv(yc, conv_w) + conv_b
    yc = jax.nn.silu(_layernorm(yc, ln_g, ln_b))
    yc = (yc @ w_pw) * jax.nn.silu(zc)
    shape4 = (bsz, seq, N_HEADS, HEAD_DIM)
    yd = _dilated_attention(q.reshape(shape4), k.reshape(shape4), v.reshape(shape4), rel_bias)
    yd = yd.reshape(bsz, seq, D_ATTN) * jax.nn.silu(zd)
    return jnp.concatenate([yc, yd], axis=-1) @ w_out


def setup_inputs(seed: int = 0) -> dict:
    key = jax.random.key(seed)
    ks = jax.random.split(key, 24)
    ne = (DEPTH + 1) // 2
    no = DEPTH // 2
    f32 = jnp.float32
    nrm = lambda k, shape, s: (jax.random.normal(k, shape, f32) * s)
    even_in = 4 * D_SHORTCONV + 2 * D_SSM
    odd_in = 3 * D_CONFORMER + 4 * D_ATTN
    n_idx = jnp.arange(SSM_STATE, dtype=f32)
    return {
        "x": jax.random.normal(ks[0], (BATCH, SEQ, D_MODEL), f32),
        "norm_g": 1.0 + nrm(ks[1], (DEPTH, D_MODEL), 0.02),
        "final_norm_g": 1.0 + nrm(ks[2], (D_MODEL,), 0.02),
        "ev_w_in": nrm(ks[3], (ne, D_MODEL, even_in), D_MODEL ** -0.5),
        "ev_conv_w": nrm(ks[4], (ne, SHORTCONV_WIDTH, D_SHORTCONV), SHORTCONV_WIDTH ** -0.5),
        "s5_lam_re": -0.5 + nrm(ks[5], (ne, SSM_GROUPS, SSM_STATE), 0.01),
        "s5_lam_im": math.pi * n_idx + nrm(ks[6], (ne, SSM_GROUPS, SSM_STATE), 0.01),
        "s5_log_dt": jax.random.uniform(ks[7], (ne, SSM_GROUPS), f32, math.log(DT_MIN), math.log(DT_MAX)),
        "s5_b_re": nrm(ks[8], (ne, SSM_GROUPS, SSM_STATE, SSM_GROUP), (2 * SSM_GROUP) ** -0.5),
        "s5_b_im": nrm(ks[9], (ne, SSM_GROUPS, SSM_STATE, SSM_GROUP), (2 * SSM_GROUP) ** -0.5),
        "s5_c_re": nrm(ks[10], (ne, SSM_GROUPS, SSM_GROUP, SSM_STATE), (2 * SSM_STATE) ** -0.5),
        "s5_c_im": nrm(ks[11], (ne, SSM_GROUPS, SSM_GROUP, SSM_STATE), (2 * SSM_STATE) ** -0.5),
        "s5_d": nrm(ks[12], (ne, D_SSM), 1.0),
        "s5_w_glu": nrm(ks[13], (ne, D_SSM, D_SSM), D_SSM ** -0.5),
        "ev_w_out": nrm(ks[14], (ne, D_SHORTCONV + D_SSM, D_MODEL), (D_SHORTCONV + D_SSM) ** -0.5),
        "od_w_in": nrm(ks[15], (no, D_MODEL, odd_in), D_MODEL ** -0.5),
        "cf_conv_w": nrm(ks[16], (no, CONFORMER_WIDTH, D_CONFORMER), CONFORMER_WIDTH ** -0.5),
        "cf_conv_b": nrm(ks[17], (no, D_CONFORMER), 0.02),
        "cf_ln_g": 1.0 + nrm(ks[18], (no, D_CONFORMER), 0.02),
        "cf_ln_b": nrm(ks[19], (no, D_CONFORMER), 0.02),
        "cf_w_pw": nrm(ks[20], (no, D_CONFORMER, D_CONFORMER), D_CONFORMER ** -0.5),
        "od_w_out": nrm(ks[21], (no, D_CONFORMER + D_ATTN, D_MODEL), (D_CONFORMER + D_ATTN) ** -0.5),
        "rel_bias": nrm(ks[22], (REL_BUCKETS, N_HEADS), 0.5),
    }


def reference(x, norm_g, final_norm_g, ev_w_in, ev_conv_w, s5_lam_re, s5_lam_im, s5_log_dt,
              s5_b_re, s5_b_im, s5_c_re, s5_c_im, s5_d, s5_w_glu, ev_w_out,
              od_w_in, cf_conv_w, cf_conv_b, cf_ln_g, cf_ln_b, cf_w_pw, od_w_out, rel_bias):
    h = x
    for layer in range(DEPTH):
        i = layer // 2
        hn = _rmsnorm(h, norm_g[layer])
        if layer % 2 == 0:
            h = h + _even_mixer(hn, ev_w_in[i], ev_conv_w[i], s5_lam_re[i], s5_lam_im[i], s5_log_dt[i],
                                s5_b_re[i], s5_b_im[i], s5_c_re[i], s5_c_im[i], s5_d[i], s5_w_glu[i],
                                ev_w_out[i])
        else:
            h = h + _odd_mixer(hn, od_w_in[i], cf_conv_w[i], cf_conv_b[i], cf_ln_g[i], cf_ln_b[i],
                               cf_w_pw[i], rel_bias, od_w_out[i])
    return _rmsnorm(h, final_norm_g)
```

```python
import functools
import math

import numpy as np
import jax
import jax.numpy as jnp
from jax import lax
from jax.experimental import pallas as pl
from jax.experimental.pallas import tpu as pltpu

F32 = jnp.float32
BF16 = jnp.bfloat16

D_MODEL = 2048
D_SHORTCONV = 1536
SHORTCONV_WIDTH = 3
D_SSM = 512
SSM_GROUP = 16
SSM_GROUPS = 32
SSM_STATE = 64
D_CONFORMER = 1024
CONFORMER_WIDTH = 31
D_ATTN = 1024
HEAD_DIM = 128
N_HEADS = 8
DILATED_PATTERNS = ((128, 1), (512, 4), (2048, 16))
ATTN_BLOCK = 128
REL_BUCKETS = 32
REL_MAX_DISTANCE = 2048
NORM_EPS = 1e-6
NEG_INF = -1e30

COL = 512
SSM_CHUNK = 16
SSM_FLAT = SSM_CHUNK * SSM_GROUP
ATTN_TILE = 2048
CONV_HALO = 32
VMEM_LIMIT = 50 * 1024 * 1024


def _params(sem, vmem=VMEM_LIMIT):
    return pltpu.CompilerParams(dimension_semantics=sem, vmem_limit_bytes=vmem)


def _dot(a, b):
    return jnp.dot(a, b, preferred_element_type=F32)


def _sigmoid(z):
    return 1.0 / (1.0 + jnp.exp(-z))


def _silu(z):
    return z * _sigmoid(z)


def _gelu_tanh(x):
    return 0.5 * x * (1.0 + jnp.tanh(math.sqrt(2.0 / math.pi) * (x + 0.044715 * (x * x * x))))


def _rms(xf, g):
    ms = jnp.mean(xf * xf, axis=-1, keepdims=True)
    return xf * lax.rsqrt(ms + NORM_EPS) * g


def _even_in_kernel(x_ref, g_ref, wa_ref, wb_ref, wc_ref, wd_ref, cw_ref,
                    ya_ref, u_ref, szb_ref, hn_sc, pbuf_sc, carry_sc, *, tm):
    i = pl.program_id(0)
    j = pl.program_id(1)

    @pl.when(j == 0)
    def _():
        hn_sc[...] = _rms(x_ref[...], g_ref[...]).astype(BF16)

    @pl.when(j < 3)
    def _():
        hn = hn_sc[...]
        p = _dot(hn, wb_ref[...]) * _dot(hn, wa_ref[...])

        @pl.when(i == 0)
        def _():
            carry_sc[j] = jnp.zeros((8, COL), F32)

        pbuf_sc[0:8, :] = carry_sc[j]
        pbuf_sc[8:8 + tm, :] = p
        carry_sc[j] = pbuf_sc[tm:tm + 8, :]
        cw = cw_ref[...]
        conv = (cw[2:3, :] * p + cw[1:2, :] * pbuf_sc[pl.ds(7, tm), :]
                + cw[0:1, :] * pbuf_sc[pl.ds(6, tm), :])
        gb = _dot(hn, wc_ref[...])
        za = _dot(hn, wd_ref[...])
        ya_ref[...] = (gb * conv * _silu(za)).astype(BF16)

    @pl.when(j == 3)
    def _():
        hn = hn_sc[...]
        u_ref[...] = _dot(hn, wa_ref[...])
        szb_ref[...] = _silu(_dot(hn, wb_ref[...])).astype(BF16)


def _even_in(x, g, w_in, conv_w, tm):
    seq = x.shape[0]
    nconv = D_SHORTCONV // COL

    def wspec(k, alt):
        return pl.BlockSpec((D_MODEL, COL), lambda i, j: (0, jnp.where(j < nconv, j + nconv * k, alt)))

    u_blk = 4 * nconv
    return pl.pallas_call(
        functools.partial(_even_in_kernel, tm=tm),
        grid=(seq // tm, nconv + 1),
        in_specs=[
            pl.BlockSpec((tm, D_MODEL), lambda i, j: (i, 0)),
            pl.BlockSpec((1, D_MODEL), lambda i, j: (0, 0)),
            wspec(0, u_blk), wspec(1, u_blk + 1), wspec(2, 3 * nconv - 1), wspec(3, 4 * nconv - 1),
            pl.BlockSpec((SHORTCONV_WIDTH, COL), lambda i, j: (0, jnp.minimum(j, nconv - 1))),
        ],
        out_specs=[
            pl.BlockSpec((tm, COL), lambda i, j: (i, jnp.minimum(j, nconv - 1))),
            pl.BlockSpec((tm, D_SSM), lambda i, j: (i, 0)),
            pl.BlockSpec((tm, D_SSM), lambda i, j: (i, 0)),
        ],
        out_shape=[
            jax.ShapeDtypeStruct((seq, D_SHORTCONV), BF16),
            jax.ShapeDtypeStruct((seq, D_SSM), F32),
            jax.ShapeDtypeStruct((seq, D_SSM), BF16),
        ],
        scratch_shapes=[
            pltpu.VMEM((tm, D_MODEL), BF16),
            pltpu.VMEM((tm + 8, COL), F32),
            pltpu.VMEM((nconv, 8, COL), F32),
        ],
        compiler_params=_params(("arbitrary", "arbitrary")),
        name="even_in",
    )(x, g, w_in, w_in, w_in, w_in, conv_w)


def _s5_tables(lam_re, lam_im, log_dt, b_re, b_im, c_re, c_im, d_skip):
    hi = lax.Precision.HIGHEST
    t_, g_, n_, p_ = SSM_CHUNK, SSM_GROUPS, SSM_STATE, SSM_GROUP
    lam_re, lam_im = lam_re.astype(F32), lam_im.astype(F32)
    dt = jnp.exp(log_dt.astype(F32))[:, None]
    xr, xi = lam_re * dt, lam_im * dt
    k = jnp.arange(t_ + 1, dtype=F32)[:, None, None]
    mag = jnp.exp(xr * k)
    pr, pi = mag * jnp.cos(xi * k), mag * jnp.sin(xi * k)
    nr, ni = pr[1] - 1.0, pi[1]
    den = lam_re * lam_re + lam_im * lam_im
    fr = (nr * lam_re + ni * lam_im) / den
    fi = (ni * lam_re - nr * lam_im) / den
    b_re, b_im = b_re.astype(F32), b_im.astype(F32)
    bbr = fr[..., None] * b_re - fi[..., None] * b_im
    bbi = fr[..., None] * b_im + fi[..., None] * b_re
    c_re, c_im = c_re.astype(F32), c_im.astype(F32)

    cpr = c_re[None] * pr[:t_, :, None, :] - c_im[None] * pi[:t_, :, None, :]
    cpi = c_re[None] * pi[:t_, :, None, :] + c_im[None] * pr[:t_, :, None, :]
    kern = (jnp.einsum('tgpn,gnq->tgpq', cpr, bbr, precision=hi)
            - jnp.einsum('tgpn,gnq->tgpq', cpi, bbi, precision=hi))
    skip = d_skip.astype(F32).reshape(g_, p_)[:, :, None] * jnp.eye(p_, dtype=F32)[None]
    kern = kern.at[0].add(skip)
    tj = np.arange(t_)
    tau = tj[None, :] - tj[:, None]
    m = jnp.where((tau >= 0)[:, :, None, None, None], kern[np.clip(tau, 0, None)], 0.0)
    m = jnp.transpose(m, (2, 0, 4, 1, 3)).reshape(g_, t_ * p_, t_ * p_)

    prj, pij = pr[t_ - 1 - tj], pi[t_ - 1 - tj]
    wr = prj[..., None] * bbr[None] - pij[..., None] * bbi[None]
    wi = prj[..., None] * bbi[None] + pij[..., None] * bbr[None]
    wr = jnp.transpose(wr, (1, 0, 3, 2)).reshape(g_, t_ * p_, n_)
    wi = jnp.transpose(wi, (1, 0, 3, 2)).reshape(g_, t_ * p_, n_)

    vr = c_re[None] * pr[1:, :, None, :] - c_im[None] * pi[1:, :, None, :]
    vi = c_re[None] * pi[1:, :, None, :] + c_im[None] * pr[1:, :, None, :]
    vr = jnp.transpose(vr, (1, 3, 0, 2)).reshape(g_, n_, t_ * p_)
    vi = -jnp.transpose(vi, (1, 3, 0, 2)).reshape(g_, n_, t_ * p_)

    flat = t_ * p_
    z = jnp.zeros((g_ // 2, flat, n_), F32)
    wr2, wi2 = wr.reshape(g_ // 2, 2, flat, n_), wi.reshape(g_ // 2, 2, flat, n_)
    wre_p = jnp.concatenate([jnp.concatenate([wr2[:, 0], z], axis=2),
                             jnp.concatenate([z, wr2[:, 1]], axis=2)], axis=1)
    wim_p = jnp.concatenate([jnp.concatenate([wi2[:, 0], z], axis=2),
                             jnp.concatenate([z, wi2[:, 1]], axis=2)], axis=1)
    zv = jnp.zeros((g_ // 2, n_, flat), F32)
    vr2, vi2 = vr.reshape(g_ // 2, 2, n_, flat), vi.reshape(g_ // 2, 2, n_, flat)
    v_even = jnp.concatenate([vr2[:, 0], zv, vi2[:, 0], zv], axis=1)
    v_odd = jnp.concatenate([zv, vr2[:, 1], zv, vi2[:, 1]], axis=1)
    v_p = jnp.stack([v_even, v_odd], axis=1).reshape(g_, 4 * n_, flat)
    a_re = pr[t_].reshape(1, g_ * n_)
    a_im = pi[t_].reshape(1, g_ * n_)
    return m.astype(BF16), wre_p.astype(BF16), wim_p.astype(BF16), v_p.astype(BF16), a_re, a_im


def _s5_kernel(x_ref, m_ref, wre_ref, wim_ref, v_ref, are_ref, aim_ref, y_ref,
               zre_sc, zim_sc, carry_sc, *, ct):
    i = pl.program_id(0)
    npair = SSM_GROUPS // 2

    @pl.when(i == 0)
    def _():
        carry_sc[...] = jnp.zeros_like(carry_sc)

    for gp in range(npair):
        xp = jnp.concatenate([x_ref[2 * gp], x_ref[2 * gp + 1]], axis=1).astype(BF16)
        zre_sc[:, gp * 128:(gp + 1) * 128] = _dot(xp, wre_ref[gp])
        zim_sc[:, gp * 128:(gp + 1) * 128] = _dot(xp, wim_ref[gp])

    are = are_ref[...]
    aim = aim_ref[...]

    def step(r, carry):
        sre, sim = carry
        zr = zre_sc[pl.ds(r, 1), :]
        zi = zim_sc[pl.ds(r, 1), :]
        zre_sc[pl.ds(r, 1), :] = sre
        zim_sc[pl.ds(r, 1), :] = sim
        return are * sre - aim * sim + zr, are * sim + aim * sre + zi

    sre, sim = lax.fori_loop(0, ct, step, (carry_sc[0:1, :], carry_sc[1:2, :]), unroll=8)
    carry_sc[0:1, :] = sre
    carry_sc[1:2, :] = sim

    for g in range(SSM_GROUPS):
        gp = g // 2
        sp = jnp.concatenate([zre_sc[:, gp * 128:(gp + 1) * 128],
                              zim_sc[:, gp * 128:(gp + 1) * 128]], axis=1).astype(BF16)
        y_ref[g] = _dot(x_ref[g].astype(BF16), m_ref[g]) + _dot(sp, v_ref[g])


def _s5(xg, tables, ct):
    m, wre, wim, v, are, aim = tables
    g_, c_, flat = xg.shape
    full = lambda a: pl.BlockSpec(a.shape, lambda i: (0,) * a.ndim)
    return pl.pallas_call(
        functools.partial(_s5_kernel, ct=ct),
        grid=(c_ // ct,),
        in_specs=[pl.BlockSpec((g_, ct, flat), lambda i: (0, i, 0)),
                  full(m), full(wre), full(wim), full(v), full(are), full(aim)],
        out_specs=pl.BlockSpec((g_, ct, flat), lambda i: (0, i, 0)),
        out_shape=jax.ShapeDtypeStruct((g_, c_, flat), F32),
        scratch_shapes=[pltpu.VMEM((ct, SSM_GROUPS * SSM_STATE), F32),
                        pltpu.VMEM((ct, SSM_GROUPS * SSM_STATE), F32),
                        pltpu.VMEM((8, SSM_GROUPS * SSM_STATE), F32)],
        compiler_params=_params(("arbitrary",)),
        name="s5_ssm",
    )(xg, m, wre, wim, v, are, aim)


def _even_out_kernel(x_ref, ya_ref, ys_ref, szb_ref, wglu_ref, wout_ref, g_ref, h_ref, hn_ref):
    gl = _gelu_tanh(ys_ref[...])
    yb = gl * _sigmoid(_dot(gl.astype(BF16), wglu_ref[...])) * szb_ref[...].astype(F32)
    acc = _dot(ya_ref[...], wout_ref[0:D_SHORTCONV, :])
    acc = acc + _dot(yb.astype(BF16), wout_ref[D_SHORTCONV:D_SHORTCONV + D_SSM, :])
    h = x_ref[...] + acc
    h_ref[...] = h
    hn_ref[...] = _rms(h, g_ref[...]).astype(BF16)


def _even_out(x, ya, ys, szb, w_glu, w_out, g, tm):
    seq = x.shape[0]
    row = lambda w: pl.BlockSpec((tm, w), lambda i: (i, 0))
    const = lambda a: pl.BlockSpec(a.shape, lambda i: (0, 0))
    return pl.pallas_call(
        _even_out_kernel,
        grid=(seq // tm,),
        in_specs=[row(D_MODEL), row(D_SHORTCONV), row(D_SSM), row(D_SSM),
                  const(w_glu), const(w_out), const(g)],
        out_specs=[row(D_MODEL), row(D_MODEL)],
        out_shape=[jax.ShapeDtypeStruct((seq, D_MODEL), F32),
                   jax.ShapeDtypeStruct((seq, D_MODEL), BF16)],
        compiler_params=_params(("arbitrary",)),
        name="even_out",
    )(x, ya, ys, szb, w_glu, w_out, g)


ODD_STEPS = 12
_SILU_STEPS = (2, 3, 10, 11)
_Q_STEPS = (4, 5)


def _odd_in_kernel(hn_ref, wa_ref, wb_ref, o_ref):
    s = pl.program_id(1)
    hn = hn_ref[...]
    is_silu = (s == 2) | (s == 3) | (s == 10) | (s == 11)

    @pl.when(s < 2)
    def _():
        o_ref[...] = (_dot(hn, wa_ref[...]) * _sigmoid(_dot(hn, wb_ref[...]))).astype(BF16)

    @pl.when(is_silu)
    def _():
        o_ref[...] = _silu(_dot(hn, wa_ref[...])).astype(BF16)

    @pl.when((s >= 2) & jnp.logical_not(is_silu))
    def _():
        scale = jnp.where((s == 4) | (s == 5), HEAD_DIM ** -0.5, 1.0).astype(F32)
        o_ref[...] = (_dot(hn, wa_ref[...]) * scale).astype(BF16)


def _odd_in(hn, w_in, tm):
    seq = hn.shape[0]
    return pl.pallas_call(
        _odd_in_kernel,
        grid=(seq // tm, ODD_STEPS),
        in_specs=[pl.BlockSpec((tm, D_MODEL), lambda i, s: (i, 0)),
                  pl.BlockSpec((D_MODEL, COL), lambda i, s: (0, jnp.where(s < 2, s, s + 2))),
                  pl.BlockSpec((D_MODEL, COL), lambda i, s: (0, jnp.where(s < 2, s + 2, 3)))],
        out_specs=pl.BlockSpec((tm, COL), lambda i, s: (i, s)),
        out_shape=jax.ShapeDtypeStruct((seq, ODD_STEPS * COL), BF16),
        compiler_params=_params(("arbitrary", "arbitrary")),
        name="odd_in",
    )(hn, w_in, w_in)


CONV_ROWS = 64


def _conformer_kernel(cur_ref, halo_ref, szc_ref, cw_ref, cb_ref, lg_ref, lb_ref, wpw_ref,
                      o_ref, buf_sc, conv_sc, *, tm):
    i = pl.program_id(0)
    halo = halo_ref[...].astype(F32)
    buf_sc[0:CONV_HALO, :] = jnp.where(i == 0, 0.0, halo)
    buf_sc[CONV_HALO:CONV_HALO + tm, :] = cur_ref[...].astype(F32)
    shift = CONV_HALO - (CONFORMER_WIDTH - 1)

    for base in range(0, tm, CONV_ROWS):
        for c in range(D_CONFORMER // 128):
            lanes = slice(c * 128, (c + 1) * 128)
            acc = jnp.zeros((CONV_ROWS, 128), F32) + cb_ref[:, lanes]
            for k in range(CONFORMER_WIDTH):
                acc = acc + cw_ref[k:k + 1, lanes] * buf_sc[base + shift + k:base + shift + k + CONV_ROWS, lanes]
            conv_sc[base:base + CONV_ROWS, lanes] = acc
    y = conv_sc[...]
    mu = jnp.mean(y, axis=-1, keepdims=True)
    yc = y - mu
    var = jnp.mean(yc * yc, axis=-1, keepdims=True)
    ln = yc * lax.rsqrt(var + NORM_EPS) * lg_ref[...] + lb_ref[...]
    o_ref[...] = (_dot(_silu(ln).astype(BF16), wpw_ref[...]) * szc_ref[...].astype(F32)).astype(BF16)


def _conformer(proj, conv_w, conv_b, ln_g, ln_b, w_pw, tm):
    seq = proj.shape[0]
    per = tm // CONV_HALO
    const = lambda a: pl.BlockSpec(a.shape, lambda i: (0, 0))
    return pl.pallas_call(
        functools.partial(_conformer_kernel, tm=tm),
        grid=(seq // tm,),
        in_specs=[pl.BlockSpec((tm, D_CONFORMER), lambda i: (i, 0)),
                  pl.BlockSpec((CONV_HALO, D_CONFORMER), lambda i: (jnp.maximum(i * per - 1, 0), 0)),
                  pl.BlockSpec((tm, D_CONFORMER), lambda i: (i, 1)),
                  const(conv_w), const(conv_b), const(ln_g), const(ln_b), const(w_pw)],
        out_specs=pl.BlockSpec((tm, D_CONFORMER), lambda i: (i, 0)),
        out_shape=jax.ShapeDtypeStruct((seq, D_CONFORMER), BF16),
        scratch_shapes=[pltpu.VMEM((tm + CONV_HALO, D_CONFORMER), F32),
                        pltpu.VMEM((tm, D_CONFORMER), F32)],
        compiler_params=_params(("arbitrary",)),
        name="conformer",
    )(proj, proj, proj, conv_w, conv_b, ln_g, ln_b, w_pw)


def _t5_bucket(dist):
    max_exact = REL_BUCKETS // 2
    d_f = jnp.maximum(dist, 1).astype(F32)
    large = max_exact + (jnp.log(d_f / max_exact) / math.log(REL_MAX_DISTANCE / max_exact)
                         * (REL_BUCKETS - max_exact)).astype(jnp.int32)
    large = jnp.minimum(large, REL_BUCKETS - 1)
    return jnp.where(dist < max_exact, dist, large)


def _attn_bias(rel_bias):
    q = ATTN_BLOCK
    qi = np.arange(q)[:, None]
    kj = np.arange(2 * q)[None, :]
    rel = q + qi - kj
    tabs = []
    for window, dilation in DILATED_PATTERNS:
        steps = window // dilation
        valid = (rel >= 0) & (rel <= steps)
        bucket = _t5_bucket(jnp.asarray(np.clip(rel, 0, None) * dilation, dtype=jnp.int32))
        bias = jnp.transpose(rel_bias.astype(F32)[bucket], (2, 0, 1))
        tabs.append(jnp.where(valid[None], bias, NEG_INF))
        tabs.append(jnp.where((valid & (kj >= q))[None], bias, NEG_INF))
    return jnp.stack(tabs, axis=1)


def _attn_kernel(q_ref, kc_ref, kp_ref, vc_ref, vp_ref, szd_ref, bias_ref, o_ref,
                 qf, kf, vf, n0, n1, n2, d0, d1, d2, m0, m1, m2):
    i = pl.program_id(1)
    t = ATTN_TILE
    blk = ATTN_BLOCK
    first_tile = i == 0
    qf[...] = q_ref[...].astype(F32)
    kf[0:t, :] = jnp.where(first_tile, 0.0, kp_ref[...].astype(F32))
    kf[t:2 * t, :] = kc_ref[...].astype(F32)
    vf[0:t, :] = jnp.where(first_tile, 0.0, vp_ref[...].astype(F32))
    vf[t:2 * t, :] = vc_ref[...].astype(F32)
    ones = jnp.ones((2 * blk, HEAD_DIM), BF16)

    def block(pidx, qstart, kstart, stride, first, num_sc, den_sc, max_sc):
        st = None if stride == 1 else stride
        qrows = pl.ds(qstart, blk, stride=st)
        krows = pl.ds(kstart, 2 * blk, stride=st)
        qb = qf[qrows, :].astype(BF16)
        kb = kf[krows, :].astype(BF16)
        vb = vf[krows, :].astype(BF16)
        s = lax.dot_general(qb, kb, (((1,), (1,)), ((), ())), preferred_element_type=F32)
        s = s + bias_ref[0, 2 * pidx + first.astype(jnp.int32)]
        mx = jnp.max(s, axis=-1, keepdims=True)
        p = jnp.exp(s - mx).astype(BF16)
        pv = _dot(p, jnp.concatenate([vb, ones], axis=1))
        num_sc[qrows, :] = pv[:, :HEAD_DIM]
        den_sc[qrows, :] = pv[:, HEAD_DIM:]
        max_sc[qrows, :] = jnp.broadcast_to(mx, (blk, HEAD_DIM))

    def dense(b, _):
        block(0, b * blk, t + (b - 1) * blk, 1, first_tile & (b == 0), n0, d0, m0)
        return 0

    lax.fori_loop(0, t // blk, dense, 0)

    d1_ = DILATED_PATTERNS[1][1]
    span1 = blk * d1_

    def dil1(n, _):
        sb = n // d1_
        r = n % d1_
        block(1, sb * span1 + r, t + (sb - 1) * span1 + r, d1_, first_tile & (sb == 0), n1, d1, m1)
        return 0

    lax.fori_loop(0, t // blk, dil1, 0)

    d2_ = DILATED_PATTERNS[2][1]

    def dil2(r, _):
        block(2, r, r, d2_, first_tile, n2, d2, m2)
        return 0

    lax.fori_loop(0, d2_, dil2, 0)

    ma, mb, mc = m0[...], m1[...], m2[...]
    mall = jnp.maximum(jnp.maximum(ma, mb), mc)
    wa, wb, wc = jnp.exp(ma - mall), jnp.exp(mb - mall), jnp.exp(mc - mall)
    numer = wa * n0[...] + wb * n1[...] + wc * n2[...]
    denom = wa * d0[...] + wb * d1[...] + wc * d2[...]
    o_ref[...] = (numer / denom * szd_ref[...].astype(F32)).astype(BF16)


def _attention(proj, bias):
    seq = proj.shape[0]
    t = ATTN_TILE
    q0 =(2 * D_CONFORMER) // HEAD_DIM
    k0 = q0 + N_HEADS
    v0 = k0 + N_HEADS
    z0 = v0 + N_HEADS
    cur = lambda c0: pl.BlockSpec((t, HEAD_DIM), lambda h, i: (i, c0 + h))
    prev = lambda c0: pl.BlockSpec((t, HEAD_DIM), lambda h, i: (jnp.maximum(i - 1, 0), c0 + h))
    sc = lambda rows: pltpu.VMEM((rows, HEAD_DIM), F32)
    return pl.pallas_call(
        _attn_kernel,
        grid=(N_HEADS, seq // t),
        in_specs=[cur(q0), cur(k0), prev(k0), cur(v0), prev(v0), cur(z0),
                  pl.BlockSpec((1,) + bias.shape[1:], lambda h, i: (h, 0, 0, 0))],
        out_specs=pl.BlockSpec((t, HEAD_DIM), lambda h, i: (i, h)),
        out_shape=jax.ShapeDtypeStruct((seq, D_ATTN), BF16),
        scratch_shapes=[sc(t), sc(2 * t), sc(2 * t)] + [sc(t)] * 9,
        compiler_params=_params(("arbitrary", "arbitrary")),
        name="dilated_attn",
    )(proj, proj, proj, proj, proj, proj, bias)


def _odd_out_kernel(h_ref, yc_ref, yd_ref, wout_ref, g_ref, o_ref):
    acc = _dot(yc_ref[...], wout_ref[0:D_CONFORMER, :])
    acc = acc + _dot(yd_ref[...], wout_ref[D_CONFORMER:D_CONFORMER + D_ATTN, :])
    o_ref[...] = _rms(h_ref[...] + acc, g_ref[...])


def _odd_out(h, yc, yd, w_out, g, tm):
    seq = h.shape[0]
    row = lambda w: pl.BlockSpec((tm, w), lambda i: (i, 0))
    const = lambda a: pl.BlockSpec(a.shape, lambda i: (0, 0))
    return pl.pallas_call(
        _odd_out_kernel,
        grid=(seq // tm,),
        in_specs=[row(D_MODEL), row(D_CONFORMER), row(D_ATTN), const(w_out), const(g)],
        out_specs=row(D_MODEL),
        out_shape=jax.ShapeDtypeStruct((seq, D_MODEL), F32),
        compiler_params=_params(("arbitrary",)),
        name="odd_out",
    )(h, yc, yd, w_out, g)


def _row_tile(seq, want):
    return min(seq, want)


def kernel(x, norm_g, final_norm_g, ev_w_in, ev_conv_w, s5_lam_re, s5_lam_im, s5_log_dt, s5_b_re, s5_b_im, s5_c_re, s5_c_im, s5_d, s5_w_glu, ev_w_out, od_w_in, cf_conv_w, cf_conv_b, cf_ln_g, cf_ln_b, cf_w_pw, od_w_out, rel_bias):
    bsz, seq, _ = x.shape
    assert bsz == 1 and seq % ATTN_TILE == 0
    x2 = x.reshape(seq, D_MODEL)
    g0 = norm_g[0].astype(F32).reshape(1, D_MODEL)
    g1 = norm_g[1].astype(F32).reshape(1, D_MODEL)
    gf = final_norm_g.astype(F32).reshape(1, D_MODEL)

    ya, u, szb = _even_in(x2, g0, ev_w_in[0].astype(BF16), ev_conv_w[0].astype(F32), _row_tile(seq, 512))
    tables = _s5_tables(s5_lam_re[0], s5_lam_im[0], s5_log_dt[0], s5_b_re[0], s5_b_im[0],
                        s5_c_re[0], s5_c_im[0], s5_d[0])
    nchunk = seq // SSM_CHUNK
    xg = u.reshape(nchunk, SSM_CHUNK, SSM_GROUPS, SSM_GROUP).transpose(2, 0, 1, 3).reshape(
        SSM_GROUPS, nchunk, SSM_FLAT)
    yg = _s5(xg, tables, min(nchunk, 64))
    ys = yg.reshape(SSM_GROUPS, nchunk, SSM_CHUNK, SSM_GROUP).transpose(1, 2, 0, 3).reshape(seq, D_SSM)
    h1, hn1 = _even_out(x2, ya, ys, szb, s5_w_glu[0].astype(BF16), ev_w_out[0].astype(BF16), g1,
                        _row_tile(seq, 256))

    proj = _odd_in(hn1, od_w_in[0].astype(BF16), _row_tile(seq, 1024))
    conv_w = jnp.concatenate([cf_conv_w[0].astype(F32), jnp.zeros((1, D_CONFORMER), F32)], axis=0)
    yc = _conformer(proj, conv_w, cf_conv_b[0].astype(F32).reshape(1, -1),
                    cf_ln_g[0].astype(F32).reshape(1, -1), cf_ln_b[0].astype(F32).reshape(1, -1),
                    cf_w_pw[0].astype(BF16), _row_tile(seq, 512))
    yd = _attention(proj, _attn_bias(rel_bias))
    out = _odd_out(h1, yc, yd, od_w_out[0].astype(BF16), gf, _row_tile(seq, 256))
    return out.reshape(bsz, seq, D_MODEL)
```

```python
import functools
import math

import numpy as np
import jax
import jax.numpy as jnp
from jax import lax
from jax.experimental import pallas as pl
from jax.experimental.pallas import tpu as pltpu

F32 = jnp.float32
BF16 = jnp.bfloat16

D_MODEL = 2048
D_SHORTCONV = 1536
SHORTCONV_WIDTH = 3
D_SSM = 512
SSM_GROUP = 16
SSM_GROUPS = 32
SSM_STATE = 64
D_CONFORMER = 1024
CONFORMER_WIDTH = 31
D_ATTN = 1024
HEAD_DIM = 128
N_HEADS = 8
DILATED_PATTERNS = ((128, 1), (512, 4), (2048, 16))
ATTN_BLOCK = 128
REL_BUCKETS = 32
REL_MAX_DISTANCE = 2048
NORM_EPS = 1e-6
NEG_INF = -1e30

COL = 512
SSM_CHUNK = 16
SSM_FLAT = SSM_CHUNK * SSM_GROUP
ATTN_TILE = 2048
CONV_HALO = 32
VMEM_LIMIT = 50 * 1024 * 1024


def _params(sem, vmem=VMEM_LIMIT):
    return pltpu.CompilerParams(dimension_semantics=sem, vmem_limit_bytes=vmem)


def _dot(a, b):
    return jnp.dot(a, b, preferred_element_type=F32)


def _sigmoid(z):
    return 1.0 / (1.0 + jnp.exp(-z))


def _silu(z):
    return z * _sigmoid(z)


def _gelu_tanh(x):
    return 0.5 * x * (1.0 + jnp.tanh(math.sqrt(2.0 / math.pi) * (x + 0.044715 * (x * x * x))))


def _rms(xf, g):
    ms = jnp.mean(xf * xf, axis=-1, keepdims=True)
    return xf * lax.rsqrt(ms + NORM_EPS) * g


def _even_in_kernel(x_ref, g_ref, wa_ref, wb_ref, wc_ref, wd_ref, cw_ref,
                    ya_ref, u_ref, szb_ref, hn_sc, pbuf_sc, carry_sc, *, tm):
    i = pl.program_id(0)
    j = pl.program_id(1)

    @pl.when(j == 0)
    def _():
        hn_sc[...] = _rms(x_ref[...], g_ref[...]).astype(BF16)

    @pl.when(j < 3)
    def _():
        hn = hn_sc[...]
        p = _dot(hn, wb_ref[...]) * _dot(hn, wa_ref[...])

        @pl.when(i == 0)
        def _():
            carry_sc[j] = jnp.zeros((8, COL), F32)

        pbuf_sc[0:8, :] = carry_sc[j]
        pbuf_sc[8:8 + tm, :] = p
        carry_sc[j] = pbuf_sc[tm:tm + 8, :]
        cw = cw_ref[...]
        conv = (cw[2:3, :] * p + cw[1:2, :] * pbuf_sc[pl.ds(7, tm), :]
                + cw[0:1, :] * pbuf_sc[pl.ds(6, tm), :])
        gb = _dot(hn, wc_ref[...])
        za = _dot(hn, wd_ref[...])
        ya_ref[...] = (gb * conv * _silu(za)).astype(BF16)

    @pl.when(j == 3)
    def _():
        hn = hn_sc[...]
        u_ref[...] = _dot(hn, wa_ref[...])
        szb_ref[...] = _silu(_dot(hn, wb_ref[...])).astype(BF16)


def _even_in(x, g, w_in, conv_w, tm):
    seq = x.shape[0]
    nconv = D_SHORTCONV // COL

    def wspec(k, alt):
        return pl.BlockSpec((D_MODEL, COL), lambda i, j: (0, jnp.where(j < nconv, j + nconv * k, alt)))

    u_blk = 4 * nconv
    return pl.pallas_call(
        functools.partial(_even_in_kernel, tm=tm),
        grid=(seq // tm, nconv + 1),
        in_specs=[
            pl.BlockSpec((tm, D_MODEL), lambda i, j: (i, 0)),
            pl.BlockSpec((1, D_MODEL), lambda i, j: (0, 0)),
            wspec(0, u_blk), wspec(1, u_blk + 1), wspec(2, 3 * nconv - 1), wspec(3, 4 * nconv - 1),
            pl.BlockSpec((SHORTCONV_WIDTH, COL), lambda i, j: (0, jnp.minimum(j, nconv - 1))),
        ],
        out_specs=[
            pl.BlockSpec((tm, COL), lambda i, j: (i, jnp.minimum(j, nconv - 1))),
            pl.BlockSpec((tm, D_SSM), lambda i, j: (i, 0)),
            pl.BlockSpec((tm, D_SSM), lambda i, j: (i, 0)),
        ],
        out_shape=[
            jax.ShapeDtypeStruct((seq, D_SHORTCONV), BF16),
            jax.ShapeDtypeStruct((seq, D_SSM), F32),
            jax.ShapeDtypeStruct((seq, D_SSM), BF16),
        ],
        scratch_shapes=[
            pltpu.VMEM((tm, D_MODEL), BF16),
            pltpu.VMEM((tm + 8, COL), F32),
            pltpu.VMEM((nconv, 8, COL), F32),
        ],
        compiler_params=_params(("arbitrary", "arbitrary")),
        name="even_in",
    )(x, g, w_in, w_in, w_in, w_in, conv_w)


def _s5_tables(lam_re, lam_im, log_dt, b_re, b_im, c_re, c_im, d_skip):
    hi = lax.Precision.HIGHEST
    t_, g_, n_, p_ = SSM_CHUNK, SSM_GROUPS, SSM_STATE, SSM_GROUP
    lam_re, lam_im = lam_re.astype(F32), lam_im.astype(F32)
    dt = jnp.exp(log_dt.astype(F32))[:, None]
    xr, xi = lam_re * dt, lam_im * dt
    k = jnp.arange(t_ + 1, dtype=F32)[:, None, None]
    mag = jnp.exp(xr * k)
    pr, pi = mag * jnp.cos(xi * k), mag * jnp.sin(xi * k)
    nr, ni = pr[1] - 1.0, pi[1]
    den = lam_re * lam_re + lam_im * lam_im
    fr = (nr * lam_re + ni * lam_im) / den
    fi = (ni * lam_re - nr * lam_im) / den
    b_re, b_im = b_re.astype(F32), b_im.astype(F32)
    bbr = fr[..., None] * b_re - fi[..., None] * b_im
    bbi = fr[..., None] * b_im + fi[..., None] * b_re
    c_re, c_im = c_re.astype(F32), c_im.astype(F32)

    cpr = c_re[None] * pr[:t_, :, None, :] - c_im[None] * pi[:t_, :, None, :]
    cpi = c_re[None] * pi[:t_, :, None, :] + c_im[None] * pr[:t_, :, None, :]
    kern = (jnp.einsum('tgpn,gnq->tgpq', cpr, bbr, precision=hi)
            - jnp.einsum('tgpn,gnq->tgpq', cpi, bbi, precision=hi))
    skip = d_skip.astype(F32).reshape(g_, p_)[:, :, None] * jnp.eye(p_, dtype=F32)[None]
    kern = kern.at[0].add(skip)
    tj = np.arange(t_)
    tau = tj[None, :] - tj[:, None]
    m = jnp.where((tau >= 0)[:, :, None, None, None], kern[np.clip(tau, 0, None)], 0.0)
    m = jnp.transpose(m, (2, 0, 4, 1, 3)).reshape(g_, t_ * p_, t_ * p_)

    prj, pij = pr[t_ - 1 - tj], pi[t_ - 1 - tj]
    wr = prj[..., None] * bbr[None] - pij[..., None] * bbi[None]
    wi = prj[..., None] * bbi[None] + pij[..., None] * bbr[None]
    wr = jnp.transpose(wr, (1, 0, 3, 2)).reshape(g_, t_ * p_, n_)
    wi = jnp.transpose(wi, (1, 0, 3, 2)).reshape(g_, t_ * p_, n_)

    vr = c_re[None] * pr[1:, :, None, :] - c_im[None] * pi[1:, :, None, :]
    vi = c_re[None] * pi[1:, :, None, :] + c_im[None] * pr[1:, :, None, :]
    vr = jnp.transpose(vr, (1, 3, 0, 2)).reshape(g_, n_, t_ * p_)
    vi = -jnp.transpose(vi, (1, 3, 0, 2)).reshape(g_, n_, t_ * p_)

    flat = t_ * p_
    z = jnp.zeros((g_ // 2, flat, n_), F32)
    wr2, wi2 = wr.reshape(g_ // 2, 2, flat, n_), wi.reshape(g_ // 2, 2, flat, n_)
    wre_p = jnp.concatenate([jnp.concatenate([wr2[:, 0], z], axis=2),
                             jnp.concatenate([z, wr2[:, 1]], axis=2)], axis=1)
    wim_p = jnp.concatenate([jnp.concatenate([wi2[:, 0], z], axis=2),
                             jnp.concatenate([z, wi2[:, 1]], axis=2)], axis=1)
    zv = jnp.zeros((g_ // 2, n_, flat), F32)
    vr2, vi2 = vr.reshape(g_ // 2, 2, n_, flat), vi.reshape(g_ // 2, 2, n_, flat)
    v_even = jnp.concatenate([vr2[:, 0], zv, vi2[:, 0], zv], axis=1)
    v_odd = jnp.concatenate([zv, vr2[:, 1], zv, vi2[:, 1]], axis=1)
    v_p = jnp.stack([v_even, v_odd], axis=1).reshape(g_, 4 * n_, flat)
    a_re = pr[t_].reshape(1, g_ * n_)
    a_im = pi[t_].reshape(1, g_ * n_)
    return m.astype(BF16), wre_p.astype(BF16), wim_p.astype(BF16), v_p.astype(BF16), a_re, a_im


def _s5_kernel(x_ref, m_ref, wre_ref, wim_ref, v_ref, are_ref, aim_ref, y_ref,
               zre_sc, zim_sc, carry_sc, *, ct):
    i = pl.program_id(0)
    npair = SSM_GROUPS // 2

    @pl.when(i == 0)
    def _():
        carry_sc[...] = jnp.zeros_like(carry_sc)

    for gp in range(npair):
        xp = jnp.concatenate([x_ref[2 * gp], x_ref[2 * gp + 1]], axis=1).astype(BF16)
        zre_sc[:, gp * 128:(gp + 1) * 128] = _dot(xp, wre_ref[gp])
        zim_sc[:, gp * 128:(gp + 1) * 128] = _dot(xp, wim_ref[gp])

    are = are_ref[...]
    aim = aim_ref[...]

    def step(r, carry):
        sre, sim = carry
        zr = zre_sc[pl.ds(r, 1), :]
        zi = zim_sc[pl.ds(r, 1), :]
        zre_sc[pl.ds(r, 1), :] = sre
        zim_sc[pl.ds(r, 1), :] = sim
        return are * sre - aim * sim + zr, are * sim + aim * sre + zi

    sre, sim = lax.fori_loop(0, ct, step, (carry_sc[0:1, :], carry_sc[1:2, :]), unroll=8)
    carry_sc[0:1, :] = sre
    carry_sc[1:2, :] = sim

    for g in range(SSM_GROUPS):
        gp = g // 2
        sp = jnp.concatenate([zre_sc[:, gp * 128:(gp + 1) * 128],
                              zim_sc[:, gp * 128:(gp + 1) * 128]], axis=1).astype(BF16)
        y_ref[g] = _dot(x_ref[g].astype(BF16), m_ref[g]) + _dot(sp, v_ref[g])


def _s5(xg, tables, ct):
    m, wre, wim, v, are, aim = tables
    g_, c_, flat = xg.shape
    full = lambda a: pl.BlockSpec(a.shape, lambda i: (0,) * a.ndim)
    return pl.pallas_call(
        functools.partial(_s5_kernel, ct=ct),
        grid=(c_ // ct,),
        in_specs=[pl.BlockSpec((g_, ct, flat), lambda i: (0, i, 0)),
                  full(m), full(wre), full(wim), full(v), full(are), full(aim)],
        out_specs=pl.BlockSpec((g_, ct, flat), lambda i: (0, i, 0)),
        out_shape=jax.ShapeDtypeStruct((g_, c_, flat), F32),
        scratch_shapes=[pltpu.VMEM((ct, SSM_GROUPS * SSM_STATE), F32),
                        pltpu.VMEM((ct, SSM_GROUPS * SSM_STATE), F32),
                        pltpu.VMEM((8, SSM_GROUPS * SSM_STATE), F32)],
        compiler_params=_params(("arbitrary",)),
        name="s5_ssm",
    )(xg, m, wre, wim, v, are, aim)


def _even_out_kernel(x_ref, ya_ref, ys_ref, szb_ref, wglu_ref, wout_ref, g_ref, h_ref, hn_ref):
    gl = _gelu_tanh(ys_ref[...])
    yb = gl * _sigmoid(_dot(gl.astype(BF16), wglu_ref[...])) * szb_ref[...].astype(F32)
    acc = _dot(ya_ref[...], wout_ref[0:D_SHORTCONV, :])
    acc = acc + _dot(yb.astype(BF16), wout_ref[D_SHORTCONV:D_SHORTCONV + D_SSM, :])
    h = x_ref[...] + acc
    h_ref[...] = h
    hn_ref[...] = _rms(h, g_ref[...]).astype(BF16)


def _even_out(x, ya, ys, szb, w_glu, w_out, g, tm):
    seq = x.shape[0]
    row = lambda w: pl.BlockSpec((tm, w), lambda i: (i, 0))
    const = lambda a: pl.BlockSpec(a.shape, lambda i: (0, 0))
    return pl.pallas_call(
        _even_out_kernel,
        grid=(seq // tm,),
        in_specs=[row(D_MODEL), row(D_SHORTCONV), row(D_SSM), row(D_SSM),
                  const(w_glu), const(w_out), const(g)],
        out_specs=[row(D_MODEL), row(D_MODEL)],
        out_shape=[jax.ShapeDtypeStruct((seq, D_MODEL), F32),
                   jax.ShapeDtypeStruct((seq, D_MODEL), BF16)],
        compiler_params=_params(("arbitrary",)),
        name="even_out",
    )(x, ya, ys, szb, w_glu, w_out, g)


ODD_STEPS = 12
_SILU_STEPS = (2, 3, 10, 11)
_Q_STEPS = (4, 5)


def _odd_in_kernel(hn_ref, wa_ref, wb_ref, o_ref):
    s = pl.program_id(1)
    hn = hn_ref[...]
    is_silu = (s == 2) | (s == 3) | (s == 10) | (s == 11)

    @pl.when(s < 2)
    def _():
        o_ref[...] = (_dot(hn, wa_ref[...]) * _sigmoid(_dot(hn, wb_ref[...]))).astype(BF16)

    @pl.when(is_silu)
    def _():
        o_ref[...] = _silu(_dot(hn, wa_ref[...])).astype(BF16)

    @pl.when((s >= 2) & jnp.logical_not(is_silu))
    def _():
        scale = jnp.where((s == 4) | (s == 5), HEAD_DIM ** -0.5, 1.0).astype(F32)
        o_ref[...] = (_dot(hn, wa_ref[...]) * scale).astype(BF16)


def _odd_in(hn, w_in, tm):
    seq = hn.shape[0]
    return pl.pallas_call(
        _odd_in_kernel,
        grid=(seq // tm, ODD_STEPS),
        in_specs=[pl.BlockSpec((tm, D_MODEL), lambda i, s: (i, 0)),
                  pl.BlockSpec((D_MODEL, COL), lambda i, s: (0, jnp.where(s < 2, s, s + 2))),
                  pl.BlockSpec((D_MODEL, COL), lambda i, s: (0, jnp.where(s < 2, s + 2, 3)))],
        out_specs=pl.BlockSpec((tm, COL), lambda i, s: (i, s)),
        out_shape=jax.ShapeDtypeStruct((seq, ODD_STEPS * COL), BF16),
        compiler_params=_params(("arbitrary", "arbitrary")),
        name="odd_in",
    )(hn, w_in, w_in)


CONV_ROWS = 64


def _conformer_kernel(cur_ref, halo_ref, szc_ref, cw_ref, cb_ref, lg_ref, lb_ref, wpw_ref,
                      o_ref, buf_sc, conv_sc, *, tm):
    i = pl.program_id(0)
    halo = halo_ref[...].astype(F32)
    buf_sc[0:CONV_HALO, :] = jnp.where(i == 0, 0.0, halo)
    buf_sc[CONV_HALO:CONV_HALO + tm, :] = cur_ref[...].astype(F32)
    shift = CONV_HALO - (CONFORMER_WIDTH - 1)

    for base in range(0, tm, CONV_ROWS):
        for c in range(D_CONFORMER // 128):
            lanes = slice(c * 128, (c + 1) * 128)
            acc = jnp.zeros((CONV_ROWS, 128), F32) + cb_ref[:, lanes]
            for k in range(CONFORMER_WIDTH):
                acc = acc + cw_ref[k:k + 1, lanes] * buf_sc[base + shift + k:base + shift + k + CONV_ROWS, lanes]
            conv_sc[base:base + CONV_ROWS, lanes] = acc
    y = conv_sc[...]
    mu = jnp.mean(y, axis=-1, keepdims=True)
    yc = y - mu
    var = jnp.mean(yc * yc, axis=-1, keepdims=True)
    ln = yc * lax.rsqrt(var + NORM_EPS) * lg_ref[...] + lb_ref[...]
    o_ref[...] = (_dot(_silu(ln).astype(BF16), wpw_ref[...]) * szc_ref[...].astype(F32)).astype(BF16)


def _conformer(proj, conv_w, conv_b, ln_g, ln_b, w_pw, tm):
    seq = proj.shape[0]
    per = tm // CONV_HALO
    const = lambda a: pl.BlockSpec(a.shape, lambda i: (0, 0))
    return pl.pallas_call(
        functools.partial(_conformer_kernel, tm=tm),
        grid=(seq // tm,),
        in_specs=[pl.BlockSpec((tm, D_CONFORMER), lambda i: (i, 0)),
                  pl.BlockSpec((CONV_HALO, D_CONFORMER), lambda i: (jnp.maximum(i * per - 1, 0), 0)),
                  pl.BlockSpec((tm, D_CONFORMER), lambda i: (i, 1)),
                  const(conv_w), const(conv_b), const(ln_g), const(ln_b), const(w_pw)],
        out_specs=pl.BlockSpec((tm, D_CONFORMER), lambda i: (i, 0)),
        out_shape=jax.ShapeDtypeStruct((seq, D_CONFORMER), BF16),
        scratch_shapes=[pltpu.VMEM((tm + CONV_HALO, D_CONFORMER), F32),
                        pltpu.VMEM((tm, D_CONFORMER), F32)],
        compiler_params=_params(("arbitrary",)),
        name="conformer",
    )(proj, proj, proj, conv_w, conv_b, ln_g, ln_b, w_pw)


def _t5_bucket(dist):
    max_exact = REL_BUCKETS // 2
    d_f = jnp.maximum(dist, 1).astype(F32)
    large = max_exact + (jnp.log(d_f / max_exact) / math.log(REL_MAX_DISTANCE / max_exact)
                         * (REL_BUCKETS - max_exact)).astype(jnp.int32)
    large = jnp.minimum(large, REL_BUCKETS - 1)
    return jnp.where(dist < max_exact, dist, large)


def _attn_bias(rel_bias):
    q = ATTN_BLOCK
    nh = rel_bias.shape[1]
    kj = np.arange(2 * q)[None, None, :]
    neg = lambda n: jnp.full((nh, n), NEG_INF, F32)
    tabs = []
    for window, dilation in DILATED_PATTERNS:
        steps = window // dilation
        assert steps == q
        rel = jnp.asarray(np.arange(steps + 1) * dilation, dtype=jnp.int32)
        by_rel = rel_bias.astype(F32)[_t5_bucket(rel)].T
        g = jnp.concatenate([by_rel[:, ::-1], neg(2 * q - 1)], axis=1)
        flat = jnp.tile(g, (1, q))[:, :q * (3 * q - 1)]
        tab = flat.reshape(nh, q, 3 * q - 1)[:, :, :2 * q]
        tabs.append(tab)
        tabs.append(jnp.where(kj >= q, tab, NEG_INF))
    return jnp.stack(tabs, axis=1)


def _attn_kernel(q_ref, kc_ref, kp_ref, vc_ref, vp_ref, szd_ref, bias_ref, o_ref,
                 qf, kf, vf, n0, n1, n2, d0, d1, d2, m0, m1, m2):
    i = pl.program_id(1)
    t = ATTN_TILE
    blk = ATTN_BLOCK
    first_tile = i == 0
    qf[...] = q_ref[...].astype(F32)
    kf[0:t, :] = jnp.where(first_tile, 0.0, kp_ref[...].astype(F32))
    kf[t:2 * t, :] = kc_ref[...].astype(F32)
    vf[0:t, :] = jnp.where(first_tile, 0.0, vp_ref[...].astype(F32))
    vf[t:2 * t, :] = vc_ref[...].astype(F32)
    ones = jnp.ones((2 * blk, HEAD_DIM), BF16)

    first = first_tile.astype(jnp.int32)

    def block(pidx, qstart, kstart, stride, at_start, num_sc, den_sc, max_sc):
        st = None if stride == 1 else stride
        qrows = pl.ds(qstart, blk, stride=st)
        krows = pl.ds(kstart, 2 * blk, stride=st)
        qb = qf[qrows, :].astype(BF16)
        kb = kf[krows, :].astype(BF16)
        vb = vf[krows, :].astype(BF16)
        s = lax.dot_general(qb, kb, (((1,), (1,)), ((), ())), preferred_element_type=F32)
        s = s + bias_ref[0, 2 * pidx + first if at_start else 2 * pidx]
        mx = jnp.max(s, axis=-1, keepdims=True)
        p = jnp.exp(s - mx).astype(BF16)
        pv = _dot(p, jnp.concatenate([vb, ones], axis=1))
        num_sc[qrows, :] = pv[:, :HEAD_DIM]
        den_sc[qrows, :] = pv[:, HEAD_DIM:]
        max_sc[qrows, :] = jnp.broadcast_to(mx, (blk, HEAD_DIM))

    scratch = ((n0, d0, m0), (n1, d1, m1), (n2, d2, m2))
    for pidx, (_, dil) in enumerate(DILATED_PATTERNS):
        span = blk * dil
        for sb in range(t // span):
            for r in range(dil):
                block(pidx, sb * span + r, t + (sb - 1) * span + r, dil, sb == 0, *scratch[pidx])

    ma, mb, mc = m0[...], m1[...], m2[...]
    mall = jnp.maximum(jnp.maximum(ma, mb), mc)
    wa, wb, wc = jnp.exp(ma - mall), jnp.exp(mb - mall), jnp.exp(mc - mall)
    numer = wa * n0[...] + wb * n1[...] + wc * n2[...]
    denom = wa * d0[...] + wb * d1[...] + wc * d2[...]
    o_ref[...] = (numer / denom * szd_ref[...].astype(F32)).astype(BF16)


def _attention(proj, bias):
    seq = proj.shape[0]
    t = ATTN_TILE
    q0 =(2 * D_CONFORMER) // HEAD_DIM
    k0 = q0 + N_HEADS
    v0 = k0 + N_HEADS
    z0 = v0 + N_HEADS
    cur = lambda c0: pl.BlockSpec((t, HEAD_DIM), lambda h, i: (i, c0 + h))
    prev = lambda c0: pl.BlockSpec((t, HEAD_DIM), lambda h, i: (jnp.maximum(i - 1, 0), c0 + h))
    sc = lambda rows: pltpu.VMEM((rows, HEAD_DIM), F32)
    return pl.pallas_call(
        _attn_kernel,
        grid=(N_HEADS, seq // t),
        in_specs=[cur(q0), cur(k0), prev(k0), cur(v0), prev(v0), cur(z0),
                  pl.BlockSpec((1,) + bias.shape[1:], lambda h, i: (h, 0, 0, 0))],
        out_specs=pl.BlockSpec((t, HEAD_DIM), lambda h, i: (i, h)),
        out_shape=jax.ShapeDtypeStruct((seq, D_ATTN), BF16),
        scratch_shapes=[sc(t), sc(2 * t), sc(2 * t)] + [sc(t)] * 9,
        compiler_params=_params(("arbitrary", "arbitrary")),
        name="dilated_attn",
    )(proj, proj, proj, proj, proj, proj, bias)


def _odd_out_kernel(h_ref, yc_ref, yd_ref, wout_ref, g_ref, o_ref):
    acc = _dot(yc_ref[...], wout_ref[0:D_CONFORMER, :])
    acc = acc + _dot(yd_ref[...], wout_ref[D_CONFORMER:D_CONFORMER + D_ATTN, :])
    o_ref[...] = _rms(h_ref[...] + acc, g_ref[...])


def _odd_out(h, yc, yd, w_out, g, tm):
    seq = h.shape[0]
    row = lambda w: pl.BlockSpec((tm, w), lambda i: (i, 0))
    const = lambda a: pl.BlockSpec(a.shape, lambda i: (0, 0))
    return pl.pallas_call(
        _odd_out_kernel,
        grid=(seq // tm,),
        in_specs=[row(D_MODEL), row(D_CONFORMER), row(D_ATTN), const(w_out), const(g)],
        out_specs=row(D_MODEL),
        out_shape=jax.ShapeDtypeStruct((seq, D_MODEL), F32),
        compiler_params=_params(("arbitrary",)),
        name="odd_out",
    )(h, yc, yd, w_out, g)


def _row_tile(seq, want):
    return min(seq, want)


def kernel(x, norm_g, final_norm_g, ev_w_in, ev_conv_w, s5_lam_re, s5_lam_im, s5_log_dt, s5_b_re, s5_b_im, s5_c_re, s5_c_im, s5_d, s5_w_glu, ev_w_out, od_w_in, cf_conv_w, cf_conv_b, cf_ln_g, cf_ln_b, cf_w_pw, od_w_out, rel_bias):
    bsz, seq, _ = x.shape
    assert bsz == 1 and seq % ATTN_TILE == 0
    x2 = x.reshape(seq, D_MODEL)
    g0 = norm_g[0].astype(F32).reshape(1, D_MODEL)
    g1 = norm_g[1].astype(F32).reshape(1, D_MODEL)
    gf = final_norm_g.astype(F32).reshape(1, D_MODEL)

    ya, u, szb = _even_in(x2, g0, ev_w_in[0].astype(BF16), ev_conv_w[0].astype(F32), _row_tile(seq, 512))
    tables = _s5_tables(s5_lam_re[0], s5_lam_im[0], s5_log_dt[0], s5_b_re[0], s5_b_im[0],
                        s5_c_re[0], s5_c_im[0], s5_d[0])
    nchunk = seq // SSM_CHUNK
    xg = u.reshape(nchunk, SSM_CHUNK, SSM_GROUPS, SSM_GROUP).transpose(2, 0, 1, 3).reshape(
        SSM_GROUPS, nchunk, SSM_FLAT)
    yg = _s5(xg, tables, min(nchunk, 64))
    ys = yg.reshape(SSM_GROUPS, nchunk, SSM_CHUNK, SSM_GROUP).transpose(1, 2, 0, 3).reshape(seq, D_SSM)
    h1, hn1 = _even_out(x2, ya, ys, szb, s5_w_glu[0].astype(BF16), ev_w_out[0].astype(BF16), g1,
                        _row_tile(seq, 256))

    proj = _odd_in(hn1, od_w_in[0].astype(BF16), _row_tile(seq, 1024))
    conv_w = jnp.concatenate([cf_conv_w[0].astype(F32), jnp.zeros((1, D_CONFORMER), F32)], axis=0)
    yc = _conformer(proj, conv_w, cf_conv_b[0].astype(F32).reshape(1, -1),
                    cf_ln_g[0].astype(F32).reshape(1, -1), cf_ln_b[0].astype(F32).reshape(1, -1),
                    cf_w_pw[0].astype(BF16), _row_tile(seq, 512))
    yd = _attention(proj, _attn_bias(rel_bias))
    out = _odd_out(h1, yc, yd, od_w_out[0].astype(BF16), gf, _row_tile(seq, 256))
    return out.reshape(bsz, seq, D_MODEL)
```

```python
import functools
import math

import numpy as np
import jax
import jax.numpy as jnp
from jax import lax
from jax.experimental import pallas as pl
from jax.experimental.pallas import tpu as pltpu

F32 = jnp.float32
BF16 = jnp.bfloat16

D_MODEL = 2048
D_SHORTCONV = 1536
SHORTCONV_WIDTH = 3
D_SSM = 512
SSM_GROUP = 16
SSM_GROUPS = 32
SSM_STATE = 64
D_CONFORMER = 1024
CONFORMER_WIDTH = 31
D_ATTN = 1024
HEAD_DIM = 128
N_HEADS = 8
DILATED_PATTERNS = ((128, 1), (512, 4), (2048, 16))
ATTN_BLOCK = 128
REL_BUCKETS = 32
REL_MAX_DISTANCE = 2048
NORM_EPS = 1e-6
NEG_INF = -1e30

COL = 512
SSM_CHUNK = 16
SSM_FLAT = SSM_CHUNK * SSM_GROUP
SSM_SLABS = D_SSM // 128
ATTN_TILE = 2048
CONV_HALO = 32
VMEM_LIMIT = 50 * 1024 * 1024


def _params(sem, vmem=VMEM_LIMIT):
    return pltpu.CompilerParams(dimension_semantics=sem, vmem_limit_bytes=vmem)


def _dot(a, b):
    return jnp.dot(a, b, preferred_element_type=F32)


def _sigmoid(z):
    return 1.0 / (1.0 + jnp.exp(-z))


def _silu(z):
    return z * _sigmoid(z)


def _gelu_tanh(x):
    return 0.5 * x * (1.0 + jnp.tanh(math.sqrt(2.0 / math.pi) * (x + 0.044715 * (x * x * x))))


def _rms(xf, g):
    ms = jnp.mean(xf * xf, axis=-1, keepdims=True)
    return xf * lax.rsqrt(ms + NORM_EPS) * g


def _even_in_kernel(x_ref, g_ref, wa_ref, wb_ref, wc_ref, wd_ref, cw_ref,
                    ya_ref, u_ref, szb_ref, hn_sc, pbuf_sc, carry_sc, *, tm):
    i = pl.program_id(0)
    j = pl.program_id(1)

    @pl.when(j == 0)
    def _():
        hn_sc[...] = _rms(x_ref[...], g_ref[...]).astype(BF16)

    @pl.when(j < 3)
    def _():
        hn = hn_sc[...]
        p = _dot(hn, wb_ref[...]) * _dot(hn, wa_ref[...])

        @pl.when(i == 0)
        def _():
            carry_sc[j] = jnp.zeros((8, COL), F32)

        pbuf_sc[0:8, :] = carry_sc[j]
        pbuf_sc[8:8 + tm, :] = p
        carry_sc[j] = pbuf_sc[tm:tm + 8, :]
        cw = cw_ref[...]
        conv = (cw[2:3, :] * p + cw[1:2, :] * pbuf_sc[pl.ds(7, tm), :]
                + cw[0:1, :] * pbuf_sc[pl.ds(6, tm), :])
        gb = _dot(hn, wc_ref[...])
        za = _dot(hn, wd_ref[...])
        ya_ref[...] = (gb * conv * _silu(za)).astype(BF16)

    @pl.when(j == 3)
    def _():
        hn = hn_sc[...]
        u = _dot(hn, wa_ref[...])
        for s in range(SSM_SLABS):
            u_ref[s] = u[:, s * 128:(s + 1) * 128]
        szb_ref[...] = _silu(_dot(hn, wb_ref[...])).astype(BF16)


def _even_in(x, g, w_in, conv_w, tm):
    seq = x.shape[0]
    nconv = D_SHORTCONV // COL

    def wspec(k, alt):
        return pl.BlockSpec((D_MODEL, COL), lambda i, j: (0, jnp.where(j < nconv, j + nconv * k, alt)))

    u_blk = 4 * nconv
    return pl.pallas_call(
        functools.partial(_even_in_kernel, tm=tm),
        grid=(seq // tm, nconv + 1),
        in_specs=[
            pl.BlockSpec((tm, D_MODEL), lambda i, j: (i, 0)),
            pl.BlockSpec((1, D_MODEL), lambda i, j: (0, 0)),
            wspec(0, u_blk), wspec(1, u_blk + 1), wspec(2, 3 * nconv - 1), wspec(3, 4 * nconv - 1),
            pl.BlockSpec((SHORTCONV_WIDTH, COL), lambda i, j: (0, jnp.minimum(j, nconv - 1))),
        ],
        out_specs=[
            pl.BlockSpec((tm, COL), lambda i, j: (i, jnp.minimum(j, nconv - 1))),
            pl.BlockSpec((SSM_SLABS, tm, 128), lambda i, j: (0, i, 0)),
            pl.BlockSpec((tm, D_SSM), lambda i, j: (i, 0)),
        ],
        out_shape=[
            jax.ShapeDtypeStruct((seq, D_SHORTCONV), BF16),
            jax.ShapeDtypeStruct((SSM_SLABS, seq, 128), F32),
            jax.ShapeDtypeStruct((seq, D_SSM), BF16),
        ],
        scratch_shapes=[
            pltpu.VMEM((tm, D_MODEL), BF16),
            pltpu.VMEM((tm + 8, COL), F32),
            pltpu.VMEM((nconv, 8, COL), F32),
        ],
        compiler_params=_params(("arbitrary", "arbitrary")),
        name="even_in",
    )(x, g, w_in, w_in, w_in, w_in, conv_w)


def _s5_tables(lam_re, lam_im, log_dt, b_re, b_im, c_re, c_im, d_skip):
    hi = lax.Precision.HIGHEST
    t_, g_, n_, p_ = SSM_CHUNK, SSM_GROUPS, SSM_STATE, SSM_GROUP
    lam_re, lam_im = lam_re.astype(F32), lam_im.astype(F32)
    dt = jnp.exp(log_dt.astype(F32))[:, None]
    xr, xi = lam_re * dt, lam_im * dt
    k = jnp.arange(t_ + 1, dtype=F32)[:, None, None]
    mag = jnp.exp(xr * k)
    pr, pi = mag * jnp.cos(xi * k), mag * jnp.sin(xi * k)
    nr, ni = pr[1] - 1.0, pi[1]
    den = lam_re * lam_re + lam_im * lam_im
    fr = (nr * lam_re + ni * lam_im) / den
    fi = (ni * lam_re - nr * lam_im) / den
    b_re, b_im = b_re.astype(F32), b_im.astype(F32)
    bbr = fr[..., None] * b_re - fi[..., None] * b_im
    bbi = fr[..., None] * b_im + fi[..., None] * b_re
    c_re, c_im = c_re.astype(F32), c_im.astype(F32)

    cpr = c_re[None] * pr[:t_, :, None, :] - c_im[None] * pi[:t_, :, None, :]
    cpi = c_re[None] * pi[:t_, :, None, :] + c_im[None] * pr[:t_, :, None, :]
    kern = (jnp.einsum('tgpn,gnq->tgpq', cpr, bbr, precision=hi)
            - jnp.einsum('tgpn,gnq->tgpq', cpi, bbi, precision=hi))
    skip = d_skip.astype(F32).reshape(g_, p_)[:, :, None] * jnp.eye(p_, dtype=F32)[None]
    kern = kern.at[0].add(skip)
    flat = t_ * p_
    last = jnp.transpose(kern[::-1], (1, 2, 0, 3)).reshape(g_, p_, flat)
    mt = jnp.stack([jnp.pad(last[:, :, (t_ - 1 - t) * p_:], ((0, 0), (0, 0), (0, (t_ - 1 - t) * p_)))
                    for t in range(t_)], axis=1).reshape(g_, flat, flat)

    prj, pij = pr[t_ - 1::-1][:t_], pi[t_ - 1::-1][:t_]
    wr = prj[..., None] * bbr[None] - pij[..., None] * bbi[None]
    wi = prj[..., None] * bbi[None] + pij[..., None] * bbr[None]
    half = (np.arange(2) == 0)[None, None, :, None, None]

    def pack_w(w):
        w = jnp.transpose(w.reshape(t_, g_ // 2, 2, n_, p_), (1, 0, 2, 4, 3))
        w = jnp.concatenate([jnp.where(half, w, 0.0), jnp.where(half, 0.0, w)], axis=-1)
        return w.reshape(g_ // 2, 2 * flat, 2 * n_)

    vr = c_re[None] * pr[1:, :, None, :] - c_im[None] * pi[1:, :, None, :]
    vi = c_re[None] * pi[1:, :, None, :] + c_im[None] * pr[1:, :, None, :]
    vr = jnp.transpose(vr, (1, 0, 2, 3)).reshape(g_ // 2, 2, flat, n_)
    vi = -jnp.transpose(vi, (1, 0, 2, 3)).reshape(g_ // 2, 2, flat, n_)
    zv = jnp.zeros((g_ // 2, flat, n_), F32)
    v_even = jnp.concatenate([vr[:, 0], zv, vi[:, 0], zv], axis=-1)
    v_odd = jnp.concatenate([zv, vr[:, 1], zv, vi[:, 1]], axis=-1)
    vt = jnp.stack([v_even, v_odd], axis=1).reshape(g_, flat, 4 * n_)
    a_re = pr[t_].reshape(1, g_ * n_)
    a_im = pi[t_].reshape(1, g_ * n_)
    return mt.astype(BF16), pack_w(wr).astype(BF16), pack_w(wi).astype(BF16), vt.astype(BF16), a_re, a_im


_NT = (((1,), (1,)), ((), ()))
_TN = (((0,), (0,)), ((), ()))


def _s5_kernel(u_ref, mt_ref, wre_ref, wim_ref, vt_ref, are_ref, aim_ref, y_ref,
               xt_sc, yt_sc, zre_sc, zim_sc, carry_sc, *, ct):
    i = pl.program_id(0)
    t_ = SSM_CHUNK

    @pl.when(i == 0)
    def _():
        carry_sc[...] = jnp.zeros_like(carry_sc)

    eye = jnp.where(lax.broadcasted_iota(jnp.int32, (128, 128), 0)
                    == lax.broadcasted_iota(jnp.int32, (128, 128), 1), 1.0, 0.0).astype(BF16)

    for s in range(SSM_SLABS):
        for j in range(t_):
            uj = u_ref[s, pl.ds(j, ct, stride=t_), :].astype(BF16)
            xt_sc[s, j] = lax.dot_general(eye, uj, _NT, preferred_element_type=F32).astype(BF16)

    pairs_per_slab = 128 // (2 * SSM_GROUP)
    for gp in range(SSM_GROUPS // 2):
        s, k = gp // pairs_per_slab, gp % pairs_per_slab
        xp = xt_sc[s, :, 32 * k:32 * k + 32, :].reshape(2 * SSM_FLAT, ct)
        zre_sc[:, gp * 128:(gp + 1) * 128] = lax.dot_general(xp, wre_ref[gp], _TN, preferred_element_type=F32)
        zim_sc[:, gp * 128:(gp + 1) * 128] = lax.dot_general(xp, wim_ref[gp], _TN, preferred_element_type=F32)

    are = are_ref[...]
    aim = aim_ref[...]

    def step(r, carry):
        sre, sim = carry
        zr = zre_sc[pl.ds(r, 1), :]
        zi = zim_sc[pl.ds(r, 1), :]
        zre_sc[pl.ds(r, 1), :] = sre
        zim_sc[pl.ds(r, 1), :] = sim
        return are * sre - aim * sim + zr, are * sim + aim * sre + zi

    sre, sim = lax.fori_loop(0, ct, step, (carry_sc[0:1, :], carry_sc[1:2, :]), unroll=8)
    carry_sc[0:1, :] = sre
    carry_sc[1:2, :] = sim

    groups_per_slab = 128 // SSM_GROUP
    for g in range(SSM_GROUPS):
        gp, s, k = g // 2, g // groups_per_slab, g % groups_per_slab
        xg = xt_sc[s, :, 16 * k:16 * k + 16, :].reshape(SSM_FLAT, ct)
        sp = jnp.concatenate([zre_sc[:, gp * 128:(gp + 1) * 128],
                              zim_sc[:, gp * 128:(gp + 1) * 128]], axis=1).astype(BF16)
        yt = _dot(mt_ref[g], xg) + lax.dot_general(vt_ref[g], sp, _NT, preferred_element_type=F32)
        yt_sc[s, :, 16 * k:16 * k + 16, :] = yt.reshape(t_, SSM_GROUP, ct)

    for s in range(SSM_SLABS):
        for t in range(t_):
            yt = yt_sc[s, t]
            hi = yt.astype(BF16)
            lo = (yt - hi.astype(F32)).astype(BF16)
            y = (lax.dot_general(hi, eye, _TN, preferred_element_type=F32)
                 + lax.dot_general(lo, eye, _TN, preferred_element_type=F32))
            y_ref[s, pl.ds(t, ct, stride=t_), :] = y


def _s5(u4, tables, ct):
    mt, wre, wim, vt, are, aim = tables
    _, seq, _ = u4.shape
    full = lambda a: pl.BlockSpec(a.shape, lambda i: (0,) * a.ndim)
    rows = pl.BlockSpec((SSM_SLABS, SSM_CHUNK * ct, 128), lambda i: (0, i, 0))
    nstate = SSM_GROUPS * SSM_STATE
    return pl.pallas_call(
        functools.partial(_s5_kernel, ct=ct),
        grid=(seq // (SSM_CHUNK * ct),),
        in_specs=[rows, full(mt), full(wre), full(wim), full(vt), full(are), full(aim)],
        out_specs=rows,
        out_shape=jax.ShapeDtypeStruct(u4.shape, F32),
        scratch_shapes=[pltpu.VMEM((SSM_SLABS, SSM_CHUNK, 128, ct), BF16),
                        pltpu.VMEM((SSM_SLABS, SSM_CHUNK, 128, ct), F32),
                        pltpu.VMEM((ct, nstate), F32),
                        pltpu.VMEM((ct, nstate), F32),
                        pltpu.VMEM((8, nstate), F32)],
        compiler_params=_params(("arbitrary",)),
        name="s5_ssm",
    )(u4, mt, wre, wim, vt, are, aim)


def _even_out_kernel(x_ref, ya_ref, ys_ref, szb_ref, wglu_ref, wout_ref, g_ref, h_ref, hn_ref):
    gl = _gelu_tanh(jnp.concatenate([ys_ref[s] for s in range(SSM_SLABS)], axis=1))
    yb = gl * _sigmoid(_dot(gl.astype(BF16), wglu_ref[...])) * szb_ref[...].astype(F32)
    acc = _dot(ya_ref[...], wout_ref[0:D_SHORTCONV, :])
    acc = acc + _dot(yb.astype(BF16), wout_ref[D_SHORTCONV:D_SHORTCONV + D_SSM, :])
    h = x_ref[...] + acc
    h_ref[...] = h
    hn_ref[...] = _rms(h, g_ref[...]).astype(BF16)


def _even_out(x, ya, ys, szb, w_glu, w_out, g, tm):
    seq = x.shape[0]
    row = lambda w: pl.BlockSpec((tm, w), lambda i: (i, 0))
    const = lambda a: pl.BlockSpec(a.shape, lambda i: (0, 0))
    return pl.pallas_call(
        _even_out_kernel,
        grid=(seq // tm,),
        in_specs=[row(D_MODEL), row(D_SHORTCONV),
                  pl.BlockSpec((SSM_SLABS, tm, 128), lambda i: (0, i, 0)), row(D_SSM),
                  const(w_glu), const(w_out), const(g)],
        out_specs=[row(D_MODEL), row(D_MODEL)],
        out_shape=[jax.ShapeDtypeStruct((seq, D_MODEL), F32),
                   jax.ShapeDtypeStruct((seq, D_MODEL), BF16)],
        compiler_params=_params(("arbitrary",)),
        name="even_out",
    )(x, ya, ys, szb, w_glu, w_out, g)


ODD_STEPS = 12
_SILU_STEPS = (2, 3, 10, 11)
_Q_STEPS = (4, 5)


def _odd_in_kernel(hn_ref, wa_ref, wb_ref, o_ref):
    s = pl.program_id(1)
    hn = hn_ref[...]
    is_silu = (s == 2) | (s == 3) | (s == 10) | (s == 11)

    @pl.when(s < 2)
    def _():
        o_ref[...] = (_dot(hn, wa_ref[...]) * _sigmoid(_dot(hn, wb_ref[...]))).astype(BF16)

    @pl.when(is_silu)
    def _():
        o_ref[...] = _silu(_dot(hn, wa_ref[...])).astype(BF16)

    @pl.when((s >= 2) & jnp.logical_not(is_silu))
    def _():
        scale = jnp.where((s == 4) | (s == 5), HEAD_DIM ** -0.5, 1.0).astype(F32)
        o_ref[...] = (_dot(hn, wa_ref[...]) * scale).astype(BF16)


def _odd_in(hn, w_in, tm):
    seq = hn.shape[0]
    return pl.pallas_call(
        _odd_in_kernel,
        grid=(seq // tm, ODD_STEPS),
        in_specs=[pl.BlockSpec((tm, D_MODEL), lambda i, s: (i, 0)),
                  pl.BlockSpec((D_MODEL, COL), lambda i, s: (0, jnp.where(s < 2, s, s + 2))),
                  pl.BlockSpec((D_MODEL, COL), lambda i, s: (0, jnp.where(s < 2, s + 2, 3)))],
        out_specs=pl.BlockSpec((tm, COL), lambda i, s: (i, s)),
        out_shape=jax.ShapeDtypeStruct((seq, ODD_STEPS * COL), BF16),
        compiler_params=_params(("arbitrary", "arbitrary")),
        name="odd_in",
    )(hn, w_in, w_in)


CONV_ROWS = 64


def _conformer_kernel(cur_ref, halo_ref, szc_ref, cw_ref, cb_ref, lg_ref, lb_ref, wpw_ref,
                      o_ref, buf_sc, conv_sc, *, tm):
    i = pl.program_id(0)
    halo = jnp.where(i == 0, 0.0, halo_ref[...].astype(F32))
    cur = cur_ref[...].astype(F32)
    nslab = D_CONFORMER // 128
    for c in range(nslab):
        buf_sc[c, 0:CONV_HALO, :] = halo[:, c * 128:(c + 1) * 128]
        buf_sc[c, CONV_HALO:CONV_HALO + tm, :] = cur[:, c * 128:(c + 1) * 128]
    shift = CONV_HALO - (CONFORMER_WIDTH - 1)

    def rows(rc, _):
        base = pl.multiple_of(rc * CONV_ROWS, CONV_ROWS)
        for c in range(nslab):
            lanes = slice(c * 128, (c + 1) * 128)
            acc = jnp.zeros((CONV_ROWS, 128), F32) + cb_ref[:, lanes]
            for k in range(CONFORMER_WIDTH):
                acc = acc + cw_ref[k:k + 1, lanes] * buf_sc[c, pl.ds(base + (shift + k), CONV_ROWS), :]
            conv_sc[pl.ds(base, CONV_ROWS), lanes] = acc
        return 0

    lax.fori_loop(0, tm // CONV_ROWS, rows, 0)
    y = conv_sc[...]
    mu = jnp.mean(y, axis=-1, keepdims=True)
    yc = y - mu
    var = jnp.mean(yc * yc, axis=-1, keepdims=True)
    ln = yc * lax.rsqrt(var + NORM_EPS) * lg_ref[...] + lb_ref[...]
    o_ref[...] = (_dot(_silu(ln).astype(BF16), wpw_ref[...]) * szc_ref[...].astype(F32)).astype(BF16)


def _conformer(proj, conv_w, conv_b, ln_g, ln_b, w_pw, tm):
    seq = proj.shape[0]
    per = tm // CONV_HALO
    const = lambda a: pl.BlockSpec(a.shape, lambda i: (0, 0))
    return pl.pallas_call(
        functools.partial(_conformer_kernel, tm=tm),
        grid=(seq // tm,),
        in_specs=[pl.BlockSpec((tm, D_CONFORMER), lambda i: (i, 0)),
                  pl.BlockSpec((CONV_HALO, D_CONFORMER), lambda i: (jnp.maximum(i * per - 1, 0), 0)),
                  pl.BlockSpec((tm, D_CONFORMER), lambda i: (i, 1)),
                  const(conv_w), const(conv_b), const(ln_g), const(ln_b), const(w_pw)],
        out_specs=pl.BlockSpec((tm, D_CONFORMER), lambda i: (i, 0)),
        out_shape=jax.ShapeDtypeStruct((seq, D_CONFORMER), BF16),
        scratch_shapes=[pltpu.VMEM((D_CONFORMER // 128, tm + CONV_HALO, 128), F32),
                        pltpu.VMEM((tm, D_CONFORMER), F32)],
        compiler_params=_params(("arbitrary",)),
        name="conformer",
    )(proj, proj, proj, conv_w, conv_b, ln_g, ln_b, w_pw)


def _t5_bucket(dist):
    max_exact = REL_BUCKETS // 2
    d_f = jnp.maximum(dist, 1).astype(F32)
    large = max_exact + (jnp.log(d_f / max_exact) / math.log(REL_MAX_DISTANCE / max_exact)
                         * (REL_BUCKETS - max_exact)).astype(jnp.int32)
    large = jnp.minimum(large, REL_BUCKETS - 1)
    return jnp.where(dist < max_exact, dist, large)


def _attn_bias(rel_bias):
    q = ATTN_BLOCK
    nh = rel_bias.shape[1]
    kj = np.arange(2 * q)[None, None, :]
    neg = lambda n: jnp.full((nh, n), NEG_INF, F32)
    tabs = []
    for window, dilation in DILATED_PATTERNS:
        steps = window // dilation
        assert steps == q
        rel = jnp.asarray(np.arange(steps + 1) * dilation, dtype=jnp.int32)
        by_rel = rel_bias.astype(F32)[_t5_bucket(rel)].T
        g = jnp.concatenate([by_rel[:, ::-1], neg(2 * q - 1)], axis=1)
        flat = jnp.tile(g, (1, q))[:, :q * (3 * q - 1)]
        tab = flat.reshape(nh, q, 3 * q - 1)[:, :, :2 * q]
        tabs.append(tab)
        tabs.append(jnp.where(kj >= q, tab, NEG_INF))
    return jnp.stack(tabs, axis=1)


def _attn_kernel(q_ref, kc_ref, kp_ref, vc_ref, vp_ref, szd_ref, bias_ref, o_ref,
                 qf, kf, vf, n0, n1, n2, d0, d1, d2, m0, m1, m2):
    i = pl.program_id(1)
    t = ATTN_TILE
    blk = ATTN_BLOCK
    first_tile = i == 0
    qf[...] = q_ref[...].astype(F32)
    kf[0:t, :] = jnp.where(first_tile, 0.0, kp_ref[...].astype(F32))
    kf[t:2 * t, :] = kc_ref[...].astype(F32)
    vf[0:t, :] = jnp.where(first_tile, 0.0, vp_ref[...].astype(F32))
    vf[t:2 * t, :] = vc_ref[...].astype(F32)
    ones = jnp.ones((2 * blk, HEAD_DIM), BF16)

    first = first_tile.astype(jnp.int32)

    def block(pidx, qstart, kstart, stride, at_start, num_sc, den_sc, max_sc):
        st = None if stride == 1 else stride
        qrows = pl.ds(qstart, blk, stride=st)
        krows = pl.ds(kstart, 2 * blk, stride=st)
        qb = qf[qrows, :].astype(BF16)
        kb = kf[krows, :].astype(BF16)
        vb = vf[krows, :].astype(BF16)
        s = lax.dot_general(qb, kb, (((1,), (1,)), ((), ())), preferred_element_type=F32)
        s = s + bias_ref[0, 2 * pidx + first if at_start else 2 * pidx]
        mx = jnp.max(s, axis=-1, keepdims=True)
        p = jnp.exp(s - mx).astype(BF16)
        pv = _dot(p, jnp.concatenate([vb, ones], axis=1))
        num_sc[qrows, :] = pv[:, :HEAD_DIM]
        den_sc[qrows, :] = pv[:, HEAD_DIM:]
        max_sc[qrows, :] = jnp.broadcast_to(mx, (blk, HEAD_DIM))

    scratch = ((n0, d0, m0), (n1, d1, m1), (n2, d2, m2))
    for pidx, (_, dil) in enumerate(DILATED_PATTERNS):
        span = blk * dil
        for sb in range(t // span):
            for r in range(dil):
                block(pidx, sb * span + r, t + (sb - 1) * span + r, dil, sb == 0, *scratch[pidx])

    ma, mb, mc = m0[...], m1[...], m2[...]
    mall = jnp.maximum(jnp.maximum(ma, mb), mc)
    wa, wb, wc = jnp.exp(ma - mall), jnp.exp(mb - mall), jnp.exp(mc - mall)
    numer = wa * n0[...] + wb * n1[...] + wc * n2[...]
    denom = wa * d0[...] + wb * d1[...] + wc * d2[...]
    o_ref[...] = (numer / denom * szd_ref[...].astype(F32)).astype(BF16)


def _attention(proj, bias):
    seq = proj.shape[0]
    t = ATTN_TILE
    q0 =(2 * D_CONFORMER) // HEAD_DIM
    k0 = q0 + N_HEADS
    v0 = k0 + N_HEADS
    z0 = v0 + N_HEADS
    cur = lambda c0: pl.BlockSpec((t, HEAD_DIM), lambda h, i: (i, c0 + h))
    prev = lambda c0: pl.BlockSpec((t, HEAD_DIM), lambda h, i: (jnp.maximum(i - 1, 0), c0 + h))
    sc = lambda rows: pltpu.VMEM((rows, HEAD_DIM), F32)
    return pl.pallas_call(
        _attn_kernel,
        grid=(N_HEADS, seq // t),
        in_specs=[cur(q0), cur(k0), prev(k0), cur(v0), prev(v0), cur(z0),
                  pl.BlockSpec((1,) + bias.shape[1:], lambda h, i: (h, 0, 0, 0))],
        out_specs=pl.BlockSpec((t, HEAD_DIM), lambda h, i: (i, h)),
        out_shape=jax.ShapeDtypeStruct((seq, D_ATTN), BF16),
        scratch_shapes=[sc(t), sc(2 * t), sc(2 * t)] + [sc(t)] * 9,
        compiler_params=_params(("arbitrary", "arbitrary")),
        name="dilated_attn",
    )(proj, proj, proj, proj, proj, proj, bias)


def _odd_out_kernel(h_ref, yc_ref, yd_ref, wout_ref, g_ref, o_ref):
    acc = _dot(yc_ref[...], wout_ref[0:D_CONFORMER, :])
    acc = acc + _dot(yd_ref[...], wout_ref[D_CONFORMER:D_CONFORMER + D_ATTN, :])
    o_ref[...] = _rms(h_ref[...] + acc, g_ref[...])


def _odd_out(h, yc, yd, w_out, g, tm):
    seq = h.shape[0]
    row = lambda w: pl.BlockSpec((tm, w), lambda i: (i, 0))
    const = lambda a: pl.BlockSpec(a.shape, lambda i: (0, 0))
    return pl.pallas_call(
        _odd_out_kernel,
        grid=(seq // tm,),
        in_specs=[row(D_MODEL), row(D_CONFORMER), row(D_ATTN), const(w_out), const(g)],
        out_specs=row(D_MODEL),
        out_shape=jax.ShapeDtypeStruct((seq, D_MODEL), F32),
        compiler_params=_params(("arbitrary",)),
        name="odd_out",
    )(h, yc, yd, w_out, g)


def _row_tile(seq, want):
    return min(seq, want)


def kernel(x, norm_g, final_norm_g, ev_w_in, ev_conv_w, s5_lam_re, s5_lam_im, s5_log_dt, s5_b_re, s5_b_im, s5_c_re, s5_c_im, s5_d, s5_w_glu, ev_w_out, od_w_in, cf_conv_w, cf_conv_b, cf_ln_g, cf_ln_b, cf_w_pw, od_w_out, rel_bias):
    bsz, seq, _ = x.shape
    assert bsz == 1 and seq % ATTN_TILE == 0
    x2 = x.reshape(seq, D_MODEL)
    g0 = norm_g[0].astype(F32).reshape(1, D_MODEL)
    g1 = norm_g[1].astype(F32).reshape(1, D_MODEL)
    gf = final_norm_g.astype(F32).reshape(1, D_MODEL)

    ya, u, szb = _even_in(x2, g0, ev_w_in[0].astype(BF16), ev_conv_w[0].astype(F32), _row_tile(seq, 512))
    tables = _s5_tables(s5_lam_re[0], s5_lam_im[0], s5_log_dt[0], s5_b_re[0], s5_b_im[0],
                        s5_c_re[0], s5_c_im[0], s5_d[0])
    ys = _s5(u, tables, min(seq // SSM_CHUNK, 128))
    h1, hn1 = _even_out(x2, ya, ys, szb, s5_w_glu[0].astype(BF16), ev_w_out[0].astype(BF16), g1,
                        _row_tile(seq, 256))

    proj = _odd_in(hn1, od_w_in[0].astype(BF16), _row_tile(seq, 1024))
    conv_w = jnp.concatenate([cf_conv_w[0].astype(F32), jnp.zeros((1, D_CONFORMER), F32)], axis=0)
    yc = _conformer(proj, conv_w, cf_conv_b[0].astype(F32).reshape(1, -1),
                    cf_ln_g[0].astype(F32).reshape(1, -1), cf_ln_b[0].astype(F32).reshape(1, -1),
                    cf_w_pw[0].astype(BF16), _row_tile(seq, 512))
    yd = _attention(proj, _attn_bias(rel_bias))
    out = _odd_out(h1, yc, yd, od_w_out[0].astype(BF16), gf, _row_tile(seq, 256))
    return out.reshape(bsz, seq, D_MODEL)
```

```python
import functools
import math

import numpy as np
import jax
import jax.numpy as jnp
from jax import lax
from jax.experimental import pallas as pl
from jax.experimental.pallas import tpu as pltpu

F32 = jnp.float32
BF16 = jnp.bfloat16

D_MODEL = 2048
D_SHORTCONV = 1536
SHORTCONV_WIDTH = 3
D_SSM = 512
SSM_GROUP = 16
SSM_GROUPS = 32
SSM_STATE = 64
D_CONFORMER = 1024
CONFORMER_WIDTH = 31
D_ATTN = 1024
HEAD_DIM = 128
N_HEADS = 8
DILATED_PATTERNS = ((128, 1), (512, 4), (2048, 16))
ATTN_BLOCK = 128
REL_BUCKETS = 32
REL_MAX_DISTANCE = 2048
NORM_EPS = 1e-6
NEG_INF = -1e30

COL = 512
SSM_CHUNK = 16
SSM_FLAT = SSM_CHUNK * SSM_GROUP
SSM_SLABS = D_SSM // 128
ATTN_TILE = 2048
CONV_HALO = 32
VMEM_LIMIT = 50 * 1024 * 1024


def _params(sem, vmem=VMEM_LIMIT):
    return pltpu.CompilerParams(dimension_semantics=sem, vmem_limit_bytes=vmem)


def _dot(a, b):
    return jnp.dot(a, b, preferred_element_type=F32)


def _sigmoid(z):
    return 1.0 / (1.0 + jnp.exp(-z))


def _silu(z):
    return z * _sigmoid(z)


def _gelu_tanh(x):
    return 0.5 * x * (1.0 + jnp.tanh(math.sqrt(2.0 / math.pi) * (x + 0.044715 * (x * x * x))))


def _rms(xf, g):
    ms = jnp.mean(xf * xf, axis=-1, keepdims=True)
    return xf * lax.rsqrt(ms + NORM_EPS) * g


def _even_in_kernel(x_ref, g_ref, wa_ref, wb_ref, wc_ref, wd_ref, cw_ref,
                    ya_ref, u_ref, szb_ref, hn_sc, pbuf_sc, carry_sc, *, tm, col):
    i = pl.program_id(0)
    j = pl.program_id(1)
    nconv = D_SHORTCONV // col

    @pl.when(j == 0)
    def _():
        hn_sc[...] = _rms(x_ref[...], g_ref[...]).astype(BF16)

    @pl.when(j < nconv)
    def _():
        hn = hn_sc[...]
        p = _dot(hn, wb_ref[...]) * _dot(hn, wa_ref[...])

        @pl.when(i == 0)
        def _():
            carry_sc[j] = jnp.zeros((8, col), F32)

        pbuf_sc[0:8, :] = carry_sc[j]
        pbuf_sc[8:8 + tm, :] = p
        carry_sc[j] = pbuf_sc[tm:tm + 8, :]
        cw = cw_ref[...]
        conv = (cw[2:3, :] * p + cw[1:2, :] * pbuf_sc[pl.ds(7, tm), :]
                + cw[0:1, :] * pbuf_sc[pl.ds(6, tm), :])
        gb = _dot(hn, wc_ref[...])
        za = _dot(hn, wd_ref[...])
        ya_ref[...] = (gb * conv * _silu(za)).astype(BF16)

    @pl.when(j >= nconv)
    def _():
        hn = hn_sc[...]
        u = _dot(hn, wa_ref[...])
        for s in range(col // 128):
            u_ref[s] = u[:, s * 128:(s + 1) * 128]
        szb_ref[...] = _silu(_dot(hn, wb_ref[...])).astype(BF16)


def _even_in(x, g, w_in, conv_w, tm, col):
    seq = x.shape[0]
    nconv = D_SHORTCONV // col
    nssm = D_SSM // col
    u_blk = 4 * nconv

    def wspec(k, alt0):
        return pl.BlockSpec((D_MODEL, col),
                            lambda i, j: (0, jnp.where(j < nconv, j + nconv * k, alt0 + j - nconv)))

    hold = lambda k: pl.BlockSpec((D_MODEL, col), lambda i, j: (0, jnp.minimum(j, nconv - 1) + nconv * k))
    ssm_step = lambda j: jnp.maximum(j - nconv, 0)
    return pl.pallas_call(
        functools.partial(_even_in_kernel, tm=tm, col=col),
        grid=(seq // tm, nconv + nssm),
        in_specs=[
            pl.BlockSpec((tm, D_MODEL), lambda i, j: (i, 0)),
            pl.BlockSpec((1, D_MODEL), lambda i, j: (0, 0)),
            wspec(0, u_blk), wspec(1, u_blk + nssm), hold(2), hold(3),
            pl.BlockSpec((SHORTCONV_WIDTH, col), lambda i, j: (0, jnp.minimum(j, nconv - 1))),
        ],
        out_specs=[
            pl.BlockSpec((tm, col), lambda i, j: (i, jnp.minimum(j, nconv - 1))),
            pl.BlockSpec((col // 128, tm, 128), lambda i, j: (ssm_step(j), i, 0)),
            pl.BlockSpec((tm, col), lambda i, j: (i, ssm_step(j))),
        ],
        out_shape=[
            jax.ShapeDtypeStruct((seq, D_SHORTCONV), BF16),
            jax.ShapeDtypeStruct((SSM_SLABS, seq, 128), F32),
            jax.ShapeDtypeStruct((seq, D_SSM), BF16),
        ],
        scratch_shapes=[
            pltpu.VMEM((tm, D_MODEL), BF16),
            pltpu.VMEM((tm + 8, col), F32),
            pltpu.VMEM((nconv, 8, col), F32),
        ],
        compiler_params=_params(("arbitrary", "arbitrary")),
        name="even_in",
    )(x, g, w_in, w_in, w_in, w_in, conv_w)


def _s5_tables(lam_re, lam_im, log_dt, b_re, b_im, c_re, c_im, d_skip):
    hi = lax.Precision.HIGHEST
    t_, g_, n_, p_ = SSM_CHUNK, SSM_GROUPS, SSM_STATE, SSM_GROUP
    lam_re, lam_im = lam_re.astype(F32), lam_im.astype(F32)
    dt = jnp.exp(log_dt.astype(F32))[:, None]
    xr, xi = lam_re * dt, lam_im * dt
    k = jnp.arange(t_ + 1, dtype=F32)[:, None, None]
    mag = jnp.exp(xr * k)
    pr, pi = mag * jnp.cos(xi * k), mag * jnp.sin(xi * k)
    nr, ni = pr[1] - 1.0, pi[1]
    den = lam_re * lam_re + lam_im * lam_im
    fr = (nr * lam_re + ni * lam_im) / den
    fi = (ni * lam_re - nr * lam_im) / den
    b_re, b_im = b_re.astype(F32), b_im.astype(F32)
    bbr = fr[..., None] * b_re - fi[..., None] * b_im
    bbi = fr[..., None] * b_im + fi[..., None] * b_re
    c_re, c_im = c_re.astype(F32), c_im.astype(F32)

    cpr = c_re[None] * pr[:t_, :, None, :] - c_im[None] * pi[:t_, :, None, :]
    cpi = c_re[None] * pi[:t_, :, None, :] + c_im[None] * pr[:t_, :, None, :]
    kern = jnp.einsum('tgpn,gnq->tgpq', jnp.concatenate([cpr, -cpi], axis=-1),
                      jnp.concatenate([bbr, bbi], axis=1), precision=hi)
    skip = d_skip.astype(F32).reshape(g_, p_)[:, :, None] * jnp.eye(p_, dtype=F32)[None]
    kern = kern.at[0].add(skip)
    flat = t_ * p_
    last = jnp.transpose(kern[::-1], (1, 2, 0, 3)).reshape(g_, p_, flat)
    mt = jnp.stack([jnp.pad(last[:, :, (t_ - 1 - t) * p_:], ((0, 0), (0, 0), (0, (t_ - 1 - t) * p_)))
                    for t in range(t_)], axis=1).reshape(g_, flat, flat)

    prj, pij = pr[t_ - 1::-1][:t_], pi[t_ - 1::-1][:t_]
    wr = prj[..., None] * bbr[None] - pij[..., None] * bbi[None]
    wi = prj[..., None] * bbi[None] + pij[..., None] * bbr[None]
    half = (np.arange(2) == 0)[None, None, :, None, None]

    def pack_w(w):
        w = jnp.transpose(w.reshape(t_, g_ // 2, 2, n_, p_), (1, 0, 2, 4, 3))
        w = jnp.concatenate([jnp.where(half, w, 0.0), jnp.where(half, 0.0, w)], axis=-1)
        return w.reshape(g_ // 2, 2 * flat, 2 * n_)

    vr = c_re[None] * pr[1:, :, None, :] - c_im[None] * pi[1:, :, None, :]
    vi = c_re[None] * pi[1:, :, None, :] + c_im[None] * pr[1:, :, None, :]
    vr = jnp.transpose(vr, (1, 0, 2, 3)).reshape(g_ // 2, 2, flat, n_)
    vi = -jnp.transpose(vi, (1, 0, 2, 3)).reshape(g_ // 2, 2, flat, n_)
    zv = jnp.zeros((g_ // 2, flat, n_), F32)
    v_even = jnp.concatenate([vr[:, 0], zv, vi[:, 0], zv], axis=-1)
    v_odd = jnp.concatenate([zv, vr[:, 1], zv, vi[:, 1]], axis=-1)
    vt = jnp.stack([v_even, v_odd], axis=1).reshape(g_, flat, 4 * n_)
    a_re = pr[t_].reshape(1, g_ * n_)
    a_im = pi[t_].reshape(1, g_ * n_)
    return mt.astype(BF16), pack_w(wr).astype(BF16), pack_w(wi).astype(BF16), vt.astype(BF16), a_re, a_im


_NT = (((1,), (1,)), ((), ()))
_TN = (((0,), (0,)), ((), ()))


def _s5_kernel(u_ref, mt_ref, wre_ref, wim_ref, vt_ref, are_ref, aim_ref, y_ref,
               xt_sc, yt_sc, zre_sc, zim_sc, carry_sc, *, ct):
    i = pl.program_id(0)
    t_ = SSM_CHUNK

    @pl.when(i == 0)
    def _():
        carry_sc[...] = jnp.zeros_like(carry_sc)

    eye = jnp.where(lax.broadcasted_iota(jnp.int32, (128, 128), 0)
                    == lax.broadcasted_iota(jnp.int32, (128, 128), 1), 1.0, 0.0).astype(BF16)

    for s in range(SSM_SLABS):
        for j in range(t_):
            uj = u_ref[s, pl.ds(j, ct, stride=t_), :].astype(BF16)
            xt_sc[s, j] = lax.dot_general(eye, uj, _NT, preferred_element_type=F32).astype(BF16)

    pairs_per_slab = 128 // (2 * SSM_GROUP)
    for gp in range(SSM_GROUPS // 2):
        s, k = gp // pairs_per_slab, gp % pairs_per_slab
        xp = xt_sc[s, :, 32 * k:32 * k + 32, :].reshape(2 * SSM_FLAT, ct)
        zre_sc[:, gp * 128:(gp + 1) * 128] = lax.dot_general(xp, wre_ref[gp], _TN, preferred_element_type=F32)
        zim_sc[:, gp * 128:(gp + 1) * 128] = lax.dot_general(xp, wim_ref[gp], _TN, preferred_element_type=F32)

    are = are_ref[...]
    aim = aim_ref[...]

    def step(r, carry):
        sre, sim = carry
        zr = zre_sc[pl.ds(r, 1), :]
        zi = zim_sc[pl.ds(r, 1), :]
        zre_sc[pl.ds(r, 1), :] = sre
        zim_sc[pl.ds(r, 1), :] = sim
        return are * sre - aim * sim + zr, are * sim + aim * sre + zi

    sre, sim = lax.fori_loop(0, ct, step, (carry_sc[0:1, :], carry_sc[1:2, :]), unroll=8)
    carry_sc[0:1, :] = sre
    carry_sc[1:2, :] = sim

    groups_per_slab = 128 // SSM_GROUP
    for g in range(SSM_GROUPS):
        gp, s, k = g // 2, g // groups_per_slab, g % groups_per_slab
        xg = xt_sc[s, :, 16 * k:16 * k + 16, :].reshape(SSM_FLAT, ct)
        sp = jnp.concatenate([zre_sc[:, gp * 128:(gp + 1) * 128],
                              zim_sc[:, gp * 128:(gp + 1) * 128]], axis=1).astype(BF16)
        yt = _dot(mt_ref[g], xg) + lax.dot_general(vt_ref[g], sp, _NT, preferred_element_type=F32)
        yt_sc[s, :, 16 * k:16 * k + 16, :] = yt.reshape(t_, SSM_GROUP, ct)

    for s in range(SSM_SLABS):
        for t in range(t_):
            yt = yt_sc[s, t]
            hi = yt.astype(BF16)
            lo = (yt - hi.astype(F32)).astype(BF16)
            y = (lax.dot_general(hi, eye, _TN, preferred_element_type=F32)
                 + lax.dot_general(lo, eye, _TN, preferred_element_type=F32))
            y_ref[s, pl.ds(t, ct, stride=t_), :] = y


def _s5(u4, tables, ct):
    mt, wre, wim, vt, are, aim = tables
    _, seq, _ = u4.shape
    full = lambda a: pl.BlockSpec(a.shape, lambda i: (0,) * a.ndim)
    rows = pl.BlockSpec((SSM_SLABS, SSM_CHUNK * ct, 128), lambda i: (0, i, 0))
    nstate = SSM_GROUPS * SSM_STATE
    return pl.pallas_call(
        functools.partial(_s5_kernel, ct=ct),
        grid=(seq // (SSM_CHUNK * ct),),
        in_specs=[rows, full(mt), full(wre), full(wim), full(vt), full(are), full(aim)],
        out_specs=rows,
        out_shape=jax.ShapeDtypeStruct(u4.shape, F32),
        scratch_shapes=[pltpu.VMEM((SSM_SLABS, SSM_CHUNK, 128, ct), BF16),
                        pltpu.VMEM((SSM_SLABS, SSM_CHUNK, 128, ct), F32),
                        pltpu.VMEM((ct, nstate), F32),
                        pltpu.VMEM((ct, nstate), F32),
                        pltpu.VMEM((8, nstate), F32)],
        compiler_params=_params(("arbitrary",)),
        name="s5_ssm",
    )(u4, mt, wre, wim, vt, are, aim)


def _even_out_kernel(x_ref, ya_ref, ys_ref, szb_ref, wglu_ref, wout_ref, g_ref, h_ref, hn_ref):
    gl = _gelu_tanh(jnp.concatenate([ys_ref[s] for s in range(SSM_SLABS)], axis=1))
    yb = gl * _sigmoid(_dot(gl.astype(BF16), wglu_ref[...])) * szb_ref[...].astype(F32)
    acc = _dot(ya_ref[...], wout_ref[0:D_SHORTCONV, :])
    acc = acc + _dot(yb.astype(BF16), wout_ref[D_SHORTCONV:D_SHORTCONV + D_SSM, :])
    h = x_ref[...] + acc
    h_ref[...] = h
    hn_ref[...] = _rms(h, g_ref[...]).astype(BF16)


def _even_out(x, ya, ys, szb, w_glu, w_out, g, tm):
    seq = x.shape[0]
    row = lambda w: pl.BlockSpec((tm, w), lambda i: (i, 0))
    const = lambda a: pl.BlockSpec(a.shape, lambda i: (0, 0))
    return pl.pallas_call(
        _even_out_kernel,
        grid=(seq // tm,),
        in_specs=[row(D_MODEL), row(D_SHORTCONV),
                  pl.BlockSpec((SSM_SLABS, tm, 128), lambda i: (0, i, 0)), row(D_SSM),
                  const(w_glu), const(w_out), const(g)],
        out_specs=[row(D_MODEL), row(D_MODEL)],
        out_shape=[jax.ShapeDtypeStruct((seq, D_MODEL), F32),
                   jax.ShapeDtypeStruct((seq, D_MODEL), BF16)],
        compiler_params=_params(("arbitrary",)),
        name="even_out",
    )(x, ya, ys, szb, w_glu, w_out, g)


ODD_STEPS = 12
_SILU_STEPS = (2, 3, 10, 11)
_Q_STEPS = (4, 5)


def _odd_in_kernel(hn_ref, wa_ref, wb_ref, o_ref):
    s = pl.program_id(1)
    hn = hn_ref[...]
    is_silu = (s == 2) | (s == 3) | (s == 10) | (s == 11)

    @pl.when(s < 2)
    def _():
        o_ref[...] = (_dot(hn, wa_ref[...]) * _sigmoid(_dot(hn, wb_ref[...]))).astype(BF16)

    @pl.when(is_silu)
    def _():
        o_ref[...] = _silu(_dot(hn, wa_ref[...])).astype(BF16)

    @pl.when((s >= 2) & jnp.logical_not(is_silu))
    def _():
        scale = jnp.where((s == 4) | (s == 5), HEAD_DIM ** -0.5, 1.0).astype(F32)
        o_ref[...] = (_dot(hn, wa_ref[...]) * scale).astype(BF16)


def _odd_in(hn, w_in, tm):
    seq = hn.shape[0]
    return pl.pallas_call(
        _odd_in_kernel,
        grid=(seq // tm, ODD_STEPS),
        in_specs=[pl.BlockSpec((tm, D_MODEL), lambda i, s: (i, 0)),
                  pl.BlockSpec((D_MODEL, COL), lambda i, s: (0, jnp.where(s < 2, s, s + 2))),
                  pl.BlockSpec((D_MODEL, COL), lambda i, s: (0, jnp.where(s < 2, s + 2, 3)))],
        out_specs=pl.BlockSpec((tm, COL), lambda i, s: (i, s)),
        out_shape=jax.ShapeDtypeStruct((seq, ODD_STEPS * COL), BF16),
        compiler_params=_params(("arbitrary", "arbitrary")),
        name="odd_in",
    )(hn, w_in, w_in)


CONV_ROWS = 64


def _conformer_kernel(cur_ref, halo_ref, szc_ref, cw_ref, cb_ref, lg_ref, lb_ref, wpw_ref,
                      o_ref, buf_sc, conv_sc, *, tm):
    i = pl.program_id(0)
    halo = jnp.where(i == 0, 0.0, halo_ref[...].astype(F32))
    cur = cur_ref[...].astype(F32)
    nslab = D_CONFORMER // 128
    for c in range(nslab):
        buf_sc[c, 0:CONV_HALO, :] = halo[:, c * 128:(c + 1) * 128]
        buf_sc[c, CONV_HALO:CONV_HALO + tm, :] = cur[:, c * 128:(c + 1) * 128]
    shift = CONV_HALO - (CONFORMER_WIDTH - 1)

    def rows(rc, _):
        base = pl.multiple_of(rc * CONV_ROWS, CONV_ROWS)
        for c in range(nslab):
            lanes = slice(c * 128, (c + 1) * 128)
            acc = jnp.zeros((CONV_ROWS, 128), F32) + cb_ref[:, lanes]
            for k in range(CONFORMER_WIDTH):
                acc = acc + cw_ref[k:k + 1, lanes] * buf_sc[c, pl.ds(base + (shift + k), CONV_ROWS), :]
            conv_sc[pl.ds(base, CONV_ROWS), lanes] = acc
        return 0

    lax.fori_loop(0, tm // CONV_ROWS, rows, 0)
    y = conv_sc[...]
    mu = jnp.mean(y, axis=-1, keepdims=True)
    yc = y - mu
    var = jnp.mean(yc * yc, axis=-1, keepdims=True)
    ln = yc * lax.rsqrt(var + NORM_EPS) * lg_ref[...] + lb_ref[...]
    o_ref[...] = (_dot(_silu(ln).astype(BF16), wpw_ref[...]) * szc_ref[...].astype(F32)).astype(BF16)


def _conformer(proj, conv_w, conv_b, ln_g, ln_b, w_pw, tm):
    seq = proj.shape[0]
    per = tm // CONV_HALO
    const = lambda a: pl.BlockSpec(a.shape, lambda i: (0, 0))
    return pl.pallas_call(
        functools.partial(_conformer_kernel, tm=tm),
        grid=(seq // tm,),
        in_specs=[pl.BlockSpec((tm, D_CONFORMER), lambda i: (i, 0)),
                  pl.BlockSpec((CONV_HALO, D_CONFORMER), lambda i: (jnp.maximum(i * per - 1, 0), 0)),
                  pl.BlockSpec((tm, D_CONFORMER), lambda i: (i, 1)),
                  const(conv_w), const(conv_b), const(ln_g), const(ln_b), const(w_pw)],
        out_specs=pl.BlockSpec((tm, D_CONFORMER), lambda i: (i, 0)),
        out_shape=jax.ShapeDtypeStruct((seq, D_CONFORMER), BF16),
        scratch_shapes=[pltpu.VMEM((D_CONFORMER // 128, tm + CONV_HALO, 128), F32),
                        pltpu.VMEM((tm, D_CONFORMER), F32)],
        compiler_params=_params(("arbitrary",)),
        name="conformer",
    )(proj, proj, proj, conv_w, conv_b, ln_g, ln_b, w_pw)


def _t5_bucket(dist):
    max_exact = REL_BUCKETS // 2
    d_f = jnp.maximum(dist, 1).astype(F32)
    large = max_exact + (jnp.log(d_f / max_exact) / math.log(REL_MAX_DISTANCE / max_exact)
                         * (REL_BUCKETS - max_exact)).astype(jnp.int32)
    large = jnp.minimum(large, REL_BUCKETS - 1)
    return jnp.where(dist < max_exact, dist, large)


def _attn_bias(rel_bias):
    q = ATTN_BLOCK
    nh = rel_bias.shape[1]
    kj = np.arange(2 * q)[None, None, :]
    neg = lambda n: jnp.full((nh, n), NEG_INF, F32)
    tabs = []
    for window, dilation in DILATED_PATTERNS:
        steps = window // dilation
        assert steps == q
        rel = jnp.asarray(np.arange(steps + 1) * dilation, dtype=jnp.int32)
        by_rel = rel_bias.astype(F32)[_t5_bucket(rel)].T
        g = jnp.concatenate([by_rel[:, ::-1], neg(2 * q - 1)], axis=1)
        flat = jnp.tile(g, (1, q))[:, :q * (3 * q - 1)]
        tab = flat.reshape(nh, q, 3 * q - 1)[:, :, :2 * q]
        tabs.append(tab)
        tabs.append(jnp.where(kj >= q, tab, NEG_INF))
    return jnp.stack(tabs, axis=1)


def _attn_kernel(q_ref, kc_ref, kp_ref, vc_ref, vp_ref, szd_ref, bias_ref, o_ref,
                 qf, kf, vf, n0, n1, n2, d0, d1, d2, m0, m1, m2):
    i = pl.program_id(1)
    t = ATTN_TILE
    blk = ATTN_BLOCK
    first_tile = i == 0
    qf[...] = q_ref[...].astype(F32)
    kf[0:t, :] = jnp.where(first_tile, 0.0, kp_ref[...].astype(F32))
    kf[t:2 * t, :] = kc_ref[...].astype(F32)
    vf[0:t, :] = jnp.where(first_tile, 0.0, vp_ref[...].astype(F32))
    vf[t:2 * t, :] = vc_ref[...].astype(F32)
    ones = jnp.ones((2 * blk, HEAD_DIM), BF16)

    first = first_tile.astype(jnp.int32)

    def block(pidx, qstart, kstart, stride, at_start, num_sc, den_sc, max_sc):
        st = None if stride == 1 else stride
        qrows = pl.ds(qstart, blk, stride=st)
        krows = pl.ds(kstart, 2 * blk, stride=st)
        qb = qf[qrows, :].astype(BF16)
        kb = kf[krows, :].astype(BF16)
        vb = vf[krows, :].astype(BF16)
        s = lax.dot_general(qb, kb, (((1,), (1,)), ((), ())), preferred_element_type=F32)
        s = s + bias_ref[0, 2 * pidx + first if at_start else 2 * pidx]
        mx = jnp.max(s, axis=-1, keepdims=True)
        p = jnp.exp(s - mx).astype(BF16)
        pv = _dot(p, jnp.concatenate([vb, ones], axis=1))
        num_sc[qrows, :] = pv[:, :HEAD_DIM]
        den_sc[qrows, :] = pv[:, HEAD_DIM:]
        max_sc[qrows, :] = jnp.broadcast_to(mx, (blk, HEAD_DIM))

    scratch = ((n0, d0, m0), (n1, d1, m1), (n2, d2, m2))
    for pidx, (_, dil) in enumerate(DILATED_PATTERNS):
        span = blk * dil
        for sb in range(t // span):
            for r in range(dil):
                block(pidx, sb * span + r, t + (sb - 1) * span + r, dil, sb == 0, *scratch[pidx])

    ma, mb, mc = m0[...], m1[...], m2[...]
    mall = jnp.maximum(jnp.maximum(ma, mb), mc)
    wa, wb, wc = jnp.exp(ma - mall), jnp.exp(mb - mall), jnp.exp(mc - mall)
    numer = wa * n0[...] + wb * n1[...] + wc * n2[...]
    denom = wa * d0[...] + wb * d1[...] + wc * d2[...]
    o_ref[...] = (numer / denom * szd_ref[...].astype(F32)).astype(BF16)


def _attention(proj, bias):
    seq = proj.shape[0]
    t = ATTN_TILE
    q0 =(2 * D_CONFORMER) // HEAD_DIM
    k0 = q0 + N_HEADS
    v0 = k0 + N_HEADS
    z0 = v0 + N_HEADS
    cur = lambda c0: pl.BlockSpec((t, HEAD_DIM), lambda h, i: (i, c0 + h))
    prev = lambda c0: pl.BlockSpec((t, HEAD_DIM), lambda h, i: (jnp.maximum(i - 1, 0), c0 + h))
    sc = lambda rows: pltpu.VMEM((rows, HEAD_DIM), F32)
    return pl.pallas_call(
        _attn_kernel,
        grid=(N_HEADS, seq // t),
        in_specs=[cur(q0), cur(k0), prev(k0), cur(v0), prev(v0), cur(z0),
                  pl.BlockSpec((1,) + bias.shape[1:], lambda h, i: (h, 0, 0, 0))],
        out_specs=pl.BlockSpec((t, HEAD_DIM), lambda h, i: (i, h)),
        out_shape=jax.ShapeDtypeStruct((seq, D_ATTN), BF16),
        scratch_shapes=[sc(t), sc(2 * t), sc(2 * t)] + [sc(t)] * 9,
        compiler_params=_params(("arbitrary", "arbitrary")),
        name="dilated_attn",
    )(proj, proj, proj, proj, proj, proj, bias)


def _odd_out_kernel(h_ref, yc_ref, yd_ref, wout_ref, g_ref, o_ref):
    acc = _dot(yc_ref[...], wout_ref[0:D_CONFORMER, :])
    acc = acc + _dot(yd_ref[...], wout_ref[D_CONFORMER:D_CONFORMER + D_ATTN, :])
    o_ref[...] = _rms(h_ref[...] + acc, g_ref[...])


def _odd_out(h, yc, yd, w_out, g, tm):
    seq = h.shape[0]
    row = lambda w: pl.BlockSpec((tm, w), lambda i: (i, 0))
    const = lambda a: pl.BlockSpec(a.shape, lambda i: (0, 0))
    return pl.pallas_call(
        _odd_out_kernel,
        grid=(seq // tm,),
        in_specs=[row(D_MODEL), row(D_CONFORMER), row(D_ATTN), const(w_out), const(g)],
        out_specs=row(D_MODEL),
        out_shape=jax.ShapeDtypeStruct((seq, D_MODEL), F32),
        compiler_params=_params(("arbitrary",)),
        name="odd_out",
    )(h, yc, yd, w_out, g)


def _row_tile(seq, want):
    return min(seq, want)


def kernel(x, norm_g, final_norm_g, ev_w_in, ev_conv_w, s5_lam_re, s5_lam_im, s5_log_dt, s5_b_re, s5_b_im, s5_c_re, s5_c_im, s5_d, s5_w_glu, ev_w_out, od_w_in, cf_conv_w, cf_conv_b, cf_ln_g, cf_ln_b, cf_w_pw, od_w_out, rel_bias):
    bsz, seq, _ = x.shape
    assert bsz == 1 and seq % ATTN_TILE == 0
    x2 = x.reshape(seq, D_MODEL)
    g0 = norm_g[0].astype(F32).reshape(1, D_MODEL)
    g1 = norm_g[1].astype(F32).reshape(1, D_MODEL)
    gf = final_norm_g.astype(F32).reshape(1, D_MODEL)

    ya, u, szb = _even_in(x2, g0, ev_w_in[0].astype(BF16), ev_conv_w[0].astype(F32), _row_tile(seq, 1024), 256)
    tables = _s5_tables(s5_lam_re[0], s5_lam_im[0], s5_log_dt[0], s5_b_re[0], s5_b_im[0],
                        s5_c_re[0], s5_c_im[0], s5_d[0])
    ys = _s5(u, tables, min(seq // SSM_CHUNK, 128))
    h1, hn1 = _even_out(x2, ya, ys, szb, s5_w_glu[0].astype(BF16), ev_w_out[0].astype(BF16), g1,
                        _row_tile(seq, 256))

    proj = _odd_in(hn1, od_w_in[0].astype(BF16), _row_tile(seq, 2048))
    conv_w = jnp.concatenate([cf_conv_w[0].astype(F32), jnp.zeros((1, D_CONFORMER), F32)], axis=0)
    yc = _conformer(proj, conv_w, cf_conv_b[0].astype(F32).reshape(1, -1),
                    cf_ln_g[0].astype(F32).reshape(1, -1), cf_ln_b[0].astype(F32).reshape(1, -1),
                    cf_w_pw[0].astype(BF16), _row_tile(seq, 512))
    yd = _attention(proj, _attn_bias(rel_bias))
    out = _odd_out(h1, yc, yd, od_w_out[0].astype(BF16), gf, _row_tile(seq, 256))
    return out.reshape(bsz, seq, D_MODEL)
```

```python
import functools
import math

import numpy as np
import jax
import jax.numpy as jnp
from jax import lax
from jax.experimental import pallas as pl
from jax.experimental.pallas import tpu as pltpu

F32 = jnp.float32
BF16 = jnp.bfloat16

D_MODEL = 2048
D_SHORTCONV = 1536
SHORTCONV_WIDTH = 3
D_SSM = 512
SSM_GROUP = 16
SSM_GROUPS = 32
SSM_STATE = 64
D_CONFORMER = 1024
CONFORMER_WIDTH = 31
D_ATTN = 1024
HEAD_DIM = 128
N_HEADS = 8
DILATED_PATTERNS = ((128, 1), (512, 4), (2048, 16))
ATTN_BLOCK = 128
REL_BUCKETS = 32
REL_MAX_DISTANCE = 2048
NORM_EPS = 1e-6
NEG_INF = -1e30

COL = 512
SSM_CHUNK = 16
SSM_FLAT = SSM_CHUNK * SSM_GROUP
SSM_SLABS = D_SSM // 128
ATTN_TILE = 2048
CONV_HALO = 32
VMEM_LIMIT = 50 * 1024 * 1024


def _params(sem, vmem=VMEM_LIMIT):
    return pltpu.CompilerParams(dimension_semantics=sem, vmem_limit_bytes=vmem)


def _dot(a, b):
    return jnp.dot(a, b, preferred_element_type=F32)


def _sigmoid(z):
    return 1.0 / (1.0 + jnp.exp(-z))


def _silu(z):
    return z * _sigmoid(z)


def _gelu_tanh(x):
    return 0.5 * x * (1.0 + jnp.tanh(math.sqrt(2.0 / math.pi) * (x + 0.044715 * (x * x * x))))


def _rms(xf, g):
    ms = jnp.mean(xf * xf, axis=-1, keepdims=True)
    return xf * lax.rsqrt(ms + NORM_EPS) * g


def _even_in_kernel(x_ref, g_ref, wa_ref, wb_ref, wc_ref, wd_ref, cw_ref,
                    ya_ref, u_ref, szb_ref, hn_sc, pbuf_sc, carry_sc, *, tm, col):
    i = pl.program_id(0)
    j = pl.program_id(1)
    nconv = D_SHORTCONV // col

    @pl.when(j == 0)
    def _():
        hn_sc[...] = _rms(x_ref[...], g_ref[...]).astype(BF16)

    @pl.when(j < nconv)
    def _():
        hn = hn_sc[...]
        p = _dot(hn, wb_ref[...]) * _dot(hn, wa_ref[...])

        @pl.when(i == 0)
        def _():
            carry_sc[j] = jnp.zeros((8, col), F32)

        pbuf_sc[0:8, :] = carry_sc[j]
        pbuf_sc[8:8 + tm, :] = p
        carry_sc[j] = pbuf_sc[tm:tm + 8, :]
        cw = cw_ref[...]
        conv = (cw[2:3, :] * p + cw[1:2, :] * pbuf_sc[pl.ds(7, tm), :]
                + cw[0:1, :] * pbuf_sc[pl.ds(6, tm), :])
        gb = _dot(hn, wc_ref[...])
        za = _dot(hn, wd_ref[...])
        ya_ref[...] = (gb * conv * _silu(za)).astype(BF16)

    @pl.when(j >= nconv)
    def _():
        hn = hn_sc[...]
        u = _dot(hn, wa_ref[...])
        for s in range(col // 128):
            u_ref[s] = u[:, s * 128:(s + 1) * 128]
        szb_ref[...] = _silu(_dot(hn, wb_ref[...])).astype(BF16)


def _even_in(x, g, w_in, conv_w, tm, col):
    seq = x.shape[0]
    nconv = D_SHORTCONV // col
    nssm = D_SSM // col
    u_blk = 4 * nconv

    def wspec(k, alt0):
        return pl.BlockSpec((D_MODEL, col),
                            lambda i, j: (0, jnp.where(j < nconv, j + nconv * k, alt0 + j - nconv)))

    hold = lambda k: pl.BlockSpec((D_MODEL, col), lambda i, j: (0, jnp.minimum(j, nconv - 1) + nconv * k))
    ssm_step = lambda j: jnp.maximum(j - nconv, 0)
    return pl.pallas_call(
        functools.partial(_even_in_kernel, tm=tm, col=col),
        grid=(seq // tm, nconv + nssm),
        in_specs=[
            pl.BlockSpec((tm, D_MODEL), lambda i, j: (i, 0)),
            pl.BlockSpec((1, D_MODEL), lambda i, j: (0, 0)),
            wspec(0, u_blk), wspec(1, u_blk + nssm), hold(2), hold(3),
            pl.BlockSpec((SHORTCONV_WIDTH, col), lambda i, j: (0, jnp.minimum(j, nconv - 1))),
        ],
        out_specs=[
            pl.BlockSpec((tm, col), lambda i, j: (i, jnp.minimum(j, nconv - 1))),
            pl.BlockSpec((col // 128, tm, 128), lambda i, j: (ssm_step(j), i, 0)),
            pl.BlockSpec((tm, col), lambda i, j: (i, ssm_step(j))),
        ],
        out_shape=[
            jax.ShapeDtypeStruct((seq, D_SHORTCONV), BF16),
            jax.ShapeDtypeStruct((SSM_SLABS, seq, 128), F32),
            jax.ShapeDtypeStruct((seq, D_SSM), BF16),
        ],
        scratch_shapes=[
            pltpu.VMEM((tm, D_MODEL), BF16),
            pltpu.VMEM((tm + 8, col), F32),
            pltpu.VMEM((nconv, 8, col), F32),
        ],
        compiler_params=_params(("arbitrary", "arbitrary")),
        name="even_in",
    )(x, g, w_in, w_in, w_in, w_in, conv_w)


def _s5_tables(lam_re, lam_im, log_dt, b_re, b_im, c_re, c_im, d_skip):
    hi = lax.Precision.HIGHEST
    t_, g_, n_, p_ = SSM_CHUNK, SSM_GROUPS, SSM_STATE, SSM_GROUP
    lam_re, lam_im = lam_re.astype(F32), lam_im.astype(F32)
    dt = jnp.exp(log_dt.astype(F32))[:, None]
    xr, xi = lam_re * dt, lam_im * dt
    k = jnp.arange(t_ + 1, dtype=F32)[:, None, None]
    mag = jnp.exp(xr * k)
    pr, pi = mag * jnp.cos(xi * k), mag * jnp.sin(xi * k)
    nr, ni = pr[1] - 1.0, pi[1]
    den = lam_re * lam_re + lam_im * lam_im
    fr = (nr * lam_re + ni * lam_im) / den
    fi = (ni * lam_re - nr * lam_im) / den
    b_re, b_im = b_re.astype(F32), b_im.astype(F32)
    bbr = fr[..., None] * b_re - fi[..., None] * b_im
    bbi = fr[..., None] * b_im + fi[..., None] * b_re
    c_re, c_im = c_re.astype(F32), c_im.astype(F32)

    cpr = c_re[None] * pr[:t_, :, None, :] - c_im[None] * pi[:t_, :, None, :]
    cpi = c_re[None] * pi[:t_, :, None, :] + c_im[None] * pr[:t_, :, None, :]
    kern = jnp.einsum('tgpn,gnq->tgpq', jnp.concatenate([cpr, -cpi], axis=-1),
                      jnp.concatenate([bbr, bbi], axis=1), precision=hi)
    skip = d_skip.astype(F32).reshape(g_, p_)[:, :, None] * jnp.eye(p_, dtype=F32)[None]
    kern = kern.at[0].add(skip)
    flat = t_ * p_
    last = jnp.transpose(kern[::-1], (1, 2, 0, 3)).reshape(g_, p_, flat)
    mt = jnp.stack([jnp.pad(last[:, :, (t_ - 1 - t) * p_:], ((0, 0), (0, 0), (0, (t_ - 1 - t) * p_)))
                    for t in range(t_)], axis=1).reshape(g_, flat, flat)

    prj, pij = pr[t_ - 1::-1][:t_], pi[t_ - 1::-1][:t_]
    wr = prj[..., None] * bbr[None] - pij[..., None] * bbi[None]
    wi = prj[..., None] * bbi[None] + pij[..., None] * bbr[None]
    half = (np.arange(2) == 0)[None, None, :, None, None]

    def pack_w(w):
        w = jnp.transpose(w.reshape(t_, g_ // 2, 2, n_, p_), (1, 0, 2, 4, 3))
        w = jnp.concatenate([jnp.where(half, w, 0.0), jnp.where(half, 0.0, w)], axis=-1)
        return w.reshape(g_ // 2, 2 * flat, 2 * n_)

    vr = c_re[None] * pr[1:, :, None, :] - c_im[None] * pi[1:, :, None, :]
    vi = c_re[None] * pi[1:, :, None, :] + c_im[None] * pr[1:, :, None, :]
    vr = jnp.transpose(vr, (1, 0, 2, 3)).reshape(g_ // 2, 2, flat, n_)
    vi = -jnp.transpose(vi, (1, 0, 2, 3)).reshape(g_ // 2, 2, flat, n_)
    zv = jnp.zeros((g_ // 2, flat, n_), F32)
    v_even = jnp.concatenate([vr[:, 0], zv, vi[:, 0], zv], axis=-1)
    v_odd = jnp.concatenate([zv, vr[:, 1], zv, vi[:, 1]], axis=-1)
    vt = jnp.stack([v_even, v_odd], axis=1).reshape(g_, flat, 4 * n_)
    a_re = pr[t_].reshape(1, g_ * n_)
    a_im = pi[t_].reshape(1, g_ * n_)
    return mt.astype(BF16), pack_w(wr).astype(BF16), pack_w(wi).astype(BF16), vt.astype(BF16), a_re, a_im


_NT = (((1,), (1,)), ((), ()))
_TN = (((0,), (0,)), ((), ()))


def _s5_kernel(u_ref, mt_ref, wre_ref, wim_ref, vt_ref, are_ref, aim_ref, y_ref,
               xt_sc, yt_sc, zre_sc, zim_sc, carry_sc, *, ct):
    i = pl.program_id(0)
    t_ = SSM_CHUNK

    @pl.when(i == 0)
    def _():
        carry_sc[...] = jnp.zeros_like(carry_sc)

    eye = jnp.where(lax.broadcasted_iota(jnp.int32, (128, 128), 0)
                    == lax.broadcasted_iota(jnp.int32, (128, 128), 1), 1.0, 0.0).astype(BF16)

    for s in range(SSM_SLABS):
        for j in range(t_):
            uj = u_ref[s, pl.ds(j, ct, stride=t_), :].astype(BF16)
            xt_sc[s, j] = lax.dot_general(eye, uj, _NT, preferred_element_type=F32).astype(BF16)

    pairs_per_slab = 128 // (2 * SSM_GROUP)
    for gp in range(SSM_GROUPS // 2):
        s, k = gp // pairs_per_slab, gp % pairs_per_slab
        xp = xt_sc[s, :, 32 * k:32 * k + 32, :].reshape(2 * SSM_FLAT, ct)
        zre_sc[:, gp * 128:(gp + 1) * 128] = lax.dot_general(xp, wre_ref[gp], _TN, preferred_element_type=F32)
        zim_sc[:, gp * 128:(gp + 1) * 128] = lax.dot_general(xp, wim_ref[gp], _TN, preferred_element_type=F32)

    are = are_ref[...]
    aim = aim_ref[...]

    def step(r, carry):
        sre, sim = carry
        zr = zre_sc[pl.ds(r, 1), :]
        zi = zim_sc[pl.ds(r, 1), :]
        zre_sc[pl.ds(r, 1), :] = sre
        zim_sc[pl.ds(r, 1), :] = sim
        return are * sre - aim * sim + zr, are * sim + aim * sre + zi

    sre, sim = lax.fori_loop(0, ct, step, (carry_sc[0:1, :], carry_sc[1:2, :]), unroll=8)
    carry_sc[0:1, :] = sre
    carry_sc[1:2, :] = sim

    groups_per_slab = 128 // SSM_GROUP
    for g in range(SSM_GROUPS):
        gp, s, k = g // 2, g // groups_per_slab, g % groups_per_slab
        xg = xt_sc[s, :, 16 * k:16 * k + 16, :].reshape(SSM_FLAT, ct)
        sp = jnp.concatenate([zre_sc[:, gp * 128:(gp + 1) * 128],
                              zim_sc[:, gp * 128:(gp + 1) * 128]], axis=1).astype(BF16)
        yt = _dot(mt_ref[g], xg) + lax.dot_general(vt_ref[g], sp, _NT, preferred_element_type=F32)
        yt_sc[s, :, 16 * k:16 * k + 16, :] = yt.reshape(t_, SSM_GROUP, ct)

    for s in range(SSM_SLABS):
        for t in range(t_):
            yt = yt_sc[s, t]
            hi = yt.astype(BF16)
            lo = (yt - hi.astype(F32)).astype(BF16)
            y = (lax.dot_general(hi, eye, _TN, preferred_element_type=F32)
                 + lax.dot_general(lo, eye, _TN, preferred_element_type=F32))
            y_ref[s, pl.ds(t, ct, stride=t_), :] = y


def _s5(u4, tables, ct):
    mt, wre, wim, vt, are, aim = tables
    _, seq, _ = u4.shape
    full = lambda a: pl.BlockSpec(a.shape, lambda i: (0,) * a.ndim)
    rows = pl.BlockSpec((SSM_SLABS, SSM_CHUNK * ct, 128), lambda i: (0, i, 0))
    nstate = SSM_GROUPS * SSM_STATE
    return pl.pallas_call(
        functools.partial(_s5_kernel, ct=ct),
        grid=(seq // (SSM_CHUNK * ct),),
        in_specs=[rows, full(mt), full(wre), full(wim), full(vt), full(are), full(aim)],
        out_specs=rows,
        out_shape=jax.ShapeDtypeStruct(u4.shape, F32),
        scratch_shapes=[pltpu.VMEM((SSM_SLABS, SSM_CHUNK, 128, ct), BF16),
                        pltpu.VMEM((SSM_SLABS, SSM_CHUNK, 128, ct), F32),
                        pltpu.VMEM((ct, nstate), F32),
                        pltpu.VMEM((ct, nstate), F32),
                        pltpu.VMEM((8, nstate), F32)],
        compiler_params=_params(("arbitrary",)),
        name="s5_ssm",
    )(u4, mt, wre, wim, vt, are, aim)


OUT_ROWS = 256


def _even_out_kernel(x_ref, ya_ref, ys_ref, szb_ref, wglu_ref, wout_ref, g_ref, h_ref, hn_ref, *, tm):
    for r0 in range(0, tm, OUT_ROWS):
        rows = slice(r0, r0 + OUT_ROWS)
        gl = _gelu_tanh(jnp.concatenate([ys_ref[s, rows, :] for s in range(SSM_SLABS)], axis=1))
        yb = gl * _sigmoid(_dot(gl.astype(BF16), wglu_ref[...])) * szb_ref[rows, :].astype(F32)
        acc = _dot(ya_ref[rows, :], wout_ref[0:D_SHORTCONV, :])
        acc = acc + _dot(yb.astype(BF16), wout_ref[D_SHORTCONV:D_SHORTCONV + D_SSM, :])
        h = x_ref[rows, :] + acc
        h_ref[rows, :] = h
        hn_ref[rows, :] = _rms(h, g_ref[...]).astype(BF16)


def _even_out(x, ya, ys, szb, w_glu, w_out, g, tm):
    seq = x.shape[0]
    row = lambda w: pl.BlockSpec((tm, w), lambda i: (i, 0))
    const = lambda a: pl.BlockSpec(a.shape, lambda i: (0, 0), pipeline_mode=pl.Buffered(1))
    return pl.pallas_call(
        functools.partial(_even_out_kernel, tm=tm),
        grid=(seq // tm,),
        in_specs=[row(D_MODEL), row(D_SHORTCONV),
                  pl.BlockSpec((SSM_SLABS, tm, 128), lambda i: (0, i, 0)), row(D_SSM),
                  const(w_glu), const(w_out), const(g)],
        out_specs=[row(D_MODEL), row(D_MODEL)],
        out_shape=[jax.ShapeDtypeStruct((seq, D_MODEL), F32),
                   jax.ShapeDtypeStruct((seq, D_MODEL), BF16)],
        compiler_params=_params(("arbitrary",)),
        name="even_out",
    )(x, ya, ys, szb, w_glu, w_out, g)


ODD_STEPS = 12
_SILU_STEPS = (2, 3, 10, 11)
_Q_STEPS = (4, 5)


def _odd_in_kernel(hn_ref, wa_ref, wb_ref, o_ref):
    s = pl.program_id(1)
    hn = hn_ref[...]
    is_silu = (s == 2) | (s == 3) | (s == 10) | (s == 11)

    @pl.when(s < 2)
    def _():
        o_ref[...] = (_dot(hn, wa_ref[...]) * _sigmoid(_dot(hn, wb_ref[...]))).astype(BF16)

    @pl.when(is_silu)
    def _():
        o_ref[...] = _silu(_dot(hn, wa_ref[...])).astype(BF16)

    @pl.when((s >= 2) & jnp.logical_not(is_silu))
    def _():
        scale = jnp.where((s == 4) | (s == 5), HEAD_DIM ** -0.5, 1.0).astype(F32)
        o_ref[...] = (_dot(hn, wa_ref[...]) * scale).astype(BF16)


def _odd_in(hn, w_in, tm):
    seq = hn.shape[0]
    return pl.pallas_call(
        _odd_in_kernel,
        grid=(seq // tm, ODD_STEPS),
        in_specs=[pl.BlockSpec((tm, D_MODEL), lambda i, s: (i, 0)),
                  pl.BlockSpec((D_MODEL, COL), lambda i, s: (0, jnp.where(s < 2, s, s + 2))),
                  pl.BlockSpec((D_MODEL, COL), lambda i, s: (0, jnp.where(s < 2, s + 2, 3)))],
        out_specs=pl.BlockSpec((tm, COL), lambda i, s: (i, s)),
        out_shape=jax.ShapeDtypeStruct((seq, ODD_STEPS * COL), BF16),
        compiler_params=_params(("arbitrary", "arbitrary")),
        name="odd_in",
    )(hn, w_in, w_in)


CONV_ROWS = 64


def _conformer_kernel(cur_ref, halo_ref, szc_ref, cw_ref, cb_ref, lg_ref, lb_ref, wpw_ref,
                      o_ref, buf_sc, conv_sc, *, tm):
    i = pl.program_id(0)
    halo = jnp.where(i == 0, 0.0, halo_ref[...].astype(F32))
    cur = cur_ref[...].astype(F32)
    nslab = D_CONFORMER // 128
    for c in range(nslab):
        buf_sc[c, 0:CONV_HALO, :] = halo[:, c * 128:(c + 1) * 128]
        buf_sc[c, CONV_HALO:CONV_HALO + tm, :] = cur[:, c * 128:(c + 1) * 128]
    shift = CONV_HALO - (CONFORMER_WIDTH - 1)

    def rows(rc, _):
        base = pl.multiple_of(rc * CONV_ROWS, CONV_ROWS)
        for c in range(nslab):
            lanes = slice(c * 128, (c + 1) * 128)
            acc = jnp.zeros((CONV_ROWS, 128), F32) + cb_ref[:, lanes]
            for k in range(CONFORMER_WIDTH):
                acc = acc + cw_ref[k:k + 1, lanes] * buf_sc[c, pl.ds(base + (shift + k), CONV_ROWS), :]
            conv_sc[pl.ds(base, CONV_ROWS), lanes] = acc
        return 0

    lax.fori_loop(0, tm // CONV_ROWS, rows, 0)
    y = conv_sc[...]
    mu = jnp.mean(y, axis=-1, keepdims=True)
    yc = y - mu
    var = jnp.mean(yc * yc, axis=-1, keepdims=True)
    ln = yc * lax.rsqrt(var + NORM_EPS) * lg_ref[...] + lb_ref[...]
    o_ref[...] = (_dot(_silu(ln).astype(BF16), wpw_ref[...]) * szc_ref[...].astype(F32)).astype(BF16)


def _conformer(proj, conv_w, conv_b, ln_g, ln_b, w_pw, tm):
    seq = proj.shape[0]
    per = tm // CONV_HALO
    const = lambda a: pl.BlockSpec(a.shape, lambda i: (0, 0))
    return pl.pallas_call(
        functools.partial(_conformer_kernel, tm=tm),
        grid=(seq // tm,),
        in_specs=[pl.BlockSpec((tm, D_CONFORMER), lambda i: (i, 0)),
                  pl.BlockSpec((CONV_HALO, D_CONFORMER), lambda i: (jnp.maximum(i * per - 1, 0), 0)),
                  pl.BlockSpec((tm, D_CONFORMER), lambda i: (i, 1)),
                  const(conv_w), const(conv_b), const(ln_g), const(ln_b), const(w_pw)],
        out_specs=pl.BlockSpec((tm, D_CONFORMER), lambda i: (i, 0)),
        out_shape=jax.ShapeDtypeStruct((seq, D_CONFORMER), BF16),
        scratch_shapes=[pltpu.VMEM((D_CONFORMER // 128, tm + CONV_HALO, 128), F32),
                        pltpu.VMEM((tm, D_CONFORMER), F32)],
        compiler_params=_params(("arbitrary",)),
        name="conformer",
    )(proj, proj, proj, conv_w, conv_b, ln_g, ln_b, w_pw)


def _t5_bucket(dist):
    max_exact = REL_BUCKETS // 2
    d_f = jnp.maximum(dist, 1).astype(F32)
    large = max_exact + (jnp.log(d_f / max_exact) / math.log(REL_MAX_DISTANCE / max_exact)
                         * (REL_BUCKETS - max_exact)).astype(jnp.int32)
    large = jnp.minimum(large, REL_BUCKETS - 1)
    return jnp.where(dist < max_exact, dist, large)


def _attn_bias(rel_bias):
    q = ATTN_BLOCK
    nh = rel_bias.shape[1]
    kj = np.arange(2 * q)[None, None, :]
    neg = lambda n: jnp.full((nh, n), NEG_INF, F32)
    tabs = []
    for window, dilation in DILATED_PATTERNS:
        steps = window // dilation
        assert steps == q
        rel = jnp.asarray(np.arange(steps + 1) * dilation, dtype=jnp.int32)
        by_rel = rel_bias.astype(F32)[_t5_bucket(rel)].T
        g = jnp.concatenate([by_rel[:, ::-1], neg(2 * q - 1)], axis=1)
        flat = jnp.tile(g, (1, q))[:, :q * (3 * q - 1)]
        tab = flat.reshape(nh, q, 3 * q - 1)[:, :, :2 * q]
        tabs.append(tab)
        tabs.append(jnp.where(kj >= q, tab, NEG_INF))
    return jnp.stack(tabs, axis=1)


def _attn_kernel(q_ref, k_ref, v_ref, szd_ref, bias_ref, o_ref,
                 qf, kf, vf, qs1, qs2, ks0, ks1, ks2, vs0, vs1, vs2,
                 n0, n1, n2, d0, d1, d2, m0, m1, m2):
    i = pl.program_id(1)
    t = ATTN_TILE
    blk = ATTN_BLOCK
    first_tile = i == 0
    qf[...] = q_ref[...].astype(F32)
    kf[...] = k_ref[...].astype(F32)
    vf[...] = v_ref[...].astype(F32)
    qstreams = (None, qs1, qs2)
    kstreams = (ks0, ks1, ks2)
    vstreams = (vs0, vs1, vs2)
    for pidx, (_, dil) in enumerate(DILATED_PATTERNS):
        n = t // dil
        per = blk + n
        for r in range(dil):
            base = r * per
            for src_f, src_ref, dst in ((kf, k_ref, kstreams[pidx]), (vf, v_ref, vstreams[pidx])):
                tail = dst[base + n:base + per, :]
                dst[base:base + blk, :] = jnp.where(first_tile, jnp.zeros_like(tail), tail)
                if dil == 1:
                    dst[base + blk:base + per, :] = src_ref[...]
                else:
                    dst[base + blk:base + per, :] = src_f[pl.ds(r, n, stride=dil), :].astype(BF16)
            if dil > 1:
                qstreams[pidx][r * n:(r + 1) * n, :] = qf[pl.ds(r, n, stride=dil), :].astype(BF16)

    ones = jnp.ones((2 * blk, HEAD_DIM), BF16)
    first = first_tile.astype(jnp.int32)

    def block(pidx, dil, r, sb, num_sc, den_sc, max_sc):
        n = t // dil
        per = blk + n
        if dil == 1:
            qb = q_ref[sb * blk:(sb + 1) * blk, :]
        else:
            qb = qstreams[pidx][r * n + sb * blk:r * n + (sb + 1) * blk, :]
        krows = slice(r * per + sb * blk, r * per + (sb + 2) * blk)
        kb = kstreams[pidx][krows, :]
        vb = vstreams[pidx][krows, :]
        s = lax.dot_general(qb, kb, _NT, preferred_element_type=F32)
        s = s + bias_ref[0, 2 * pidx + first if sb == 0 else 2 * pidx]
        mx = jnp.max(s, axis=-1, keepdims=True)
        p = jnp.exp(s - mx).astype(BF16)
        pv = _dot(p, jnp.concatenate([vb, ones], axis=1))
        qrows = pl.ds(sb * blk * dil + r, blk, stride=None if dil == 1 else dil)
        num_sc[qrows, :] = pv[:, :HEAD_DIM]
        den_sc[qrows, :] = pv[:, HEAD_DIM:]
        max_sc[qrows, :] = jnp.broadcast_to(mx, (blk, HEAD_DIM))

    scratch = ((n0, d0, m0), (n1, d1, m1), (n2, d2, m2))
    for pidx, (_, dil) in enumerate(DILATED_PATTERNS):
        for sb in range(t // (blk * dil)):
            for r in range(dil):
                block(pidx, dil, r, sb, *scratch[pidx])

    ma, mb, mc = m0[...], m1[...], m2[...]
    mall = jnp.maximum(jnp.maximum(ma, mb), mc)
    wa, wb, wc = jnp.exp(ma - mall), jnp.exp(mb - mall), jnp.exp(mc - mall)
    numer = wa * n0[...] + wb * n1[...] + wc * n2[...]
    denom = wa * d0[...] + wb * d1[...] + wc * d2[...]
    o_ref[...] = (numer / denom * szd_ref[...].astype(F32)).astype(BF16)


def _attention(proj, bias):
    seq = proj.shape[0]
    t = ATTN_TILE
    q0 = (2 * D_CONFORMER) // HEAD_DIM
    k0 = q0 + N_HEADS
    v0 = k0 + N_HEADS
    z0 = v0 + N_HEADS
    cur = lambda c0: pl.BlockSpec((t, HEAD_DIM), lambda h, i: (i, c0 + h))
    f32 = lambda rows: pltpu.VMEM((rows, HEAD_DIM), F32)
    bf16 = lambda rows: pltpu.VMEM((rows, HEAD_DIM), BF16)
    streams = [bf16(dil * (ATTN_BLOCK + t // dil)) for _, dil in DILATED_PATTERNS]
    return pl.pallas_call(
        _attn_kernel,
        grid=(N_HEADS, seq // t),
        in_specs=[cur(q0), cur(k0), cur(v0), cur(z0),
                  pl.BlockSpec((1,) + bias.shape[1:], lambda h, i: (h, 0, 0, 0))],
        out_specs=pl.BlockSpec((t, HEAD_DIM), lambda h, i: (i, h)),
        out_shape=jax.ShapeDtypeStruct((seq, D_ATTN), BF16),
        scratch_shapes=[f32(t)] * 3 + [bf16(t)] * 2 + streams + streams + [f32(t)] * 9,
        compiler_params=_params(("arbitrary", "arbitrary")),
        name="dilated_attn",
    )(proj, proj, proj, proj, bias)


def _odd_out_kernel(h_ref, yc_ref, yd_ref, wout_ref, g_ref, o_ref, *, tm):
    for r0 in range(0, tm, OUT_ROWS):
        rows = slice(r0, r0 + OUT_ROWS)
        acc = _dot(yc_ref[rows, :], wout_ref[0:D_CONFORMER, :])
        acc = acc + _dot(yd_ref[rows, :], wout_ref[D_CONFORMER:D_CONFORMER + D_ATTN, :])
        o_ref[rows, :] = _rms(h_ref[rows, :] + acc, g_ref[...])


def _odd_out(h, yc, yd, w_out, g, tm):
    seq = h.shape[0]
    row = lambda w: pl.BlockSpec((tm, w), lambda i: (i, 0))
    const = lambda a: pl.BlockSpec(a.shape, lambda i: (0, 0), pipeline_mode=pl.Buffered(1))
    return pl.pallas_call(
        functools.partial(_odd_out_kernel, tm=tm),
        grid=(seq // tm,),
        in_specs=[row(D_MODEL), row(D_CONFORMER), row(D_ATTN), const(w_out), const(g)],
        out_specs=row(D_MODEL),
        out_shape=jax.ShapeDtypeStruct((seq, D_MODEL), F32),
        compiler_params=_params(("arbitrary",)),
        name="odd_out",
    )(h, yc, yd, w_out, g)


def _row_tile(seq, want):
    return min(seq, want)


def kernel(x, norm_g, final_norm_g, ev_w_in, ev_conv_w, s5_lam_re, s5_lam_im, s5_log_dt, s5_b_re, s5_b_im, s5_c_re, s5_c_im, s5_d, s5_w_glu, ev_w_out, od_w_in, cf_conv_w, cf_conv_b, cf_ln_g, cf_ln_b, cf_w_pw, od_w_out, rel_bias):
    bsz, seq, _ = x.shape
    assert bsz == 1 and seq % ATTN_TILE == 0
    x2 = x.reshape(seq, D_MODEL)
    g0 = norm_g[0].astype(F32).reshape(1, D_MODEL)
    g1 = norm_g[1].astype(F32).reshape(1, D_MODEL)
    gf = final_norm_g.astype(F32).reshape(1, D_MODEL)

    ya, u, szb = _even_in(x2, g0, ev_w_in[0].astype(BF16), ev_conv_w[0].astype(F32), _row_tile(seq, 1024), 256)
    tables = _s5_tables(s5_lam_re[0], s5_lam_im[0], s5_log_dt[0], s5_b_re[0], s5_b_im[0],
                        s5_c_re[0], s5_c_im[0], s5_d[0])
    ys = _s5(u, tables, min(seq // SSM_CHUNK, 128))
    h1, hn1 = _even_out(x2, ya, ys, szb, s5_w_glu[0].astype(BF16), ev_w_out[0].astype(BF16), g1,
                        _row_tile(seq, 512))

    proj = _odd_in(hn1, od_w_in[0].astype(BF16), _row_tile(seq, 2048))
    conv_w = jnp.concatenate([cf_conv_w[0].astype(F32), jnp.zeros((1, D_CONFORMER), F32)], axis=0)
    yc = _conformer(proj, conv_w, cf_conv_b[0].astype(F32).reshape(1, -1),
                    cf_ln_g[0].astype(F32).reshape(1, -1), cf_ln_b[0].astype(F32).reshape(1, -1),
                    cf_w_pw[0].astype(BF16), _row_tile(seq, 512))
    yd = _attention(proj, _attn_bias(rel_bias))
    out = _odd_out(h1, yc, yd, od_w_out[0].astype(BF16), gf, _row_tile(seq, 512))
    return out.reshape(bsz, seq, D_MODEL)
```

```python
import functools
import math

import numpy as np
import jax
import jax.numpy as jnp
from jax import lax
from jax.experimental import pallas as pl
from jax.experimental.pallas import tpu as pltpu

F32 = jnp.float32
BF16 = jnp.bfloat16

D_MODEL = 2048
D_SHORTCONV = 1536
SHORTCONV_WIDTH = 3
D_SSM = 512
SSM_GROUP = 16
SSM_GROUPS = 32
SSM_STATE = 64
D_CONFORMER = 1024
CONFORMER_WIDTH = 31
D_ATTN = 1024
HEAD_DIM = 128
N_HEADS = 8
DILATED_PATTERNS = ((128, 1), (512, 4), (2048, 16))
ATTN_BLOCK = 128
REL_BUCKETS = 32
REL_MAX_DISTANCE = 2048
NORM_EPS = 1e-6
NEG_INF = -1e30

COL = 512
SSM_CHUNK = 16
SSM_FLAT = SSM_CHUNK * SSM_GROUP
SSM_SLABS = D_SSM // 128
ATTN_TILE = 2048
MERGE_DIL = 4
CONV_HALO = 32
VMEM_LIMIT = 50 * 1024 * 1024


def _params(sem, vmem=VMEM_LIMIT):
    return pltpu.CompilerParams(dimension_semantics=sem, vmem_limit_bytes=vmem)


def _dot(a, b):
    return jnp.dot(a, b, preferred_element_type=F32)


def _sigmoid(z):
    return 1.0 / (1.0 + jnp.exp(-z))


def _silu(z):
    return z * _sigmoid(z)


def _gelu_tanh(x):
    return 0.5 * x * (1.0 + jnp.tanh(math.sqrt(2.0 / math.pi) * (x + 0.044715 * (x * x * x))))


def _rms(xf, g):
    ms = jnp.mean(xf * xf, axis=-1, keepdims=True)
    return xf * lax.rsqrt(ms + NORM_EPS) * g


def _even_in_kernel(x_ref, g_ref, wa_ref, wb_ref, wc_ref, wd_ref, cw_ref,
                    ya_ref, u_ref, szb_ref, hn_sc, pbuf_sc, carry_sc, *, tm, col):
    i = pl.program_id(0)
    j = pl.program_id(1)
    nconv = D_SHORTCONV // col

    @pl.when(j == 0)
    def _():
        hn_sc[...] = _rms(x_ref[...], g_ref[...]).astype(BF16)

    @pl.when(j < nconv)
    def _():
        hn = hn_sc[...]
        p = _dot(hn, wb_ref[...]) * _dot(hn, wa_ref[...])

        @pl.when(i == 0)
        def _():
            carry_sc[j] = jnp.zeros((8, col), F32)

        pbuf_sc[0:8, :] = carry_sc[j]
        pbuf_sc[8:8 + tm, :] = p
        carry_sc[j] = pbuf_sc[tm:tm + 8, :]
        cw = cw_ref[...]
        conv = (cw[2:3, :] * p + cw[1:2, :] * pbuf_sc[pl.ds(7, tm), :]
                + cw[0:1, :] * pbuf_sc[pl.ds(6, tm), :])
        gb = _dot(hn, wc_ref[...])
        za = _dot(hn, wd_ref[...])
        ya_ref[...] = (gb * conv * _silu(za)).astype(BF16)

    @pl.when(j >= nconv)
    def _():
        hn = hn_sc[...]
        u = _dot(hn, wa_ref[...])
        for s in range(col // 128):
            u_ref[s] = u[:, s * 128:(s + 1) * 128]
        szb_ref[...] = _silu(_dot(hn, wb_ref[...])).astype(BF16)


def _even_in(x, g, w_in, conv_w, tm, col):
    seq = x.shape[0]
    nconv = D_SHORTCONV // col
    nssm = D_SSM // col
    u_blk = 4 * nconv

    def wspec(k, alt0):
        return pl.BlockSpec((D_MODEL, col),
                            lambda i, j: (0, jnp.where(j < nconv, j + nconv * k, alt0 + j - nconv)))

    hold = lambda k: pl.BlockSpec((D_MODEL, col), lambda i, j: (0, jnp.minimum(j, nconv - 1) + nconv * k))
    ssm_step = lambda j: jnp.maximum(j - nconv, 0)
    return pl.pallas_call(
        functools.partial(_even_in_kernel, tm=tm, col=col),
        grid=(seq // tm, nconv + nssm),
        in_specs=[
            pl.BlockSpec((tm, D_MODEL), lambda i, j: (i, 0)),
            pl.BlockSpec((1, D_MODEL), lambda i, j: (0, 0)),
            wspec(0, u_blk), wspec(1, u_blk + nssm), hold(2), hold(3),
            pl.BlockSpec((SHORTCONV_WIDTH, col), lambda i, j: (0, jnp.minimum(j, nconv - 1))),
        ],
        out_specs=[
            pl.BlockSpec((tm, col), lambda i, j: (i, jnp.minimum(j, nconv - 1))),
            pl.BlockSpec((col // 128, tm, 128), lambda i, j: (ssm_step(j), i, 0)),
            pl.BlockSpec((tm, col), lambda i, j: (i, ssm_step(j))),
        ],
        out_shape=[
            jax.ShapeDtypeStruct((seq, D_SHORTCONV), BF16),
            jax.ShapeDtypeStruct((SSM_SLABS, seq, 128), F32),
            jax.ShapeDtypeStruct((seq, D_SSM), BF16),
        ],
        scratch_shapes=[
            pltpu.VMEM((tm, D_MODEL), BF16),
            pltpu.VMEM((tm + 8, col), F32),
            pltpu.VMEM((nconv, 8, col), F32),
        ],
        compiler_params=_params(("arbitrary", "arbitrary")),
        name="even_in",
    )(x, g, w_in, w_in, w_in, w_in, conv_w)


def _s5_tables(lam_re, lam_im, log_dt, b_re, b_im, c_re, c_im, d_skip):
    hi = lax.Precision.HIGHEST
    t_, g_, n_, p_ = SSM_CHUNK, SSM_GROUPS, SSM_STATE, SSM_GROUP
    lam_re, lam_im = lam_re.astype(F32), lam_im.astype(F32)
    dt = jnp.exp(log_dt.astype(F32))[:, None]
    xr, xi = lam_re * dt, lam_im * dt
    k = jnp.arange(t_ + 1, dtype=F32)[:, None, None]
    mag = jnp.exp(xr * k)
    pr, pi = mag * jnp.cos(xi * k), mag * jnp.sin(xi * k)
    nr, ni = pr[1] - 1.0, pi[1]
    den = lam_re * lam_re + lam_im * lam_im
    fr = (nr * lam_re + ni * lam_im) / den
    fi = (ni * lam_re - nr * lam_im) / den
    b_re, b_im = b_re.astype(F32), b_im.astype(F32)
    bbr = fr[..., None] * b_re - fi[..., None] * b_im
    bbi = fr[..., None] * b_im + fi[..., None] * b_re
    c_re, c_im = c_re.astype(F32), c_im.astype(F32)

    cpr = c_re[None] * pr[:t_, :, None, :] - c_im[None] * pi[:t_, :, None, :]
    cpi = c_re[None] * pi[:t_, :, None, :] + c_im[None] * pr[:t_, :, None, :]
    kern = jnp.einsum('tgpn,gnq->tgpq', jnp.concatenate([cpr, -cpi], axis=-1),
                      jnp.concatenate([bbr, bbi], axis=1), precision=hi)
    skip = d_skip.astype(F32).reshape(g_, p_)[:, :, None] * jnp.eye(p_, dtype=F32)[None]
    kern = kern.at[0].add(skip)
    flat = t_ * p_
    last = jnp.transpose(kern[::-1], (1, 2, 0, 3)).reshape(g_, p_, flat)
    mt = jnp.stack([jnp.pad(last[:, :, (t_ - 1 - t) * p_:], ((0, 0), (0, 0), (0, (t_ - 1 - t) * p_)))
                    for t in range(t_)], axis=1).reshape(g_, flat, flat)

    prj, pij = pr[t_ - 1::-1][:t_], pi[t_ - 1::-1][:t_]
    wr = prj[..., None] * bbr[None] - pij[..., None] * bbi[None]
    wi = prj[..., None] * bbi[None] + pij[..., None] * bbr[None]
    half = (np.arange(2) == 0)[None, None, :, None, None]

    def pack_w(w):
        w = jnp.transpose(w.reshape(t_, g_ // 2, 2, n_, p_), (1, 0, 2, 4, 3))
        w = jnp.concatenate([jnp.where(half, w, 0.0), jnp.where(half, 0.0, w)], axis=-1)
        return w.reshape(g_ // 2, 2 * flat, 2 * n_)

    vr = c_re[None] * pr[1:, :, None, :] - c_im[None] * pi[1:, :, None, :]
    vi = c_re[None] * pi[1:, :, None, :] + c_im[None] * pr[1:, :, None, :]
    vr = jnp.transpose(vr, (1, 0, 2, 3)).reshape(g_ // 2, 2, flat, n_)
    vi = -jnp.transpose(vi, (1, 0, 2, 3)).reshape(g_ // 2, 2, flat, n_)
    zv = jnp.zeros((g_ // 2, flat, n_), F32)
    v_even = jnp.concatenate([vr[:, 0], zv, vi[:, 0], zv], axis=-1)
    v_odd = jnp.concatenate([zv, vr[:, 1], zv, vi[:, 1]], axis=-1)
    vt = jnp.stack([v_even, v_odd], axis=1).reshape(g_, flat, 4 * n_)
    a_re = pr[t_].reshape(1, g_ * n_)
    a_im = pi[t_].reshape(1, g_ * n_)
    return mt.astype(BF16), pack_w(wr).astype(BF16), pack_w(wi).astype(BF16), vt.astype(BF16), a_re, a_im


_NT = (((1,), (1,)), ((), ()))
_TN = (((0,), (0,)), ((), ()))


def _s5_kernel(u_ref, mt_ref, wre_ref, wim_ref, vt_ref, are_ref, aim_ref, y_ref,
               xt_sc, yt_sc, zre_sc, zim_sc, carry_sc, *, ct):
    i = pl.program_id(0)
    t_ = SSM_CHUNK

    @pl.when(i == 0)
    def _():
        carry_sc[...] = jnp.zeros_like(carry_sc)

    eye = jnp.where(lax.broadcasted_iota(jnp.int32, (128, 128), 0)
                    == lax.broadcasted_iota(jnp.int32, (128, 128), 1), 1.0, 0.0).astype(BF16)

    for s in range(SSM_SLABS):
        for j in range(t_):
            uj = u_ref[s, pl.ds(j, ct, stride=t_), :].astype(BF16)
            xt_sc[s, j] = lax.dot_general(eye, uj, _NT, preferred_element_type=F32).astype(BF16)

    pairs_per_slab = 128 // (2 * SSM_GROUP)
    for gp in range(SSM_GROUPS // 2):
        s, k = gp // pairs_per_slab, gp % pairs_per_slab
        xp = xt_sc[s, :, 32 * k:32 * k + 32, :].reshape(2 * SSM_FLAT, ct)
        zre_sc[:, gp * 128:(gp + 1) * 128] = lax.dot_general(xp, wre_ref[gp], _TN, preferred_element_type=F32)
        zim_sc[:, gp * 128:(gp + 1) * 128] = lax.dot_general(xp, wim_ref[gp], _TN, preferred_element_type=F32)

    are = are_ref[...]
    aim = aim_ref[...]

    def step(r, carry):
        sre, sim = carry
        zr = zre_sc[pl.ds(r, 1), :]
        zi = zim_sc[pl.ds(r, 1), :]
        zre_sc[pl.ds(r, 1), :] = sre
        zim_sc[pl.ds(r, 1), :] = sim
        return are * sre - aim * sim + zr, are * sim + aim * sre + zi

    sre, sim = lax.fori_loop(0, ct, step, (carry_sc[0:1, :], carry_sc[1:2, :]), unroll=8)
    carry_sc[0:1, :] = sre
    carry_sc[1:2, :] = sim

    groups_per_slab = 128 // SSM_GROUP
    for g in range(SSM_GROUPS):
        gp, s, k = g // 2, g // groups_per_slab, g % groups_per_slab
        xg = xt_sc[s, :, 16 * k:16 * k + 16, :].reshape(SSM_FLAT, ct)
        sp = jnp.concatenate([zre_sc[:, gp * 128:(gp + 1) * 128],
                              zim_sc[:, gp * 128:(gp + 1) * 128]], axis=1).astype(BF16)
        yt = _dot(mt_ref[g], xg) + lax.dot_general(vt_ref[g], sp, _NT, preferred_element_type=F32)
        yt_sc[s, :, 16 * k:16 * k + 16, :] = yt.reshape(t_, SSM_GROUP, ct)

    for s in range(SSM_SLABS):
        for t in range(t_):
            yt = yt_sc[s, t]
            hi = yt.astype(BF16)
            lo = (yt - hi.astype(F32)).astype(BF16)
            y = (lax.dot_general(hi, eye, _TN, preferred_element_type=F32)
                 + lax.dot_general(lo, eye, _TN, preferred_element_type=F32))
            y_ref[s, pl.ds(t, ct, stride=t_), :] = y


def _s5(u4, tables, ct):
    mt, wre, wim, vt, are, aim = tables
    _, seq, _ = u4.shape
    full = lambda a: pl.BlockSpec(a.shape, lambda i: (0,) * a.ndim)
    rows = pl.BlockSpec((SSM_SLABS, SSM_CHUNK * ct, 128), lambda i: (0, i, 0))
    nstate = SSM_GROUPS * SSM_STATE
    return pl.pallas_call(
        functools.partial(_s5_kernel, ct=ct),
        grid=(seq // (SSM_CHUNK * ct),),
        in_specs=[rows, full(mt), full(wre), full(wim), full(vt), full(are), full(aim)],
        out_specs=rows,
        out_shape=jax.ShapeDtypeStruct(u4.shape, F32),
        scratch_shapes=[pltpu.VMEM((SSM_SLABS, SSM_CHUNK, 128, ct), BF16),
                        pltpu.VMEM((SSM_SLABS, SSM_CHUNK, 128, ct), F32),
                        pltpu.VMEM((ct, nstate), F32),
                        pltpu.VMEM((ct, nstate), F32),
                        pltpu.VMEM((8, nstate), F32)],
        compiler_params=_params(("arbitrary",)),
        name="s5_ssm",
    )(u4, mt, wre, wim, vt, are, aim)


OUT_ROWS = 256


def _even_out_kernel(x_ref, ya_ref, ys_ref, szb_ref, wglu_ref, wout_ref, g_ref, h_ref, hn_ref, *, tm):
    for r0 in range(0, tm, OUT_ROWS):
        rows = slice(r0, r0 + OUT_ROWS)
        gl = _gelu_tanh(jnp.concatenate([ys_ref[s, rows, :] for s in range(SSM_SLABS)], axis=1))
        yb = gl * _sigmoid(_dot(gl.astype(BF16), wglu_ref[...])) * szb_ref[rows, :].astype(F32)
        acc = _dot(ya_ref[rows, :], wout_ref[0:D_SHORTCONV, :])
        acc = acc + _dot(yb.astype(BF16), wout_ref[D_SHORTCONV:D_SHORTCONV + D_SSM, :])
        h = x_ref[rows, :] + acc
        h_ref[rows, :] = h
        hn_ref[rows, :] = _rms(h, g_ref[...]).astype(BF16)


def _even_out(x, ya, ys, szb, w_glu, w_out, g, tm):
    seq = x.shape[0]
    row = lambda w: pl.BlockSpec((tm, w), lambda i: (i, 0))
    const = lambda a: pl.BlockSpec(a.shape, lambda i: (0, 0), pipeline_mode=pl.Buffered(1))
    return pl.pallas_call(
        functools.partial(_even_out_kernel, tm=tm),
        grid=(seq // tm,),
        in_specs=[row(D_MODEL), row(D_SHORTCONV),
                  pl.BlockSpec((SSM_SLABS, tm, 128), lambda i: (0, i, 0)), row(D_SSM),
                  const(w_glu), const(w_out), const(g)],
        out_specs=[row(D_MODEL), row(D_MODEL)],
        out_shape=[jax.ShapeDtypeStruct((seq, D_MODEL), F32),
                   jax.ShapeDtypeStruct((seq, D_MODEL), BF16)],
        compiler_params=_params(("arbitrary",)),
        name="even_out",
    )(x, ya, ys, szb, w_glu, w_out, g)


ODD_STEPS = 12
_SILU_STEPS = (2, 3, 10, 11)
_Q_STEPS = (4, 5)


def _odd_in_kernel(hn_ref, wa_ref, wb_ref, o_ref):
    s = pl.program_id(1)
    hn = hn_ref[...]
    is_silu = (s == 2) | (s == 3) | (s == 10) | (s == 11)

    @pl.when(s < 2)
    def _():
        o_ref[...] = (_dot(hn, wa_ref[...]) * _sigmoid(_dot(hn, wb_ref[...]))).astype(BF16)

    @pl.when(is_silu)
    def _():
        o_ref[...] = _silu(_dot(hn, wa_ref[...])).astype(BF16)

    @pl.when((s >= 2) & jnp.logical_not(is_silu))
    def _():
        scale = jnp.where((s == 4) | (s == 5), HEAD_DIM ** -0.5, 1.0).astype(F32)
        o_ref[...] = (_dot(hn, wa_ref[...]) * scale).astype(BF16)


def _odd_in(hn, w_in, tm):
    seq = hn.shape[0]
    return pl.pallas_call(
        _odd_in_kernel,
        grid=(seq // tm, ODD_STEPS),
        in_specs=[pl.BlockSpec((tm, D_MODEL), lambda i, s: (i, 0)),
                  pl.BlockSpec((D_MODEL, COL), lambda i, s: (0, jnp.where(s < 2, s, s + 2))),
                  pl.BlockSpec((D_MODEL, COL), lambda i, s: (0, jnp.where(s < 2, s + 2, 3)))],
        out_specs=pl.BlockSpec((tm, COL), lambda i, s: (i, s)),
        out_shape=jax.ShapeDtypeStruct((seq, ODD_STEPS * COL), BF16),
        compiler_params=_params(("arbitrary", "arbitrary")),
        name="odd_in",
    )(hn, w_in, w_in)


CONV_ROWS = 64


def _conformer_kernel(cur_ref, halo_ref, szc_ref, cw_ref, cb_ref, lg_ref, lb_ref, wpw_ref,
                      o_ref, buf_sc, conv_sc, *, tm):
    i = pl.program_id(0)
    halo = jnp.where(i == 0, 0.0, halo_ref[...].astype(F32))
    cur = cur_ref[...].astype(F32)
    nslab = D_CONFORMER // 128
    for c in range(nslab):
        buf_sc[c, 0:CONV_HALO, :] = halo[:, c * 128:(c + 1) * 128]
        buf_sc[c, CONV_HALO:CONV_HALO + tm, :] = cur[:, c * 128:(c + 1) * 128]
    shift = CONV_HALO - (CONFORMER_WIDTH - 1)

    def rows(rc, _):
        base = pl.multiple_of(rc * CONV_ROWS, CONV_ROWS)
        for c in range(nslab):
            lanes = slice(c * 128, (c + 1) * 128)
            acc = jnp.zeros((CONV_ROWS, 128), F32) + cb_ref[:, lanes]
            for k in range(CONFORMER_WIDTH):
                acc = acc + cw_ref[k:k + 1, lanes] * buf_sc[c, pl.ds(base + (shift + k), CONV_ROWS), :]
            conv_sc[pl.ds(base, CONV_ROWS), lanes] = acc
        return 0

    lax.fori_loop(0, tm // CONV_ROWS, rows, 0)
    y = conv_sc[...]
    mu = jnp.mean(y, axis=-1, keepdims=True)
    yc = y - mu
    var = jnp.mean(yc * yc, axis=-1, keepdims=True)
    ln = yc * lax.rsqrt(var + NORM_EPS) * lg_ref[...] + lb_ref[...]
    o_ref[...] = (_dot(_silu(ln).astype(BF16), wpw_ref[...]) * szc_ref[...].astype(F32)).astype(BF16)


def _conformer(proj, conv_w, conv_b, ln_g, ln_b, w_pw, tm):
    seq = proj.shape[0]
    per = tm // CONV_HALO
    const = lambda a: pl.BlockSpec(a.shape, lambda i: (0, 0))
    return pl.pallas_call(
        functools.partial(_conformer_kernel, tm=tm),
        grid=(seq // tm,),
        in_specs=[pl.BlockSpec((tm, D_CONFORMER), lambda i: (i, 0)),
                  pl.BlockSpec((CONV_HALO, D_CONFORMER), lambda i: (jnp.maximum(i * per - 1, 0), 0)),
                  pl.BlockSpec((tm, D_CONFORMER), lambda i: (i, 1)),
                  const(conv_w), const(conv_b), const(ln_g), const(ln_b), const(w_pw)],
        out_specs=pl.BlockSpec((tm, D_CONFORMER), lambda i: (i, 0)),
        out_shape=jax.ShapeDtypeStruct((seq, D_CONFORMER), BF16),
        scratch_shapes=[pltpu.VMEM((D_CONFORMER // 128, tm + CONV_HALO, 128), F32),
                        pltpu.VMEM((tm, D_CONFORMER), F32)],
        compiler_params=_params(("arbitrary",)),
        name="conformer",
    )(proj, proj, proj, conv_w, conv_b, ln_g, ln_b, w_pw)


def _t5_bucket(dist):
    max_exact = REL_BUCKETS // 2
    d_f = jnp.maximum(dist, 1).astype(F32)
    large = max_exact + (jnp.log(d_f / max_exact) / math.log(REL_MAX_DISTANCE / max_exact)
                         * (REL_BUCKETS - max_exact)).astype(jnp.int32)
    large = jnp.minimum(large, REL_BUCKETS - 1)
    return jnp.where(dist < max_exact, dist, large)


def _attn_bias(rel_bias):
    q = ATTN_BLOCK
    nh = rel_bias.shape[1]
    kj = np.arange(2 * q)[None, None, :]
    neg = lambda n: jnp.full((nh, n), NEG_INF, F32)
    tabs = []
    for window, dilation in DILATED_PATTERNS:
        steps = window // dilation
        assert steps == q
        rel = jnp.asarray(np.arange(steps + 1) * dilation, dtype=jnp.int32)
        by_rel = rel_bias.astype(F32)[_t5_bucket(rel)].T
        g = jnp.concatenate([by_rel[:, ::-1], neg(2 * q - 1)], axis=1)
        flat = jnp.tile(g, (1, q))[:, :q * (3 * q - 1)]
        tab = flat.reshape(nh, q, 3 * q - 1)[:, :, :2 * q]
        if dilation == 1:
            tab = tab.reshape(nh, q // MERGE_DIL, MERGE_DIL, 2 * q).transpose(0, 2, 1, 3).reshape(nh, q, 2 * q)
        tabs.append(tab)
        tabs.append(jnp.where(kj >= q, tab, NEG_INF))
    return jnp.stack(tabs, axis=1)


def _attn_kernel(q_ref, k_ref, v_ref, szd_ref, bias_ref, o_ref,
                 qf, kf, vf, qs1, qs2, ks0, ks1, ks2, vs0, vs1, vs2,
                 o0, o1, o2, l0, l1, l2):
    i = pl.program_id(1)
    t = ATTN_TILE
    blk = ATTN_BLOCK
    first_tile = i == 0
    qf[...] = q_ref[...].astype(F32)
    kf[...] = k_ref[...].astype(F32)
    vf[...] = v_ref[...].astype(F32)
    qstreams = (None, qs1, qs2)
    kstreams = (ks0, ks1, ks2)
    vstreams = (vs0, vs1, vs2)
    for pidx, (_, dil) in enumerate(DILATED_PATTERNS):
        n = t // dil
        per = blk + n
        for r in range(dil):
            base = r * per
            for src_f, src_ref, dst in ((kf, k_ref, kstreams[pidx]), (vf, v_ref, vstreams[pidx])):
                tail = dst[base + n:base + per, :]
                dst[base:base + blk, :] = jnp.where(first_tile, jnp.zeros_like(tail), tail)
                if dil == 1:
                    dst[base + blk:base + per, :] = src_ref[...]
                else:
                    dst[base + blk:base + per, :] = src_f[pl.ds(r, n, stride=dil), :].astype(BF16)
            if dil > 1:
                qstreams[pidx][r * n:(r + 1) * n, :] = qf[pl.ds(r, n, stride=dil), :].astype(BF16)

    ones = jnp.ones((2 * blk, HEAD_DIM), BF16)
    first = first_tile.astype(jnp.int32)
    md = MERGE_DIL
    mspan = t // md

    def block(pidx, dil, r, sb, out_sc, lse_sc):
        n = t // dil
        per = blk + n
        if dil == 1:
            qb = jnp.concatenate([qf[pl.ds(sb * blk + c, blk // md, stride=md), :] for c in range(md)],
                                 axis=0).astype(BF16)
        else:
            qb = qstreams[pidx][r * n + sb * blk:r * n + (sb + 1) * blk, :]
        krows = slice(r * per + sb * blk, r * per + (sb + 2) * blk)
        kb = kstreams[pidx][krows, :]
        vb = vstreams[pidx][krows, :]
        s = lax.dot_general(qb, kb, _NT, preferred_element_type=F32)
        s = s + bias_ref[0, 2 * pidx + first if sb == 0 else 2 * pidx]
        mx = jnp.max(s, axis=-1, keepdims=True)
        p = jnp.exp(s - mx).astype(BF16)
        pv = _dot(p, jnp.concatenate([vb, ones], axis=1))
        den = pv[:, HEAD_DIM:]
        out = pv[:, :HEAD_DIM] / den
        lse = mx + jnp.log(den)
        if dil == 1:
            piece = blk // md
            for c in range(md):
                rows = slice(c * mspan + sb * piece, c * mspan + (sb + 1) * piece)
                out_sc[rows, :] = out[c * piece:(c + 1) * piece, :]
                lse_sc[rows, :] = lse[c * piece:(c + 1) * piece, :]
        elif dil == md:
            rows = slice(r * mspan + sb * blk, r * mspan + (sb + 1) * blk)
            out_sc[rows, :] = out
            lse_sc[rows, :] = lse
        else:
            step = dil // md
            rows = pl.ds((r % md) * mspan + sb * blk * step + r // md, blk, stride=step)
            out_sc[rows, :] = out
            lse_sc[rows, :] = lse

    scratch = ((o0, l0), (o1, l1), (o2, l2))
    for pidx, (_, dil) in enumerate(DILATED_PATTERNS):
        for sb in range(t // (blk * dil)):
            for r in range(dil):
                block(pidx, dil, r, sb, *scratch[pidx])

    la, lb, lc = l0[...], l1[...], l2[...]
    lall = jnp.maximum(jnp.maximum(la, lb), lc)
    wa, wb, wc = jnp.exp(la - lall), jnp.exp(lb - lall), jnp.exp(lc - lall)
    merged = (wa * o0[...] + wb * o1[...] + wc * o2[...]) / (wa + wb + wc)
    for c in range(md):
        qf[pl.ds(c, mspan, stride=md), :] = merged[c * mspan:(c + 1) * mspan, :]
    o_ref[...] = (qf[...] * szd_ref[...].astype(F32)).astype(BF16)


def _attention(proj, bias):
    seq = proj.shape[0]
    t = ATTN_TILE
    q0 = (2 * D_CONFORMER) // HEAD_DIM
    k0 = q0 + N_HEADS
    v0 = k0 + N_HEADS
    z0 = v0 + N_HEADS
    cur = lambda c0: pl.BlockSpec((t, HEAD_DIM), lambda h, i: (i, c0 + h))
    f32 = lambda rows: pltpu.VMEM((rows, HEAD_DIM), F32)
    bf16 = lambda rows: pltpu.VMEM((rows, HEAD_DIM), BF16)
    streams = [bf16(dil * (ATTN_BLOCK + t // dil)) for _, dil in DILATED_PATTERNS]
    return pl.pallas_call(
        _attn_kernel,
        grid=(N_HEADS, seq // t),
        in_specs=[cur(q0), cur(k0), cur(v0), cur(z0),
                  pl.BlockSpec((1,) + bias.shape[1:], lambda h, i: (h, 0, 0, 0))],
        out_specs=pl.BlockSpec((t, HEAD_DIM), lambda h, i: (i, h)),
        out_shape=jax.ShapeDtypeStruct((seq, D_ATTN), BF16),
        scratch_shapes=[f32(t)] * 3 + [bf16(t)] * 2 + streams + streams + [f32(t)] * 6,
        compiler_params=_params(("arbitrary", "arbitrary")),
        name="dilated_attn",
    )(proj, proj, proj, proj, bias)


def _odd_out_kernel(h_ref, yc_ref, yd_ref, wout_ref, g_ref, o_ref, *, tm):
    for r0 in range(0, tm, OUT_ROWS):
        rows = slice(r0, r0 + OUT_ROWS)
        acc = _dot(yc_ref[rows, :], wout_ref[0:D_CONFORMER, :])
        acc = acc + _dot(yd_ref[rows, :], wout_ref[D_CONFORMER:D_CONFORMER + D_ATTN, :])
        o_ref[rows, :] = _rms(h_ref[rows, :] + acc, g_ref[...])


def _odd_out(h, yc, yd, w_out, g, tm):
    seq = h.shape[0]
    row = lambda w: pl.BlockSpec((tm, w), lambda i: (i, 0))
    const = lambda a: pl.BlockSpec(a.shape, lambda i: (0, 0), pipeline_mode=pl.Buffered(1))
    return pl.pallas_call(
        functools.partial(_odd_out_kernel, tm=tm),
        grid=(seq // tm,),
        in_specs=[row(D_MODEL), row(D_CONFORMER), row(D_ATTN), const(w_out), const(g)],
        out_specs=row(D_MODEL),
        out_shape=jax.ShapeDtypeStruct((seq, D_MODEL), F32),
        compiler_params=_params(("arbitrary",)),
        name="odd_out",
    )(h, yc, yd, w_out, g)


def _row_tile(seq, want):
    return min(seq, want)


def kernel(x, norm_g, final_norm_g, ev_w_in, ev_conv_w, s5_lam_re, s5_lam_im, s5_log_dt, s5_b_re, s5_b_im, s5_c_re, s5_c_im, s5_d, s5_w_glu, ev_w_out, od_w_in, cf_conv_w, cf_conv_b, cf_ln_g, cf_ln_b, cf_w_pw, od_w_out, rel_bias):
    bsz, seq, _ = x.shape
    assert bsz == 1 and seq % ATTN_TILE == 0
    x2 = x.reshape(seq, D_MODEL)
    g0 = norm_g[0].astype(F32).reshape(1, D_MODEL)
    g1 = norm_g[1].astype(F32).reshape(1, D_MODEL)
    gf = final_norm_g.astype(F32).reshape(1, D_MODEL)

    ya, u, szb = _even_in(x2, g0, ev_w_in[0].astype(BF16), ev_conv_w[0].astype(F32), _row_tile(seq, 1024), 256)
    tables = _s5_tables(s5_lam_re[0], s5_lam_im[0], s5_log_dt[0], s5_b_re[0], s5_b_im[0],
                        s5_c_re[0], s5_c_im[0], s5_d[0])
    ys = _s5(u, tables, min(seq // SSM_CHUNK, 128))
    h1, hn1 = _even_out(x2, ya, ys, szb, s5_w_glu[0].astype(BF16), ev_w_out[0].astype(BF16), g1,
                        _row_tile(seq, 512))

    proj = _odd_in(hn1, od_w_in[0].astype(BF16), _row_tile(seq, 2048))
    conv_w = jnp.concatenate([cf_conv_w[0].astype(F32), jnp.zeros((1, D_CONFORMER), F32)], axis=0)
    yc = _conformer(proj, conv_w, cf_conv_b[0].astype(F32).reshape(1, -1),
                    cf_ln_g[0].astype(F32).reshape(1, -1), cf_ln_b[0].astype(F32).reshape(1, -1),
                    cf_w_pw[0].astype(BF16), _row_tile(seq, 512))
    yd = _attention(proj, _attn_bias(rel_bias))
    out = _odd_out(h1, yc, yd, od_w_out[0].astype(BF16), gf, _row_tile(seq, 512))
    return out.reshape(bsz, seq, D_MODEL)
```

```python
import functools
import math

import numpy as np
import jax
import jax.numpy as jnp
from jax import lax
from jax.experimental import pallas as pl
from jax.experimental.pallas import tpu as pltpu

F32 = jnp.float32
BF16 = jnp.bfloat16

D_MODEL = 2048
D_SHORTCONV = 1536
SHORTCONV_WIDTH = 3
D_SSM = 512
SSM_GROUP = 16
SSM_GROUPS = 32
SSM_STATE = 64
D_CONFORMER = 1024
CONFORMER_WIDTH = 31
D_ATTN = 1024
HEAD_DIM = 128
N_HEADS = 8
DILATED_PATTERNS = ((128, 1), (512, 4), (2048, 16))
ATTN_BLOCK = 128
REL_BUCKETS = 32
REL_MAX_DISTANCE = 2048
NORM_EPS = 1e-6
NEG_INF = -1e30

COL = 512
IN_ROWS = 512
SSM_CHUNK = 16
SSM_FLAT = SSM_CHUNK * SSM_GROUP
SSM_SLABS = D_SSM // 128
ATTN_TILE = 2048
LOG2E = math.log2(math.e)
Q_SCALE = HEAD_DIM ** -0.5 * LOG2E
MERGE_DIL = 4
CONV_HALO = 32
VMEM_LIMIT = 50 * 1024 * 1024


def _params(sem, vmem=VMEM_LIMIT):
    return pltpu.CompilerParams(dimension_semantics=sem, vmem_limit_bytes=vmem)


def _dot(a, b):
    return jnp.dot(a, b, preferred_element_type=F32)


def _sigmoid(z):
    return 1.0 / (1.0 + jnp.exp(-z))


def _silu(z):
    return z * _sigmoid(z)


def _gelu_tanh(x):
    return 0.5 * x * (1.0 + jnp.tanh(math.sqrt(2.0 / math.pi) * (x + 0.044715 * (x * x * x))))


def _rms(xf, g):
    ms = jnp.mean(xf * xf, axis=-1, keepdims=True)
    return xf * lax.rsqrt(ms + NORM_EPS) * g


def _even_in_kernel(x_ref, g_ref, wa_ref, wb_ref, wc_ref, wd_ref, cw_ref,
                    ya_ref, u_ref, szb_ref, hn_sc, pbuf_sc, carry_sc, *, tm, col):
    i = pl.program_id(0)
    j = pl.program_id(1)
    nconv = D_SHORTCONV // col

    @pl.when(j == 0)
    def _():
        hn_sc[...] = _rms(x_ref[...], g_ref[...]).astype(BF16)

    @pl.when(j < nconv)
    def _():
        @pl.when(i == 0)
        def _():
            carry_sc[j] = jnp.zeros((8, col), F32)

        pbuf_sc[0:8, :] = carry_sc[j]
        cw = cw_ref[...]
        for r0 in range(0, tm, IN_ROWS):
            hn = hn_sc[r0:r0 + IN_ROWS, :]
            p = _dot(hn, wb_ref[...]) * _dot(hn, wa_ref[...])
            pbuf_sc[8 + r0:8 + r0 + IN_ROWS, :] = p
            conv = (cw[2:3, :] * p + cw[1:2, :] * pbuf_sc[pl.ds(7 + r0, IN_ROWS), :]
                    + cw[0:1, :] * pbuf_sc[pl.ds(6 + r0, IN_ROWS), :])
            gb = _dot(hn, wc_ref[...])
            za = _dot(hn, wd_ref[...])
            ya_ref[r0:r0 + IN_ROWS, :] = (gb * conv * _silu(za)).astype(BF16)
        carry_sc[j] = pbuf_sc[tm:tm + 8, :]

    @pl.when(j >= nconv)
    def _():
        hn = hn_sc[...]
        u = _dot(hn, wa_ref[...])
        for s in range(col // 128):
            u_ref[s] = u[:, s * 128:(s + 1) * 128]
        szb_ref[...] = _silu(_dot(hn, wb_ref[...])).astype(BF16)


def _even_in(x, g, w_in, conv_w, tm, col):
    seq = x.shape[0]
    nconv = D_SHORTCONV // col
    nssm = D_SSM // col
    u_blk = 4 * nconv

    def wspec(k, alt0):
        return pl.BlockSpec((D_MODEL, col),
                            lambda i, j: (0, jnp.where(j < nconv, j + nconv * k, alt0 + j - nconv)))

    hold = lambda k: pl.BlockSpec((D_MODEL, col), lambda i, j: (0, jnp.minimum(j, nconv - 1) + nconv * k))
    ssm_step = lambda j: jnp.maximum(j - nconv, 0)
    return pl.pallas_call(
        functools.partial(_even_in_kernel, tm=tm, col=col),
        grid=(seq // tm, nconv + nssm),
        in_specs=[
            pl.BlockSpec((tm, D_MODEL), lambda i, j: (i, 0)),
            pl.BlockSpec((1, D_MODEL), lambda i, j: (0, 0)),
            wspec(0, u_blk), wspec(1, u_blk + nssm), hold(2), hold(3),
            pl.BlockSpec((SHORTCONV_WIDTH, col), lambda i, j: (0, jnp.minimum(j, nconv - 1))),
        ],
        out_specs=[
            pl.BlockSpec((tm, col), lambda i, j: (i, jnp.minimum(j, nconv - 1))),
            pl.BlockSpec((col // 128, tm, 128), lambda i, j: (ssm_step(j), i, 0)),
            pl.BlockSpec((tm, col), lambda i, j: (i, ssm_step(j))),
        ],
        out_shape=[
            jax.ShapeDtypeStruct((seq, D_SHORTCONV), BF16),
            jax.ShapeDtypeStruct((SSM_SLABS, seq, 128), F32),
            jax.ShapeDtypeStruct((seq, D_SSM), BF16),
        ],
        scratch_shapes=[
            pltpu.VMEM((tm, D_MODEL), BF16),
            pltpu.VMEM((tm + 8, col), F32),
            pltpu.VMEM((nconv, 8, col), F32),
        ],
        compiler_params=_params(("arbitrary", "arbitrary")),
        name="even_in",
    )(x, g, w_in, w_in, w_in, w_in, conv_w)


def _s5_tables(lam_re, lam_im, log_dt, b_re, b_im, c_re, c_im, d_skip):
    hi = lax.Precision.HIGH
    t_, g_, n_, p_ = SSM_CHUNK, SSM_GROUPS, SSM_STATE, SSM_GROUP
    lam_re, lam_im = lam_re.astype(F32), lam_im.astype(F32)
    dt = jnp.exp(log_dt.astype(F32))[:, None]
    xr, xi = lam_re * dt, lam_im * dt
    k = jnp.arange(t_ + 1, dtype=F32)[:, None, None]
    mag = jnp.exp(xr * k)
    pr, pi = mag * jnp.cos(xi * k), mag * jnp.sin(xi * k)
    nr, ni = pr[1] - 1.0, pi[1]
    den = lam_re * lam_re + lam_im * lam_im
    fr = (nr * lam_re + ni * lam_im) / den
    fi = (ni * lam_re - nr * lam_im) / den
    b_re, b_im = b_re.astype(F32), b_im.astype(F32)
    bbr = fr[..., None] * b_re - fi[..., None] * b_im
    bbi = fr[..., None] * b_im + fi[..., None] * b_re
    c_re, c_im = c_re.astype(F32), c_im.astype(F32)

    cpr = c_re[None] * pr[:t_, :, None, :] - c_im[None] * pi[:t_, :, None, :]
    cpi = c_re[None] * pi[:t_, :, None, :] + c_im[None] * pr[:t_, :, None, :]
    kern = jnp.einsum('tgpn,gnq->tgpq', jnp.concatenate([cpr, -cpi], axis=-1),
                      jnp.concatenate([bbr, bbi], axis=1), precision=hi)
    skip = d_skip.astype(F32).reshape(g_, p_)[:, :, None] * jnp.eye(p_, dtype=F32)[None]
    kern = kern.at[0].add(skip)
    flat = t_ * p_
    last = jnp.transpose(kern[::-1], (1, 2, 0, 3)).reshape(g_, p_, flat)
    mt = jnp.stack([jnp.pad(last[:, :, (t_ - 1 - t) * p_:], ((0, 0), (0, 0), (0, (t_ - 1 - t) * p_)))
                    for t in range(t_)], axis=1).reshape(g_, flat, flat)

    prj, pij = pr[t_ - 1::-1][:t_], pi[t_ - 1::-1][:t_]
    wr = prj[..., None] * bbr[None] - pij[..., None] * bbi[None]
    wi = prj[..., None] * bbi[None] + pij[..., None] * bbr[None]
    half = (np.arange(2) == 0)[None, None, :, None, None]

    def pack_w(w):
        w = jnp.transpose(w.reshape(t_, g_ // 2, 2, n_, p_), (1, 0, 2, 4, 3))
        w = jnp.concatenate([jnp.where(half, w, 0.0), jnp.where(half, 0.0, w)], axis=-1)
        return w.reshape(g_ // 2, 2 * flat, 2 * n_)

    vr = c_re[None] * pr[1:, :, None, :] - c_im[None] * pi[1:, :, None, :]
    vi = c_re[None] * pi[1:, :, None, :] + c_im[None] * pr[1:, :, None, :]
    vr = jnp.transpose(vr, (1, 0, 2, 3)).reshape(g_ // 2, 2, flat, n_)
    vi = -jnp.transpose(vi, (1, 0, 2, 3)).reshape(g_ // 2, 2, flat, n_)
    zv = jnp.zeros((g_ // 2, flat, n_), F32)
    v_even = jnp.concatenate([vr[:, 0], zv, vi[:, 0], zv], axis=-1)
    v_odd = jnp.concatenate([zv, vr[:, 1], zv, vi[:, 1]], axis=-1)
    vt = jnp.stack([v_even, v_odd], axis=1).reshape(g_, flat, 4 * n_)
    a_re = pr[t_].reshape(1, g_ * n_)
    a_im = pi[t_].reshape(1, g_ * n_)
    return mt.astype(BF16), pack_w(wr).astype(BF16), pack_w(wi).astype(BF16), vt.astype(BF16), a_re, a_im


_NT = (((1,), (1,)), ((), ()))
_TN = (((0,), (0,)), ((), ()))


def _s5_kernel(u_ref, mt_ref, wre_ref, wim_ref, vt_ref, are_ref, aim_ref, y_ref,
               xt_sc, yt_sc, zre_sc, zim_sc, carry_sc, *, ct):
    i = pl.program_id(0)
    t_ = SSM_CHUNK

    @pl.when(i == 0)
    def _():
        carry_sc[...] = jnp.zeros_like(carry_sc)

    eye = jnp.where(lax.broadcasted_iota(jnp.int32, (128, 128), 0)
                    == lax.broadcasted_iota(jnp.int32, (128, 128), 1), 1.0, 0.0).astype(BF16)

    for s in range(SSM_SLABS):
        for j in range(t_):
            uj = u_ref[s, pl.ds(j, ct, stride=t_), :].astype(BF16)
            xt_sc[s, j] = lax.dot_general(eye, uj, _NT, preferred_element_type=F32).astype(BF16)

    pairs_per_slab = 128 // (2 * SSM_GROUP)
    for gp in range(SSM_GROUPS // 2):
        s, k = gp // pairs_per_slab, gp % pairs_per_slab
        xp = xt_sc[s, :, 32 * k:32 * k + 32, :].reshape(2 * SSM_FLAT, ct)
        zre_sc[:, gp * 128:(gp + 1) * 128] = lax.dot_general(xp, wre_ref[gp], _TN, preferred_element_type=F32)
        zim_sc[:, gp * 128:(gp + 1) * 128] = lax.dot_general(xp, wim_ref[gp], _TN, preferred_element_type=F32)

    are = are_ref[...]
    aim = aim_ref[...]

    def step(r, carry):
        sre, sim = carry
        zr = zre_sc[pl.ds(r, 1), :]
        zi = zim_sc[pl.ds(r, 1), :]
        zre_sc[pl.ds(r, 1), :] = sre
        zim_sc[pl.ds(r, 1), :] = sim
        return are * sre - aim * sim + zr, are * sim + aim * sre + zi

    sre, sim = lax.fori_loop(0, ct, step, (carry_sc[0:1, :], carry_sc[1:2, :]), unroll=8)
    carry_sc[0:1, :] = sre
    carry_sc[1:2, :] = sim

    groups_per_slab = 128 // SSM_GROUP
    for g in range(SSM_GROUPS):
        gp, s, k = g // 2, g // groups_per_slab, g % groups_per_slab
        xg = xt_sc[s, :, 16 * k:16 * k + 16, :].reshape(SSM_FLAT, ct)
        sp = jnp.concatenate([zre_sc[:, gp * 128:(gp + 1) * 128],
                              zim_sc[:, gp * 128:(gp + 1) * 128]], axis=1).astype(BF16)
        yt = _dot(mt_ref[g], xg) + lax.dot_general(vt_ref[g], sp, _NT, preferred_element_type=F32)
        yt_sc[s, :, 16 * k:16 * k + 16, :] = yt.reshape(t_, SSM_GROUP, ct)

    for s in range(SSM_SLABS):
        for t in range(t_):
            yt = yt_sc[s, t]
            hi = yt.astype(BF16)
            lo = (yt - hi.astype(F32)).astype(BF16)
            y = (lax.dot_general(hi, eye, _TN, preferred_element_type=F32)
                 + lax.dot_general(lo, eye, _TN, preferred_element_type=F32))
            y_ref[s, pl.ds(t, ct, stride=t_), :] = y


def _s5(u4, tables, ct):
    mt, wre, wim, vt, are, aim = tables
    _, seq, _ = u4.shape
    full = lambda a: pl.BlockSpec(a.shape, lambda i: (0,) * a.ndim)
    rows = pl.BlockSpec((SSM_SLABS, SSM_CHUNK * ct, 128), lambda i: (0, i, 0))
    nstate = SSM_GROUPS * SSM_STATE
    return pl.pallas_call(
        functools.partial(_s5_kernel, ct=ct),
        grid=(seq // (SSM_CHUNK * ct),),
        in_specs=[rows, full(mt), full(wre), full(wim), full(vt), full(are), full(aim)],
        out_specs=rows,
        out_shape=jax.ShapeDtypeStruct(u4.shape, F32),
        scratch_shapes=[pltpu.VMEM((SSM_SLABS, SSM_CHUNK, 128, ct), BF16),
                        pltpu.VMEM((SSM_SLABS, SSM_CHUNK, 128, ct), F32),
                        pltpu.VMEM((ct, nstate), F32),
                        pltpu.VMEM((ct, nstate), F32),
                        pltpu.VMEM((8, nstate), F32)],
        compiler_params=_params(("arbitrary",)),
        name="s5_ssm",
    )(u4, mt, wre, wim, vt, are, aim)


OUT_ROWS = 256


def _even_out_kernel(x_ref, ya_ref, ys_ref, szb_ref, wglu_ref, wout_ref, g_ref, h_ref, hn_ref, *, tm):
    for r0 in range(0, tm, OUT_ROWS):
        rows = slice(r0, r0 + OUT_ROWS)
        gl = _gelu_tanh(jnp.concatenate([ys_ref[s, rows, :] for s in range(SSM_SLABS)], axis=1))
        yb = gl * _sigmoid(_dot(gl.astype(BF16), wglu_ref[...])) * szb_ref[rows, :].astype(F32)
        acc = _dot(ya_ref[rows, :], wout_ref[0:D_SHORTCONV, :])
        acc = acc + _dot(yb.astype(BF16), wout_ref[D_SHORTCONV:D_SHORTCONV + D_SSM, :])
        h = x_ref[rows, :] + acc
        h_ref[rows, :] = h
        hn_ref[rows, :] = _rms(h, g_ref[...]).astype(BF16)


def _even_out(x, ya, ys, szb, w_glu, w_out, g, tm):
    seq = x.shape[0]
    row = lambda w: pl.BlockSpec((tm, w), lambda i: (i, 0))
    const = lambda a: pl.BlockSpec(a.shape, lambda i: (0, 0), pipeline_mode=pl.Buffered(1))
    return pl.pallas_call(
        functools.partial(_even_out_kernel, tm=tm),
        grid=(seq // tm,),
        in_specs=[row(D_MODEL), row(D_SHORTCONV),
                  pl.BlockSpec((SSM_SLABS, tm, 128), lambda i: (0, i, 0)), row(D_SSM),
                  const(w_glu), const(w_out), const(g)],
        out_specs=[row(D_MODEL), row(D_MODEL)],
        out_shape=[jax.ShapeDtypeStruct((seq, D_MODEL), F32),
                   jax.ShapeDtypeStruct((seq, D_MODEL), BF16)],
        compiler_params=_params(("arbitrary",)),
        name="even_out",
    )(x, ya, ys, szb, w_glu, w_out, g)


ODD_STEPS = 12
_SILU_STEPS = (2, 3, 10, 11)
_Q_STEPS = (4, 5)


def _odd_in_kernel(hn_ref, wa_ref, wb_ref, o_ref, *, tm):
    s = pl.program_id(1)
    is_silu = (s == 2) | (s == 3) | (s == 10) | (s == 11)
    blocks = [slice(r0, r0 + IN_ROWS) for r0 in range(0, tm, IN_ROWS)]

    @pl.when(s < 2)
    def _():
        for rows in blocks:
            hn = hn_ref[rows, :]
            o_ref[rows, :] = (_dot(hn, wa_ref[...]) * _sigmoid(_dot(hn, wb_ref[...]))).astype(BF16)

    @pl.when(is_silu)
    def _():
        for rows in blocks:
            o_ref[rows, :] = _silu(_dot(hn_ref[rows, :], wa_ref[...])).astype(BF16)

    @pl.when((s >= 2) & jnp.logical_not(is_silu))
    def _():
        scale = jnp.where((s == 4) | (s == 5), Q_SCALE, 1.0).astype(F32)
        for rows in blocks:
            o_ref[rows, :] = (_dot(hn_ref[rows, :], wa_ref[...]) * scale).astype(BF16)


def _odd_in(hn, w_in, tm):
    seq = hn.shape[0]
    return pl.pallas_call(
        functools.partial(_odd_in_kernel, tm=tm),
        grid=(seq // tm, ODD_STEPS),
        in_specs=[pl.BlockSpec((tm, D_MODEL), lambda i, s: (i, 0)),
                  pl.BlockSpec((D_MODEL, COL), lambda i, s: (0, jnp.where(s < 2, s, s + 2))),
                  pl.BlockSpec((D_MODEL, COL), lambda i, s: (0, jnp.where(s < 2, s + 2, 3)))],
        out_specs=pl.BlockSpec((tm, COL), lambda i, s: (i, s)),
        out_shape=jax.ShapeDtypeStruct((seq, ODD_STEPS * COL), BF16),
        compiler_params=_params(("arbitrary", "arbitrary")),
        name="odd_in",
    )(hn, w_in, w_in)


CONV_ROWS = 64


def _conformer_kernel(cur_ref, halo_ref, szc_ref, cw_ref, cb_ref, lg_ref, lb_ref, wpw_ref,
                      o_ref, buf_sc, conv_sc, *, tm):
    i = pl.program_id(0)
    halo = jnp.where(i == 0, 0.0, halo_ref[...].astype(F32))
    cur = cur_ref[...].astype(F32)
    nslab = D_CONFORMER // 128
    for c in range(nslab):
        buf_sc[c, 0:CONV_HALO, :] = halo[:, c * 128:(c + 1) * 128]
        buf_sc[c, CONV_HALO:CONV_HALO + tm, :] = cur[:, c * 128:(c + 1) * 128]
    shift = CONV_HALO - (CONFORMER_WIDTH - 1)

    def rows(rc, _):
        base = pl.multiple_of(rc * CONV_ROWS, CONV_ROWS)
        for c in range(nslab):
            lanes = slice(c * 128, (c + 1) * 128)
            acc = jnp.zeros((CONV_ROWS, 128), F32) + cb_ref[:, lanes]
            for k in range(CONFORMER_WIDTH):
                acc = acc + cw_ref[k:k + 1, lanes] * buf_sc[c, pl.ds(base + (shift + k), CONV_ROWS), :]
            conv_sc[pl.ds(base, CONV_ROWS), lanes] = acc
        return 0

    lax.fori_loop(0, tm // CONV_ROWS, rows, 0)
    y = conv_sc[...]
    mu = jnp.mean(y, axis=-1, keepdims=True)
    yc = y - mu
    var = jnp.mean(yc * yc, axis=-1, keepdims=True)
    ln = yc * lax.rsqrt(var + NORM_EPS) * lg_ref[...] + lb_ref[...]
    o_ref[...] = (_dot(_silu(ln).astype(BF16), wpw_ref[...]) * szc_ref[...].astype(F32)).astype(BF16)


def _conformer(proj, conv_w, conv_b, ln_g, ln_b, w_pw, tm):
    seq = proj.shape[0]
    per = tm // CONV_HALO
    const = lambda a: pl.BlockSpec(a.shape, lambda i: (0, 0))
    return pl.pallas_call(
        functools.partial(_conformer_kernel, tm=tm),
        grid=(seq // tm,),
        in_specs=[pl.BlockSpec((tm, D_CONFORMER), lambda i: (i, 0)),
                  pl.BlockSpec((CONV_HALO, D_CONFORMER), lambda i: (jnp.maximum(i * per - 1, 0), 0)),
                  pl.BlockSpec((tm, D_CONFORMER), lambda i: (i, 1)),
                  const(conv_w), const(conv_b), const(ln_g), const(ln_b), const(w_pw)],
        out_specs=pl.BlockSpec((tm, D_CONFORMER), lambda i: (i, 0)),
        out_shape=jax.ShapeDtypeStruct((seq, D_CONFORMER), BF16),
        scratch_shapes=[pltpu.VMEM((D_CONFORMER // 128, tm + CONV_HALO, 128), F32),
                        pltpu.VMEM((tm, D_CONFORMER), F32)],
        compiler_params=_params(("arbitrary",)),
        name="conformer",
    )(proj, proj, proj, conv_w, conv_b, ln_g, ln_b, w_pw)


def _t5_bucket(dist):
    max_exact = REL_BUCKETS // 2
    d_f = jnp.maximum(dist, 1).astype(F32)
    large = max_exact + (jnp.log(d_f / max_exact) / math.log(REL_MAX_DISTANCE / max_exact)
                         * (REL_BUCKETS - max_exact)).astype(jnp.int32)
    large = jnp.minimum(large, REL_BUCKETS - 1)
    return jnp.where(dist < max_exact, dist, large)


def _attn_bias(rel_bias):
    q = ATTN_BLOCK
    nh = rel_bias.shape[1]
    kj = np.arange(2 * q)[None, None, :]
    neg = lambda n: jnp.full((nh, n), NEG_INF, F32)
    tabs = []
    for window, dilation in DILATED_PATTERNS:
        steps = window // dilation
        assert steps == q
        rel = jnp.asarray(np.arange(steps + 1) * dilation, dtype=jnp.int32)
        by_rel = rel_bias.astype(F32)[_t5_bucket(rel)].T
        g = jnp.concatenate([by_rel[:, ::-1], neg(2 * q - 1)], axis=1)
        flat = jnp.tile(g, (1, q))[:, :q * (3 * q - 1)]
        tab = flat.reshape(nh, q, 3 * q - 1)[:, :, :2 * q]
        if dilation == 1:
            tab = tab.reshape(nh, q // MERGE_DIL, MERGE_DIL, 2 * q).transpose(0, 2, 1, 3).reshape(nh, q, 2 * q)
        tab = tab * LOG2E
        tabs.append(tab)
        tabs.append(jnp.where(kj >= q, tab, NEG_INF))
    return jnp.stack(tabs, axis=1)


def _attn_kernel(q_ref, k_ref, v_ref, szd_ref, bias_ref, o_ref,
                 qf, kf, vf, qs1, qs2, ks0, ks1, ks2, vs0, vs1, vs2,
                 o0, o1, o2, l0, l1, l2):
    i = pl.program_id(1)
    t = ATTN_TILE
    blk = ATTN_BLOCK
    first_tile = i == 0
    qf[...] = q_ref[...].astype(F32)
    kf[...] = k_ref[...].astype(F32)
    vf[...] = v_ref[...].astype(F32)
    qstreams = (None, qs1, qs2)
    kstreams = (ks0, ks1, ks2)
    vstreams = (vs0, vs1, vs2)
    for pidx, (_, dil) in enumerate(DILATED_PATTERNS):
        n = t // dil
        per = blk + n
        for r in range(dil):
            base = r * per
            for src_f, src_ref, dst in ((kf, k_ref, kstreams[pidx]), (vf, v_ref, vstreams[pidx])):
                tail = dst[base + n:base + per, :]
                dst[base:base + blk, :] = jnp.where(first_tile, jnp.zeros_like(tail), tail)
                if dil == 1:
                    dst[base + blk:base + per, :] = src_ref[...]
                else:
                    dst[base + blk:base + per, :] = src_f[pl.ds(r, n, stride=dil), :].astype(BF16)
            if dil > 1:
                qstreams[pidx][r * n:(r + 1) * n, :] = qf[pl.ds(r, n, stride=dil), :].astype(BF16)

    ones = jnp.ones((2 * blk, HEAD_DIM), BF16)
    first = first_tile.astype(jnp.int32)
    md = MERGE_DIL
    mspan = t // md

    def block(pidx, dil, r, sb, out_sc, lse_sc):
        n = t // dil
        per = blk + n
        if dil == 1:
            qb = jnp.concatenate([qf[pl.ds(sb * blk + c, blk // md, stride=md), :] for c in range(md)],
                                 axis=0).astype(BF16)
        else:
            qb = qstreams[pidx][r * n + sb * blk:r * n + (sb + 1) * blk, :]
        krows = slice(r * per + sb * blk, r * per + (sb + 2) * blk)
        kb = kstreams[pidx][krows, :]
        vb = vstreams[pidx][krows, :]
        s = lax.dot_general(qb, kb, _NT, preferred_element_type=F32)
        s = s + bias_ref[0, 2 * pidx + first if sb == 0 else 2 * pidx]
        mx = jnp.max(s, axis=-1, keepdims=True)
        p = jnp.exp2(s - mx).astype(BF16)
        pv = _dot(p, jnp.concatenate([vb, ones], axis=1))
        den = pv[:, HEAD_DIM:]
        out = pv[:, :HEAD_DIM] / den
        lse = mx + jnp.log2(den)
        if dil == 1:
            piece = blk // md
            for c in range(md):
                rows = slice(c * mspan + sb * piece, c * mspan + (sb + 1) * piece)
                out_sc[rows, :] = out[c * piece:(c + 1) * piece, :]
                lse_sc[rows, :] = lse[c * piece:(c + 1) * piece, :]
        elif dil == md:
            rows = slice(r * mspan + sb * blk, r * mspan + (sb + 1) * blk)
            out_sc[rows, :] = out
            lse_sc[rows, :] = lse
        else:
            step = dil // md
            rows = pl.ds((r % md) * mspan + sb * blk * step + r // md, blk, stride=step)
            out_sc[rows, :] = out
            lse_sc[rows, :] = lse

    scratch = ((o0, l0), (o1, l1), (o2, l2))
    for pidx, (_, dil) in enumerate(DILATED_PATTERNS):
        for sb in range(t // (blk * dil)):
            for r in range(dil):
                block(pidx, dil, r, sb, *scratch[pidx])

    la, lb, lc = l0[...], l1[...], l2[...]
    lall = jnp.maximum(jnp.maximum(la, lb), lc)
    wa, wb, wc = jnp.exp2(la - lall), jnp.exp2(lb - lall), jnp.exp2(lc - lall)
    merged = (wa * o0[...] + wb * o1[...] + wc * o2[...]) / (wa + wb + wc)
    for c in range(md):
        qf[pl.ds(c, mspan, stride=md), :] = merged[c * mspan:(c + 1) * mspan, :]
    o_ref[...] = (qf[...] * szd_ref[...].astype(F32)).astype(BF16)


def _attention(proj, bias):
    seq = proj.shape[0]
    t = ATTN_TILE
    q0 = (2 * D_CONFORMER) // HEAD_DIM
    k0 = q0 + N_HEADS
    v0 = k0 + N_HEADS
    z0 = v0 + N_HEADS
    cur = lambda c0: pl.BlockSpec((t, HEAD_DIM), lambda h, i: (i, c0 + h))
    f32 = lambda rows: pltpu.VMEM((rows, HEAD_DIM), F32)
    bf16 = lambda rows: pltpu.VMEM((rows, HEAD_DIM), BF16)
    streams = [bf16(dil * (ATTN_BLOCK + t // dil)) for _, dil in DILATED_PATTERNS]
    return pl.pallas_call(
        _attn_kernel,
        grid=(N_HEADS, seq // t),
        in_specs=[cur(q0), cur(k0), cur(v0), cur(z0),
                  pl.BlockSpec((1,) + bias.shape[1:], lambda h, i: (h, 0, 0, 0))],
        out_specs=pl.BlockSpec((t, HEAD_DIM), lambda h, i: (i, h)),
        out_shape=jax.ShapeDtypeStruct((seq, D_ATTN), BF16),
        scratch_shapes=[f32(t)] * 3 + [bf16(t)] * 2 + streams + streams + [f32(t)] * 6,
        compiler_params=_params(("arbitrary", "arbitrary")),
        name="dilated_attn",
    )(proj, proj, proj, proj, bias)


def _odd_out_kernel(h_ref, yc_ref, yd_ref, wout_ref, g_ref, o_ref, *, tm):
    for r0 in range(0, tm, OUT_ROWS):
        rows = slice(r0, r0 + OUT_ROWS)
        acc = _dot(yc_ref[rows, :], wout_ref[0:D_CONFORMER, :])
        acc = acc + _dot(yd_ref[rows, :], wout_ref[D_CONFORMER:D_CONFORMER + D_ATTN, :])
        o_ref[rows, :] = _rms(h_ref[rows, :] + acc, g_ref[...])


def _odd_out(h, yc, yd, w_out, g, tm):
    seq = h.shape[0]
    row = lambda w: pl.BlockSpec((tm, w), lambda i: (i, 0))
    const = lambda a: pl.BlockSpec(a.shape, lambda i: (0, 0), pipeline_mode=pl.Buffered(1))
    return pl.pallas_call(
        functools.partial(_odd_out_kernel, tm=tm),
        grid=(seq // tm,),
        in_specs=[row(D_MODEL), row(D_CONFORMER), row(D_ATTN), const(w_out), const(g)],
        out_specs=row(D_MODEL),
        out_shape=jax.ShapeDtypeStruct((seq, D_MODEL), F32),
        compiler_params=_params(("arbitrary",)),
        name="odd_out",
    )(h, yc, yd, w_out, g)


def _row_tile(seq, want):
    return min(seq, want)


def kernel(x, norm_g, final_norm_g, ev_w_in, ev_conv_w, s5_lam_re, s5_lam_im, s5_log_dt, s5_b_re, s5_b_im, s5_c_re, s5_c_im, s5_d, s5_w_glu, ev_w_out, od_w_in, cf_conv_w, cf_conv_b, cf_ln_g, cf_ln_b, cf_w_pw, od_w_out, rel_bias):
    bsz, seq, _ = x.shape
    assert bsz == 1 and seq % ATTN_TILE == 0
    x2 = x.reshape(seq, D_MODEL)
    g0 = norm_g[0].astype(F32).reshape(1, D_MODEL)
    g1 = norm_g[1].astype(F32).reshape(1, D_MODEL)
    gf = final_norm_g.astype(F32).reshape(1, D_MODEL)

    ya, u, szb = _even_in(x2, g0, ev_w_in[0].astype(BF16), ev_conv_w[0].astype(F32), _row_tile(seq, 1024), 256)
    tables = _s5_tables(s5_lam_re[0], s5_lam_im[0], s5_log_dt[0], s5_b_re[0], s5_b_im[0],
                        s5_c_re[0], s5_c_im[0], s5_d[0])
    ys = _s5(u, tables, min(seq // SSM_CHUNK, 128))
    h1, hn1 = _even_out(x2, ya, ys, szb, s5_w_glu[0].astype(BF16), ev_w_out[0].astype(BF16), g1,
                        _row_tile(seq, 512))

    proj = _odd_in(hn1, od_w_in[0].astype(BF16), _row_tile(seq, 2048))
    conv_w = jnp.concatenate([cf_conv_w[0].astype(F32), jnp.zeros((1, D_CONFORMER), F32)], axis=0)
    yc = _conformer(proj, conv_w, cf_conv_b[0].astype(F32).reshape(1, -1),
                    cf_ln_g[0].astype(F32).reshape(1, -1), cf_ln_b[0].astype(F32).reshape(1, -1),
                    cf_w_pw[0].astype(BF16), _row_tile(seq, 512))
    yd = _attention(proj, _attn_bias(rel_bias))
    out = _odd_out(h1, yc, yd, od_w_out[0].astype(BF16), gf, _row_tile(seq, 512))
    return out.reshape(bsz, seq, D_MODEL)
```

```python
import functools
import math

import numpy as np
import jax
import jax.numpy as jnp
from jax import lax
from jax.experimental import pallas as pl
from jax.experimental.pallas import tpu as pltpu

F32 = jnp.float32
BF16 = jnp.bfloat16

D_MODEL = 2048
D_SHORTCONV = 1536
SHORTCONV_WIDTH = 3
D_SSM = 512
SSM_GROUP = 16
SSM_GROUPS = 32
SSM_STATE = 64
D_CONFORMER = 1024
CONFORMER_WIDTH = 31
D_ATTN = 1024
HEAD_DIM = 128
N_HEADS = 8
DILATED_PATTERNS = ((128, 1), (512, 4), (2048, 16))
ATTN_BLOCK = 128
REL_BUCKETS = 32
REL_MAX_DISTANCE = 2048
NORM_EPS = 1e-6
NEG_INF = -1e30

COL = 512
IN_ROWS = 512
SSM_CHUNK = 16
SSM_FLAT = SSM_CHUNK * SSM_GROUP
SSM_SLABS = D_SSM // 128
ATTN_TILE = 2048
LOG2E = math.log2(math.e)
Q_SCALE = HEAD_DIM ** -0.5 * LOG2E
MERGE_DIL = 4
CONV_HALO = 32
VMEM_LIMIT = 50 * 1024 * 1024


def _params(sem, vmem=VMEM_LIMIT):
    return pltpu.CompilerParams(dimension_semantics=sem, vmem_limit_bytes=vmem)


def _dot(a, b):
    return jnp.dot(a, b, preferred_element_type=F32)


def _sigmoid(z):
    return 1.0 / (1.0 + jnp.exp(-z))


def _silu(z):
    return z * _sigmoid(z)


def _gelu_tanh(x):
    return 0.5 * x * (1.0 + jnp.tanh(math.sqrt(2.0 / math.pi) * (x + 0.044715 * (x * x * x))))


def _rms(xf, g):
    ms = jnp.mean(xf * xf, axis=-1, keepdims=True)
    return xf * lax.rsqrt(ms + NORM_EPS) * g


def _even_in_kernel(x_ref, g_ref, wa_ref, wb_ref, wc_ref, wd_ref, cw_ref,
                    ya_ref, u_ref, szb_ref, hn_sc, pbuf_sc, carry_sc, *, tm, col):
    i = pl.program_id(0)
    j = pl.program_id(1)
    nconv = D_SHORTCONV // col

    @pl.when(j == 0)
    def _():
        hn_sc[...] = _rms(x_ref[...], g_ref[...]).astype(BF16)

    @pl.when(j < nconv)
    def _():
        @pl.when(i == 0)
        def _():
            carry_sc[j] = jnp.zeros((8, col), F32)

        pbuf_sc[0:8, :] = carry_sc[j]
        cw = cw_ref[...]
        for r0 in range(0, tm, IN_ROWS):
            hn = hn_sc[r0:r0 + IN_ROWS, :]
            p = _dot(hn, wb_ref[...]) * _dot(hn, wa_ref[...])
            pbuf_sc[8 + r0:8 + r0 + IN_ROWS, :] = p
            conv = (cw[2:3, :] * p + cw[1:2, :] * pbuf_sc[pl.ds(7 + r0, IN_ROWS), :]
                    + cw[0:1, :] * pbuf_sc[pl.ds(6 + r0, IN_ROWS), :])
            gb = _dot(hn, wc_ref[...])
            za = _dot(hn, wd_ref[...])
            ya_ref[r0:r0 + IN_ROWS, :] = (gb * conv * _silu(za)).astype(BF16)
        carry_sc[j] = pbuf_sc[tm:tm + 8, :]

    @pl.when(j >= nconv)
    def _():
        hn = hn_sc[...]
        u = _dot(hn, wa_ref[...])
        for s in range(col // 128):
            u_ref[s] = u[:, s * 128:(s + 1) * 128]
        szb_ref[...] = _silu(_dot(hn, wb_ref[...])).astype(BF16)


def _even_in(x, g, w_in, conv_w, tm, col):
    seq = x.shape[0]
    nconv = D_SHORTCONV // col
    nssm = D_SSM // col
    u_blk = 4 * nconv

    def wspec(k, alt0):
        return pl.BlockSpec((D_MODEL, col),
                            lambda i, j: (0, jnp.where(j < nconv, j + nconv * k, alt0 + j - nconv)))

    hold = lambda k: pl.BlockSpec((D_MODEL, col), lambda i, j: (0, jnp.minimum(j, nconv - 1) + nconv * k))
    ssm_step = lambda j: jnp.maximum(j - nconv, 0)
    return pl.pallas_call(
        functools.partial(_even_in_kernel, tm=tm, col=col),
        grid=(seq // tm, nconv + nssm),
        in_specs=[
            pl.BlockSpec((tm, D_MODEL), lambda i, j: (i, 0)),
            pl.BlockSpec((1, D_MODEL), lambda i, j: (0, 0)),
            wspec(0, u_blk), wspec(1, u_blk + nssm), hold(2), hold(3),
            pl.BlockSpec((SHORTCONV_WIDTH, col), lambda i, j: (0, jnp.minimum(j, nconv - 1))),
        ],
        out_specs=[
            pl.BlockSpec((tm, col), lambda i, j: (i, jnp.minimum(j, nconv - 1))),
            pl.BlockSpec((col // 128, tm, 128), lambda i, j: (ssm_step(j), i, 0)),
            pl.BlockSpec((tm, col), lambda i, j: (i, ssm_step(j))),
        ],
        out_shape=[
            jax.ShapeDtypeStruct((seq, D_SHORTCONV), BF16),
            jax.ShapeDtypeStruct((SSM_SLABS, seq, 128), F32),
            jax.ShapeDtypeStruct((seq, D_SSM), BF16),
        ],
        scratch_shapes=[
            pltpu.VMEM((tm, D_MODEL), BF16),
            pltpu.VMEM((tm + 8, col), F32),
            pltpu.VMEM((nconv, 8, col), F32),
        ],
        compiler_params=_params(("arbitrary", "arbitrary")),
        name="even_in",
    )(x, g, w_in, w_in, w_in, w_in, conv_w)


def _s5_tables(lam_re, lam_im, log_dt, b_re, b_im, c_re, c_im, d_skip):
    hi = lax.Precision.HIGH
    t_, g_, n_, p_ = SSM_CHUNK, SSM_GROUPS, SSM_STATE, SSM_GROUP
    lam_re, lam_im = lam_re.astype(F32), lam_im.astype(F32)
    dt = jnp.exp(log_dt.astype(F32))[:, None]
    xr, xi = lam_re * dt, lam_im * dt
    k = jnp.arange(t_ + 1, dtype=F32)[:, None, None]
    mag = jnp.exp(xr * k)
    pr, pi = mag * jnp.cos(xi * k), mag * jnp.sin(xi * k)
    nr, ni = pr[1] - 1.0, pi[1]
    den = lam_re * lam_re + lam_im * lam_im
    fr = (nr * lam_re + ni * lam_im) / den
    fi = (ni * lam_re - nr * lam_im) / den
    b_re, b_im = b_re.astype(F32), b_im.astype(F32)
    bbr = fr[..., None] * b_re - fi[..., None] * b_im
    bbi = fr[..., None] * b_im + fi[..., None] * b_re
    c_re, c_im = c_re.astype(F32), c_im.astype(F32)

    cpr = c_re[None] * pr[:t_, :, None, :] - c_im[None] * pi[:t_, :, None, :]
    cpi = c_re[None] * pi[:t_, :, None, :] + c_im[None] * pr[:t_, :, None, :]
    kern = jnp.einsum('tgpn,gnq->tgpq', jnp.concatenate([cpr, -cpi], axis=-1),
                      jnp.concatenate([bbr, bbi], axis=1), precision=hi)
    skip = d_skip.astype(F32).reshape(g_, p_)[:, :, None] * jnp.eye(p_, dtype=F32)[None]
    kern = kern.at[0].add(skip)
    flat = t_ * p_
    last = jnp.transpose(kern[::-1], (1, 2, 0, 3)).reshape(g_, p_, flat)
    mt = jnp.stack([jnp.pad(last[:, :, (t_ - 1 - t) * p_:], ((0, 0), (0, 0), (0, (t_ - 1 - t) * p_)))
                    for t in range(t_)], axis=1).reshape(g_, flat, flat)

    prj, pij = pr[t_ - 1::-1][:t_], pi[t_ - 1::-1][:t_]
    wr = prj[..., None] * bbr[None] - pij[..., None] * bbi[None]
    wi = prj[..., None] * bbi[None] + pij[..., None] * bbr[None]
    half = (np.arange(2) == 0)[None, None, :, None, None]

    def pack_w(w):
        w = jnp.transpose(w.reshape(t_, g_ // 2, 2, n_, p_), (1, 0, 2, 4, 3))
        w = jnp.concatenate([jnp.where(half, w, 0.0), jnp.where(half, 0.0, w)], axis=-1)
        return w.reshape(g_ // 2, 2 * flat, 2 * n_)

    vr = c_re[None] * pr[1:, :, None, :] - c_im[None] * pi[1:, :, None, :]
    vi = c_re[None] * pi[1:, :, None, :] + c_im[None] * pr[1:, :, None, :]
    vr = jnp.transpose(vr, (1, 0, 2, 3)).reshape(g_ // 2, 2, flat, n_)
    vi = -jnp.transpose(vi, (1, 0, 2, 3)).reshape(g_ // 2, 2, flat, n_)
    zv = jnp.zeros((g_ // 2, flat, n_), F32)
    v_even = jnp.concatenate([vr[:, 0], zv, vi[:, 0], zv], axis=-1)
    v_odd = jnp.concatenate([zv, vr[:, 1], zv, vi[:, 1]], axis=-1)
    vt = jnp.stack([v_even, v_odd], axis=1).reshape(g_, flat, 4 * n_)
    a_re = pr[t_].reshape(1, g_ * n_)
    a_im = pi[t_].reshape(1, g_ * n_)
    return mt.astype(BF16), pack_w(wr).astype(BF16), pack_w(wi).astype(BF16), vt.astype(BF16), a_re, a_im


_NT = (((1,), (1,)), ((), ()))
_TN = (((0,), (0,)), ((), ()))


def _s5_kernel(u_ref, mt_ref, wre_ref, wim_ref, vt_ref, are_ref, aim_ref, y_ref,
               xt_sc, yt_sc, zre_sc, zim_sc, carry_sc, *, ct):
    i = pl.program_id(0)
    t_ = SSM_CHUNK

    @pl.when(i == 0)
    def _():
        carry_sc[...] = jnp.zeros_like(carry_sc)

    eye = jnp.where(lax.broadcasted_iota(jnp.int32, (128, 128), 0)
                    == lax.broadcasted_iota(jnp.int32, (128, 128), 1), 1.0, 0.0).astype(BF16)

    for s in range(SSM_SLABS):
        for j in range(t_):
            uj = u_ref[s, pl.ds(j, ct, stride=t_), :].astype(BF16)
            xt_sc[s, j] = lax.dot_general(eye, uj, _NT, preferred_element_type=F32).astype(BF16)

    pairs_per_slab = 128 // (2 * SSM_GROUP)
    for gp in range(SSM_GROUPS // 2):
        s, k = gp // pairs_per_slab, gp % pairs_per_slab
        xp = xt_sc[s, :, 32 * k:32 * k + 32, :].reshape(2 * SSM_FLAT, ct)
        zre_sc[:, gp * 128:(gp + 1) * 128] = lax.dot_general(xp, wre_ref[gp], _TN, preferred_element_type=F32)
        zim_sc[:, gp * 128:(gp + 1) * 128] = lax.dot_general(xp, wim_ref[gp], _TN, preferred_element_type=F32)

    are = are_ref[...]
    aim = aim_ref[...]

    def step(r, carry):
        sre, sim = carry
        zr = zre_sc[pl.ds(r, 1), :]
        zi = zim_sc[pl.ds(r, 1), :]
        zre_sc[pl.ds(r, 1), :] = sre
        zim_sc[pl.ds(r, 1), :] = sim
        return are * sre - aim * sim + zr, are * sim + aim * sre + zi

    sre, sim = lax.fori_loop(0, ct, step, (carry_sc[0:1, :], carry_sc[1:2, :]), unroll=8)
    carry_sc[0:1, :] = sre
    carry_sc[1:2, :] = sim

    groups_per_slab = 128 // SSM_GROUP
    for g in range(SSM_GROUPS):
        gp, s, k = g // 2, g // groups_per_slab, g % groups_per_slab
        xg = xt_sc[s, :, 16 * k:16 * k + 16, :].reshape(SSM_FLAT, ct)
        sp = jnp.concatenate([zre_sc[:, gp * 128:(gp + 1) * 128],
                              zim_sc[:, gp * 128:(gp + 1) * 128]], axis=1).astype(BF16)
        yt = _dot(mt_ref[g], xg) + lax.dot_general(vt_ref[g], sp, _NT, preferred_element_type=F32)
        yt_sc[s, :, 16 * k:16 * k + 16, :] = yt.reshape(t_, SSM_GROUP, ct)

    for s in range(SSM_SLABS):
        for t in range(t_):
            yt = yt_sc[s, t]
            hi = yt.astype(BF16)
            lo = (yt - hi.astype(F32)).astype(BF16)
            y = (lax.dot_general(hi, eye, _TN, preferred_element_type=F32)
                 + lax.dot_general(lo, eye, _TN, preferred_element_type=F32))
            y_ref[s, pl.ds(t, ct, stride=t_), :] = y


def _s5(u4, tables, ct):
    mt, wre, wim, vt, are, aim = tables
    _, seq, _ = u4.shape
    full = lambda a: pl.BlockSpec(a.shape, lambda i: (0,) * a.ndim)
    rows = pl.BlockSpec((SSM_SLABS, SSM_CHUNK * ct, 128), lambda i: (0, i, 0))
    nstate = SSM_GROUPS * SSM_STATE
    return pl.pallas_call(
        functools.partial(_s5_kernel, ct=ct),
        grid=(seq // (SSM_CHUNK * ct),),
        in_specs=[rows, full(mt), full(wre), full(wim), full(vt), full(are), full(aim)],
        out_specs=rows,
        out_shape=jax.ShapeDtypeStruct(u4.shape, F32),
        scratch_shapes=[pltpu.VMEM((SSM_SLABS, SSM_CHUNK, 128, ct), BF16),
                        pltpu.VMEM((SSM_SLABS, SSM_CHUNK, 128, ct), F32),
                        pltpu.VMEM((ct, nstate), F32),
                        pltpu.VMEM((ct, nstate), F32),
                        pltpu.VMEM((8, nstate), F32)],
        compiler_params=_params(("arbitrary",)),
        name="s5_ssm",
    )(u4, mt, wre, wim, vt, are, aim)


OUT_ROWS = 256


def _even_out_kernel(x_ref, ya_ref, ys_ref, szb_ref, wglu_ref, wout_ref, g_ref, h_ref, hn_ref, *, tm):
    for r0 in range(0, tm, OUT_ROWS):
        rows = slice(r0, r0 + OUT_ROWS)
        gl = _gelu_tanh(jnp.concatenate([ys_ref[s, rows, :] for s in range(SSM_SLABS)], axis=1))
        yb = gl * _sigmoid(_dot(gl.astype(BF16), wglu_ref[...])) * szb_ref[rows, :].astype(F32)
        acc = _dot(ya_ref[rows, :], wout_ref[0:D_SHORTCONV, :])
        acc = acc + _dot(yb.astype(BF16), wout_ref[D_SHORTCONV:D_SHORTCONV + D_SSM, :])
        h = x_ref[rows, :] + acc
        h_ref[rows, :] = h
        hn_ref[rows, :] = _rms(h, g_ref[...]).astype(BF16)


def _even_out(x, ya, ys, szb, w_glu, w_out, g, tm):
    seq = x.shape[0]
    row = lambda w: pl.BlockSpec((tm, w), lambda i: (i, 0))
    const = lambda a: pl.BlockSpec(a.shape, lambda i: (0, 0), pipeline_mode=pl.Buffered(1))
    return pl.pallas_call(
        functools.partial(_even_out_kernel, tm=tm),
        grid=(seq // tm,),
        in_specs=[row(D_MODEL), row(D_SHORTCONV),
                  pl.BlockSpec((SSM_SLABS, tm, 128), lambda i: (0, i, 0)), row(D_SSM),
                  const(w_glu), const(w_out), const(g)],
        out_specs=[row(D_MODEL), row(D_MODEL)],
        out_shape=[jax.ShapeDtypeStruct((seq, D_MODEL), F32),
                   jax.ShapeDtypeStruct((seq, D_MODEL), BF16)],
        compiler_params=_params(("arbitrary",)),
        name="even_out",
    )(x, ya, ys, szb, w_glu, w_out, g)


ODD_STEPS = 12
_SILU_STEPS = (2, 3, 10, 11)
_Q_STEPS = (4, 5)


def _odd_in_kernel(hn_ref, wa_ref, wb_ref, o_ref, *, tm):
    s = pl.program_id(1)
    is_silu = (s == 2) | (s == 3) | (s == 10) | (s == 11)
    blocks = [slice(r0, r0 + IN_ROWS) for r0 in range(0, tm, IN_ROWS)]

    @pl.when(s < 2)
    def _():
        for rows in blocks:
            hn = hn_ref[rows, :]
            o_ref[rows, :] = (_dot(hn, wa_ref[...]) * _sigmoid(_dot(hn, wb_ref[...]))).astype(BF16)

    @pl.when(is_silu)
    def _():
        for rows in blocks:
            o_ref[rows, :] = _silu(_dot(hn_ref[rows, :], wa_ref[...])).astype(BF16)

    @pl.when((s >= 2) & jnp.logical_not(is_silu))
    def _():
        scale = jnp.where((s == 4) | (s == 5), Q_SCALE, 1.0).astype(F32)
        for rows in blocks:
            o_ref[rows, :] = (_dot(hn_ref[rows, :], wa_ref[...]) * scale).astype(BF16)


def _odd_in(hn, w_in, tm):
    seq = hn.shape[0]
    return pl.pallas_call(
        functools.partial(_odd_in_kernel, tm=tm),
        grid=(seq // tm, ODD_STEPS),
        in_specs=[pl.BlockSpec((tm, D_MODEL), lambda i, s: (i, 0)),
                  pl.BlockSpec((D_MODEL, COL), lambda i, s: (0, jnp.where(s < 2, s, s + 2))),
                  pl.BlockSpec((D_MODEL, COL), lambda i, s: (0, jnp.where(s < 2, s + 2, 3)))],
        out_specs=pl.BlockSpec((tm, COL), lambda i, s: (i, s)),
        out_shape=jax.ShapeDtypeStruct((seq, ODD_STEPS * COL), BF16),
        compiler_params=_params(("arbitrary", "arbitrary")),
        name="odd_in",
    )(hn, w_in, w_in)


CONV_ROWS = 64


def _conformer_kernel(cur_ref, halo_ref, szc_ref, cw_ref, cb_ref, lg_ref, lb_ref, wpw_ref,
                      o_ref, buf_sc, conv_sc, *, tm):
    i = pl.program_id(0)
    halo = jnp.where(i == 0, 0.0, halo_ref[...].astype(F32))
    cur = cur_ref[...].astype(F32)
    nslab = D_CONFORMER // 128
    for c in range(nslab):
        buf_sc[c, 0:CONV_HALO, :] = halo[:, c * 128:(c + 1) * 128]
        buf_sc[c, CONV_HALO:CONV_HALO + tm, :] = cur[:, c * 128:(c + 1) * 128]
    shift = CONV_HALO - (CONFORMER_WIDTH - 1)

    def rows(rc, _):
        base = pl.multiple_of(rc * CONV_ROWS, CONV_ROWS)
        for c in range(nslab):
            lanes = slice(c * 128, (c + 1) * 128)
            acc = jnp.zeros((CONV_ROWS, 128), F32) + cb_ref[:, lanes]
            for k in range(CONFORMER_WIDTH):
                acc = acc + cw_ref[k:k + 1, lanes] * buf_sc[c, pl.ds(base + (shift + k), CONV_ROWS), :]
            conv_sc[pl.ds(base, CONV_ROWS), lanes] = acc
        return 0

    lax.fori_loop(0, tm // CONV_ROWS, rows, 0)
    y = conv_sc[...]
    mu = jnp.mean(y, axis=-1, keepdims=True)
    yc = y - mu
    var = jnp.mean(yc * yc, axis=-1, keepdims=True)
    ln = yc * lax.rsqrt(var + NORM_EPS) * lg_ref[...] + lb_ref[...]
    o_ref[...] = (_dot(_silu(ln).astype(BF16), wpw_ref[...]) * szc_ref[...].astype(F32)).astype(BF16)


def _conformer(proj, conv_w, conv_b, ln_g, ln_b, w_pw, tm):
    seq = proj.shape[0]
    per = tm // CONV_HALO
    const = lambda a: pl.BlockSpec(a.shape, lambda i: (0, 0))
    return pl.pallas_call(
        functools.partial(_conformer_kernel, tm=tm),
        grid=(seq // tm,),
        in_specs=[pl.BlockSpec((tm, D_CONFORMER), lambda i: (i, 0)),
                  pl.BlockSpec((CONV_HALO, D_CONFORMER), lambda i: (jnp.maximum(i * per - 1, 0), 0)),
                  pl.BlockSpec((tm, D_CONFORMER), lambda i: (i, 1)),
                  const(conv_w), const(conv_b), const(ln_g), const(ln_b), const(w_pw)],
        out_specs=pl.BlockSpec((tm, D_CONFORMER), lambda i: (i, 0)),
        out_shape=jax.ShapeDtypeStruct((seq, D_CONFORMER), BF16),
        scratch_shapes=[pltpu.VMEM((D_CONFORMER // 128, tm + CONV_HALO, 128), F32),
                        pltpu.VMEM((tm, D_CONFORMER), F32)],
        compiler_params=_params(("arbitrary",)),
        name="conformer",
    )(proj, proj, proj, conv_w, conv_b, ln_g, ln_b, w_pw)


def _t5_bucket(dist):
    max_exact = REL_BUCKETS // 2
    d_f = jnp.maximum(dist, 1).astype(F32)
    large = max_exact + (jnp.log(d_f / max_exact) / math.log(REL_MAX_DISTANCE / max_exact)
                         * (REL_BUCKETS - max_exact)).astype(jnp.int32)
    large = jnp.minimum(large, REL_BUCKETS - 1)
    return jnp.where(dist < max_exact, dist, large)


def _attn_bias(rel_bias):
    q = ATTN_BLOCK
    nh = rel_bias.shape[1]
    kj = np.arange(2 * q)[None, None, :]
    neg = lambda n: jnp.full((nh, n), NEG_INF, F32)
    tabs = []
    for window, dilation in DILATED_PATTERNS:
        steps = window // dilation
        assert steps == q
        rel = jnp.asarray(np.arange(steps + 1) * dilation, dtype=jnp.int32)
        by_rel = rel_bias.astype(F32)[_t5_bucket(rel)].T
        g = jnp.concatenate([by_rel[:, ::-1], neg(2 * q - 1)], axis=1)
        flat = jnp.tile(g, (1, q))[:, :q * (3 * q - 1)]
        tab = flat.reshape(nh, q, 3 * q - 1)[:, :, :2 * q]
        if dilation == 1:
            tab = tab.reshape(nh, q // MERGE_DIL, MERGE_DIL, 2 * q).transpose(0, 2, 1, 3).reshape(nh, q, 2 * q)
        tab = tab * LOG2E
        tabs.append(tab)
        tabs.append(jnp.where(kj >= q, tab, NEG_INF))
    return jnp.stack(tabs, axis=1)


def _attn_kernel(q_ref, k_ref, v_ref, szd_ref, bias_ref, o_ref,
                 qf, kf, vf, qm, km, vm, qs1, qs2, ks0, ks1, ks2, vs0, vs1, vs2,
                 o0, o1, o2, l0, l1, l2):
    i = pl.program_id(1)
    t = ATTN_TILE
    blk = ATTN_BLOCK
    first_tile = i == 0
    qf[...] = q_ref[...].astype(F32)
    kf[...] = k_ref[...].astype(F32)
    vf[...] = v_ref[...].astype(F32)
    qstreams = (None, qs1, qs2)
    kstreams = (ks0, ks1, ks2)
    vstreams = (vs0, vs1, vs2)
    md = MERGE_DIL
    mspan = t // md

    def stream(src_f, src_m, dil, r):
        n = t // dil
        if dil == md:
            x = src_f[pl.ds(r, n, stride=dil), :]
            src_m[r * n:(r + 1) * n, :] = x
            return x
        return src_m[pl.ds((r % md) * mspan + r // md, n, stride=dil // md), :]

    for pidx, (_, dil) in enumerate(DILATED_PATTERNS):
        n = t // dil
        per = blk + n
        for r in range(dil):
            base = r * per
            for src_f, src_m, src_ref, dst in ((kf, km, k_ref, kstreams[pidx]), (vf, vm, v_ref, vstreams[pidx])):
                tail = dst[base + n:base + per, :]
                dst[base:base + blk, :] = jnp.where(first_tile, jnp.zeros_like(tail), tail)
                if dil == 1:
                    dst[base + blk:base + per, :] = src_ref[...]
                else:
                    dst[base + blk:base + per, :] = stream(src_f, src_m, dil, r).astype(BF16)
            if dil > 1:
                qstreams[pidx][r * n:(r + 1) * n, :] = stream(qf, qm, dil, r).astype(BF16)

    ones = jnp.ones((2 * blk, HEAD_DIM), BF16)
    first = first_tile.astype(jnp.int32)

    def block(pidx, dil, r, sb, out_sc, lse_sc):
        n = t // dil
        per = blk + n
        if dil == 1:
            qb = jnp.concatenate([qf[pl.ds(sb * blk + c, blk // md, stride=md), :] for c in range(md)],
                                 axis=0).astype(BF16)
        else:
            qb = qstreams[pidx][r * n + sb * blk:r * n + (sb + 1) * blk, :]
        krows = slice(r * per + sb * blk, r * per + (sb + 2) * blk)
        kb = kstreams[pidx][krows, :]
        vb = vstreams[pidx][krows, :]
        s = lax.dot_general(qb, kb, _NT, preferred_element_type=F32)
        s = s + bias_ref[0, 2 * pidx + first if sb == 0 else 2 * pidx]
        mx = jnp.max(s, axis=-1, keepdims=True)
        p = jnp.exp2(s - mx).astype(BF16)
        pv = _dot(p, jnp.concatenate([vb, ones], axis=1))
        den = pv[:, HEAD_DIM:]
        out = pv[:, :HEAD_DIM] / den
        lse = mx + jnp.log2(den)
        if dil == 1:
            piece = blk // md
            for c in range(md):
                rows = slice(c * mspan + sb * piece, c * mspan + (sb + 1) * piece)
                out_sc[rows, :] = out[c * piece:(c + 1) * piece, :]
                lse_sc[rows, :] = lse[c * piece:(c + 1) * piece, :]
        elif dil == md:
            rows = slice(r * mspan + sb * blk, r * mspan + (sb + 1) * blk)
            out_sc[rows, :] = out
            lse_sc[rows, :] = lse
        else:
            step = dil // md
            rows = pl.ds((r % md) * mspan + sb * blk * step + r // md, blk, stride=step)
            out_sc[rows, :] = out
            lse_sc[rows, :] = lse

    scratch = ((o0, l0), (o1, l1), (o2, l2))
    for pidx, (_, dil) in enumerate(DILATED_PATTERNS):
        for sb in range(t // (blk * dil)):
            for r in range(dil):
                block(pidx, dil, r, sb, *scratch[pidx])

    la, lb, lc = l0[...], l1[...], l2[...]
    lall = jnp.maximum(jnp.maximum(la, lb), lc)
    wa, wb, wc = jnp.exp2(la - lall), jnp.exp2(lb - lall), jnp.exp2(lc - lall)
    merged = (wa * o0[...] + wb * o1[...] + wc * o2[...]) / (wa + wb + wc)
    for c in range(md):
        qf[pl.ds(c, mspan, stride=md), :] = merged[c * mspan:(c + 1) * mspan, :]
    o_ref[...] = (qf[...] * szd_ref[...].astype(F32)).astype(BF16)


def _attention(proj, bias):
    seq = proj.shape[0]
    t = ATTN_TILE
    q0 = (2 * D_CONFORMER) // HEAD_DIM
    k0 = q0 + N_HEADS
    v0 = k0 + N_HEADS
    z0 = v0 + N_HEADS
    cur = lambda c0: pl.BlockSpec((t, HEAD_DIM), lambda h, i: (i, c0 + h))
    f32 = lambda rows: pltpu.VMEM((rows, HEAD_DIM), F32)
    bf16 = lambda rows: pltpu.VMEM((rows, HEAD_DIM), BF16)
    streams = [bf16(dil * (ATTN_BLOCK + t // dil)) for _, dil in DILATED_PATTERNS]
    return pl.pallas_call(
        _attn_kernel,
        grid=(N_HEADS, seq // t),
        in_specs=[cur(q0), cur(k0), cur(v0), cur(z0),
                  pl.BlockSpec((1,) + bias.shape[1:], lambda h, i: (h, 0, 0, 0))],
        out_specs=pl.BlockSpec((t, HEAD_DIM), lambda h, i: (i, h)),
        out_shape=jax.ShapeDtypeStruct((seq, D_ATTN), BF16),
        scratch_shapes=[f32(t)] * 6 + [bf16(t)] * 2 + streams + streams + [f32(t)] * 6,
        compiler_params=_params(("arbitrary", "arbitrary")),
        name="dilated_attn",
    )(proj, proj, proj, proj, bias)


def _odd_out_kernel(h_ref, yc_ref, yd_ref, wout_ref, g_ref, o_ref, *, tm):
    for r0 in range(0, tm, OUT_ROWS):
        rows = slice(r0, r0 + OUT_ROWS)
        acc = _dot(yc_ref[rows, :], wout_ref[0:D_CONFORMER, :])
        acc = acc + _dot(yd_ref[rows, :], wout_ref[D_CONFORMER:D_CONFORMER + D_ATTN, :])
        o_ref[rows, :] = _rms(h_ref[rows, :] + acc, g_ref[...])


def _odd_out(h, yc, yd, w_out, g, tm):
    seq = h.shape[0]
    row = lambda w: pl.BlockSpec((tm, w), lambda i: (i, 0))
    const = lambda a: pl.BlockSpec(a.shape, lambda i: (0, 0), pipeline_mode=pl.Buffered(1))
    return pl.pallas_call(
        functools.partial(_odd_out_kernel, tm=tm),
        grid=(seq // tm,),
        in_specs=[row(D_MODEL), row(D_CONFORMER), row(D_ATTN), const(w_out), const(g)],
        out_specs=row(D_MODEL),
        out_shape=jax.ShapeDtypeStruct((seq, D_MODEL), F32),
        compiler_params=_params(("arbitrary",)),
        name="odd_out",
    )(h, yc, yd, w_out, g)


def _row_tile(seq, want):
    return min(seq, want)


def kernel(x, norm_g, final_norm_g, ev_w_in, ev_conv_w, s5_lam_re, s5_lam_im, s5_log_dt, s5_b_re, s5_b_im, s5_c_re, s5_c_im, s5_d, s5_w_glu, ev_w_out, od_w_in, cf_conv_w, cf_conv_b, cf_ln_g, cf_ln_b, cf_w_pw, od_w_out, rel_bias):
    bsz, seq, _ = x.shape
    assert bsz == 1 and seq % ATTN_TILE == 0
    x2 = x.reshape(seq, D_MODEL)
    g0 = norm_g[0].astype(F32).reshape(1, D_MODEL)
    g1 = norm_g[1].astype(F32).reshape(1, D_MODEL)
    gf = final_norm_g.astype(F32).reshape(1, D_MODEL)

    ya, u, szb = _even_in(x2, g0, ev_w_in[0].astype(BF16), ev_conv_w[0].astype(F32), _row_tile(seq, 1024), 512)
    tables = _s5_tables(s5_lam_re[0], s5_lam_im[0], s5_log_dt[0], s5_b_re[0], s5_b_im[0],
                        s5_c_re[0], s5_c_im[0], s5_d[0])
    ys = _s5(u, tables, min(seq // SSM_CHUNK, 128))
    h1, hn1 = _even_out(x2, ya, ys, szb, s5_w_glu[0].astype(BF16), ev_w_out[0].astype(BF16), g1,
                        _row_tile(seq, 512))

    proj = _odd_in(hn1, od_w_in[0].astype(BF16), _row_tile(seq, 2048))
    conv_w = jnp.concatenate([cf_conv_w[0].astype(F32), jnp.zeros((1, D_CONFORMER), F32)], axis=0)
    yc = _conformer(proj, conv_w, cf_conv_b[0].astype(F32).reshape(1, -1),
                    cf_ln_g[0].astype(F32).reshape(1, -1), cf_ln_b[0].astype(F32).reshape(1, -1),
                    cf_w_pw[0].astype(BF16), _row_tile(seq, 512))
    yd = _attention(proj, _attn_bias(rel_bias))
    out = _odd_out(h1, yc, yd, od_w_out[0].astype(BF16), gf, _row_tile(seq, 512))
    return out.reshape(bsz, seq, D_MODEL)
```

```python
import functools
import math

import numpy as np
import jax
import jax.numpy as jnp
from jax import lax
from jax.experimental import pallas as pl
from jax.experimental.pallas import tpu as pltpu

F32 = jnp.float32
BF16 = jnp.bfloat16

D_MODEL = 2048
D_SHORTCONV = 1536
SHORTCONV_WIDTH = 3
D_SSM = 512
SSM_GROUP = 16
SSM_GROUPS = 32
SSM_STATE = 64
D_CONFORMER = 1024
CONFORMER_WIDTH = 31
D_ATTN = 1024
HEAD_DIM = 128
N_HEADS = 8
DILATED_PATTERNS = ((128, 1), (512, 4), (2048, 16))
ATTN_BLOCK = 128
REL_BUCKETS = 32
REL_MAX_DISTANCE = 2048
NORM_EPS = 1e-6
NEG_INF = -1e30

IN_ROWS = 512
SSM_CHUNK = 16
SSM_FLAT = SSM_CHUNK * SSM_GROUP
SSM_SLABS = D_SSM // 128
ATTN_TILE = 2048
LOG2E = math.log2(math.e)
Q_SCALE = HEAD_DIM ** -0.5 * LOG2E
MERGE_DIL = 4
CONV_HALO = 32
VMEM_LIMIT = 50 * 1024 * 1024


def _params(sem, vmem=VMEM_LIMIT):
    return pltpu.CompilerParams(dimension_semantics=sem, vmem_limit_bytes=vmem)


def _dot(a, b):
    return jnp.dot(a, b, preferred_element_type=F32)


def _sigmoid(z):
    return 1.0 / (1.0 + jnp.exp(-z))


def _silu(z):
    return z * _sigmoid(z)


def _gelu_tanh(x):
    return 0.5 * x * (1.0 + jnp.tanh(math.sqrt(2.0 / math.pi) * (x + 0.044715 * (x * x * x))))


def _rms(xf, g):
    ms = jnp.mean(xf * xf, axis=-1, keepdims=True)
    return xf * lax.rsqrt(ms + NORM_EPS) * g


def _even_in_kernel(x_ref, g_ref, wa_ref, wb_ref, wc_ref, wd_ref, cw_ref,
                    ya_ref, u_ref, szb_ref, hn_sc, pbuf_sc, carry_sc, *, tm, col):
    i = pl.program_id(0)
    j = pl.program_id(1)
    nconv = D_SHORTCONV // col

    @pl.when(j == 0)
    def _():
        hn_sc[...] = _rms(x_ref[...], g_ref[...]).astype(BF16)

    @pl.when(j < nconv)
    def _():
        @pl.when(i == 0)
        def _():
            carry_sc[j] = jnp.zeros((8, col), F32)

        pbuf_sc[0:8, :] = carry_sc[j]
        cw = cw_ref[...]
        for r0 in range(0, tm, IN_ROWS):
            hn = hn_sc[r0:r0 + IN_ROWS, :]
            p = _dot(hn, wb_ref[...]) * _dot(hn, wa_ref[...])
            pbuf_sc[8 + r0:8 + r0 + IN_ROWS, :] = p
            conv = (cw[2:3, :] * p + cw[1:2, :] * pbuf_sc[pl.ds(7 + r0, IN_ROWS), :]
                    + cw[0:1, :] * pbuf_sc[pl.ds(6 + r0, IN_ROWS), :])
            gb = _dot(hn, wc_ref[...])
            za = _dot(hn, wd_ref[...])
            ya_ref[r0:r0 + IN_ROWS, :] = (gb * conv * _silu(za)).astype(BF16)
        carry_sc[j] = pbuf_sc[tm:tm + 8, :]

    @pl.when(j >= nconv)
    def _():
        hn = hn_sc[...]
        u = _dot(hn, wa_ref[...])
        for s in range(col // 128):
            u_ref[s] = u[:, s * 128:(s + 1) * 128]
        szb_ref[...] = _silu(_dot(hn, wb_ref[...])).astype(BF16)


def _even_in(x, g, w_in, conv_w, tm, col):
    seq = x.shape[0]
    nconv = D_SHORTCONV // col
    nssm = D_SSM // col
    u_blk = 4 * nconv

    def wspec(k, alt0):
        return pl.BlockSpec((D_MODEL, col),
                            lambda i, j: (0, jnp.where(j < nconv, j + nconv * k, alt0 + j - nconv)))

    hold = lambda k: pl.BlockSpec((D_MODEL, col), lambda i, j: (0, jnp.minimum(j, nconv - 1) + nconv * k))
    ssm_step = lambda j: jnp.maximum(j - nconv, 0)
    return pl.pallas_call(
        functools.partial(_even_in_kernel, tm=tm, col=col),
        grid=(seq // tm, nconv + nssm),
        in_specs=[
            pl.BlockSpec((tm, D_MODEL), lambda i, j: (i, 0)),
            pl.BlockSpec((1, D_MODEL), lambda i, j: (0, 0)),
            wspec(0, u_blk), wspec(1, u_blk + nssm), hold(2), hold(3),
            pl.BlockSpec((SHORTCONV_WIDTH, col), lambda i, j: (0, jnp.minimum(j, nconv - 1))),
        ],
        out_specs=[
            pl.BlockSpec((tm, col), lambda i, j: (i, jnp.minimum(j, nconv - 1))),
            pl.BlockSpec((col // 128, tm, 128), lambda i, j: (ssm_step(j), i, 0)),
            pl.BlockSpec((tm, col), lambda i, j: (i, ssm_step(j))),
        ],
        out_shape=[
            jax.ShapeDtypeStruct((seq, D_SHORTCONV), BF16),
            jax.ShapeDtypeStruct((SSM_SLABS, seq, 128), F32),
            jax.ShapeDtypeStruct((seq, D_SSM), BF16),
        ],
        scratch_shapes=[
            pltpu.VMEM((tm, D_MODEL), BF16),
            pltpu.VMEM((tm + 8, col), F32),
            pltpu.VMEM((nconv, 8, col), F32),
        ],
        compiler_params=_params(("arbitrary", "arbitrary")),
        name="even_in",
    )(x, g, w_in, w_in, w_in, w_in, conv_w)


def _s5_tables(lam_re, lam_im, log_dt, b_re, b_im, c_re, c_im, d_skip):
    hi = lax.Precision.HIGH
    t_, g_, n_, p_ = SSM_CHUNK, SSM_GROUPS, SSM_STATE, SSM_GROUP
    lam_re, lam_im = lam_re.astype(F32), lam_im.astype(F32)
    dt = jnp.exp(log_dt.astype(F32))[:, None]
    xr, xi = lam_re * dt, lam_im * dt
    k = jnp.arange(t_ + 1, dtype=F32)[:, None, None]
    mag = jnp.exp(xr * k)
    pr, pi = mag * jnp.cos(xi * k), mag * jnp.sin(xi * k)
    nr, ni = pr[1] - 1.0, pi[1]
    den = lam_re * lam_re + lam_im * lam_im
    fr = (nr * lam_re + ni * lam_im) / den
    fi = (ni * lam_re - nr * lam_im) / den
    b_re, b_im = b_re.astype(F32), b_im.astype(F32)
    bbr = fr[..., None] * b_re - fi[..., None] * b_im
    bbi = fr[..., None] * b_im + fi[..., None] * b_re
    c_re, c_im = c_re.astype(F32), c_im.astype(F32)

    cpr = c_re[None] * pr[:t_, :, None, :] - c_im[None] * pi[:t_, :, None, :]
    cpi = c_re[None] * pi[:t_, :, None, :] + c_im[None] * pr[:t_, :, None, :]
    kern = jnp.einsum('tgpn,gnq->tgpq', jnp.concatenate([cpr, -cpi], axis=-1),
                      jnp.concatenate([bbr, bbi], axis=1), precision=hi)
    skip = d_skip.astype(F32).reshape(g_, p_)[:, :, None] * jnp.eye(p_, dtype=F32)[None]
    kern = kern.at[0].add(skip)
    flat = t_ * p_
    last = jnp.transpose(kern[::-1], (1, 2, 0, 3)).reshape(g_, p_, flat)
    mt = jnp.stack([jnp.pad(last[:, :, (t_ - 1 - t) * p_:], ((0, 0), (0, 0), (0, (t_ - 1 - t) * p_)))
                    for t in range(t_)], axis=1).reshape(g_, flat, flat)

    prj, pij = pr[t_ - 1::-1][:t_], pi[t_ - 1::-1][:t_]
    wr = prj[..., None] * bbr[None] - pij[..., None] * bbi[None]
    wi = prj[..., None] * bbi[None] + pij[..., None] * bbr[None]
    half = (np.arange(2) == 0)[None, None, :, None, None]

    def pack_w(w):
        w = jnp.transpose(w.reshape(t_, g_ // 2, 2, n_, p_), (1, 0, 2, 4, 3))
        w = jnp.concatenate([jnp.where(half, w, 0.0), jnp.where(half, 0.0, w)], axis=-1)
        return w.reshape(g_ // 2, 2 * flat, 2 * n_)

    vr = c_re[None] * pr[1:, :, None, :] - c_im[None] * pi[1:, :, None, :]
    vi = c_re[None] * pi[1:, :, None, :] + c_im[None] * pr[1:, :, None, :]
    vr = jnp.transpose(vr, (1, 0, 2, 3)).reshape(g_ // 2, 2, flat, n_)
    vi = -jnp.transpose(vi, (1, 0, 2, 3)).reshape(g_ // 2, 2, flat, n_)
    zv = jnp.zeros((g_ // 2, flat, n_), F32)
    v_even = jnp.concatenate([vr[:, 0], zv, vi[:, 0], zv], axis=-1)
    v_odd = jnp.concatenate([zv, vr[:, 1], zv, vi[:, 1]], axis=-1)
    vt = jnp.stack([v_even, v_odd], axis=1).reshape(g_, flat, 4 * n_)
    a_re = pr[t_].reshape(1, g_ * n_)
    a_im = pi[t_].reshape(1, g_ * n_)
    return mt.astype(BF16), pack_w(wr).astype(BF16), pack_w(wi).astype(BF16), vt.astype(BF16), a_re, a_im


_NT = (((1,), (1,)), ((), ()))
_TN = (((0,), (0,)), ((), ()))


def _s5_kernel(u_ref, mt_ref, wre_ref, wim_ref, vt_ref, are_ref, aim_ref, y_ref,
               xt_sc, yt_sc, zre_sc, zim_sc, carry_sc, *, ct):
    i = pl.program_id(0)
    t_ = SSM_CHUNK

    @pl.when(i == 0)
    def _():
        carry_sc[...] = jnp.zeros_like(carry_sc)

    eye = jnp.where(lax.broadcasted_iota(jnp.int32, (128, 128), 0)
                    == lax.broadcasted_iota(jnp.int32, (128, 128), 1), 1.0, 0.0).astype(BF16)

    for s in range(SSM_SLABS):
        for j in range(t_):
            uj = u_ref[s, pl.ds(j, ct, stride=t_), :].astype(BF16)
            xt_sc[s, j] = lax.dot_general(eye, uj, _NT, preferred_element_type=F32).astype(BF16)

    pairs_per_slab = 128 // (2 * SSM_GROUP)
    for gp in range(SSM_GROUPS // 2):
        s, k = gp // pairs_per_slab, gp % pairs_per_slab
        xp = xt_sc[s, :, 32 * k:32 * k + 32, :].reshape(2 * SSM_FLAT, ct)
        zre_sc[:, gp * 128:(gp + 1) * 128] = lax.dot_general(xp, wre_ref[gp], _TN, preferred_element_type=F32)
        zim_sc[:, gp * 128:(gp + 1) * 128] = lax.dot_general(xp, wim_ref[gp], _TN, preferred_element_type=F32)

    are = are_ref[...]
    aim = aim_ref[...]

    def step(r, carry):
        sre, sim = carry
        zr = zre_sc[pl.ds(r, 1), :]
        zi = zim_sc[pl.ds(r, 1), :]
        zre_sc[pl.ds(r, 1), :] = sre
        zim_sc[pl.ds(r, 1), :] = sim
        return are * sre - aim * sim + zr, are * sim + aim * sre + zi

    sre, sim = lax.fori_loop(0, ct, step, (carry_sc[0:1, :], carry_sc[1:2, :]), unroll=8)
    carry_sc[0:1, :] = sre
    carry_sc[1:2, :] = sim

    groups_per_slab = 128 // SSM_GROUP
    for g in range(SSM_GROUPS):
        gp, s, k = g // 2, g // groups_per_slab, g % groups_per_slab
        xg = xt_sc[s, :, 16 * k:16 * k + 16, :].reshape(SSM_FLAT, ct)
        sp = jnp.concatenate([zre_sc[:, gp * 128:(gp + 1) * 128],
                              zim_sc[:, gp * 128:(gp + 1) * 128]], axis=1).astype(BF16)
        yt = _dot(mt_ref[g], xg) + lax.dot_general(vt_ref[g], sp, _NT, preferred_element_type=F32)
        yt_sc[s, :, 16 * k:16 * k + 16, :] = yt.reshape(t_, SSM_GROUP, ct)

    for s in range(SSM_SLABS):
        for t in range(t_):
            yt = yt_sc[s, t]
            hi = yt.astype(BF16)
            lo = (yt - hi.astype(F32)).astype(BF16)
            y = (lax.dot_general(hi, eye, _TN, preferred_element_type=F32)
                 + lax.dot_general(lo, eye, _TN, preferred_element_type=F32))
            y_ref[s, pl.ds(t, ct, stride=t_), :] = y


def _s5(u4, tables, ct):
    mt, wre, wim, vt, are, aim = tables
    _, seq, _ = u4.shape
    full = lambda a: pl.BlockSpec(a.shape, lambda i: (0,) * a.ndim)
    rows = pl.BlockSpec((SSM_SLABS, SSM_CHUNK * ct, 128), lambda i: (0, i, 0))
    nstate = SSM_GROUPS * SSM_STATE
    return pl.pallas_call(
        functools.partial(_s5_kernel, ct=ct),
        grid=(seq // (SSM_CHUNK * ct),),
        in_specs=[rows, full(mt), full(wre), full(wim), full(vt), full(are), full(aim)],
        out_specs=rows,
        out_shape=jax.ShapeDtypeStruct(u4.shape, F32),
        scratch_shapes=[pltpu.VMEM((SSM_SLABS, SSM_CHUNK, 128, ct), BF16),
                        pltpu.VMEM((SSM_SLABS, SSM_CHUNK, 128, ct), F32),
                        pltpu.VMEM((ct, nstate), F32),
                        pltpu.VMEM((ct, nstate), F32),
                        pltpu.VMEM((8, nstate), F32)],
        compiler_params=_params(("arbitrary",)),
        name="s5_ssm",
    )(u4, mt, wre, wim, vt, are, aim)


OUT_ROWS = 256


def _even_out_kernel(x_ref, ya_ref, ys_ref, szb_ref, wglu_ref, wout_ref, g_ref, h_ref, hn_ref, *, tm):
    for r0 in range(0, tm, OUT_ROWS):
        rows = slice(r0, r0 + OUT_ROWS)
        gl = _gelu_tanh(jnp.concatenate([ys_ref[s, rows, :] for s in range(SSM_SLABS)], axis=1))
        yb = gl * _sigmoid(_dot(gl.astype(BF16), wglu_ref[...])) * szb_ref[rows, :].astype(F32)
        acc = _dot(ya_ref[rows, :], wout_ref[0:D_SHORTCONV, :])
        acc = acc + _dot(yb.astype(BF16), wout_ref[D_SHORTCONV:D_SHORTCONV + D_SSM, :])
        h = x_ref[rows, :] + acc
        h_ref[rows, :] = h
        hn_ref[rows, :] = _rms(h, g_ref[...]).astype(BF16)


def _even_out(x, ya, ys, szb, w_glu, w_out, g, tm):
    seq = x.shape[0]
    row = lambda w: pl.BlockSpec((tm, w), lambda i: (i, 0))
    const = lambda a: pl.BlockSpec(a.shape, lambda i: (0, 0), pipeline_mode=pl.Buffered(1))
    return pl.pallas_call(
        functools.partial(_even_out_kernel, tm=tm),
        grid=(seq // tm,),
        in_specs=[row(D_MODEL), row(D_SHORTCONV),
                  pl.BlockSpec((SSM_SLABS, tm, 128), lambda i: (0, i, 0)), row(D_SSM),
                  const(w_glu), const(w_out), const(g)],
        out_specs=[row(D_MODEL), row(D_MODEL)],
        out_shape=[jax.ShapeDtypeStruct((seq, D_MODEL), F32),
                   jax.ShapeDtypeStruct((seq, D_MODEL), BF16)],
        compiler_params=_params(("arbitrary",)),
        name="even_out",
    )(x, ya, ys, szb, w_glu, w_out, g)


ODD_GROUPS = 6


def _odd_in_kernel(hn_ref, wa_ref, wb_ref, o_ref, *, tm, col):
    s = pl.program_id(1)
    nb = D_CONFORMER // col
    group = s // nb
    is_silu = (group == 1) | (group == 5)
    blocks = [slice(r0, r0 + IN_ROWS) for r0 in range(0, tm, IN_ROWS)]

    @pl.when(group == 0)
    def _():
        for rows in blocks:
            hn = hn_ref[rows, :]
            o_ref[rows, :] = (_dot(hn, wa_ref[...]) * _sigmoid(_dot(hn, wb_ref[...]))).astype(BF16)

    @pl.when(is_silu)
    def _():
        for rows in blocks:
            o_ref[rows, :] = _silu(_dot(hn_ref[rows, :], wa_ref[...])).astype(BF16)

    @pl.when((group > 0) & jnp.logical_not(is_silu))
    def _():
        scale = jnp.where(group == 2, Q_SCALE, 1.0).astype(F32)
        for rows in blocks:
            o_ref[rows, :] = (_dot(hn_ref[rows, :], wa_ref[...]) * scale).astype(BF16)


def _odd_in(hn, w_in, tm, col):
    seq = hn.shape[0]
    nb = D_CONFORMER // col
    return pl.pallas_call(
        functools.partial(_odd_in_kernel, tm=tm, col=col),
        grid=(seq // tm, ODD_GROUPS * nb),
        in_specs=[pl.BlockSpec((tm, D_MODEL), lambda i, s: (i, 0)),
                  pl.BlockSpec((D_MODEL, col), lambda i, s: (0, jnp.where(s < nb, s, s + nb))),
                  pl.BlockSpec((D_MODEL, col), lambda i, s: (0, jnp.where(s < nb, s + nb, 2 * nb - 1)))],
        out_specs=pl.BlockSpec((tm, col), lambda i, s: (i, s)),
        out_shape=jax.ShapeDtypeStruct((seq, ODD_GROUPS * D_CONFORMER), BF16),
        compiler_params=_params(("arbitrary", "arbitrary")),
        name="odd_in",
    )(hn, w_in, w_in)


CONV_ROWS = 64


def _conformer_kernel(cur_ref, halo_ref, szc_ref, cw_ref, cb_ref, lg_ref, lb_ref, wpw_ref,
                      o_ref, buf_sc, conv_sc, *, tm):
    i = pl.program_id(0)
    nslab = D_CONFORMER // 128
    for c in range(nslab):
        lanes = slice(c * 128, (c + 1) * 128)
        buf_sc[c, 0:CONV_HALO, :] = jnp.where(i == 0, 0.0, halo_ref[:, lanes].astype(F32))
        buf_sc[c, CONV_HALO:CONV_HALO + tm, :] = cur_ref[:, lanes].astype(F32)
    shift = CONV_HALO - (CONFORMER_WIDTH - 1)

    def rows(rc, _):
        base = pl.multiple_of(rc * CONV_ROWS, CONV_ROWS)
        for c in range(nslab):
            lanes = slice(c * 128, (c + 1) * 128)
            acc = jnp.zeros((CONV_ROWS, 128), F32) + cb_ref[:, lanes]
            for k in range(CONFORMER_WIDTH):
                acc = acc + cw_ref[k:k + 1, lanes] * buf_sc[c, pl.ds(base + (shift + k), CONV_ROWS), :]
            conv_sc[pl.ds(base, CONV_ROWS), lanes] = acc
        return 0

    lax.fori_loop(0, tm // CONV_ROWS, rows, 0)
    for r0 in range(0, tm, OUT_ROWS):
        sub = slice(r0, r0 + OUT_ROWS)
        y = conv_sc[sub, :]
        mu = jnp.mean(y, axis=-1, keepdims=True)
        yc = y - mu
        var = jnp.mean(yc * yc, axis=-1, keepdims=True)
        ln = yc * lax.rsqrt(var + NORM_EPS) * lg_ref[...] + lb_ref[...]
        o_ref[sub, :] = (_dot(_silu(ln).astype(BF16), wpw_ref[...]) * szc_ref[sub, :].astype(F32)).astype(BF16)


def _conformer(proj, conv_w, conv_b, ln_g, ln_b, w_pw, tm):
    seq = proj.shape[0]
    per = tm // CONV_HALO
    const = lambda a: pl.BlockSpec(a.shape, lambda i: (0, 0))
    return pl.pallas_call(
        functools.partial(_conformer_kernel, tm=tm),
        grid=(seq // tm,),
        in_specs=[pl.BlockSpec((tm, D_CONFORMER), lambda i: (i, 0)),
                  pl.BlockSpec((CONV_HALO, D_CONFORMER), lambda i: (jnp.maximum(i * per - 1, 0), 0)),
                  pl.BlockSpec((tm, D_CONFORMER), lambda i: (i, 1)),
                  const(conv_w), const(conv_b), const(ln_g), const(ln_b), const(w_pw)],
        out_specs=pl.BlockSpec((tm, D_CONFORMER), lambda i: (i, 0)),
        out_shape=jax.ShapeDtypeStruct((seq, D_CONFORMER), BF16),
        scratch_shapes=[pltpu.VMEM((D_CONFORMER // 128, tm + CONV_HALO, 128), F32),
                        pltpu.VMEM((tm, D_CONFORMER), F32)],
        compiler_params=_params(("arbitrary",)),
        name="conformer",
    )(proj, proj, proj, conv_w, conv_b, ln_g, ln_b, w_pw)


def _t5_bucket(dist):
    max_exact = REL_BUCKETS // 2
    d_f = jnp.maximum(dist, 1).astype(F32)
    large = max_exact + (jnp.log(d_f / max_exact) / math.log(REL_MAX_DISTANCE / max_exact)
                         * (REL_BUCKETS - max_exact)).astype(jnp.int32)
    large = jnp.minimum(large, REL_BUCKETS - 1)
    return jnp.where(dist < max_exact, dist, large)


def _attn_bias(rel_bias):
    q = ATTN_BLOCK
    nh = rel_bias.shape[1]
    kj = np.arange(2 * q)[None, None, :]
    neg = lambda n: jnp.full((nh, n), NEG_INF, F32)
    tabs = []
    for window, dilation in DILATED_PATTERNS:
        steps = window // dilation
        assert steps == q
        rel = jnp.asarray(np.arange(steps + 1) * dilation, dtype=jnp.int32)
        by_rel = rel_bias.astype(F32)[_t5_bucket(rel)].T
        g = jnp.concatenate([by_rel[:, ::-1], neg(2 * q - 1)], axis=1)
        flat = jnp.tile(g, (1, q))[:, :q * (3 * q - 1)]
        tab = flat.reshape(nh, q, 3 * q - 1)[:, :, :2 * q]
        if dilation == 1:
            tab = tab.reshape(nh, q // MERGE_DIL, MERGE_DIL, 2 * q).transpose(0, 2, 1, 3).reshape(nh, q, 2 * q)
        tab = tab * LOG2E
        tabs.append(tab)
        tabs.append(jnp.where(kj >= q, tab, NEG_INF))
    return jnp.stack(tabs, axis=1)


def _attn_kernel(q_ref, k_ref, v_ref, szd_ref, bias_ref, o_ref,
                 qf, kf, vf, qm, km, vm, qs1, qs2, ks0, ks1, ks2, vs0, vs1, vs2,
                 o0, o1, o2, l0, l1, l2):
    i = pl.program_id(1)
    t = ATTN_TILE
    blk = ATTN_BLOCK
    first_tile = i == 0
    qf[...] = q_ref[...].astype(F32)
    kf[...] = k_ref[...].astype(F32)
    vf[...] = v_ref[...].astype(F32)
    qstreams = (None, qs1, qs2)
    kstreams = (ks0, ks1, ks2)
    vstreams = (vs0, vs1, vs2)
    md = MERGE_DIL
    mspan = t // md

    def stream(src_f, src_m, dil, r):
        n = t // dil
        if dil == md:
            x = src_f[pl.ds(r, n, stride=dil), :]
            src_m[r * n:(r + 1) * n, :] = x
            return x
        return src_m[pl.ds((r % md) * mspan + r // md, n, stride=dil // md), :]

    for pidx, (_, dil) in enumerate(DILATED_PATTERNS):
        n = t // dil
        per = blk + n
        for r in range(dil):
            base = r * per
            for src_f, src_m, src_ref, dst in ((kf, km, k_ref, kstreams[pidx]), (vf, vm, v_ref, vstreams[pidx])):
                tail = dst[base + n:base + per, :]
                dst[base:base + blk, :] = jnp.where(first_tile, jnp.zeros_like(tail), tail)
                if dil == 1:
                    dst[base + blk:base + per, :] = src_ref[...]
                else:
                    dst[base + blk:base + per, :] = stream(src_f, src_m, dil, r).astype(BF16)
            if dil > 1:
                qstreams[pidx][r * n:(r + 1) * n, :] = stream(qf, qm, dil, r).astype(BF16)

    ones = jnp.ones((2 * blk, HEAD_DIM), BF16)
    first = first_tile.astype(jnp.int32)

    def block(pidx, dil, r, sb, out_sc, lse_sc):
        n = t // dil
        per = blk + n
        if dil == 1:
            qb = jnp.concatenate([qf[pl.ds(sb * blk + c, blk // md, stride=md), :] for c in range(md)],
                                 axis=0).astype(BF16)
        else:
            qb = qstreams[pidx][r * n + sb * blk:r * n + (sb + 1) * blk, :]
        krows = slice(r * per + sb * blk, r * per + (sb + 2) * blk)
        kb = kstreams[pidx][krows, :]
        vb = vstreams[pidx][krows, :]
        s = lax.dot_general(qb, kb, _NT, preferred_element_type=F32)
        s = s + bias_ref[0, 2 * pidx + first if sb == 0 else 2 * pidx]
        mx = jnp.max(s, axis=-1, keepdims=True)
        p = jnp.exp2(s - mx).astype(BF16)
        pv = _dot(p, jnp.concatenate([vb, ones], axis=1))
        den = pv[:, HEAD_DIM:]
        out = pv[:, :HEAD_DIM] / den
        lse = mx + jnp.log2(den)
        if dil == 1:
            piece = blk // md
            for c in range(md):
                rows = slice(c * mspan + sb * piece, c * mspan + (sb + 1) * piece)
                out_sc[rows, :] = out[c * piece:(c + 1) * piece, :]
                lse_sc[rows, :] = lse[c * piece:(c + 1) * piece, :]
        elif dil == md:
            rows = slice(r * mspan + sb * blk, r * mspan + (sb + 1) * blk)
            out_sc[rows, :] = out
            lse_sc[rows, :] = lse
        else:
            step = dil // md
            rows = pl.ds((r % md) * mspan + sb * blk * step + r // md, blk, stride=step)
            out_sc[rows, :] = out
            lse_sc[rows, :] = lse

    scratch = ((o0, l0), (o1, l1), (o2, l2))
    for pidx, (_, dil) in enumerate(DILATED_PATTERNS):
        for sb in range(t // (blk * dil)):
            for r in range(dil):
                block(pidx, dil, r, sb, *scratch[pidx])

    la, lb, lc = l0[...], l1[...], l2[...]
    lall = jnp.maximum(jnp.maximum(la, lb), lc)
    wa, wb, wc = jnp.exp2(la - lall), jnp.exp2(lb - lall), jnp.exp2(lc - lall)
    merged = (wa * o0[...] + wb * o1[...] + wc * o2[...]) / (wa + wb + wc)
    for c in range(md):
        qf[pl.ds(c, mspan, stride=md), :] = merged[c * mspan:(c + 1) * mspan, :]
    o_ref[...] = (qf[...] * szd_ref[...].astype(F32)).astype(BF16)


def _attention(proj, bias):
    seq = proj.shape[0]
    t = ATTN_TILE
    q0 = (2 * D_CONFORMER) // HEAD_DIM
    k0 = q0 + N_HEADS
    v0 = k0 + N_HEADS
    z0 = v0 + N_HEADS
    cur = lambda c0: pl.BlockSpec((t, HEAD_DIM), lambda h, i: (i, c0 + h))
    f32 = lambda rows: pltpu.VMEM((rows, HEAD_DIM), F32)
    bf16 = lambda rows: pltpu.VMEM((rows, HEAD_DIM), BF16)
    streams = [bf16(dil * (ATTN_BLOCK + t // dil)) for _, dil in DILATED_PATTERNS]
    return pl.pallas_call(
        _attn_kernel,
        grid=(N_HEADS, seq // t),
        in_specs=[cur(q0), cur(k0), cur(v0), cur(z0),
                  pl.BlockSpec((1,) + bias.shape[1:], lambda h, i: (h, 0, 0, 0))],
        out_specs=pl.BlockSpec((t, HEAD_DIM), lambda h, i: (i, h)),
        out_shape=jax.ShapeDtypeStruct((seq, D_ATTN), BF16),
        scratch_shapes=[f32(t)] * 6 + [bf16(t)] * 2 + streams + streams + [f32(t)] * 6,
        compiler_params=_params(("arbitrary", "arbitrary")),
        name="dilated_attn",
    )(proj, proj, proj, proj, bias)


def _odd_out_kernel(h_ref, yc_ref, yd_ref, wout_ref, g_ref, o_ref, *, tm):
    for r0 in range(0, tm, OUT_ROWS):
        rows = slice(r0, r0 + OUT_ROWS)
        acc = _dot(yc_ref[rows, :], wout_ref[0:D_CONFORMER, :])
        acc = acc + _dot(yd_ref[rows, :], wout_ref[D_CONFORMER:D_CONFORMER + D_ATTN, :])
        o_ref[rows, :] = _rms(h_ref[rows, :] + acc, g_ref[...])


def _odd_out(h, yc, yd, w_out, g, tm):
    seq = h.shape[0]
    row = lambda w: pl.BlockSpec((tm, w), lambda i: (i, 0))
    const = lambda a: pl.BlockSpec(a.shape, lambda i: (0, 0), pipeline_mode=pl.Buffered(1))
    return pl.pallas_call(
        functools.partial(_odd_out_kernel, tm=tm),
        grid=(seq // tm,),
        in_specs=[row(D_MODEL), row(D_CONFORMER), row(D_ATTN), const(w_out), const(g)],
        out_specs=row(D_MODEL),
        out_shape=jax.ShapeDtypeStruct((seq, D_MODEL), F32),
        compiler_params=_params(("arbitrary",)),
        name="odd_out",
    )(h, yc, yd, w_out, g)


def _row_tile(seq, want):
    return min(seq, want)


def kernel(x, norm_g, final_norm_g, ev_w_in, ev_conv_w, s5_lam_re, s5_lam_im, s5_log_dt, s5_b_re, s5_b_im, s5_c_re, s5_c_im, s5_d, s5_w_glu, ev_w_out, od_w_in, cf_conv_w, cf_conv_b, cf_ln_g, cf_ln_b, cf_w_pw, od_w_out, rel_bias):
    bsz, seq, _ = x.shape
    assert bsz == 1 and seq % ATTN_TILE == 0
    x2 = x.reshape(seq, D_MODEL)
    g0 = norm_g[0].astype(F32).reshape(1, D_MODEL)
    g1 = norm_g[1].astype(F32).reshape(1, D_MODEL)
    gf = final_norm_g.astype(F32).reshape(1, D_MODEL)

    ya, u, szb = _even_in(x2, g0, ev_w_in[0].astype(BF16), ev_conv_w[0].astype(F32), _row_tile(seq, 1024), 512)
    tables = _s5_tables(s5_lam_re[0], s5_lam_im[0], s5_log_dt[0], s5_b_re[0], s5_b_im[0],
                        s5_c_re[0], s5_c_im[0], s5_d[0])
    ys = _s5(u, tables, min(seq // SSM_CHUNK, 128))
    h1, hn1 = _even_out(x2, ya, ys, szb, s5_w_glu[0].astype(BF16), ev_w_out[0].astype(BF16), g1,
                        _row_tile(seq, 512))

    proj = _odd_in(hn1, od_w_in[0].astype(BF16), _row_tile(seq, 2048), 1024)
    conv_w = jnp.concatenate([cf_conv_w[0].astype(F32), jnp.zeros((1, D_CONFORMER), F32)], axis=0)
    yc = _conformer(proj, conv_w, cf_conv_b[0].astype(F32).reshape(1, -1),
                    cf_ln_g[0].astype(F32).reshape(1, -1), cf_ln_b[0].astype(F32).reshape(1, -1),
                    cf_w_pw[0].astype(BF16), _row_tile(seq, 1024))
    yd = _attention(proj, _attn_bias(rel_bias))
    out = _odd_out(h1, yc, yd, od_w_out[0].astype(BF16), gf, _row_tile(seq, 512))
    return out.reshape(bsz, seq, D_MODEL)
```

```python
import functools
import math

import numpy as np
import jax
import jax.numpy as jnp
from jax import lax
from jax.experimental import pallas as pl
from jax.experimental.pallas import tpu as pltpu

F32 = jnp.float32
BF16 = jnp.bfloat16

D_MODEL = 2048
D_SHORTCONV = 1536
SHORTCONV_WIDTH = 3
D_SSM = 512
SSM_GROUP = 16
SSM_GROUPS = 32
SSM_STATE = 64
D_CONFORMER = 1024
CONFORMER_WIDTH = 31
D_ATTN = 1024
HEAD_DIM = 128
N_HEADS = 8
DILATED_PATTERNS = ((128, 1), (512, 4), (2048, 16))
ATTN_BLOCK = 128
REL_BUCKETS = 32
REL_MAX_DISTANCE = 2048
NORM_EPS = 1e-6
NEG_INF = -1e30

IN_ROWS = 512
EVEN_ROWS = 256
SSM_CHUNK = 16
SSM_FLAT = SSM_CHUNK * SSM_GROUP
SSM_SLABS = D_SSM // 128
ATTN_TILE = 2048
LOG2E = math.log2(math.e)
Q_SCALE = HEAD_DIM ** -0.5 * LOG2E
MERGE_DIL = 4
CONV_HALO = 32
VMEM_LIMIT = 50 * 1024 * 1024


def _params(sem, vmem=VMEM_LIMIT):
    return pltpu.CompilerParams(dimension_semantics=sem, vmem_limit_bytes=vmem)


def _dot(a, b):
    return jnp.dot(a, b, preferred_element_type=F32)


def _sigmoid(z):
    return 1.0 / (1.0 + jnp.exp(-z))


def _silu(z):
    return z * _sigmoid(z)


def _gelu_tanh(x):
    return 0.5 * x * (1.0 + jnp.tanh(math.sqrt(2.0 / math.pi) * (x + 0.044715 * (x * x * x))))


def _rms(xf, g):
    ms = jnp.mean(xf * xf, axis=-1, keepdims=True)
    return xf * lax.rsqrt(ms + NORM_EPS) * g


def _even_in_kernel(x_ref, g_ref, wa_ref, wb_ref, wc_ref, wd_ref, cw_ref,
                    ya_ref, u_ref, szb_ref, hn_sc, pbuf_sc, carry_sc, *, tm, col):
    i = pl.program_id(0)
    j = pl.program_id(1)
    nconv = D_SHORTCONV // col

    def conv_step(first):
        @pl.when(i == 0)
        def _():
            carry_sc[j] = jnp.zeros((8, col), F32)

        pbuf_sc[0:8, :] = carry_sc[j]
        cw = cw_ref[...]
        for r0 in range(0, tm, EVEN_ROWS):
            rows = slice(r0, r0 + EVEN_ROWS)
            if first:
                hn = _rms(x_ref[rows, :], g_ref[...]).astype(BF16)
                hn_sc[rows, :] = hn
            else:
                hn = hn_sc[rows, :]
            p = _dot(hn, wb_ref[...]) * _dot(hn, wa_ref[...])
            pbuf_sc[8 + r0:8 + r0 + EVEN_ROWS, :] = p
            conv = (cw[2:3, :] * p + cw[1:2, :] * pbuf_sc[pl.ds(7 + r0, EVEN_ROWS), :]
                    + cw[0:1, :] * pbuf_sc[pl.ds(6 + r0, EVEN_ROWS), :])
            gb = _dot(hn, wc_ref[...])
            za = _dot(hn, wd_ref[...])
            ya_ref[rows, :] = (gb * conv * _silu(za)).astype(BF16)
        carry_sc[j] = pbuf_sc[tm:tm + 8, :]

    pl.when(j == 0)(functools.partial(conv_step, True))
    pl.when((j > 0) & (j < nconv))(functools.partial(conv_step, False))

    @pl.when(j >= nconv)
    def _():
        for r0 in range(0, tm, EVEN_ROWS):
            rows = slice(r0, r0 + EVEN_ROWS)
            hn = hn_sc[rows, :]
            u = _dot(hn, wa_ref[...])
            for s in range(col // 128):
                u_ref[s, rows, :] = u[:, s * 128:(s + 1) * 128]
            szb_ref[rows, :] = _silu(_dot(hn, wb_ref[...])).astype(BF16)


def _even_in(x, g, w_in, conv_w, tm, col):
    seq = x.shape[0]
    nconv = D_SHORTCONV // col
    nssm = D_SSM // col
    u_blk = 4 * nconv

    def wspec(k, alt0):
        return pl.BlockSpec((D_MODEL, col),
                            lambda i, j: (0, jnp.where(j < nconv, j + nconv * k, alt0 + j - nconv)))

    hold = lambda k: pl.BlockSpec((D_MODEL, col), lambda i, j: (0, jnp.minimum(j, nconv - 1) + nconv * k))
    ssm_step = lambda j: jnp.maximum(j - nconv, 0)
    return pl.pallas_call(
        functools.partial(_even_in_kernel, tm=tm, col=col),
        grid=(seq // tm, nconv + nssm),
        in_specs=[
            pl.BlockSpec((tm, D_MODEL), lambda i, j: (i, 0)),
            pl.BlockSpec((1, D_MODEL), lambda i, j: (0, 0)),
            wspec(0, u_blk), wspec(1, u_blk + nssm), hold(2), hold(3),
            pl.BlockSpec((SHORTCONV_WIDTH, col), lambda i, j: (0, jnp.minimum(j, nconv - 1))),
        ],
        out_specs=[
            pl.BlockSpec((tm, col), lambda i, j: (i, jnp.minimum(j, nconv - 1))),
            pl.BlockSpec((col // 128, tm, 128), lambda i, j: (ssm_step(j), i, 0)),
            pl.BlockSpec((tm, col), lambda i, j: (i, ssm_step(j))),
        ],
        out_shape=[
            jax.ShapeDtypeStruct((seq, D_SHORTCONV), BF16),
            jax.ShapeDtypeStruct((SSM_SLABS, seq, 128), F32),
            jax.ShapeDtypeStruct((seq, D_SSM), BF16),
        ],
        scratch_shapes=[
            pltpu.VMEM((tm, D_MODEL), BF16),
            pltpu.VMEM((tm + 8, col), F32),
            pltpu.VMEM((nconv, 8, col), F32),
        ],
        compiler_params=_params(("arbitrary", "arbitrary")),
        name="even_in",
    )(x, g, w_in, w_in, w_in, w_in, conv_w)


def _s5_tables(lam_re, lam_im, log_dt, b_re, b_im, c_re, c_im, d_skip):
    hi = lax.Precision.HIGH
    t_, g_, n_, p_ = SSM_CHUNK, SSM_GROUPS, SSM_STATE, SSM_GROUP
    lam_re, lam_im = lam_re.astype(F32), lam_im.astype(F32)
    dt = jnp.exp(log_dt.astype(F32))[:, None]
    xr, xi = lam_re * dt, lam_im * dt
    k = jnp.arange(t_ + 1, dtype=F32)[:, None, None]
    mag = jnp.exp(xr * k)
    pr, pi = mag * jnp.cos(xi * k), mag * jnp.sin(xi * k)
    nr, ni = pr[1] - 1.0, pi[1]
    den = lam_re * lam_re + lam_im * lam_im
    fr = (nr * lam_re + ni * lam_im) / den
    fi = (ni * lam_re - nr * lam_im) / den
    b_re, b_im = b_re.astype(F32), b_im.astype(F32)
    bbr = fr[..., None] * b_re - fi[..., None] * b_im
    bbi = fr[..., None] * b_im + fi[..., None] * b_re
    c_re, c_im = c_re.astype(F32), c_im.astype(F32)

    cpr = c_re[None] * pr[:t_, :, None, :] - c_im[None] * pi[:t_, :, None, :]
    cpi = c_re[None] * pi[:t_, :, None, :] + c_im[None] * pr[:t_, :, None, :]
    kern = jnp.einsum('tgpn,gnq->tgpq', jnp.concatenate([cpr, -cpi], axis=-1),
                      jnp.concatenate([bbr, bbi], axis=1), precision=hi)
    skip = d_skip.astype(F32).reshape(g_, p_)[:, :, None] * jnp.eye(p_, dtype=F32)[None]
    kern = kern.at[0].add(skip)
    flat = t_ * p_
    last = jnp.transpose(kern[::-1], (1, 2, 0, 3)).reshape(g_, p_, flat)
    mt = jnp.stack([jnp.pad(last[:, :, (t_ - 1 - t) * p_:], ((0, 0), (0, 0), (0, (t_ - 1 - t) * p_)))
                    for t in range(t_)], axis=1).reshape(g_, flat, flat)

    prj, pij = pr[t_ - 1::-1][:t_], pi[t_ - 1::-1][:t_]
    wr = prj[..., None] * bbr[None] - pij[..., None] * bbi[None]
    wi = prj[..., None] * bbi[None] + pij[..., None] * bbr[None]
    half = (np.arange(2) == 0)[None, None, :, None, None]

    def pack_w(w):
        w = jnp.transpose(w.reshape(t_, g_ // 2, 2, n_, p_), (1, 0, 2, 4, 3))
        w = jnp.concatenate([jnp.where(half, w, 0.0), jnp.where(half, 0.0, w)], axis=-1)
        return w.reshape(g_ // 2, 2 * flat, 2 * n_)

    vr = c_re[None] * pr[1:, :, None, :] - c_im[None] * pi[1:, :, None, :]
    vi = c_re[None] * pi[1:, :, None, :] + c_im[None] * pr[1:, :, None, :]
    vr = jnp.transpose(vr, (1, 0, 2, 3)).reshape(g_ // 2, 2, flat, n_)
    vi = -jnp.transpose(vi, (1, 0, 2, 3)).reshape(g_ // 2, 2, flat, n_)
    zv = jnp.zeros((g_ // 2, flat, n_), F32)
    v_even = jnp.concatenate([vr[:, 0], zv, vi[:, 0], zv], axis=-1)
    v_odd = jnp.concatenate([zv, vr[:, 1], zv, vi[:, 1]], axis=-1)
    vt = jnp.stack([v_even, v_odd], axis=1).reshape(g_, flat, 4 * n_)
    a_re = pr[t_].reshape(1, g_ * n_)
    a_im = pi[t_].reshape(1, g_ * n_)
    return mt.astype(BF16), pack_w(wr).astype(BF16), pack_w(wi).astype(BF16), vt.astype(BF16), a_re, a_im


_NT = (((1,), (1,)), ((), ()))
_TN = (((0,), (0,)), ((), ()))


def _s5_kernel(u_ref, mt_ref, wre_ref, wim_ref, vt_ref, are_ref, aim_ref, y_ref,
               xt_sc, yt_sc, zre_sc, zim_sc, carry_sc, *, ct):
    i = pl.program_id(0)
    t_ = SSM_CHUNK

    @pl.when(i == 0)
    def _():
        carry_sc[...] = jnp.zeros_like(carry_sc)

    eye = jnp.where(lax.broadcasted_iota(jnp.int32, (128, 128), 0)
                    == lax.broadcasted_iota(jnp.int32, (128, 128), 1), 1.0, 0.0).astype(BF16)

    for s in range(SSM_SLABS):
        for j in range(t_):
            uj = u_ref[s, pl.ds(j, ct, stride=t_), :].astype(BF16)
            xt_sc[s, j] = lax.dot_general(eye, uj, _NT, preferred_element_type=F32).astype(BF16)

    pairs_per_slab = 128 // (2 * SSM_GROUP)
    for gp in range(SSM_GROUPS // 2):
        s, k = gp // pairs_per_slab, gp % pairs_per_slab
        xp = xt_sc[s, :, 32 * k:32 * k + 32, :].reshape(2 * SSM_FLAT, ct)
        zre_sc[:, gp * 128:(gp + 1) * 128] = lax.dot_general(xp, wre_ref[gp], _TN, preferred_element_type=F32)
        zim_sc[:, gp * 128:(gp + 1) * 128] = lax.dot_general(xp, wim_ref[gp], _TN, preferred_element_type=F32)

    are = are_ref[...]
    aim = aim_ref[...]

    def step(r, carry):
        sre, sim = carry
        zr = zre_sc[pl.ds(r, 1), :]
        zi = zim_sc[pl.ds(r, 1), :]
        zre_sc[pl.ds(r, 1), :] = sre
        zim_sc[pl.ds(r, 1), :] = sim
        return are * sre - aim * sim + zr, are * sim + aim * sre + zi

    sre, sim = lax.fori_loop(0, ct, step, (carry_sc[0:1, :], carry_sc[1:2, :]), unroll=8)
    carry_sc[0:1, :] = sre
    carry_sc[1:2, :] = sim

    groups_per_slab = 128 // SSM_GROUP
    for g in range(SSM_GROUPS):
        gp, s, k = g // 2, g // groups_per_slab, g % groups_per_slab
        xg = xt_sc[s, :, 16 * k:16 * k + 16, :].reshape(SSM_FLAT, ct)
        sp = jnp.concatenate([zre_sc[:, gp * 128:(gp + 1) * 128],
                              zim_sc[:, gp * 128:(gp + 1) * 128]], axis=1).astype(BF16)
        yt = _dot(mt_ref[g], xg) + lax.dot_general(vt_ref[g], sp, _NT, preferred_element_type=F32)
        yt_sc[s, :, 16 * k:16 * k + 16, :] = yt.reshape(t_, SSM_GROUP, ct)

    for s in range(SSM_SLABS):
        for t in range(t_):
            yt = yt_sc[s, t]
            hi = yt.astype(BF16)
            lo = (yt - hi.astype(F32)).astype(BF16)
            y = (lax.dot_general(hi, eye, _TN, preferred_element_type=F32)
                 + lax.dot_general(lo, eye, _TN, preferred_element_type=F32))
            y_ref[s, pl.ds(t, ct, stride=t_), :] = y


def _s5(u4, tables, ct):
    mt, wre, wim, vt, are, aim = tables
    _, seq, _ = u4.shape
    full = lambda a: pl.BlockSpec(a.shape, lambda i: (0,) * a.ndim)
    rows = pl.BlockSpec((SSM_SLABS, SSM_CHUNK * ct, 128), lambda i: (0, i, 0))
    nstate = SSM_GROUPS * SSM_STATE
    return pl.pallas_call(
        functools.partial(_s5_kernel, ct=ct),
        grid=(seq // (SSM_CHUNK * ct),),
        in_specs=[rows, full(mt), full(wre), full(wim), full(vt), full(are), full(aim)],
        out_specs=rows,
        out_shape=jax.ShapeDtypeStruct(u4.shape, F32),
        scratch_shapes=[pltpu.VMEM((SSM_SLABS, SSM_CHUNK, 128, ct), BF16),
                        pltpu.VMEM((SSM_SLABS, SSM_CHUNK, 128, ct), F32),
                        pltpu.VMEM((ct, nstate), F32),
                        pltpu.VMEM((ct, nstate), F32),
                        pltpu.VMEM((8, nstate), F32)],
        compiler_params=_params(("arbitrary",)),
        name="s5_ssm",
    )(u4, mt, wre, wim, vt, are, aim)


OUT_ROWS = 256


def _even_out_kernel(x_ref, ya_ref, ys_ref, szb_ref, wglu_ref, wout_ref, g_ref, h_ref, hn_ref, *, tm):
    for r0 in range(0, tm, OUT_ROWS):
        rows = slice(r0, r0 + OUT_ROWS)
        gl = _gelu_tanh(jnp.concatenate([ys_ref[s, rows, :] for s in range(SSM_SLABS)], axis=1))
        yb = gl * _sigmoid(_dot(gl.astype(BF16), wglu_ref[...])) * szb_ref[rows, :].astype(F32)
        acc = _dot(ya_ref[rows, :], wout_ref[0:D_SHORTCONV, :])
        acc = acc + _dot(yb.astype(BF16), wout_ref[D_SHORTCONV:D_SHORTCONV + D_SSM, :])
        h = x_ref[rows, :] + acc
        h_ref[rows, :] = h
        hn_ref[rows, :] = _rms(h, g_ref[...]).astype(BF16)


def _even_out(x, ya, ys, szb, w_glu, w_out, g, tm):
    seq = x.shape[0]
    row = lambda w: pl.BlockSpec((tm, w), lambda i: (i, 0))
    const = lambda a: pl.BlockSpec(a.shape, lambda i: (0, 0), pipeline_mode=pl.Buffered(1))
    return pl.pallas_call(
        functools.partial(_even_out_kernel, tm=tm),
        grid=(seq // tm,),
        in_specs=[row(D_MODEL), row(D_SHORTCONV),
                  pl.BlockSpec((SSM_SLABS, tm, 128), lambda i: (0, i, 0)), row(D_SSM),
                  const(w_glu), const(w_out), const(g)],
        out_specs=[row(D_MODEL), row(D_MODEL)],
        out_shape=[jax.ShapeDtypeStruct((seq, D_MODEL), F32),
                   jax.ShapeDtypeStruct((seq, D_MODEL), BF16)],
        compiler_params=_params(("arbitrary",)),
        name="even_out",
    )(x, ya, ys, szb, w_glu, w_out, g)


ODD_GROUPS = 6


def _odd_in_kernel(hn_ref, wa_ref, wb_ref, o_ref, *, tm, col):
    s = pl.program_id(1)
    nb = D_CONFORMER // col
    group = s // nb
    is_silu = (group == 1) | (group == 5)
    blocks = [slice(r0, r0 + IN_ROWS) for r0 in range(0, tm, IN_ROWS)]

    @pl.when(group == 0)
    def _():
        for rows in blocks:
            hn = hn_ref[rows, :]
            o_ref[rows, :] = (_dot(hn, wa_ref[...]) * _sigmoid(_dot(hn, wb_ref[...]))).astype(BF16)

    @pl.when(is_silu)
    def _():
        for rows in blocks:
            o_ref[rows, :] = _silu(_dot(hn_ref[rows, :], wa_ref[...])).astype(BF16)

    @pl.when((group > 0) & jnp.logical_not(is_silu))
    def _():
        scale = jnp.where(group == 2, Q_SCALE, 1.0).astype(F32)
        for rows in blocks:
            o_ref[rows, :] = (_dot(hn_ref[rows, :], wa_ref[...]) * scale).astype(BF16)


def _odd_in(hn, w_in, tm, col):
    seq = hn.shape[0]
    nb = D_CONFORMER // col
    return pl.pallas_call(
        functools.partial(_odd_in_kernel, tm=tm, col=col),
        grid=(seq // tm, ODD_GROUPS * nb),
        in_specs=[pl.BlockSpec((tm, D_MODEL), lambda i, s: (i, 0)),
                  pl.BlockSpec((D_MODEL, col), lambda i, s: (0, jnp.where(s < nb, s, s + nb))),
                  pl.BlockSpec((D_MODEL, col), lambda i, s: (0, jnp.where(s < nb, s + nb, 2 * nb - 1)))],
        out_specs=pl.BlockSpec((tm, col), lambda i, s: (i, s)),
        out_shape=jax.ShapeDtypeStruct((seq, ODD_GROUPS * D_CONFORMER), BF16),
        compiler_params=_params(("arbitrary", "arbitrary")),
        name="odd_in",
    )(hn, w_in, w_in)


CONV_ROWS = 64


def _conformer_kernel(cur_ref, halo_ref, szc_ref, cw_ref, cb_ref, lg_ref, lb_ref, wpw_ref,
                      o_ref, buf_sc, conv_sc, *, tm):
    i = pl.program_id(0)
    nslab = D_CONFORMER // 128
    for c in range(nslab):
        lanes = slice(c * 128, (c + 1) * 128)
        buf_sc[c, 0:CONV_HALO, :] = jnp.where(i == 0, 0.0, halo_ref[:, lanes].astype(F32))
        buf_sc[c, CONV_HALO:CONV_HALO + tm, :] = cur_ref[:, lanes].astype(F32)
    shift = CONV_HALO - (CONFORMER_WIDTH - 1)

    def rows(rc, _):
        base = pl.multiple_of(rc * CONV_ROWS, CONV_ROWS)
        for c in range(nslab):
            lanes = slice(c * 128, (c + 1) * 128)
            acc = jnp.zeros((CONV_ROWS, 128), F32) + cb_ref[:, lanes]
            for k in range(CONFORMER_WIDTH):
                acc = acc + cw_ref[k:k + 1, lanes] * buf_sc[c, pl.ds(base + (shift + k), CONV_ROWS), :]
            conv_sc[pl.ds(base, CONV_ROWS), lanes] = acc
        return 0

    lax.fori_loop(0, tm // CONV_ROWS, rows, 0)
    for r0 in range(0, tm, OUT_ROWS):
        sub = slice(r0, r0 + OUT_ROWS)
        y = conv_sc[sub, :]
        mu = jnp.mean(y, axis=-1, keepdims=True)
        yc = y - mu
        var = jnp.mean(yc * yc, axis=-1, keepdims=True)
        ln = yc * lax.rsqrt(var + NORM_EPS) * lg_ref[...] + lb_ref[...]
        o_ref[sub, :] = (_dot(_silu(ln).astype(BF16), wpw_ref[...]) * szc_ref[sub, :].astype(F32)).astype(BF16)


def _conformer(proj, conv_w, conv_b, ln_g, ln_b, w_pw, tm):
    seq = proj.shape[0]
    per = tm // CONV_HALO
    const = lambda a: pl.BlockSpec(a.shape, lambda i: (0, 0))
    return pl.pallas_call(
        functools.partial(_conformer_kernel, tm=tm),
        grid=(seq // tm,),
        in_specs=[pl.BlockSpec((tm, D_CONFORMER), lambda i: (i, 0)),
                  pl.BlockSpec((CONV_HALO, D_CONFORMER), lambda i: (jnp.maximum(i * per - 1, 0), 0)),
                  pl.BlockSpec((tm, D_CONFORMER), lambda i: (i, 1)),
                  const(conv_w), const(conv_b), const(ln_g), const(ln_b), const(w_pw)],
        out_specs=pl.BlockSpec((tm, D_CONFORMER), lambda i: (i, 0)),
        out_shape=jax.ShapeDtypeStruct((seq, D_CONFORMER), BF16),
        scratch_shapes=[pltpu.VMEM((D_CONFORMER // 128, tm + CONV_HALO, 128), F32),
                        pltpu.VMEM((tm, D_CONFORMER), F32)],
        compiler_params=_params(("arbitrary",)),
        name="conformer",
    )(proj, proj, proj, conv_w, conv_b, ln_g, ln_b, w_pw)


def _t5_bucket(dist):
    max_exact = REL_BUCKETS // 2
    d_f = jnp.maximum(dist, 1).astype(F32)
    large = max_exact + (jnp.log(d_f / max_exact) / math.log(REL_MAX_DISTANCE / max_exact)
                         * (REL_BUCKETS - max_exact)).astype(jnp.int32)
    large = jnp.minimum(large, REL_BUCKETS - 1)
    return jnp.where(dist < max_exact, dist, large)


def _attn_bias(rel_bias):
    q = ATTN_BLOCK
    nh = rel_bias.shape[1]
    kj = np.arange(2 * q)[None, None, :]
    neg = lambda n: jnp.full((nh, n), NEG_INF, F32)
    tabs = []
    for window, dilation in DILATED_PATTERNS:
        steps = window // dilation
        assert steps == q
        rel = jnp.asarray(np.arange(steps + 1) * dilation, dtype=jnp.int32)
        by_rel = rel_bias.astype(F32)[_t5_bucket(rel)].T
        g = jnp.concatenate([by_rel[:, ::-1], neg(2 * q - 1)], axis=1)
        flat = jnp.tile(g, (1, q))[:, :q * (3 * q - 1)]
        tab = flat.reshape(nh, q, 3 * q - 1)[:, :, :2 * q]
        if dilation == 1:
            tab = tab.reshape(nh, q // MERGE_DIL, MERGE_DIL, 2 * q).transpose(0, 2, 1, 3).reshape(nh, q, 2 * q)
        tab = tab * LOG2E
        tabs.append(tab)
        tabs.append(jnp.where(kj >= q, tab, NEG_INF))
    return jnp.stack(tabs, axis=1)


def _attn_kernel(q_ref, k_ref, v_ref, szd_ref, bias_ref, o_ref,
                 qf, kf, vf, qm, km, vm, qs1, qs2, ks0, ks1, ks2, vs0, vs1, vs2,
                 o0, o1, o2, l0, l1, l2):
    i = pl.program_id(1)
    t = ATTN_TILE
    blk = ATTN_BLOCK
    first_tile = i == 0
    qf[...] = q_ref[...].astype(F32)
    kf[...] = k_ref[...].astype(F32)
    vf[...] = v_ref[...].astype(F32)
    qstreams = (None, qs1, qs2)
    kstreams = (ks0, ks1, ks2)
    vstreams = (vs0, vs1, vs2)
    md = MERGE_DIL
    mspan = t // md

    def stream(src_f, src_m, dil, r):
        n = t // dil
        if dil == md:
            x = src_f[pl.ds(r, n, stride=dil), :]
            src_m[r * n:(r + 1) * n, :] = x
            return x
        return src_m[pl.ds((r % md) * mspan + r // md, n, stride=dil // md), :]

    for pidx, (_, dil) in enumerate(DILATED_PATTERNS):
        n = t // dil
        per = blk + n
        for r in range(dil):
            base = r * per
            for src_f, src_m, src_ref, dst in ((kf, km, k_ref, kstreams[pidx]), (vf, vm, v_ref, vstreams[pidx])):
                tail = dst[base + n:base + per, :]
                dst[base:base + blk, :] = jnp.where(first_tile, jnp.zeros_like(tail), tail)
                if dil == 1:
                    dst[base + blk:base + per, :] = src_ref[...]
                else:
                    dst[base + blk:base + per, :] = stream(src_f, src_m, dil, r).astype(BF16)
            if dil > 1:
                qstreams[pidx][r * n:(r + 1) * n, :] = stream(qf, qm, dil, r).astype(BF16)

    ones = jnp.ones((2 * blk, HEAD_DIM), BF16)
    first = first_tile.astype(jnp.int32)

    def block(pidx, dil, r, sb, out_sc, lse_sc):
        n = t // dil
        per = blk + n
        if dil == 1:
            qb = jnp.concatenate([qf[pl.ds(sb * blk + c, blk // md, stride=md), :] for c in range(md)],
                                 axis=0).astype(BF16)
        else:
            qb = qstreams[pidx][r * n + sb * blk:r * n + (sb + 1) * blk, :]
        krows = slice(r * per + sb * blk, r * per + (sb + 2) * blk)
        kb = kstreams[pidx][krows, :]
        vb = vstreams[pidx][krows, :]
        s = lax.dot_general(qb, kb, _NT, preferred_element_type=F32)
        s = s + bias_ref[0, 2 * pidx + first if sb == 0 else 2 * pidx]
        mx = jnp.max(s, axis=-1, keepdims=True)
        p = jnp.exp2(s - mx).astype(BF16)
        pv = _dot(p, jnp.concatenate([vb, ones], axis=1))
        den = pv[:, HEAD_DIM:]
        out = pv[:, :HEAD_DIM] / den
        lse = mx + jnp.log2(den)
        if dil == 1:
            piece = blk // md
            for c in range(md):
                rows = slice(c * mspan + sb * piece, c * mspan + (sb + 1) * piece)
                out_sc[rows, :] = out[c * piece:(c + 1) * piece, :]
                lse_sc[rows, :] = lse[c * piece:(c + 1) * piece, :]
        elif dil == md:
            rows = slice(r * mspan + sb * blk, r * mspan + (sb + 1) * blk)
            out_sc[rows, :] = out
            lse_sc[rows, :] = lse
        else:
            step = dil // md
            rows = pl.ds((r % md) * mspan + sb * blk * step + r // md, blk, stride=step)
            out_sc[rows, :] = out
            lse_sc[rows, :] = lse

    scratch = ((o0, l0), (o1, l1), (o2, l2))
    for pidx, (_, dil) in enumerate(DILATED_PATTERNS):
        for sb in range(t // (blk * dil)):
            for r in range(dil):
                block(pidx, dil, r, sb, *scratch[pidx])

    la, lb, lc = l0[...], l1[...], l2[...]
    lall = jnp.maximum(jnp.maximum(la, lb), lc)
    wa, wb, wc = jnp.exp2(la - lall), jnp.exp2(lb - lall), jnp.exp2(lc - lall)
    merged = (wa * o0[...] + wb * o1[...] + wc * o2[...]) / (wa + wb + wc)
    for c in range(md):
        qf[pl.ds(c, mspan, stride=md), :] = merged[c * mspan:(c + 1) * mspan, :]
    o_ref[...] = (qf[...] * szd_ref[...].astype(F32)).astype(BF16)


def _attention(proj, bias):
    seq = proj.shape[0]
    t = ATTN_TILE
    q0 = (2 * D_CONFORMER) // HEAD_DIM
    k0 = q0 + N_HEADS
    v0 = k0 + N_HEADS
    z0 = v0 + N_HEADS
    cur = lambda c0: pl.BlockSpec((t, HEAD_DIM), lambda h, i: (i, c0 + h))
    f32 = lambda rows: pltpu.VMEM((rows, HEAD_DIM), F32)
    bf16 = lambda rows: pltpu.VMEM((rows, HEAD_DIM), BF16)
    streams = [bf16(dil * (ATTN_BLOCK + t // dil)) for _, dil in DILATED_PATTERNS]
    return pl.pallas_call(
        _attn_kernel,
        grid=(N_HEADS, seq // t),
        in_specs=[cur(q0), cur(k0), cur(v0), cur(z0),
                  pl.BlockSpec((1,) + bias.shape[1:], lambda h, i: (h, 0, 0, 0))],
        out_specs=pl.BlockSpec((t, HEAD_DIM), lambda h, i: (i, h)),
        out_shape=jax.ShapeDtypeStruct((seq, D_ATTN), BF16),
        scratch_shapes=[f32(t)] * 6 + [bf16(t)] * 2 + streams + streams + [f32(t)] * 6,
        compiler_params=_params(("arbitrary", "arbitrary")),
        name="dilated_attn",
    )(proj, proj, proj, proj, bias)


def _odd_out_kernel(h_ref, yc_ref, yd_ref, wout_ref, g_ref, o_ref, *, tm):
    for r0 in range(0, tm, OUT_ROWS):
        rows = slice(r0, r0 + OUT_ROWS)
        acc = _dot(yc_ref[rows, :], wout_ref[0:D_CONFORMER, :])
        acc = acc + _dot(yd_ref[rows, :], wout_ref[D_CONFORMER:D_CONFORMER + D_ATTN, :])
        o_ref[rows, :] = _rms(h_ref[rows, :] + acc, g_ref[...])


def _odd_out(h, yc, yd, w_out, g, tm):
    seq = h.shape[0]
    row = lambda w: pl.BlockSpec((tm, w), lambda i: (i, 0))
    const = lambda a: pl.BlockSpec(a.shape, lambda i: (0, 0), pipeline_mode=pl.Buffered(1))
    return pl.pallas_call(
        functools.partial(_odd_out_kernel, tm=tm),
        grid=(seq // tm,),
        in_specs=[row(D_MODEL), row(D_CONFORMER), row(D_ATTN), const(w_out), const(g)],
        out_specs=row(D_MODEL),
        out_shape=jax.ShapeDtypeStruct((seq, D_MODEL), F32),
        compiler_params=_params(("arbitrary",)),
        name="odd_out",
    )(h, yc, yd, w_out, g)


def _row_tile(seq, want):
    return min(seq, want)


def kernel(x, norm_g, final_norm_g, ev_w_in, ev_conv_w, s5_lam_re, s5_lam_im, s5_log_dt, s5_b_re, s5_b_im, s5_c_re, s5_c_im, s5_d, s5_w_glu, ev_w_out, od_w_in, cf_conv_w, cf_conv_b, cf_ln_g, cf_ln_b, cf_w_pw, od_w_out, rel_bias):
    bsz, seq, _ = x.shape
    assert bsz == 1 and seq % ATTN_TILE == 0
    x2 = x.reshape(seq, D_MODEL)
    g0 = norm_g[0].astype(F32).reshape(1, D_MODEL)
    g1 = norm_g[1].astype(F32).reshape(1, D_MODEL)
    gf = final_norm_g.astype(F32).reshape(1, D_MODEL)

    ya, u, szb = _even_in(x2, g0, ev_w_in[0].astype(BF16), ev_conv_w[0].astype(F32), _row_tile(seq, 1024), 512)
    tables = _s5_tables(s5_lam_re[0], s5_lam_im[0], s5_log_dt[0], s5_b_re[0], s5_b_im[0],
                        s5_c_re[0], s5_c_im[0], s5_d[0])
    ys = _s5(u, tables, min(seq // SSM_CHUNK, 128))
    h1, hn1 = _even_out(x2, ya, ys, szb, s5_w_glu[0].astype(BF16), ev_w_out[0].astype(BF16), g1,
                        _row_tile(seq, 512))

    proj = _odd_in(hn1, od_w_in[0].astype(BF16), _row_tile(seq, 2048), 512)
    conv_w = jnp.concatenate([cf_conv_w[0].astype(F32), jnp.zeros((1, D_CONFORMER), F32)], axis=0)
    yc = _conformer(proj, conv_w, cf_conv_b[0].astype(F32).reshape(1, -1),
                    cf_ln_g[0].astype(F32).reshape(1, -1), cf_ln_b[0].astype(F32).reshape(1, -1),
                    cf_w_pw[0].astype(BF16), _row_tile(seq, 1024))
    yd = _attention(proj, _attn_bias(rel_bias))
    out = _odd_out(h1, yc, yd, od_w_out[0].astype(BF16), gf, _row_tile(seq, 512))
    return out.reshape(bsz, seq, D_MODEL)
```

```python
import functools
import math

import numpy as np
import jax
import jax.numpy as jnp
from jax import lax
from jax.experimental import pallas as pl
from jax.experimental.pallas import tpu as pltpu

F32 = jnp.float32
BF16 = jnp.bfloat16

D_MODEL = 2048
D_SHORTCONV = 1536
SHORTCONV_WIDTH = 3
D_SSM = 512
SSM_GROUP = 16
SSM_GROUPS = 32
SSM_STATE = 64
D_CONFORMER = 1024
CONFORMER_WIDTH = 31
D_ATTN = 1024
HEAD_DIM = 128
N_HEADS = 8
DILATED_PATTERNS = ((128, 1), (512, 4), (2048, 16))
ATTN_BLOCK = 128
REL_BUCKETS = 32
REL_MAX_DISTANCE = 2048
NORM_EPS = 1e-6
NEG_INF = -1e30

IN_ROWS = 512
EVEN_ROWS = 256
SSM_CHUNK = 16
SSM_FLAT = SSM_CHUNK * SSM_GROUP
SSM_SLABS = D_SSM // 128
ATTN_TILE = 2048
LOG2E = math.log2(math.e)
Q_SCALE = HEAD_DIM ** -0.5 * LOG2E
MERGE_DIL = 4
CONV_HALO = 32
VMEM_LIMIT = 50 * 1024 * 1024


def _params(sem, vmem=VMEM_LIMIT):
    return pltpu.CompilerParams(dimension_semantics=sem, vmem_limit_bytes=vmem)


def _dot(a, b):
    return jnp.dot(a, b, preferred_element_type=F32)


def _sigmoid(z):
    return 1.0 / (1.0 + jnp.exp(-z))


def _silu(z):
    return z * _sigmoid(z)


def _gelu_tanh(x):
    return 0.5 * x * (1.0 + jnp.tanh(math.sqrt(2.0 / math.pi) * (x + 0.044715 * (x * x * x))))


def _cast_slice_spec(a, nsteps, step_of):
    rows = a.shape[0] // nsteps
    assert rows * nsteps == a.shape[0] and rows % 16 == 0, (a.shape, nsteps)
    return pl.BlockSpec((rows, a.shape[1]), lambda *idx: (step_of(*idx), 0))


def _rms(xf, g):
    ms = jnp.mean(xf * xf, axis=-1, keepdims=True)
    return xf * lax.rsqrt(ms + NORM_EPS) * g


def _even_in_kernel(x_ref, g_ref, wa_ref, wb_ref, wc_ref, wd_ref, cw_ref, c0_ref, c1_ref,
                    ya_ref, u_ref, szb_ref, c0b_ref, c1b_ref, hn_sc, pbuf_sc, carry_sc, *, tm, col):
    i = pl.program_id(0)
    j = pl.program_id(1)
    nconv = D_SHORTCONV // col
    c0b_ref[...] = c0_ref[...].astype(BF16)
    c1b_ref[...] = c1_ref[...].astype(BF16)

    def conv_step(first):
        @pl.when(i == 0)
        def _():
            carry_sc[j] = jnp.zeros((8, col), F32)

        pbuf_sc[0:8, :] = carry_sc[j]
        cw = cw_ref[...]
        for r0 in range(0, tm, EVEN_ROWS):
            rows = slice(r0, r0 + EVEN_ROWS)
            if first:
                hn = _rms(x_ref[rows, :], g_ref[...]).astype(BF16)
                hn_sc[rows, :] = hn
            else:
                hn = hn_sc[rows, :]
            p = _dot(hn, wb_ref[...]) * _dot(hn, wa_ref[...])
            pbuf_sc[8 + r0:8 + r0 + EVEN_ROWS, :] = p
            conv = (cw[2:3, :] * p + cw[1:2, :] * pbuf_sc[pl.ds(7 + r0, EVEN_ROWS), :]
                    + cw[0:1, :] * pbuf_sc[pl.ds(6 + r0, EVEN_ROWS), :])
            gb = _dot(hn, wc_ref[...])
            za = _dot(hn, wd_ref[...])
            ya_ref[rows, :] = (gb * conv * _silu(za)).astype(BF16)
        carry_sc[j] = pbuf_sc[tm:tm + 8, :]

    pl.when(j == 0)(functools.partial(conv_step, True))
    pl.when((j > 0) & (j < nconv))(functools.partial(conv_step, False))

    @pl.when(j >= nconv)
    def _():
        for r0 in range(0, tm, EVEN_ROWS):
            rows = slice(r0, r0 + EVEN_ROWS)
            hn = hn_sc[rows, :]
            u = _dot(hn, wa_ref[...])
            for s in range(col // 128):
                u_ref[s, rows, :] = u[:, s * 128:(s + 1) * 128]
            szb_ref[rows, :] = _silu(_dot(hn, wb_ref[...])).astype(BF16)


def _even_in(x, g, w_in, conv_w, cast0, cast1, tm, col):
    seq = x.shape[0]
    nconv = D_SHORTCONV // col
    nssm = D_SSM // col
    u_blk = 4 * nconv

    def wspec(k, alt0):
        return pl.BlockSpec((D_MODEL, col),
                            lambda i, j: (0, jnp.where(j < nconv, j + nconv * k, alt0 + j - nconv)))

    hold = lambda k: pl.BlockSpec((D_MODEL, col), lambda i, j: (0, jnp.minimum(j, nconv - 1) + nconv * k))
    ssm_step = lambda j: jnp.maximum(j - nconv, 0)
    nj = nconv + nssm
    cast_spec = lambda a: _cast_slice_spec(a, (seq // tm) * nj, lambda i, j: i * nj + j)
    return pl.pallas_call(
        functools.partial(_even_in_kernel, tm=tm, col=col),
        grid=(seq // tm, nj),
        in_specs=[
            pl.BlockSpec((tm, D_MODEL), lambda i, j: (i, 0)),
            pl.BlockSpec((1, D_MODEL), lambda i, j: (0, 0)),
            wspec(0, u_blk), wspec(1, u_blk + nssm), hold(2), hold(3),
            pl.BlockSpec((SHORTCONV_WIDTH, col), lambda i, j: (0, jnp.minimum(j, nconv - 1))),
            cast_spec(cast0), cast_spec(cast1),
        ],
        out_specs=[
            pl.BlockSpec((tm, col), lambda i, j: (i, jnp.minimum(j, nconv - 1))),
            pl.BlockSpec((col // 128, tm, 128), lambda i, j: (ssm_step(j), i, 0)),
            pl.BlockSpec((tm, col), lambda i, j: (i, ssm_step(j))),
            cast_spec(cast0), cast_spec(cast1),
        ],
        out_shape=[
            jax.ShapeDtypeStruct((seq, D_SHORTCONV), BF16),
            jax.ShapeDtypeStruct((SSM_SLABS, seq, 128), F32),
            jax.ShapeDtypeStruct((seq, D_SSM), BF16),
            jax.ShapeDtypeStruct(cast0.shape, BF16),
            jax.ShapeDtypeStruct(cast1.shape, BF16),
        ],
        scratch_shapes=[
            pltpu.VMEM((tm, D_MODEL), BF16),
            pltpu.VMEM((tm + 8, col), F32),
            pltpu.VMEM((nconv, 8, col), F32),
        ],
        compiler_params=_params(("arbitrary", "arbitrary")),
        name="even_in",
    )(x, g, w_in, w_in, w_in, w_in, conv_w, cast0, cast1)


def _s5_tables(lam_re, lam_im, log_dt, b_re, b_im, c_re, c_im, d_skip):
    hi = lax.Precision.HIGH
    t_, g_, n_, p_ = SSM_CHUNK, SSM_GROUPS, SSM_STATE, SSM_GROUP
    lam_re, lam_im = lam_re.astype(F32), lam_im.astype(F32)
    dt = jnp.exp(log_dt.astype(F32))[:, None]
    xr, xi = lam_re * dt, lam_im * dt
    k = jnp.arange(t_ + 1, dtype=F32)[:, None, None]
    mag = jnp.exp(xr * k)
    pr, pi = mag * jnp.cos(xi * k), mag * jnp.sin(xi * k)
    nr, ni = pr[1] - 1.0, pi[1]
    den = lam_re * lam_re + lam_im * lam_im
    fr = (nr * lam_re + ni * lam_im) / den
    fi = (ni * lam_re - nr * lam_im) / den
    b_re, b_im = b_re.astype(F32), b_im.astype(F32)
    bbr = fr[..., None] * b_re - fi[..., None] * b_im
    bbi = fr[..., None] * b_im + fi[..., None] * b_re
    c_re, c_im = c_re.astype(F32), c_im.astype(F32)

    cpr = c_re[None] * pr[:t_, :, None, :] - c_im[None] * pi[:t_, :, None, :]
    cpi = c_re[None] * pi[:t_, :, None, :] + c_im[None] * pr[:t_, :, None, :]
    kern = jnp.einsum('tgpn,gnq->tgpq', jnp.concatenate([cpr, -cpi], axis=-1),
                      jnp.concatenate([bbr, bbi], axis=1), precision=hi)
    skip = d_skip.astype(F32).reshape(g_, p_)[:, :, None] * jnp.eye(p_, dtype=F32)[None]
    kern = kern.at[0].add(skip)
    flat = t_ * p_
    last = jnp.transpose(kern[::-1], (1, 2, 0, 3)).reshape(g_, p_, flat)
    mt = jnp.stack([jnp.pad(last[:, :, (t_ - 1 - t) * p_:], ((0, 0), (0, 0), (0, (t_ - 1 - t) * p_)))
                    for t in range(t_)], axis=1).reshape(g_, flat, flat)

    prj, pij = pr[t_ - 1::-1][:t_], pi[t_ - 1::-1][:t_]
    wr = prj[..., None] * bbr[None] - pij[..., None] * bbi[None]
    wi = prj[..., None] * bbi[None] + pij[..., None] * bbr[None]
    half = (np.arange(2) == 0)[None, None, :, None, None]

    def pack_w(w):
        w = jnp.transpose(w.reshape(t_, g_ // 2, 2, n_, p_), (1, 0, 2, 4, 3))
        w = jnp.concatenate([jnp.where(half, w, 0.0), jnp.where(half, 0.0, w)], axis=-1)
        return w.reshape(g_ // 2, 2 * flat, 2 * n_)

    vr = c_re[None] * pr[1:, :, None, :] - c_im[None] * pi[1:, :, None, :]
    vi = c_re[None] * pi[1:, :, None, :] + c_im[None] * pr[1:, :, None, :]
    vr = jnp.transpose(vr, (1, 0, 2, 3)).reshape(g_ // 2, 2, flat, n_)
    vi = -jnp.transpose(vi, (1, 0, 2, 3)).reshape(g_ // 2, 2, flat, n_)
    zv = jnp.zeros((g_ // 2, flat, n_), F32)
    v_even = jnp.concatenate([vr[:, 0], zv, vi[:, 0], zv], axis=-1)
    v_odd = jnp.concatenate([zv, vr[:, 1], zv, vi[:, 1]], axis=-1)
    vt = jnp.stack([v_even, v_odd], axis=1).reshape(g_, flat, 4 * n_)
    a_re = pr[t_].reshape(1, g_ * n_)
    a_im = pi[t_].reshape(1, g_ * n_)
    return mt.astype(BF16), pack_w(wr).astype(BF16), pack_w(wi).astype(BF16), vt.astype(BF16), a_re, a_im


_NT = (((1,), (1,)), ((), ()))
_TN = (((0,), (0,)), ((), ()))


def _s5_kernel(u_ref, mt_ref, wre_ref, wim_ref, vt_ref, are_ref, aim_ref, y_ref,
               xt_sc, yt_sc, zre_sc, zim_sc, carry_sc, *, ct):
    i = pl.program_id(0)
    t_ = SSM_CHUNK

    @pl.when(i == 0)
    def _():
        carry_sc[...] = jnp.zeros_like(carry_sc)

    eye = jnp.where(lax.broadcasted_iota(jnp.int32, (128, 128), 0)
                    == lax.broadcasted_iota(jnp.int32, (128, 128), 1), 1.0, 0.0).astype(BF16)

    for s in range(SSM_SLABS):
        for j in range(t_):
            uj = u_ref[s, pl.ds(j, ct, stride=t_), :].astype(BF16)
            xt_sc[s, j] = lax.dot_general(eye, uj, _NT, preferred_element_type=F32).astype(BF16)

    pairs_per_slab = 128 // (2 * SSM_GROUP)
    for gp in range(SSM_GROUPS // 2):
        s, k = gp // pairs_per_slab, gp % pairs_per_slab
        xp = xt_sc[s, :, 32 * k:32 * k + 32, :].reshape(2 * SSM_FLAT, ct)
        zre_sc[:, gp * 128:(gp + 1) * 128] = lax.dot_general(xp, wre_ref[gp], _TN, preferred_element_type=F32)
        zim_sc[:, gp * 128:(gp + 1) * 128] = lax.dot_general(xp, wim_ref[gp], _TN, preferred_element_type=F32)

    are = are_ref[...]
    aim = aim_ref[...]

    def step(r, carry):
        sre, sim = carry
        zr = zre_sc[pl.ds(r, 1), :]
        zi = zim_sc[pl.ds(r, 1), :]
        zre_sc[pl.ds(r, 1), :] = sre
        zim_sc[pl.ds(r, 1), :] = sim
        return are * sre - aim * sim + zr, are * sim + aim * sre + zi

    sre, sim = lax.fori_loop(0, ct, step, (carry_sc[0:1, :], carry_sc[1:2, :]), unroll=8)
    carry_sc[0:1, :] = sre
    carry_sc[1:2, :] = sim

    groups_per_slab = 128 // SSM_GROUP
    for g in range(SSM_GROUPS):
        gp, s, k = g // 2, g // groups_per_slab, g % groups_per_slab
        xg = xt_sc[s, :, 16 * k:16 * k + 16, :].reshape(SSM_FLAT, ct)
        sp = jnp.concatenate([zre_sc[:, gp * 128:(gp + 1) * 128],
                              zim_sc[:, gp * 128:(gp + 1) * 128]], axis=1).astype(BF16)
        yt = _dot(mt_ref[g], xg) + lax.dot_general(vt_ref[g], sp, _NT, preferred_element_type=F32)
        yt_sc[s, :, 16 * k:16 * k + 16, :] = yt.reshape(t_, SSM_GROUP, ct)

    for s in range(SSM_SLABS):
        for t in range(t_):
            yt = yt_sc[s, t]
            hi = yt.astype(BF16)
            lo = (yt - hi.astype(F32)).astype(BF16)
            y = (lax.dot_general(hi, eye, _TN, preferred_element_type=F32)
                 + lax.dot_general(lo, eye, _TN, preferred_element_type=F32))
            y_ref[s, pl.ds(t, ct, stride=t_), :] = y


def _s5(u4, tables, ct):
    mt, wre, wim, vt, are, aim = tables
    _, seq, _ = u4.shape
    full = lambda a: pl.BlockSpec(a.shape, lambda i: (0,) * a.ndim)
    rows = pl.BlockSpec((SSM_SLABS, SSM_CHUNK * ct, 128), lambda i: (0, i, 0))
    nstate = SSM_GROUPS * SSM_STATE
    return pl.pallas_call(
        functools.partial(_s5_kernel, ct=ct),
        grid=(seq // (SSM_CHUNK * ct),),
        in_specs=[rows, full(mt), full(wre), full(wim), full(vt), full(are), full(aim)],
        out_specs=rows,
        out_shape=jax.ShapeDtypeStruct(u4.shape, F32),
        scratch_shapes=[pltpu.VMEM((SSM_SLABS, SSM_CHUNK, 128, ct), BF16),
                        pltpu.VMEM((SSM_SLABS, SSM_CHUNK, 128, ct), F32),
                        pltpu.VMEM((ct, nstate), F32),
                        pltpu.VMEM((ct, nstate), F32),
                        pltpu.VMEM((8, nstate), F32)],
        compiler_params=_params(("arbitrary",)),
        name="s5_ssm",
    )(u4, mt, wre, wim, vt, are, aim)


OUT_ROWS = 256


def _even_out_kernel(x_ref, ya_ref, ys_ref, szb_ref, wglu_ref, wout_ref, g_ref, c0_ref,
                     h_ref, hn_ref, c0b_ref, *, tm):
    c0b_ref[...] = c0_ref[...].astype(BF16)
    for r0 in range(0, tm, OUT_ROWS):
        rows = slice(r0, r0 + OUT_ROWS)
        gl = _gelu_tanh(jnp.concatenate([ys_ref[s, rows, :] for s in range(SSM_SLABS)], axis=1))
        yb = gl * _sigmoid(_dot(gl.astype(BF16), wglu_ref[...])) * szb_ref[rows, :].astype(F32)
        acc = _dot(ya_ref[rows, :], wout_ref[0:D_SHORTCONV, :])
        acc = acc + _dot(yb.astype(BF16), wout_ref[D_SHORTCONV:D_SHORTCONV + D_SSM, :])
        h = x_ref[rows, :] + acc
        h_ref[rows, :] = h
        hn_ref[rows, :] = _rms(h, g_ref[...]).astype(BF16)


def _even_out(x, ya, ys, szb, w_glu, w_out, g, cast0, tm):
    seq = x.shape[0]
    cast_spec = _cast_slice_spec(cast0, seq // tm, lambda i: i)
    row = lambda w: pl.BlockSpec((tm, w), lambda i: (i, 0))
    const = lambda a: pl.BlockSpec(a.shape, lambda i: (0, 0), pipeline_mode=pl.Buffered(1))
    return pl.pallas_call(
        functools.partial(_even_out_kernel, tm=tm),
        grid=(seq // tm,),
        in_specs=[row(D_MODEL), row(D_SHORTCONV),
                  pl.BlockSpec((SSM_SLABS, tm, 128), lambda i: (0, i, 0)), row(D_SSM),
                  const(w_glu), const(w_out), const(g), cast_spec],
        out_specs=[row(D_MODEL), row(D_MODEL), cast_spec],
        out_shape=[jax.ShapeDtypeStruct((seq, D_MODEL), F32),
                   jax.ShapeDtypeStruct((seq, D_MODEL), BF16),
                   jax.ShapeDtypeStruct(cast0.shape, BF16)],
        compiler_params=_params(("arbitrary",)),
        name="even_out",
    )(x, ya, ys, szb, w_glu, w_out, g, cast0)


ODD_GROUPS = 6


def _odd_in_kernel(hn_ref, wa_ref, wb_ref, o_ref, *, tm, col):
    s = pl.program_id(1)
    nb = D_CONFORMER // col
    group = s // nb
    is_silu = (group == 1) | (group == 5)
    blocks = [slice(r0, r0 + IN_ROWS) for r0 in range(0, tm, IN_ROWS)]

    @pl.when(group == 0)
    def _():
        for rows in blocks:
            hn = hn_ref[rows, :]
            o_ref[rows, :] = (_dot(hn, wa_ref[...]) * _sigmoid(_dot(hn, wb_ref[...]))).astype(BF16)

    @pl.when(is_silu)
    def _():
        for rows in blocks:
            o_ref[rows, :] = _silu(_dot(hn_ref[rows, :], wa_ref[...])).astype(BF16)

    @pl.when((group > 0) & jnp.logical_not(is_silu))
    def _():
        scale = jnp.where(group == 2, Q_SCALE, 1.0).astype(F32)
        for rows in blocks:
            o_ref[rows, :] = (_dot(hn_ref[rows, :], wa_ref[...]) * scale).astype(BF16)


def _odd_in(hn, w_in, tm, col):
    seq = hn.shape[0]
    nb = D_CONFORMER // col
    return pl.pallas_call(
        functools.partial(_odd_in_kernel, tm=tm, col=col),
        grid=(seq // tm, ODD_GROUPS * nb),
        in_specs=[pl.BlockSpec((tm, D_MODEL), lambda i, s: (i, 0)),
                  pl.BlockSpec((D_MODEL, col), lambda i, s: (0, jnp.where(s < nb, s, s + nb))),
                  pl.BlockSpec((D_MODEL, col), lambda i, s: (0, jnp.where(s < nb, s + nb, 2 * nb - 1)))],
        out_specs=pl.BlockSpec((tm, col), lambda i, s: (i, s)),
        out_shape=jax.ShapeDtypeStruct((seq, ODD_GROUPS * D_CONFORMER), BF16),
        compiler_params=_params(("arbitrary", "arbitrary")),
        name="odd_in",
    )(hn, w_in, w_in)


CONV_ROWS = 64


def _conformer_kernel(cur_ref, halo_ref, szc_ref, cw_ref, cb_ref, lg_ref, lb_ref, wpw_ref,
                      o_ref, buf_sc, conv_sc, *, tm):
    i = pl.program_id(0)
    nslab = D_CONFORMER // 128
    for c in range(nslab):
        lanes = slice(c * 128, (c + 1) * 128)
        buf_sc[c, 0:CONV_HALO, :] = jnp.where(i == 0, 0.0, halo_ref[:, lanes].astype(F32))
        buf_sc[c, CONV_HALO:CONV_HALO + tm, :] = cur_ref[:, lanes].astype(F32)
    shift = CONV_HALO - (CONFORMER_WIDTH - 1)

    def rows(rc, _):
        base = pl.multiple_of(rc * CONV_ROWS, CONV_ROWS)
        for c in range(nslab):
            lanes = slice(c * 128, (c + 1) * 128)
            acc = jnp.zeros((CONV_ROWS, 128), F32) + cb_ref[:, lanes]
            for k in range(CONFORMER_WIDTH):
                acc = acc + cw_ref[k:k + 1, lanes] * buf_sc[c, pl.ds(base + (shift + k), CONV_ROWS), :]
            conv_sc[pl.ds(base, CONV_ROWS), lanes] = acc
        return 0

    lax.fori_loop(0, tm // CONV_ROWS, rows, 0)
    for r0 in range(0, tm, OUT_ROWS):
        sub = slice(r0, r0 + OUT_ROWS)
        y = conv_sc[sub, :]
        mu = jnp.mean(y, axis=-1, keepdims=True)
        yc = y - mu
        var = jnp.mean(yc * yc, axis=-1, keepdims=True)
        ln = yc * lax.rsqrt(var + NORM_EPS) * lg_ref[...] + lb_ref[...]
        o_ref[sub, :] = (_dot(_silu(ln).astype(BF16), wpw_ref[...]) * szc_ref[sub, :].astype(F32)).astype(BF16)


def _conformer(proj, conv_w, conv_b, ln_g, ln_b, w_pw, tm):
    seq = proj.shape[0]
    per = tm // CONV_HALO
    const = lambda a: pl.BlockSpec(a.shape, lambda i: (0, 0))
    return pl.pallas_call(
        functools.partial(_conformer_kernel, tm=tm),
        grid=(seq // tm,),
        in_specs=[pl.BlockSpec((tm, D_CONFORMER), lambda i: (i, 0)),
                  pl.BlockSpec((CONV_HALO, D_CONFORMER), lambda i: (jnp.maximum(i * per - 1, 0), 0)),
                  pl.BlockSpec((tm, D_CONFORMER), lambda i: (i, 1)),
                  const(conv_w), const(conv_b), const(ln_g), const(ln_b), const(w_pw)],
        out_specs=pl.BlockSpec((tm, D_CONFORMER), lambda i: (i, 0)),
        out_shape=jax.ShapeDtypeStruct((seq, D_CONFORMER), BF16),
        scratch_shapes=[pltpu.VMEM((D_CONFORMER // 128, tm + CONV_HALO, 128), F32),
                        pltpu.VMEM((tm, D_CONFORMER), F32)],
        compiler_params=_params(("arbitrary",)),
        name="conformer",
    )(proj, proj, proj, conv_w, conv_b, ln_g, ln_b, w_pw)


def _t5_bucket(dist):
    max_exact = REL_BUCKETS // 2
    d_f = jnp.maximum(dist, 1).astype(F32)
    large = max_exact + (jnp.log(d_f / max_exact) / math.log(REL_MAX_DISTANCE / max_exact)
                         * (REL_BUCKETS - max_exact)).astype(jnp.int32)
    large = jnp.minimum(large, REL_BUCKETS - 1)
    return jnp.where(dist < max_exact, dist, large)


def _attn_bias(rel_bias):
    q = ATTN_BLOCK
    nh = rel_bias.shape[1]
    kj = np.arange(2 * q)[None, None, :]
    neg = lambda n: jnp.full((nh, n), NEG_INF, F32)
    tabs = []
    for window, dilation in DILATED_PATTERNS:
        steps = window // dilation
        assert steps == q
        rel = jnp.asarray(np.arange(steps + 1) * dilation, dtype=jnp.int32)
        by_rel = rel_bias.astype(F32)[_t5_bucket(rel)].T
        g = jnp.concatenate([by_rel[:, ::-1], neg(2 * q - 1)], axis=1)
        flat = jnp.tile(g, (1, q))[:, :q * (3 * q - 1)]
        tab = flat.reshape(nh, q, 3 * q - 1)[:, :, :2 * q]
        if dilation == 1:
            tab = tab.reshape(nh, q // MERGE_DIL, MERGE_DIL, 2 * q).transpose(0, 2, 1, 3).reshape(nh, q, 2 * q)
        tab = tab * LOG2E
        tabs.append(tab)
        tabs.append(jnp.where(kj >= q, tab, NEG_INF))
    return jnp.stack(tabs, axis=1)


def _attn_kernel(q_ref, k_ref, v_ref, szd_ref, bias_ref, o_ref,
                 qf, kf, vf, qm, km, vm, qs1, qs2, ks0, ks1, ks2, vs0, vs1, vs2,
                 o0, o1, o2, l0, l1, l2):
    i = pl.program_id(1)
    t = ATTN_TILE
    blk = ATTN_BLOCK
    first_tile = i == 0
    qf[...] = q_ref[...].astype(F32)
    kf[...] = k_ref[...].astype(F32)
    vf[...] = v_ref[...].astype(F32)
    qstreams = (None, qs1, qs2)
    kstreams = (ks0, ks1, ks2)
    vstreams = (vs0, vs1, vs2)
    md = MERGE_DIL
    mspan = t // md

    def stream(src_f, src_m, dil, r):
        n = t // dil
        if dil == md:
            x = src_f[pl.ds(r, n, stride=dil), :]
            src_m[r * n:(r + 1) * n, :] = x
            return x
        return src_m[pl.ds((r % md) * mspan + r // md, n, stride=dil // md), :]

    for pidx, (_, dil) in enumerate(DILATED_PATTERNS):
        n = t // dil
        per = blk + n
        for r in range(dil):
            base = r * per
            for src_f, src_m, src_ref, dst in ((kf, km, k_ref, kstreams[pidx]), (vf, vm, v_ref, vstreams[pidx])):
                tail = dst[base + n:base + per, :]
                dst[base:base + blk, :] = jnp.where(first_tile, jnp.zeros_like(tail), tail)
                if dil == 1:
                    dst[base + blk:base + per, :] = src_ref[...]
                else:
                    dst[base + blk:base + per, :] = stream(src_f, src_m, dil, r).astype(BF16)
            if dil > 1:
                qstreams[pidx][r * n:(r + 1) * n, :] = stream(qf, qm, dil, r).astype(BF16)

    ones = jnp.ones((2 * blk, HEAD_DIM), BF16)
    first = first_tile.astype(jnp.int32)

    def block(pidx, dil, r, sb, out_sc, lse_sc):
        n = t // dil
        per = blk + n
        if dil == 1:
            qb = jnp.concatenate([qf[pl.ds(sb * blk + c, blk // md, stride=md), :] for c in range(md)],
                                 axis=0).astype(BF16)
        else:
            qb = qstreams[pidx][r * n + sb * blk:r * n + (sb + 1) * blk, :]
        krows = slice(r * per + sb * blk, r * per + (sb + 2) * blk)
        kb = kstreams[pidx][krows, :]
        vb = vstreams[pidx][krows, :]
        s = lax.dot_general(qb, kb, _NT, preferred_element_type=F32)
        s = s + bias_ref[0, 2 * pidx + first if sb == 0 else 2 * pidx]
        mx = jnp.max(s, axis=-1, keepdims=True)
        p = jnp.exp2(s - mx).astype(BF16)
        pv = _dot(p, jnp.concatenate([vb, ones], axis=1))
        den = pv[:, HEAD_DIM:]
        out = pv[:, :HEAD_DIM] / den
        lse = mx + jnp.log2(den)
        if dil == 1:
            piece = blk // md
            for c in range(md):
                rows = slice(c * mspan + sb * piece, c * mspan + (sb + 1) * piece)
                out_sc[rows, :] = out[c * piece:(c + 1) * piece, :]
                lse_sc[rows, :] = lse[c * piece:(c + 1) * piece, :]
        elif dil == md:
            rows = slice(r * mspan + sb * blk, r * mspan + (sb + 1) * blk)
            out_sc[rows, :] = out
            lse_sc[rows, :] = lse
        else:
            step = dil // md
            rows = pl.ds((r % md) * mspan + sb * blk * step + r // md, blk, stride=step)
            out_sc[rows, :] = out
            lse_sc[rows, :] = lse

    scratch = ((o0, l0), (o1, l1), (o2, l2))
    for pidx, (_, dil) in enumerate(DILATED_PATTERNS):
        for sb in range(t // (blk * dil)):
            for r in range(dil):
                block(pidx, dil, r, sb, *scratch[pidx])

    la, lb, lc = l0[...], l1[...], l2[...]
    lall = jnp.maximum(jnp.maximum(la, lb), lc)
    wa, wb, wc = jnp.exp2(la - lall), jnp.exp2(lb - lall), jnp.exp2(lc - lall)
    merged = (wa * o0[...] + wb * o1[...] + wc * o2[...]) / (wa + wb + wc)
    for c in range(md):
        qf[pl.ds(c, mspan, stride=md), :] = merged[c * mspan:(c + 1) * mspan, :]
    o_ref[...] = (qf[...] * szd_ref[...].astype(F32)).astype(BF16)


def _attention(proj, bias):
    seq = proj.shape[0]
    t = ATTN_TILE
    q0 = (2 * D_CONFORMER) // HEAD_DIM
    k0 = q0 + N_HEADS
    v0 = k0 + N_HEADS
    z0 = v0 + N_HEADS
    cur = lambda c0: pl.BlockSpec((t, HEAD_DIM), lambda h, i: (i, c0 + h))
    f32 = lambda rows: pltpu.VMEM((rows, HEAD_DIM), F32)
    bf16 = lambda rows: pltpu.VMEM((rows, HEAD_DIM), BF16)
    streams = [bf16(dil * (ATTN_BLOCK + t // dil)) for _, dil in DILATED_PATTERNS]
    return pl.pallas_call(
        _attn_kernel,
        grid=(N_HEADS, seq // t),
        in_specs=[cur(q0), cur(k0), cur(v0), cur(z0),
                  pl.BlockSpec((1,) + bias.shape[1:], lambda h, i: (h, 0, 0, 0))],
        out_specs=pl.BlockSpec((t, HEAD_DIM), lambda h, i: (i, h)),
        out_shape=jax.ShapeDtypeStruct((seq, D_ATTN), BF16),
        scratch_shapes=[f32(t)] * 6 + [bf16(t)] * 2 + streams + streams + [f32(t)] * 6,
        compiler_params=_params(("arbitrary", "arbitrary")),
        name="dilated_attn",
    )(proj, proj, proj, proj, bias)


def _odd_out_kernel(h_ref, yc_ref, yd_ref, wout_ref, g_ref, o_ref, *, tm):
    for r0 in range(0, tm, OUT_ROWS):
        rows = slice(r0, r0 + OUT_ROWS)
        acc = _dot(yc_ref[rows, :], wout_ref[0:D_CONFORMER, :])
        acc = acc + _dot(yd_ref[rows, :], wout_ref[D_CONFORMER:D_CONFORMER + D_ATTN, :])
        o_ref[rows, :] = _rms(h_ref[rows, :] + acc, g_ref[...])


def _odd_out(h, yc, yd, w_out, g, tm):
    seq = h.shape[0]
    row = lambda w: pl.BlockSpec((tm, w), lambda i: (i, 0))
    const = lambda a: pl.BlockSpec(a.shape, lambda i: (0, 0), pipeline_mode=pl.Buffered(1))
    return pl.pallas_call(
        functools.partial(_odd_out_kernel, tm=tm),
        grid=(seq // tm,),
        in_specs=[row(D_MODEL), row(D_CONFORMER), row(D_ATTN), const(w_out), const(g)],
        out_specs=row(D_MODEL),
        out_shape=jax.ShapeDtypeStruct((seq, D_MODEL), F32),
        compiler_params=_params(("arbitrary",)),
        name="odd_out",
    )(h, yc, yd, w_out, g)


def _row_tile(seq, want):
    return min(seq, want)


def kernel(x, norm_g, final_norm_g, ev_w_in, ev_conv_w, s5_lam_re, s5_lam_im, s5_log_dt, s5_b_re, s5_b_im, s5_c_re, s5_c_im, s5_d, s5_w_glu, ev_w_out, od_w_in, cf_conv_w, cf_conv_b, cf_ln_g, cf_ln_b, cf_w_pw, od_w_out, rel_bias):
    bsz, seq, _ = x.shape
    assert bsz == 1 and seq % ATTN_TILE == 0
    x2 = x.reshape(seq, D_MODEL)
    g0 = norm_g[0].astype(F32).reshape(1, D_MODEL)
    g1 = norm_g[1].astype(F32).reshape(1, D_MODEL)
    gf = final_norm_g.astype(F32).reshape(1, D_MODEL)

    ya, u, szb, ev_w_out_b, od_w_out_b = _even_in(
        x2, g0, ev_w_in[0].astype(BF16), ev_conv_w[0].astype(F32), ev_w_out[0].astype(F32),
        od_w_out[0].astype(F32), _row_tile(seq, 1024), 512)
    tables = _s5_tables(s5_lam_re[0], s5_lam_im[0], s5_log_dt[0], s5_b_re[0], s5_b_im[0],
                        s5_c_re[0], s5_c_im[0], s5_d[0])
    ys = _s5(u, tables, min(seq // SSM_CHUNK, 128))
    h1, hn1, od_w_in_b = _even_out(x2, ya, ys, szb, s5_w_glu[0].astype(BF16), ev_w_out_b, g1,
                                   od_w_in[0].astype(F32), _row_tile(seq, 512))

    proj = _odd_in(hn1, od_w_in_b, _row_tile(seq, 2048), 512)
    conv_w = jnp.concatenate([cf_conv_w[0].astype(F32), jnp.zeros((1, D_CONFORMER), F32)], axis=0)
    yc = _conformer(proj, conv_w, cf_conv_b[0].astype(F32).reshape(1, -1),
                    cf_ln_g[0].astype(F32).reshape(1, -1), cf_ln_b[0].astype(F32).reshape(1, -1),
                    cf_w_pw[0].astype(BF16), _row_tile(seq, 1024))
    yd = _attention(proj, _attn_bias(rel_bias))
    out = _odd_out(h1, yc, yd, od_w_out_b, gf, _row_tile(seq, 512))
    return out.reshape(bsz, seq, D_MODEL)
```

```python
import functools
import math

import numpy as np
import jax
import jax.numpy as jnp
from jax import lax
from jax.experimental import pallas as pl
from jax.experimental.pallas import tpu as pltpu

F32 = jnp.float32
BF16 = jnp.bfloat16

D_MODEL = 2048
D_SHORTCONV = 1536
SHORTCONV_WIDTH = 3
D_SSM = 512
SSM_GROUP = 16
SSM_GROUPS = 32
SSM_STATE = 64
D_CONFORMER = 1024
CONFORMER_WIDTH = 31
D_ATTN = 1024
HEAD_DIM = 128
N_HEADS = 8
DILATED_PATTERNS = ((128, 1), (512, 4), (2048, 16))
ATTN_BLOCK = 128
REL_BUCKETS = 32
REL_MAX_DISTANCE = 2048
NORM_EPS = 1e-6
NEG_INF = -1e30

IN_ROWS = 512
EVEN_ROWS = 256
SSM_CHUNK = 16
SSM_FLAT = SSM_CHUNK * SSM_GROUP
SSM_SLABS = D_SSM // 128
ATTN_TILE = 2048
LOG2E = math.log2(math.e)
Q_SCALE = HEAD_DIM ** -0.5 * LOG2E
MERGE_DIL = 4
CONV_HALO = 32
VMEM_LIMIT = 50 * 1024 * 1024


def _params(sem, vmem=VMEM_LIMIT):
    return pltpu.CompilerParams(dimension_semantics=sem, vmem_limit_bytes=vmem)


def _dot(a, b):
    return jnp.dot(a, b, preferred_element_type=F32)


def _sigmoid(z):
    return 1.0 / (1.0 + jnp.exp(-z))


def _silu(z):
    return z * _sigmoid(z)


def _gelu_tanh(x):
    return 0.5 * x * (1.0 + jnp.tanh(math.sqrt(2.0 / math.pi) * (x + 0.044715 * (x * x * x))))


def _cast_slice_spec(a, nsteps, step_of):
    rows = a.shape[0] // nsteps
    assert rows * nsteps == a.shape[0] and rows % 16 == 0, (a.shape, nsteps)
    return pl.BlockSpec((rows, a.shape[1]), lambda *idx: (step_of(*idx), 0))


def _rms(xf, g):
    ms = jnp.mean(xf * xf, axis=-1, keepdims=True)
    return xf * lax.rsqrt(ms + NORM_EPS) * g


def _even_in_kernel(x_ref, g_ref, wa_ref, wb_ref, wc_ref, wd_ref, cw_ref, c0_ref, c1_ref,
                    ya_ref, u_ref, szb_ref, c0b_ref, c1b_ref, hn_sc, pbuf_sc, carry_sc, *, tm, col):
    i = pl.program_id(0)
    j = pl.program_id(1)
    nconv = D_SHORTCONV // col
    c0b_ref[...] = c0_ref[...].astype(BF16)
    c1b_ref[...] = c1_ref[...].astype(BF16)

    def conv_step(first):
        @pl.when(i == 0)
        def _():
            carry_sc[j] = jnp.zeros((8, col), F32)

        pbuf_sc[0:8, :] = carry_sc[j]
        cw = cw_ref[...]
        for r0 in range(0, tm, EVEN_ROWS):
            rows = slice(r0, r0 + EVEN_ROWS)
            if first:
                hn = _rms(x_ref[rows, :], g_ref[...]).astype(BF16)
                hn_sc[rows, :] = hn
            else:
                hn = hn_sc[rows, :]
            p = _dot(hn, wb_ref[...]) * _dot(hn, wa_ref[...])
            pbuf_sc[8 + r0:8 + r0 + EVEN_ROWS, :] = p
            conv = (cw[2:3, :] * p + cw[1:2, :] * pbuf_sc[pl.ds(7 + r0, EVEN_ROWS), :]
                    + cw[0:1, :] * pbuf_sc[pl.ds(6 + r0, EVEN_ROWS), :])
            gb = _dot(hn, wc_ref[...])
            za = _dot(hn, wd_ref[...])
            ya_ref[rows, :] = (gb * conv * _silu(za)).astype(BF16)
        carry_sc[j] = pbuf_sc[tm:tm + 8, :]

    pl.when(j == 0)(functools.partial(conv_step, True))
    pl.when((j > 0) & (j < nconv))(functools.partial(conv_step, False))

    @pl.when(j >= nconv)
    def _():
        for r0 in range(0, tm, EVEN_ROWS):
            rows = slice(r0, r0 + EVEN_ROWS)
            hn = hn_sc[rows, :]
            u = _dot(hn, wa_ref[...])
            for s in range(col // 128):
                u_ref[s, rows, :] = u[:, s * 128:(s + 1) * 128]
            szb_ref[rows, :] = _silu(_dot(hn, wb_ref[...])).astype(BF16)


def _even_in(x, g, w_in, conv_w, cast0, cast1, tm, col):
    seq = x.shape[0]
    nconv = D_SHORTCONV // col
    nssm = D_SSM // col
    u_blk = 4 * nconv

    def wspec(k, alt0):
        return pl.BlockSpec((D_MODEL, col),
                            lambda i, j: (0, jnp.where(j < nconv, j + nconv * k, alt0 + j - nconv)))

    hold = lambda k: pl.BlockSpec((D_MODEL, col), lambda i, j: (0, jnp.minimum(j, nconv - 1) + nconv * k))
    ssm_step = lambda j: jnp.maximum(j - nconv, 0)
    nj = nconv + nssm
    cast_spec = lambda a: _cast_slice_spec(a, (seq // tm) * nj, lambda i, j: i * nj + j)
    return pl.pallas_call(
        functools.partial(_even_in_kernel, tm=tm, col=col),
        grid=(seq // tm, nj),
        in_specs=[
            pl.BlockSpec((tm, D_MODEL), lambda i, j: (i, 0)),
            pl.BlockSpec((1, D_MODEL), lambda i, j: (0, 0)),
            wspec(0, u_blk), wspec(1, u_blk + nssm), hold(2), hold(3),
            pl.BlockSpec((SHORTCONV_WIDTH, col), lambda i, j: (0, jnp.minimum(j, nconv - 1))),
            cast_spec(cast0), cast_spec(cast1),
        ],
        out_specs=[
            pl.BlockSpec((tm, col), lambda i, j: (i, jnp.minimum(j, nconv - 1))),
            pl.BlockSpec((col // 128, tm, 128), lambda i, j: (ssm_step(j), i, 0)),
            pl.BlockSpec((tm, col), lambda i, j: (i, ssm_step(j))),
            cast_spec(cast0), cast_spec(cast1),
        ],
        out_shape=[
            jax.ShapeDtypeStruct((seq, D_SHORTCONV), BF16),
            jax.ShapeDtypeStruct((SSM_SLABS, seq, 128), F32),
            jax.ShapeDtypeStruct((seq, D_SSM), BF16),
            jax.ShapeDtypeStruct(cast0.shape, BF16),
            jax.ShapeDtypeStruct(cast1.shape, BF16),
        ],
        scratch_shapes=[
            pltpu.VMEM((tm, D_MODEL), BF16),
            pltpu.VMEM((tm + 8, col), F32),
            pltpu.VMEM((nconv, 8, col), F32),
        ],
        compiler_params=_params(("arbitrary", "arbitrary")),
        name="even_in",
    )(x, g, w_in, w_in, w_in, w_in, conv_w, cast0, cast1)


def _s5_tables(lam_re, lam_im, log_dt, b_re, b_im, c_re, c_im, d_skip):
    hi = lax.Precision.HIGH
    t_, g_, n_, p_ = SSM_CHUNK, SSM_GROUPS, SSM_STATE, SSM_GROUP
    lam_re, lam_im = lam_re.astype(F32), lam_im.astype(F32)
    dt = jnp.exp(log_dt.astype(F32))[:, None]
    xr, xi = lam_re * dt, lam_im * dt
    k = jnp.arange(t_ + 1, dtype=F32)[:, None, None]
    mag = jnp.exp(xr * k)
    pr, pi = mag * jnp.cos(xi * k), mag * jnp.sin(xi * k)
    nr, ni = pr[1] - 1.0, pi[1]
    den = lam_re * lam_re + lam_im * lam_im
    fr = (nr * lam_re + ni * lam_im) / den
    fi = (ni * lam_re - nr * lam_im) / den
    b_re, b_im = b_re.astype(F32), b_im.astype(F32)
    bbr = fr[..., None] * b_re - fi[..., None] * b_im
    bbi = fr[..., None] * b_im + fi[..., None] * b_re
    c_re, c_im = c_re.astype(F32), c_im.astype(F32)

    cpr = c_re[None] * pr[:t_, :, None, :] - c_im[None] * pi[:t_, :, None, :]
    cpi = c_re[None] * pi[:t_, :, None, :] + c_im[None] * pr[:t_, :, None, :]
    kern = jnp.einsum('tgpn,gnq->tgpq', jnp.concatenate([cpr, -cpi], axis=-1),
                      jnp.concatenate([bbr, bbi], axis=1), precision=hi)
    skip = d_skip.astype(F32).reshape(g_, p_)[:, :, None] * jnp.eye(p_, dtype=F32)[None]
    kern = kern.at[0].add(skip)
    flat = t_ * p_
    last = jnp.transpose(kern[::-1], (1, 2, 0, 3)).reshape(g_, p_, flat)
    period = jnp.concatenate([last, jnp.zeros_like(last)], axis=-1)
    rows = jnp.tile(period, (1, 1, t_ + 1))[:, :, :t_ * (2 * flat + p_)].reshape(g_, p_, t_, 2 * flat + p_)
    mt = jnp.transpose(rows[:, :, ::-1, :flat], (0, 2, 1, 3)).reshape(g_, flat, flat)

    prj, pij = pr[t_ - 1::-1][:t_], pi[t_ - 1::-1][:t_]
    wr = prj[..., None] * bbr[None] - pij[..., None] * bbi[None]
    wi = prj[..., None] * bbi[None] + pij[..., None] * bbr[None]
    half = (np.arange(2) == 0)[None, None, :, None, None]

    def pack_w(w):
        w = jnp.transpose(w.reshape(t_, g_ // 2, 2, n_, p_), (1, 0, 2, 4, 3))
        w = jnp.concatenate([jnp.where(half, w, 0.0), jnp.where(half, 0.0, w)], axis=-1)
        return w.reshape(g_ // 2, 2 * flat, 2 * n_)

    vr = c_re[None] * pr[1:, :, None, :] - c_im[None] * pi[1:, :, None, :]
    vi = c_re[None] * pi[1:, :, None, :] + c_im[None] * pr[1:, :, None, :]
    vr = jnp.transpose(vr, (1, 0, 2, 3)).reshape(g_ // 2, 2, flat, n_)
    vi = -jnp.transpose(vi, (1, 0, 2, 3)).reshape(g_ // 2, 2, flat, n_)
    zv = jnp.zeros((g_ // 2, flat, n_), F32)
    v_even = jnp.concatenate([vr[:, 0], zv, vi[:, 0], zv], axis=-1)
    v_odd = jnp.concatenate([zv, vr[:, 1], zv, vi[:, 1]], axis=-1)
    vt = jnp.stack([v_even, v_odd], axis=1).reshape(g_, flat, 4 * n_)
    a_re = pr[t_].reshape(1, g_ * n_)
    a_im = pi[t_].reshape(1, g_ * n_)
    return mt.astype(BF16), pack_w(wr).astype(BF16), pack_w(wi).astype(BF16), vt.astype(BF16), a_re, a_im


_NT = (((1,), (1,)), ((), ()))
_TN = (((0,), (0,)), ((), ()))


def _s5_kernel(u_ref, mt_ref, wre_ref, wim_ref, vt_ref, are_ref, aim_ref, y_ref,
               xt_sc, yt_sc, zre_sc, zim_sc, carry_sc, *, ct):
    i = pl.program_id(0)
    t_ = SSM_CHUNK

    @pl.when(i == 0)
    def _():
        carry_sc[...] = jnp.zeros_like(carry_sc)

    for s in range(SSM_SLABS):
        for j in range(t_):
            xt_sc[s, j] = u_ref[s, pl.ds(j, ct, stride=t_), :].T.astype(BF16)

    pairs_per_slab = 128 // (2 * SSM_GROUP)
    for gp in range(SSM_GROUPS // 2):
        s, k = gp // pairs_per_slab, gp % pairs_per_slab
        xp = xt_sc[s, :, 32 * k:32 * k + 32, :].reshape(2 * SSM_FLAT, ct)
        zre_sc[:, gp * 128:(gp + 1) * 128] = lax.dot_general(xp, wre_ref[gp], _TN, preferred_element_type=F32)
        zim_sc[:, gp * 128:(gp + 1) * 128] = lax.dot_general(xp, wim_ref[gp], _TN, preferred_element_type=F32)

    are = are_ref[...]
    aim = aim_ref[...]

    def step(r, carry):
        sre, sim = carry
        zr = zre_sc[pl.ds(r, 1), :]
        zi = zim_sc[pl.ds(r, 1), :]
        zre_sc[pl.ds(r, 1), :] = sre
        zim_sc[pl.ds(r, 1), :] = sim
        return are * sre - aim * sim + zr, are * sim + aim * sre + zi

    sre, sim = lax.fori_loop(0, ct, step, (carry_sc[0:1, :], carry_sc[1:2, :]), unroll=8)
    carry_sc[0:1, :] = sre
    carry_sc[1:2, :] = sim

    groups_per_slab = 128 // SSM_GROUP
    for g in range(SSM_GROUPS):
        gp, s, k = g // 2, g // groups_per_slab, g % groups_per_slab
        xg = xt_sc[s, :, 16 * k:16 * k + 16, :].reshape(SSM_FLAT, ct)
        sp = jnp.concatenate([zre_sc[:, gp * 128:(gp + 1) * 128],
                              zim_sc[:, gp * 128:(gp + 1) * 128]], axis=1).astype(BF16)
        yt = _dot(mt_ref[g], xg) + lax.dot_general(vt_ref[g], sp, _NT, preferred_element_type=F32)
        yt_sc[s, :, 16 * k:16 * k + 16, :] = yt.reshape(t_, SSM_GROUP, ct)

    for s in range(SSM_SLABS):
        for t in range(t_):
            y_ref[s, pl.ds(t, ct, stride=t_), :] = yt_sc[s, t].T


def _s5(u4, tables, ct):
    mt, wre, wim, vt, are, aim = tables
    _, seq, _ = u4.shape
    full = lambda a: pl.BlockSpec(a.shape, lambda i: (0,) * a.ndim)
    rows = pl.BlockSpec((SSM_SLABS, SSM_CHUNK * ct, 128), lambda i: (0, i, 0))
    nstate = SSM_GROUPS * SSM_STATE
    return pl.pallas_call(
        functools.partial(_s5_kernel, ct=ct),
        grid=(seq // (SSM_CHUNK * ct),),
        in_specs=[rows, full(mt), full(wre), full(wim), full(vt), full(are), full(aim)],
        out_specs=rows,
        out_shape=jax.ShapeDtypeStruct(u4.shape, F32),
        scratch_shapes=[pltpu.VMEM((SSM_SLABS, SSM_CHUNK, 128, ct), BF16),
                        pltpu.VMEM((SSM_SLABS, SSM_CHUNK, 128, ct), F32),
                        pltpu.VMEM((ct, nstate), F32),
                        pltpu.VMEM((ct, nstate), F32),
                        pltpu.VMEM((8, nstate), F32)],
        compiler_params=_params(("arbitrary",)),
        name="s5_ssm",
    )(u4, mt, wre, wim, vt, are, aim)


OUT_ROWS = 256


def _even_out_kernel(x_ref, ya_ref, ys_ref, szb_ref, wglu_ref, wout_ref, g_ref, c0_ref,
                     h_ref, hn_ref, c0b_ref, *, tm):
    c0b_ref[...] = c0_ref[...].astype(BF16)
    for r0 in range(0, tm, OUT_ROWS):
        rows = slice(r0, r0 + OUT_ROWS)
        gl = _gelu_tanh(jnp.concatenate([ys_ref[s, rows, :] for s in range(SSM_SLABS)], axis=1))
        yb = gl * _sigmoid(_dot(gl.astype(BF16), wglu_ref[...])) * szb_ref[rows, :].astype(F32)
        acc = _dot(ya_ref[rows, :], wout_ref[0:D_SHORTCONV, :])
        acc = acc + _dot(yb.astype(BF16), wout_ref[D_SHORTCONV:D_SHORTCONV + D_SSM, :])
        h = x_ref[rows, :] + acc
        h_ref[rows, :] = h
        hn_ref[rows, :] = _rms(h, g_ref[...]).astype(BF16)


def _even_out(x, ya, ys, szb, w_glu, w_out, g, cast0, tm):
    seq = x.shape[0]
    cast_spec = _cast_slice_spec(cast0, seq // tm, lambda i: i)
    row = lambda w: pl.BlockSpec((tm, w), lambda i: (i, 0))
    const = lambda a: pl.BlockSpec(a.shape, lambda i: (0, 0), pipeline_mode=pl.Buffered(1))
    return pl.pallas_call(
        functools.partial(_even_out_kernel, tm=tm),
        grid=(seq // tm,),
        in_specs=[row(D_MODEL), row(D_SHORTCONV),
                  pl.BlockSpec((SSM_SLABS, tm, 128), lambda i: (0, i, 0)), row(D_SSM),
                  const(w_glu), const(w_out), const(g), cast_spec],
        out_specs=[row(D_MODEL), row(D_MODEL), cast_spec],
        out_shape=[jax.ShapeDtypeStruct((seq, D_MODEL), F32),
                   jax.ShapeDtypeStruct((seq, D_MODEL), BF16),
                   jax.ShapeDtypeStruct(cast0.shape, BF16)],
        compiler_params=_params(("arbitrary",)),
        name="even_out",
    )(x, ya, ys, szb, w_glu, w_out, g, cast0)


ODD_GROUPS = 6


def _odd_in_kernel(hn_ref, wa_ref, wb_ref, o_ref, *, tm, col):
    s = pl.program_id(1)
    nb = D_CONFORMER // col
    group = s // nb
    is_silu = (group == 1) | (group == 5)
    blocks = [slice(r0, r0 + IN_ROWS) for r0 in range(0, tm, IN_ROWS)]

    @pl.when(group == 0)
    def _():
        for rows in blocks:
            hn = hn_ref[rows, :]
            o_ref[rows, :] = (_dot(hn, wa_ref[...]) * _sigmoid(_dot(hn, wb_ref[...]))).astype(BF16)

    @pl.when(is_silu)
    def _():
        for rows in blocks:
            o_ref[rows, :] = _silu(_dot(hn_ref[rows, :], wa_ref[...])).astype(BF16)

    @pl.when((group > 0) & jnp.logical_not(is_silu))
    def _():
        scale = jnp.where(group == 2, Q_SCALE, 1.0).astype(F32)
        for rows in blocks:
            o_ref[rows, :] = (_dot(hn_ref[rows, :], wa_ref[...]) * scale).astype(BF16)


def _odd_in(hn, w_in, tm, col):
    seq = hn.shape[0]
    nb = D_CONFORMER // col
    return pl.pallas_call(
        functools.partial(_odd_in_kernel, tm=tm, col=col),
        grid=(seq // tm, ODD_GROUPS * nb),
        in_specs=[pl.BlockSpec((tm, D_MODEL), lambda i, s: (i, 0)),
                  pl.BlockSpec((D_MODEL, col), lambda i, s: (0, jnp.where(s < nb, s, s + nb))),
                  pl.BlockSpec((D_MODEL, col), lambda i, s: (0, jnp.where(s < nb, s + nb, 2 * nb - 1)))],
        out_specs=pl.BlockSpec((tm, col), lambda i, s: (i, s)),
        out_shape=jax.ShapeDtypeStruct((seq, ODD_GROUPS * D_CONFORMER), BF16),
        compiler_params=_params(("arbitrary", "arbitrary")),
        name="odd_in",
    )(hn, w_in, w_in)


CONV_ROWS = 64


def _conformer_kernel(cur_ref, halo_ref, szc_ref, cw_ref, cb_ref, lg_ref, lb_ref, wpw_ref,
                      o_ref, buf_sc, conv_sc, *, tm):
    i = pl.program_id(0)
    nslab = D_CONFORMER // 128
    for c in range(nslab):
        lanes = slice(c * 128, (c + 1) * 128)
        buf_sc[c, 0:CONV_HALO, :] = jnp.where(i == 0, 0.0, halo_ref[:, lanes].astype(F32))
        buf_sc[c, CONV_HALO:CONV_HALO + tm, :] = cur_ref[:, lanes].astype(F32)
    shift = CONV_HALO - (CONFORMER_WIDTH - 1)

    def rows(rc, _):
        base = pl.multiple_of(rc * CONV_ROWS, CONV_ROWS)
        for c in range(nslab):
            lanes = slice(c * 128, (c + 1) * 128)
            acc = jnp.zeros((CONV_ROWS, 128), F32) + cb_ref[:, lanes]
            for k in range(CONFORMER_WIDTH):
                acc = acc + cw_ref[k:k + 1, lanes] * buf_sc[c, pl.ds(base + (shift + k), CONV_ROWS), :]
            conv_sc[pl.ds(base, CONV_ROWS), lanes] = acc
        return 0

    lax.fori_loop(0, tm // CONV_ROWS, rows, 0)
    for r0 in range(0, tm, OUT_ROWS):
        sub = slice(r0, r0 + OUT_ROWS)
        y = conv_sc[sub, :]
        mu = jnp.mean(y, axis=-1, keepdims=True)
        yc = y - mu
        var = jnp.mean(yc * yc, axis=-1, keepdims=True)
        ln = yc * lax.rsqrt(var + NORM_EPS) * lg_ref[...] + lb_ref[...]
        o_ref[sub, :] = (_dot(_silu(ln).astype(BF16), wpw_ref[...]) * szc_ref[sub, :].astype(F32)).astype(BF16)


def _conformer(proj, conv_w, conv_b, ln_g, ln_b, w_pw, tm):
    seq = proj.shape[0]
    per = tm // CONV_HALO
    const = lambda a: pl.BlockSpec(a.shape, lambda i: (0, 0))
    return pl.pallas_call(
        functools.partial(_conformer_kernel, tm=tm),
        grid=(seq // tm,),
        in_specs=[pl.BlockSpec((tm, D_CONFORMER), lambda i: (i, 0)),
                  pl.BlockSpec((CONV_HALO, D_CONFORMER), lambda i: (jnp.maximum(i * per - 1, 0), 0)),
                  pl.BlockSpec((tm, D_CONFORMER), lambda i: (i, 1)),
                  const(conv_w), const(conv_b), const(ln_g), const(ln_b), const(w_pw)],
        out_specs=pl.BlockSpec((tm, D_CONFORMER), lambda i: (i, 0)),
        out_shape=jax.ShapeDtypeStruct((seq, D_CONFORMER), BF16),
        scratch_shapes=[pltpu.VMEM((D_CONFORMER // 128, tm + CONV_HALO, 128), F32),
                        pltpu.VMEM((tm, D_CONFORMER), F32)],
        compiler_params=_params(("arbitrary",)),
        name="conformer",
    )(proj, proj, proj, conv_w, conv_b, ln_g, ln_b, w_pw)


def _t5_bucket(dist):
    max_exact = REL_BUCKETS // 2
    d_f = jnp.maximum(dist, 1).astype(F32)
    large = max_exact + (jnp.log(d_f / max_exact) / math.log(REL_MAX_DISTANCE / max_exact)
                         * (REL_BUCKETS - max_exact)).astype(jnp.int32)
    large = jnp.minimum(large, REL_BUCKETS - 1)
    return jnp.where(dist < max_exact, dist, large)


def _attn_bias(rel_bias):
    q = ATTN_BLOCK
    nh = rel_bias.shape[1]
    kj = np.arange(2 * q)[None, None, :]
    neg = lambda n: jnp.full((nh, n), NEG_INF, F32)
    tabs = []
    for window, dilation in DILATED_PATTERNS:
        steps = window // dilation
        assert steps == q
        rel = jnp.asarray(np.arange(steps + 1) * dilation, dtype=jnp.int32)
        by_rel = rel_bias.astype(F32)[_t5_bucket(rel)].T
        g = jnp.concatenate([by_rel[:, ::-1], neg(2 * q - 1)], axis=1)
        flat = jnp.tile(g, (1, q))[:, :q * (3 * q - 1)]
        tab = flat.reshape(nh, q, 3 * q - 1)[:, :, :2 * q]
        if dilation == 1:
            tab = tab.reshape(nh, q // MERGE_DIL, MERGE_DIL, 2 * q).transpose(0, 2, 1, 3).reshape(nh, q, 2 * q)
        tab = tab * LOG2E
        tabs.append(tab)
        tabs.append(jnp.where(kj >= q, tab, NEG_INF))
    return jnp.stack(tabs, axis=1)


def _attn_kernel(q_ref, k_ref, v_ref, szd_ref, bias_ref, o_ref,
                 qf, kf, vf, qm, km, vm, qs1, qs2, ks0, ks1, ks2, vs0, vs1, vs2,
                 o0, o1, o2, l0, l1, l2):
    i = pl.program_id(1)
    t = ATTN_TILE
    blk = ATTN_BLOCK
    first_tile = i == 0
    qf[...] = q_ref[...].astype(F32)
    kf[...] = k_ref[...].astype(F32)
    vf[...] = v_ref[...].astype(F32)
    qstreams = (None, qs1, qs2)
    kstreams = (ks0, ks1, ks2)
    vstreams = (vs0, vs1, vs2)
    md = MERGE_DIL
    mspan = t // md

    def stream(src_f, src_m, dil, r):
        n = t // dil
        if dil == md:
            x = src_f[pl.ds(r, n, stride=dil), :]
            src_m[r * n:(r + 1) * n, :] = x
            return x
        return src_m[pl.ds((r % md) * mspan + r // md, n, stride=dil // md), :]

    for pidx, (_, dil) in enumerate(DILATED_PATTERNS):
        n = t // dil
        per = blk + n
        for r in range(dil):
            base = r * per
            for src_f, src_m, src_ref, dst in ((kf, km, k_ref, kstreams[pidx]), (vf, vm, v_ref, vstreams[pidx])):
                tail = dst[base + n:base + per, :]
                dst[base:base + blk, :] = jnp.where(first_tile, jnp.zeros_like(tail), tail)
                if dil == 1:
                    dst[base + blk:base + per, :] = src_ref[...]
                else:
                    dst[base + blk:base + per, :] = stream(src_f, src_m, dil, r).astype(BF16)
            if dil > 1:
                qstreams[pidx][r * n:(r + 1) * n, :] = stream(qf, qm, dil, r).astype(BF16)

    ones = jnp.ones((2 * blk, HEAD_DIM), BF16)
    first = first_tile.astype(jnp.int32)

    def block(pidx, dil, r, sb, out_sc, lse_sc):
        n = t // dil
        per = blk + n
        if dil == 1:
            qb = jnp.concatenate([qf[pl.ds(sb * blk + c, blk // md, stride=md), :] for c in range(md)],
                                 axis=0).astype(BF16)
        else:
            qb = qstreams[pidx][r * n + sb * blk:r * n + (sb + 1) * blk, :]
        krows = slice(r * per + sb * blk, r * per + (sb + 2) * blk)
        kb = kstreams[pidx][krows, :]
        vb = vstreams[pidx][krows, :]
        s = lax.dot_general(qb, kb, _NT, preferred_element_type=F32)
        s = s + bias_ref[0, 2 * pidx + first if sb == 0 else 2 * pidx]
        mx = jnp.max(s, axis=-1, keepdims=True)
        p = jnp.exp2(s - mx).astype(BF16)
        pv = _dot(p, jnp.concatenate([vb, ones], axis=1))
        den = pv[:, HEAD_DIM:]
        out = pv[:, :HEAD_DIM] / den
        lse = mx + jnp.log2(den)
        if dil == 1:
            piece = blk // md
            for c in range(md):
                rows = slice(c * mspan + sb * piece, c * mspan + (sb + 1) * piece)
                out_sc[rows, :] = out[c * piece:(c + 1) * piece, :]
                lse_sc[rows, :] = lse[c * piece:(c + 1) * piece, :]
        elif dil == md:
            rows = slice(r * mspan + sb * blk, r * mspan + (sb + 1) * blk)
            out_sc[rows, :] = out
            lse_sc[rows, :] = lse
        else:
            step = dil // md
            rows = pl.ds((r % md) * mspan + sb * blk * step + r // md, blk, stride=step)
            out_sc[rows, :] = out
            lse_sc[rows, :] = lse

    scratch = ((o0, l0), (o1, l1), (o2, l2))
    for pidx, (_, dil) in enumerate(DILATED_PATTERNS):
        for sb in range(t // (blk * dil)):
            for r in range(dil):
                block(pidx, dil, r, sb, *scratch[pidx])

    la, lb, lc = l0[...], l1[...], l2[...]
    lall = jnp.maximum(jnp.maximum(la, lb), lc)
    wa, wb, wc = jnp.exp2(la - lall), jnp.exp2(lb - lall), jnp.exp2(lc - lall)
    merged = (wa * o0[...] + wb * o1[...] + wc * o2[...]) / (wa + wb + wc)
    for c in range(md):
        qf[pl.ds(c, mspan, stride=md), :] = merged[c * mspan:(c + 1) * mspan, :]
    o_ref[...] = (qf[...] * szd_ref[...].astype(F32)).astype(BF16)


def _attention(proj, bias):
    seq = proj.shape[0]
    t = ATTN_TILE
    q0 = (2 * D_CONFORMER) // HEAD_DIM
    k0 = q0 + N_HEADS
    v0 = k0 + N_HEADS
    z0 = v0 + N_HEADS
    cur = lambda c0: pl.BlockSpec((t, HEAD_DIM), lambda h, i: (i, c0 + h))
    f32 = lambda rows: pltpu.VMEM((rows, HEAD_DIM), F32)
    bf16 = lambda rows: pltpu.VMEM((rows, HEAD_DIM), BF16)
    streams = [bf16(dil * (ATTN_BLOCK + t // dil)) for _, dil in DILATED_PATTERNS]
    return pl.pallas_call(
        _attn_kernel,
        grid=(N_HEADS, seq // t),
        in_specs=[cur(q0), cur(k0), cur(v0), cur(z0),
                  pl.BlockSpec((1,) + bias.shape[1:], lambda h, i: (h, 0, 0, 0))],
        out_specs=pl.BlockSpec((t, HEAD_DIM), lambda h, i: (i, h)),
        out_shape=jax.ShapeDtypeStruct((seq, D_ATTN), BF16),
        scratch_shapes=[f32(t)] * 6 + [bf16(t)] * 2 + streams + streams + [f32(t)] * 6,
        compiler_params=_params(("arbitrary", "arbitrary")),
        name="dilated_attn",
    )(proj, proj, proj, proj, bias)


def _odd_out_kernel(h_ref, yc_ref, yd_ref, wout_ref, g_ref, o_ref, *, tm):
    for r0 in range(0, tm, OUT_ROWS):
        rows = slice(r0, r0 + OUT_ROWS)
        acc = _dot(yc_ref[rows, :], wout_ref[0:D_CONFORMER, :])
        acc = acc + _dot(yd_ref[rows, :], wout_ref[D_CONFORMER:D_CONFORMER + D_ATTN, :])
        o_ref[rows, :] = _rms(h_ref[rows, :] + acc, g_ref[...])


def _odd_out(h, yc, yd, w_out, g, tm):
    seq = h.shape[0]
    row = lambda w: pl.BlockSpec((tm, w), lambda i: (i, 0))
    const = lambda a: pl.BlockSpec(a.shape, lambda i: (0, 0), pipeline_mode=pl.Buffered(1))
    return pl.pallas_call(
        functools.partial(_odd_out_kernel, tm=tm),
        grid=(seq // tm,),
        in_specs=[row(D_MODEL), row(D_CONFORMER), row(D_ATTN), const(w_out), const(g)],
        out_specs=row(D_MODEL),
        out_shape=jax.ShapeDtypeStruct((seq, D_MODEL), F32),
        compiler_params=_params(("arbitrary",)),
        name="odd_out",
    )(h, yc, yd, w_out, g)


def _row_tile(seq, want):
    return min(seq, want)


def kernel(x, norm_g, final_norm_g, ev_w_in, ev_conv_w, s5_lam_re, s5_lam_im, s5_log_dt, s5_b_re, s5_b_im, s5_c_re, s5_c_im, s5_d, s5_w_glu, ev_w_out, od_w_in, cf_conv_w, cf_conv_b, cf_ln_g, cf_ln_b, cf_w_pw, od_w_out, rel_bias):
    bsz, seq, _ = x.shape
    assert bsz == 1 and seq % ATTN_TILE == 0
    x2 = x.reshape(seq, D_MODEL)
    g0 = norm_g[0].astype(F32).reshape(1, D_MODEL)
    g1 = norm_g[1].astype(F32).reshape(1, D_MODEL)
    gf = final_norm_g.astype(F32).reshape(1, D_MODEL)

    ya, u, szb, ev_w_out_b, od_w_out_b = _even_in(
        x2, g0, ev_w_in[0].astype(BF16), ev_conv_w[0].astype(F32), ev_w_out[0].astype(F32),
        od_w_out[0].astype(F32), _row_tile(seq, 1024), 512)
    tables = _s5_tables(s5_lam_re[0], s5_lam_im[0], s5_log_dt[0], s5_b_re[0], s5_b_im[0],
                        s5_c_re[0], s5_c_im[0], s5_d[0])
    ys = _s5(u, tables, min(seq // SSM_CHUNK, 128))
    h1, hn1, od_w_in_b = _even_out(x2, ya, ys, szb, s5_w_glu[0].astype(BF16), ev_w_out_b, g1,
                                   od_w_in[0].astype(F32), _row_tile(seq, 512))

    proj = _odd_in(hn1, od_w_in_b, _row_tile(seq, 2048), 512)
    conv_w = jnp.concatenate([cf_conv_w[0].astype(F32), jnp.zeros((1, D_CONFORMER), F32)], axis=0)
    yc = _conformer(proj, conv_w, cf_conv_b[0].astype(F32).reshape(1, -1),
                    cf_ln_g[0].astype(F32).reshape(1, -1), cf_ln_b[0].astype(F32).reshape(1, -1),
                    cf_w_pw[0].astype(BF16), _row_tile(seq, 1024))
    yd = _attention(proj, _attn_bias(rel_bias))
    out = _odd_out(h1, yc, yd, od_w_out_b, gf, _row_tile(seq, 512))
    return out.reshape(bsz, seq, D_MODEL)
```

```python
import functools
import math

import numpy as np
import jax
import jax.numpy as jnp
from jax import lax
from jax.experimental import pallas as pl
from jax.experimental.pallas import tpu as pltpu

F32 = jnp.float32
BF16 = jnp.bfloat16

D_MODEL = 2048
D_SHORTCONV = 1536
SHORTCONV_WIDTH = 3
D_SSM = 512
SSM_GROUP = 16
SSM_GROUPS = 32
SSM_STATE = 64
D_CONFORMER = 1024
CONFORMER_WIDTH = 31
D_ATTN = 1024
HEAD_DIM = 128
N_HEADS = 8
DILATED_PATTERNS = ((128, 1), (512, 4), (2048, 16))
ATTN_BLOCK = 128
REL_BUCKETS = 32
REL_MAX_DISTANCE = 2048
NORM_EPS = 1e-6
NEG_INF = -1e30

IN_ROWS = 512
EVEN_ROWS = 256
SSM_CHUNK = 16
SSM_FLAT = SSM_CHUNK * SSM_GROUP
SSM_SLABS = D_SSM // 128
ATTN_TILE = 2048
LOG2E = math.log2(math.e)
Q_SCALE = HEAD_DIM ** -0.5 * LOG2E
MERGE_DIL = 4
CONV_HALO = 32
VMEM_LIMIT = 50 * 1024 * 1024


def _params(sem, vmem=VMEM_LIMIT):
    return pltpu.CompilerParams(dimension_semantics=sem, vmem_limit_bytes=vmem)


def _dot(a, b):
    return jnp.dot(a, b, preferred_element_type=F32)


def _sigmoid(z):
    return 1.0 / (1.0 + jnp.exp(-z))


def _silu(z):
    return z * _sigmoid(z)


def _gelu_tanh(x):
    return 0.5 * x * (1.0 + jnp.tanh(math.sqrt(2.0 / math.pi) * (x + 0.044715 * (x * x * x))))


def _cast_slice_spec(a, nsteps, step_of):
    rows = a.shape[0] // nsteps
    assert rows * nsteps == a.shape[0] and rows % 16 == 0, (a.shape, nsteps)
    return pl.BlockSpec((rows, a.shape[1]), lambda *idx: (step_of(*idx), 0))


def _rms(xf, g):
    ms = jnp.mean(xf * xf, axis=-1, keepdims=True)
    return xf * lax.rsqrt(ms + NORM_EPS) * g


def _even_in_kernel(x_ref, g_ref, wa_ref, wb_ref, wc_ref, wd_ref, cw_ref, c0_ref, c1_ref,
                    ya_ref, u_ref, szb_ref, c0b_ref, c1b_ref, hn_sc, pbuf_sc, carry_sc, *, tm, col):
    i = pl.program_id(0)
    j = pl.program_id(1)
    nconv = D_SHORTCONV // col
    c0b_ref[...] = c0_ref[...].astype(BF16)
    c1b_ref[...] = c1_ref[...].astype(BF16)

    def conv_step(first):
        @pl.when(i == 0)
        def _():
            carry_sc[j] = jnp.zeros((8, col), F32)

        pbuf_sc[0:8, :] = carry_sc[j]
        cw = cw_ref[...]
        for r0 in range(0, tm, EVEN_ROWS):
            rows = slice(r0, r0 + EVEN_ROWS)
            if first:
                hn = _rms(x_ref[rows, :], g_ref[...]).astype(BF16)
                hn_sc[rows, :] = hn
            else:
                hn = hn_sc[rows, :]
            p = _dot(hn, wb_ref[...]) * _dot(hn, wa_ref[...])
            pbuf_sc[8 + r0:8 + r0 + EVEN_ROWS, :] = p
            conv = (cw[2:3, :] * p + cw[1:2, :] * pbuf_sc[pl.ds(7 + r0, EVEN_ROWS), :]
                    + cw[0:1, :] * pbuf_sc[pl.ds(6 + r0, EVEN_ROWS), :])
            gb = _dot(hn, wc_ref[...])
            za = _dot(hn, wd_ref[...])
            ya_ref[rows, :] = (gb * conv * _silu(za)).astype(BF16)
        carry_sc[j] = pbuf_sc[tm:tm + 8, :]

    pl.when(j == 0)(functools.partial(conv_step, True))
    pl.when((j > 0) & (j < nconv))(functools.partial(conv_step, False))

    @pl.when(j >= nconv)
    def _():
        for r0 in range(0, tm, EVEN_ROWS):
            rows = slice(r0, r0 + EVEN_ROWS)
            hn = hn_sc[rows, :]
            u = _dot(hn, wa_ref[...])
            for s in range(col // 128):
                u_ref[s, rows, :] = u[:, s * 128:(s + 1) * 128]
            szb_ref[rows, :] = _silu(_dot(hn, wb_ref[...])).astype(BF16)


def _even_in(x, g, w_in, conv_w, cast0, cast1, tm, col):
    seq = x.shape[0]
    nconv = D_SHORTCONV // col
    nssm = D_SSM // col
    u_blk = 4 * nconv

    def wspec(k, alt0):
        return pl.BlockSpec((D_MODEL, col),
                            lambda i, j: (0, jnp.where(j < nconv, j + nconv * k, alt0 + j - nconv)))

    hold = lambda k: pl.BlockSpec((D_MODEL, col), lambda i, j: (0, jnp.minimum(j, nconv - 1) + nconv * k))
    ssm_step = lambda j: jnp.maximum(j - nconv, 0)
    nj = nconv + nssm
    cast_spec = lambda a: _cast_slice_spec(a, (seq // tm) * nj, lambda i, j: i * nj + j)
    return pl.pallas_call(
        functools.partial(_even_in_kernel, tm=tm, col=col),
        grid=(seq // tm, nj),
        in_specs=[
            pl.BlockSpec((tm, D_MODEL), lambda i, j: (i, 0)),
            pl.BlockSpec((1, D_MODEL), lambda i, j: (0, 0)),
            wspec(0, u_blk), wspec(1, u_blk + nssm), hold(2), hold(3),
            pl.BlockSpec((SHORTCONV_WIDTH, col), lambda i, j: (0, jnp.minimum(j, nconv - 1))),
            cast_spec(cast0), cast_spec(cast1),
        ],
        out_specs=[
            pl.BlockSpec((tm, col), lambda i, j: (i, jnp.minimum(j, nconv - 1))),
            pl.BlockSpec((col // 128, tm, 128), lambda i, j: (ssm_step(j), i, 0)),
            pl.BlockSpec((tm, col), lambda i, j: (i, ssm_step(j))),
            cast_spec(cast0), cast_spec(cast1),
        ],
        out_shape=[
            jax.ShapeDtypeStruct((seq, D_SHORTCONV), BF16),
            jax.ShapeDtypeStruct((SSM_SLABS, seq, 128), F32),
            jax.ShapeDtypeStruct((seq, D_SSM), BF16),
            jax.ShapeDtypeStruct(cast0.shape, BF16),
            jax.ShapeDtypeStruct(cast1.shape, BF16),
        ],
        scratch_shapes=[
            pltpu.VMEM((tm, D_MODEL), BF16),
            pltpu.VMEM((tm + 8, col), F32),
            pltpu.VMEM((nconv, 8, col), F32),
        ],
        compiler_params=_params(("arbitrary", "arbitrary")),
        name="even_in",
    )(x, g, w_in, w_in, w_in, w_in, conv_w, cast0, cast1)


def _s5_tables(lam_re, lam_im, log_dt, b_re, b_im, c_re, c_im, d_skip):
    hi = lax.Precision.HIGH
    t_, g_, n_, p_ = SSM_CHUNK, SSM_GROUPS, SSM_STATE, SSM_GROUP
    lam_re, lam_im = lam_re.astype(F32), lam_im.astype(F32)
    dt = jnp.exp(log_dt.astype(F32))[:, None]
    xr, xi = lam_re * dt, lam_im * dt
    k = jnp.arange(t_ + 1, dtype=F32)[:, None, None]
    mag = jnp.exp(xr * k)
    pr, pi = mag * jnp.cos(xi * k), mag * jnp.sin(xi * k)
    nr, ni = pr[1] - 1.0, pi[1]
    den = lam_re * lam_re + lam_im * lam_im
    fr = (nr * lam_re + ni * lam_im) / den
    fi = (ni * lam_re - nr * lam_im) / den
    b_re, b_im = b_re.astype(F32), b_im.astype(F32)
    bbr = fr[..., None] * b_re - fi[..., None] * b_im
    bbi = fr[..., None] * b_im + fi[..., None] * b_re
    c_re, c_im = c_re.astype(F32), c_im.astype(F32)

    cpr = c_re[None] * pr[:t_, :, None, :] - c_im[None] * pi[:t_, :, None, :]
    cpi = c_re[None] * pi[:t_, :, None, :] + c_im[None] * pr[:t_, :, None, :]
    kern = jnp.einsum('tgpn,gnq->tgpq', jnp.concatenate([cpr, -cpi], axis=-1),
                      jnp.concatenate([bbr, bbi], axis=1), precision=hi)
    skip = d_skip.astype(F32).reshape(g_, p_)[:, :, None] * jnp.eye(p_, dtype=F32)[None]
    kern = kern.at[0].add(skip)
    flat = t_ * p_
    last = jnp.transpose(kern[::-1], (1, 2, 0, 3)).reshape(g_, p_, flat)
    mt = jnp.stack([jnp.pad(last[:, :, (t_ - 1 - t) * p_:], ((0, 0), (0, 0), (0, (t_ - 1 - t) * p_)))
                    for t in range(t_)], axis=1).reshape(g_, flat, flat)

    prj, pij = pr[t_ - 1::-1][:t_], pi[t_ - 1::-1][:t_]
    wr = prj[..., None] * bbr[None] - pij[..., None] * bbi[None]
    wi = prj[..., None] * bbi[None] + pij[..., None] * bbr[None]
    half = (np.arange(2) == 0)[None, None, :, None, None]

    def pack_w(w):
        w = jnp.transpose(w.reshape(t_, g_ // 2, 2, n_, p_), (1, 0, 2, 4, 3))
        w = jnp.concatenate([jnp.where(half, w, 0.0), jnp.where(half, 0.0, w)], axis=-1)
        return w.reshape(g_ // 2, 2 * flat, 2 * n_)

    vr = c_re[None] * pr[1:, :, None, :] - c_im[None] * pi[1:, :, None, :]
    vi = c_re[None] * pi[1:, :, None, :] + c_im[None] * pr[1:, :, None, :]
    vr = jnp.transpose(vr, (1, 0, 2, 3)).reshape(g_ // 2, 2, flat, n_)
    vi = -jnp.transpose(vi, (1, 0, 2, 3)).reshape(g_ // 2, 2, flat, n_)
    zv = jnp.zeros((g_ // 2, flat, n_), F32)
    v_even = jnp.concatenate([vr[:, 0], zv, vi[:, 0], zv], axis=-1)
    v_odd = jnp.concatenate([zv, vr[:, 1], zv, vi[:, 1]], axis=-1)
    vt = jnp.stack([v_even, v_odd], axis=1).reshape(g_, flat, 4 * n_)
    a_re = pr[t_].reshape(1, g_ * n_)
    a_im = pi[t_].reshape(1, g_ * n_)
    return mt.astype(BF16), pack_w(wr).astype(BF16), pack_w(wi).astype(BF16), vt.astype(BF16), a_re, a_im


_NT = (((1,), (1,)), ((), ()))
_TN = (((0,), (0,)), ((), ()))


def _s5_kernel(u_ref, mt_ref, wre_ref, wim_ref, vt_ref, are_ref, aim_ref, y_ref,
               xt_sc, yt_sc, zre_sc, zim_sc, carry_sc, *, ct):
    i = pl.program_id(0)
    t_ = SSM_CHUNK

    @pl.when(i == 0)
    def _():
        carry_sc[...] = jnp.zeros_like(carry_sc)

    for s in range(SSM_SLABS):
        for j in range(t_):
            xt_sc[s, j] = u_ref[s, pl.ds(j, ct, stride=t_), :].T.astype(BF16)

    pairs_per_slab = 128 // (2 * SSM_GROUP)
    for gp in range(SSM_GROUPS // 2):
        s, k = gp // pairs_per_slab, gp % pairs_per_slab
        xp = xt_sc[s, :, 32 * k:32 * k + 32, :].reshape(2 * SSM_FLAT, ct)
        zre_sc[:, gp * 128:(gp + 1) * 128] = lax.dot_general(xp, wre_ref[gp], _TN, preferred_element_type=F32)
        zim_sc[:, gp * 128:(gp + 1) * 128] = lax.dot_general(xp, wim_ref[gp], _TN, preferred_element_type=F32)

    are = are_ref[...]
    aim = aim_ref[...]

    def step(r, carry):
        sre, sim = carry
        zr = zre_sc[pl.ds(r, 1), :]
        zi = zim_sc[pl.ds(r, 1), :]
        zre_sc[pl.ds(r, 1), :] = sre
        zim_sc[pl.ds(r, 1), :] = sim
        return are * sre - aim * sim + zr, are * sim + aim * sre + zi

    sre, sim = lax.fori_loop(0, ct, step, (carry_sc[0:1, :], carry_sc[1:2, :]), unroll=8)
    carry_sc[0:1, :] = sre
    carry_sc[1:2, :] = sim

    groups_per_slab = 128 // SSM_GROUP
    for g in range(SSM_GROUPS):
        gp, s, k = g // 2, g // groups_per_slab, g % groups_per_slab
        xg = xt_sc[s, :, 16 * k:16 * k + 16, :].reshape(SSM_FLAT, ct)
        sp = jnp.concatenate([zre_sc[:, gp * 128:(gp + 1) * 128],
                              zim_sc[:, gp * 128:(gp + 1) * 128]], axis=1).astype(BF16)
        yt = _dot(mt_ref[g], xg) + lax.dot_general(vt_ref[g], sp, _NT, preferred_element_type=F32)
        yt_sc[s, :, 16 * k:16 * k + 16, :] = yt.reshape(t_, SSM_GROUP, ct)

    for s in range(SSM_SLABS):
        for t in range(t_):
            y_ref[s, pl.ds(t, ct, stride=t_), :] = yt_sc[s, t].T


def _s5(u4, tables, ct):
    mt, wre, wim, vt, are, aim = tables
    _, seq, _ = u4.shape
    full = lambda a: pl.BlockSpec(a.shape, lambda i: (0,) * a.ndim)
    rows = pl.BlockSpec((SSM_SLABS, SSM_CHUNK * ct, 128), lambda i: (0, i, 0))
    nstate = SSM_GROUPS * SSM_STATE
    return pl.pallas_call(
        functools.partial(_s5_kernel, ct=ct),
        grid=(seq // (SSM_CHUNK * ct),),
        in_specs=[rows, full(mt), full(wre), full(wim), full(vt), full(are), full(aim)],
        out_specs=rows,
        out_shape=jax.ShapeDtypeStruct(u4.shape, F32),
        scratch_shapes=[pltpu.VMEM((SSM_SLABS, SSM_CHUNK, 128, ct), BF16),
                        pltpu.VMEM((SSM_SLABS, SSM_CHUNK, 128, ct), F32),
                        pltpu.VMEM((ct, nstate), F32),
                        pltpu.VMEM((ct, nstate), F32),
                        pltpu.VMEM((8, nstate), F32)],
        compiler_params=_params(("arbitrary",)),
        name="s5_ssm",
    )(u4, mt, wre, wim, vt, are, aim)


OUT_ROWS = 256


def _even_out_kernel(x_ref, ya_ref, ys_ref, szb_ref, wglu_ref, wout_ref, g_ref, c0_ref,
                     h_ref, hn_ref, c0b_ref, *, tm):
    c0b_ref[...] = c0_ref[...].astype(BF16)
    for r0 in range(0, tm, OUT_ROWS):
        rows = slice(r0, r0 + OUT_ROWS)
        gl = _gelu_tanh(jnp.concatenate([ys_ref[s, rows, :] for s in range(SSM_SLABS)], axis=1))
        yb = gl * _sigmoid(_dot(gl.astype(BF16), wglu_ref[...])) * szb_ref[rows, :].astype(F32)
        acc = _dot(ya_ref[rows, :], wout_ref[0:D_SHORTCONV, :])
        acc = acc + _dot(yb.astype(BF16), wout_ref[D_SHORTCONV:D_SHORTCONV + D_SSM, :])
        h = x_ref[rows, :] + acc
        h_ref[rows, :] = h
        hn_ref[rows, :] = _rms(h, g_ref[...]).astype(BF16)


def _even_out(x, ya, ys, szb, w_glu, w_out, g, cast0, tm):
    seq = x.shape[0]
    cast_spec = _cast_slice_spec(cast0, seq // tm, lambda i: i)
    row = lambda w: pl.BlockSpec((tm, w), lambda i: (i, 0))
    const = lambda a: pl.BlockSpec(a.shape, lambda i: (0, 0), pipeline_mode=pl.Buffered(1))
    return pl.pallas_call(
        functools.partial(_even_out_kernel, tm=tm),
        grid=(seq // tm,),
        in_specs=[row(D_MODEL), row(D_SHORTCONV),
                  pl.BlockSpec((SSM_SLABS, tm, 128), lambda i: (0, i, 0)), row(D_SSM),
                  const(w_glu), const(w_out), const(g), cast_spec],
        out_specs=[row(D_MODEL), row(D_MODEL), cast_spec],
        out_shape=[jax.ShapeDtypeStruct((seq, D_MODEL), F32),
                   jax.ShapeDtypeStruct((seq, D_MODEL), BF16),
                   jax.ShapeDtypeStruct(cast0.shape, BF16)],
        compiler_params=_params(("arbitrary",)),
        name="even_out",
    )(x, ya, ys, szb, w_glu, w_out, g, cast0)


ODD_GROUPS = 6


def _odd_in_kernel(hn_ref, wa_ref, wb_ref, o_ref, *, tm, col):
    s = pl.program_id(1)
    nb = D_CONFORMER // col
    group = s // nb
    is_silu = (group == 1) | (group == 5)
    blocks = [slice(r0, r0 + IN_ROWS) for r0 in range(0, tm, IN_ROWS)]

    @pl.when(group == 0)
    def _():
        for rows in blocks:
            hn = hn_ref[rows, :]
            o_ref[rows, :] = (_dot(hn, wa_ref[...]) * _sigmoid(_dot(hn, wb_ref[...]))).astype(BF16)

    @pl.when(is_silu)
    def _():
        for rows in blocks:
            o_ref[rows, :] = _silu(_dot(hn_ref[rows, :], wa_ref[...])).astype(BF16)

    @pl.when((group > 0) & jnp.logical_not(is_silu))
    def _():
        scale = jnp.where(group == 2, Q_SCALE, 1.0).astype(F32)
        for rows in blocks:
            o_ref[rows, :] = (_dot(hn_ref[rows, :], wa_ref[...]) * scale).astype(BF16)


def _odd_in(hn, w_in, tm, col):
    seq = hn.shape[0]
    nb = D_CONFORMER // col
    return pl.pallas_call(
        functools.partial(_odd_in_kernel, tm=tm, col=col),
        grid=(seq // tm, ODD_GROUPS * nb),
        in_specs=[pl.BlockSpec((tm, D_MODEL), lambda i, s: (i, 0)),
                  pl.BlockSpec((D_MODEL, col), lambda i, s: (0, jnp.where(s < nb, s, s + nb))),
                  pl.BlockSpec((D_MODEL, col), lambda i, s: (0, jnp.where(s < nb, s + nb, 2 * nb - 1)))],
        out_specs=pl.BlockSpec((tm, col), lambda i, s: (i, s)),
        out_shape=jax.ShapeDtypeStruct((seq, ODD_GROUPS * D_CONFORMER), BF16),
        compiler_params=_params(("arbitrary", "arbitrary")),
        name="odd_in",
    )(hn, w_in, w_in)


CONV_ROWS = 64


def _conformer_kernel(cur_ref, halo_ref, szc_ref, cw_ref, cb_ref, lg_ref, lb_ref, wpw_ref,
                      o_ref, buf_sc, conv_sc, *, tm):
    i = pl.program_id(0)
    nslab = D_CONFORMER // 128
    for c in range(nslab):
        lanes = slice(c * 128, (c + 1) * 128)
        buf_sc[c, 0:CONV_HALO, :] = jnp.where(i == 0, 0.0, halo_ref[:, lanes].astype(F32))
        buf_sc[c, CONV_HALO:CONV_HALO + tm, :] = cur_ref[:, lanes].astype(F32)
    shift = CONV_HALO - (CONFORMER_WIDTH - 1)

    def rows(rc, _):
        base = pl.multiple_of(rc * CONV_ROWS, CONV_ROWS)
        for c in range(nslab):
            lanes = slice(c * 128, (c + 1) * 128)
            acc = jnp.zeros((CONV_ROWS, 128), F32) + cb_ref[:, lanes]
            for k in range(CONFORMER_WIDTH):
                acc = acc + cw_ref[k:k + 1, lanes] * buf_sc[c, pl.ds(base + (shift + k), CONV_ROWS), :]
            conv_sc[pl.ds(base, CONV_ROWS), lanes] = acc
        return 0

    lax.fori_loop(0, tm // CONV_ROWS, rows, 0)
    for r0 in range(0, tm, OUT_ROWS):
        sub = slice(r0, r0 + OUT_ROWS)
        y = conv_sc[sub, :]
        mu = jnp.mean(y, axis=-1, keepdims=True)
        yc = y - mu
        var = jnp.mean(yc * yc, axis=-1, keepdims=True)
        ln = yc * lax.rsqrt(var + NORM_EPS) * lg_ref[...] + lb_ref[...]
        o_ref[sub, :] = (_dot(_silu(ln).astype(BF16), wpw_ref[...]) * szc_ref[sub, :].astype(F32)).astype(BF16)


def _conformer(proj, conv_w, conv_b, ln_g, ln_b, w_pw, tm):
    seq = proj.shape[0]
    per = tm // CONV_HALO
    const = lambda a: pl.BlockSpec(a.shape, lambda i: (0, 0))
    return pl.pallas_call(
        functools.partial(_conformer_kernel, tm=tm),
        grid=(seq // tm,),
        in_specs=[pl.BlockSpec((tm, D_CONFORMER), lambda i: (i, 0)),
                  pl.BlockSpec((CONV_HALO, D_CONFORMER), lambda i: (jnp.maximum(i * per - 1, 0), 0)),
                  pl.BlockSpec((tm, D_CONFORMER), lambda i: (i, 1)),
                  const(conv_w), const(conv_b), const(ln_g), const(ln_b), const(w_pw)],
        out_specs=pl.BlockSpec((tm, D_CONFORMER), lambda i: (i, 0)),
        out_shape=jax.ShapeDtypeStruct((seq, D_CONFORMER), BF16),
        scratch_shapes=[pltpu.VMEM((D_CONFORMER // 128, tm + CONV_HALO, 128), F32),
                        pltpu.VMEM((tm, D_CONFORMER), F32)],
        compiler_params=_params(("arbitrary",)),
        name="conformer",
    )(proj, proj, proj, conv_w, conv_b, ln_g, ln_b, w_pw)


def _t5_bucket(dist):
    max_exact = REL_BUCKETS // 2
    d_f = jnp.maximum(dist, 1).astype(F32)
    large = max_exact + (jnp.log(d_f / max_exact) / math.log(REL_MAX_DISTANCE / max_exact)
                         * (REL_BUCKETS - max_exact)).astype(jnp.int32)
    large = jnp.minimum(large, REL_BUCKETS - 1)
    return jnp.where(dist < max_exact, dist, large)


def _attn_bias(rel_bias):
    q = ATTN_BLOCK
    nh = rel_bias.shape[1]
    kj = np.arange(2 * q)[None, None, :]
    neg = lambda n: jnp.full((nh, n), NEG_INF, F32)
    tabs = []
    for window, dilation in DILATED_PATTERNS:
        steps = window // dilation
        assert steps == q
        rel = jnp.asarray(np.arange(steps + 1) * dilation, dtype=jnp.int32)
        by_rel = rel_bias.astype(F32)[_t5_bucket(rel)].T
        g = jnp.concatenate([by_rel[:, ::-1], neg(2 * q - 1)], axis=1)
        flat = jnp.tile(g, (1, q))[:, :q * (3 * q - 1)]
        tab = flat.reshape(nh, q, 3 * q - 1)[:, :, :2 * q]
        if dilation == 1:
            tab = tab.reshape(nh, q // MERGE_DIL, MERGE_DIL, 2 * q).transpose(0, 2, 1, 3).reshape(nh, q, 2 * q)
        tab = tab * LOG2E
        tabs.append(tab)
        tabs.append(jnp.where(kj >= q, tab, NEG_INF))
    return jnp.stack(tabs, axis=1)


def _attn_kernel(q_ref, k_ref, v_ref, szd_ref, bias_ref, o_ref,
                 qf, kf, vf, qm, km, vm, qs1, qs2, ks0, ks1, ks2, vs0, vs1, vs2,
                 o0, o1, o2, l0, l1, l2):
    i = pl.program_id(1)
    t = ATTN_TILE
    blk = ATTN_BLOCK
    first_tile = i == 0
    qf[...] = q_ref[...].astype(F32)
    kf[...] = k_ref[...].astype(F32)
    vf[...] = v_ref[...].astype(F32)
    qstreams = (None, qs1, qs2)
    kstreams = (ks0, ks1, ks2)
    vstreams = (vs0, vs1, vs2)
    md = MERGE_DIL
    mspan = t // md

    def stream(src_f, src_m, dil, r):
        n = t // dil
        if dil == md:
            x = src_f[pl.ds(r, n, stride=dil), :]
            src_m[r * n:(r + 1) * n, :] = x
            return x
        return src_m[pl.ds((r % md) * mspan + r // md, n, stride=dil // md), :]

    for pidx, (_, dil) in enumerate(DILATED_PATTERNS):
        n = t // dil
        per = blk + n
        for r in range(dil):
            base = r * per
            for src_f, src_m, src_ref, dst in ((kf, km, k_ref, kstreams[pidx]), (vf, vm, v_ref, vstreams[pidx])):
                tail = dst[base + n:base + per, :]
                dst[base:base + blk, :] = jnp.where(first_tile, jnp.zeros_like(tail), tail)
                if dil == 1:
                    dst[base + blk:base + per, :] = src_ref[...]
                else:
                    dst[base + blk:base + per, :] = stream(src_f, src_m, dil, r).astype(BF16)
            if dil > 1:
                qstreams[pidx][r * n:(r + 1) * n, :] = stream(qf, qm, dil, r).astype(BF16)

    ones = jnp.ones((2 * blk, HEAD_DIM), BF16)
    first = first_tile.astype(jnp.int32)

    def block(pidx, dil, r, sb, out_sc, lse_sc):
        n = t // dil
        per = blk + n
        if dil == 1:
            qb = jnp.concatenate([qf[pl.ds(sb * blk + c, blk // md, stride=md), :] for c in range(md)],
                                 axis=0).astype(BF16)
        else:
            qb = qstreams[pidx][r * n + sb * blk:r * n + (sb + 1) * blk, :]
        krows = slice(r * per + sb * blk, r * per + (sb + 2) * blk)
        kb = kstreams[pidx][krows, :]
        vb = vstreams[pidx][krows, :]
        s = lax.dot_general(qb, kb, _NT, preferred_element_type=F32)
        s = s + bias_ref[0, 2 * pidx + first if sb == 0 else 2 * pidx]
        mx = jnp.max(s, axis=-1, keepdims=True)
        p = jnp.exp2(s - mx).astype(BF16)
        pv = _dot(p, jnp.concatenate([vb, ones], axis=1))
        den = pv[:, HEAD_DIM:]
        out = pv[:, :HEAD_DIM] / den
        lse = mx + jnp.log2(den)
        if dil == 1:
            piece = blk // md
            for c in range(md):
                rows = slice(c * mspan + sb * piece, c * mspan + (sb + 1) * piece)
                out_sc[rows, :] = out[c * piece:(c + 1) * piece, :]
                lse_sc[rows, :] = lse[c * piece:(c + 1) * piece, :]
        elif dil == md:
            rows = slice(r * mspan + sb * blk, r * mspan + (sb + 1) * blk)
            out_sc[rows, :] = out
            lse_sc[rows, :] = lse
        else:
            step = dil // md
            rows = pl.ds((r % md) * mspan + sb * blk * step + r // md, blk, stride=step)
            out_sc[rows, :] = out
            lse_sc[rows, :] = lse

    scratch = ((o0, l0), (o1, l1), (o2, l2))
    for pidx, (_, dil) in enumerate(DILATED_PATTERNS):
        for sb in range(t // (blk * dil)):
            for r in range(dil):
                block(pidx, dil, r, sb, *scratch[pidx])

    la, lb, lc = l0[...], l1[...], l2[...]
    lall = jnp.maximum(jnp.maximum(la, lb), lc)
    wa, wb, wc = jnp.exp2(la - lall), jnp.exp2(lb - lall), jnp.exp2(lc - lall)
    merged = (wa * o0[...] + wb * o1[...] + wc * o2[...]) / (wa + wb + wc)
    for c in range(md):
        qf[pl.ds(c, mspan, stride=md), :] = merged[c * mspan:(c + 1) * mspan, :]
    o_ref[...] = (qf[...] * szd_ref[...].astype(F32)).astype(BF16)


def _attention(proj, bias):
    seq = proj.shape[0]
    t = ATTN_TILE
    q0 = (2 * D_CONFORMER) // HEAD_DIM
    k0 = q0 + N_HEADS
    v0 = k0 + N_HEADS
    z0 = v0 + N_HEADS
    cur = lambda c0: pl.BlockSpec((t, HEAD_DIM), lambda h, i: (i, c0 + h))
    f32 = lambda rows: pltpu.VMEM((rows, HEAD_DIM), F32)
    bf16 = lambda rows: pltpu.VMEM((rows, HEAD_DIM), BF16)
    streams = [bf16(dil * (ATTN_BLOCK + t // dil)) for _, dil in DILATED_PATTERNS]
    return pl.pallas_call(
        _attn_kernel,
        grid=(N_HEADS, seq // t),
        in_specs=[cur(q0), cur(k0), cur(v0), cur(z0),
                  pl.BlockSpec((1,) + bias.shape[1:], lambda h, i: (h, 0, 0, 0))],
        out_specs=pl.BlockSpec((t, HEAD_DIM), lambda h, i: (i, h)),
        out_shape=jax.ShapeDtypeStruct((seq, D_ATTN), BF16),
        scratch_shapes=[f32(t)] * 6 + [bf16(t)] * 2 + streams + streams + [f32(t)] * 6,
        compiler_params=_params(("arbitrary", "arbitrary")),
        name="dilated_attn",
    )(proj, proj, proj, proj, bias)


def _odd_out_kernel(h_ref, yc_ref, yd_ref, wout_ref, g_ref, o_ref, *, tm):
    for r0 in range(0, tm, OUT_ROWS):
        rows = slice(r0, r0 + OUT_ROWS)
        acc = _dot(yc_ref[rows, :], wout_ref[0:D_CONFORMER, :])
        acc = acc + _dot(yd_ref[rows, :], wout_ref[D_CONFORMER:D_CONFORMER + D_ATTN, :])
        o_ref[rows, :] = _rms(h_ref[rows, :] + acc, g_ref[...])


def _odd_out(h, yc, yd, w_out, g, tm):
    seq = h.shape[0]
    row = lambda w: pl.BlockSpec((tm, w), lambda i: (i, 0))
    const = lambda a: pl.BlockSpec(a.shape, lambda i: (0, 0), pipeline_mode=pl.Buffered(1))
    return pl.pallas_call(
        functools.partial(_odd_out_kernel, tm=tm),
        grid=(seq // tm,),
        in_specs=[row(D_MODEL), row(D_CONFORMER), row(D_ATTN), const(w_out), const(g)],
        out_specs=row(D_MODEL),
        out_shape=jax.ShapeDtypeStruct((seq, D_MODEL), F32),
        compiler_params=_params(("arbitrary",)),
        name="odd_out",
    )(h, yc, yd, w_out, g)


def _row_tile(seq, want):
    return min(seq, want)


def kernel(x, norm_g, final_norm_g, ev_w_in, ev_conv_w, s5_lam_re, s5_lam_im, s5_log_dt, s5_b_re, s5_b_im, s5_c_re, s5_c_im, s5_d, s5_w_glu, ev_w_out, od_w_in, cf_conv_w, cf_conv_b, cf_ln_g, cf_ln_b, cf_w_pw, od_w_out, rel_bias):
    bsz, seq, _ = x.shape
    assert bsz == 1 and seq % ATTN_TILE == 0
    x2 = x.reshape(seq, D_MODEL)
    g0 = norm_g[0].astype(F32).reshape(1, D_MODEL)
    g1 = norm_g[1].astype(F32).reshape(1, D_MODEL)
    gf = final_norm_g.astype(F32).reshape(1, D_MODEL)

    ya, u, szb, ev_w_out_b, od_w_out_b = _even_in(
        x2, g0, ev_w_in[0].astype(BF16), ev_conv_w[0].astype(F32), ev_w_out[0].astype(F32),
        od_w_out[0].astype(F32), _row_tile(seq, 1024), 512)
    tables = _s5_tables(s5_lam_re[0], s5_lam_im[0], s5_log_dt[0], s5_b_re[0], s5_b_im[0],
                        s5_c_re[0], s5_c_im[0], s5_d[0])
    ys = _s5(u, tables, min(seq // SSM_CHUNK, 128))
    h1, hn1, od_w_in_b = _even_out(x2, ya, ys, szb, s5_w_glu[0].astype(BF16), ev_w_out_b, g1,
                                   od_w_in[0].astype(F32), _row_tile(seq, 512))

    proj = _odd_in(hn1, od_w_in_b, _row_tile(seq, 2048), 512)
    conv_w = jnp.concatenate([cf_conv_w[0].astype(F32), jnp.zeros((1, D_CONFORMER), F32)], axis=0)
    yc = _conformer(proj, conv_w, cf_conv_b[0].astype(F32).reshape(1, -1),
                    cf_ln_g[0].astype(F32).reshape(1, -1), cf_ln_b[0].astype(F32).reshape(1, -1),
                    cf_w_pw[0].astype(BF16), _row_tile(seq, 1024))
    yd = _attention(proj, _attn_bias(rel_bias))
    out = _odd_out(h1, yc, yd, od_w_out_b, gf, _row_tile(seq, 512))
    return out.reshape(bsz, seq, D_MODEL)
```

```python
import functools
import math

import numpy as np
import jax
import jax.numpy as jnp
from jax import lax
from jax.experimental import pallas as pl
from jax.experimental.pallas import tpu as pltpu

F32 = jnp.float32
BF16 = jnp.bfloat16

D_MODEL = 2048
D_SHORTCONV = 1536
SHORTCONV_WIDTH = 3
D_SSM = 512
SSM_GROUP = 16
SSM_GROUPS = 32
SSM_STATE = 64
D_CONFORMER = 1024
CONFORMER_WIDTH = 31
D_ATTN = 1024
HEAD_DIM = 128
N_HEADS = 8
DILATED_PATTERNS = ((128, 1), (512, 4), (2048, 16))
ATTN_BLOCK = 128
REL_BUCKETS = 32
REL_MAX_DISTANCE = 2048
NORM_EPS = 1e-6
NEG_INF = -1e30

IN_ROWS = 512
EVEN_ROWS = 256
SSM_CHUNK = 16
SSM_FLAT = SSM_CHUNK * SSM_GROUP
SSM_SLABS = D_SSM // 128
ATTN_TILE = 2048
LOG2E = math.log2(math.e)
Q_SCALE = HEAD_DIM ** -0.5 * LOG2E
MERGE_DIL = 4
CONV_HALO = 32
VMEM_LIMIT = 50 * 1024 * 1024

EVEN_IN_TILE = (1024, 512)
EVEN_OUT_ROWS = 512
ODD_IN_TILE = (2048, 512)
CONFORMER_ROWS = 1024
ODD_OUT_ROWS = 512
SSM_CHUNK_ROWS = 128


def _params(sem, vmem=VMEM_LIMIT):
    return pltpu.CompilerParams(dimension_semantics=sem, vmem_limit_bytes=vmem)


def _dot(a, b):
    return jnp.dot(a, b, preferred_element_type=F32)


def _sigmoid(z):
    return 1.0 / (1.0 + jnp.exp(-z))


def _silu(z):
    return z * _sigmoid(z)


def _gelu_tanh(x):
    return 0.5 * x * (1.0 + jnp.tanh(math.sqrt(2.0 / math.pi) * (x + 0.044715 * (x * x * x))))


def _cast_slice_spec(a, nsteps, step_of):
    rows = a.shape[0] // nsteps
    assert rows * nsteps == a.shape[0] and rows % 16 == 0, (a.shape, nsteps)
    return pl.BlockSpec((rows, a.shape[1]), lambda *idx: (step_of(*idx), 0))


def _rms(xf, g):
    ms = jnp.mean(xf * xf, axis=-1, keepdims=True)
    return xf * lax.rsqrt(ms + NORM_EPS) * g


def _even_in_kernel(x_ref, g_ref, wa_ref, wb_ref, wc_ref, wd_ref, cw_ref, c0_ref, c1_ref,
                    ya_ref, u_ref, szb_ref, c0b_ref, c1b_ref, hn_sc, pbuf_sc, carry_sc, *, tm, col):
    i = pl.program_id(0)
    j = pl.program_id(1)
    nconv = D_SHORTCONV // col
    c0b_ref[...] = c0_ref[...].astype(BF16)
    c1b_ref[...] = c1_ref[...].astype(BF16)

    def conv_step(first):
        @pl.when(i == 0)
        def _():
            carry_sc[j] = jnp.zeros((8, col), F32)

        pbuf_sc[0:8, :] = carry_sc[j]
        cw = cw_ref[...]
        for r0 in range(0, tm, EVEN_ROWS):
            rows = slice(r0, r0 + EVEN_ROWS)
            if first:
                hn = _rms(x_ref[rows, :], g_ref[...]).astype(BF16)
                hn_sc[rows, :] = hn
            else:
                hn = hn_sc[rows, :]
            p = _dot(hn, wb_ref[...]) * _dot(hn, wa_ref[...])
            pbuf_sc[8 + r0:8 + r0 + EVEN_ROWS, :] = p
            conv = (cw[2:3, :] * p + cw[1:2, :] * pbuf_sc[pl.ds(7 + r0, EVEN_ROWS), :]
                    + cw[0:1, :] * pbuf_sc[pl.ds(6 + r0, EVEN_ROWS), :])
            gb = _dot(hn, wc_ref[...])
            za = _dot(hn, wd_ref[...])
            ya_ref[rows, :] = (gb * conv * _silu(za)).astype(BF16)
        carry_sc[j] = pbuf_sc[tm:tm + 8, :]

    pl.when(j == 0)(functools.partial(conv_step, True))
    pl.when((j > 0) & (j < nconv))(functools.partial(conv_step, False))

    @pl.when(j >= nconv)
    def _():
        for r0 in range(0, tm, EVEN_ROWS):
            rows = slice(r0, r0 + EVEN_ROWS)
            hn = hn_sc[rows, :]
            u = _dot(hn, wa_ref[...])
            for s in range(col // 128):
                u_ref[s, rows, :] = u[:, s * 128:(s + 1) * 128]
            szb_ref[rows, :] = _silu(_dot(hn, wb_ref[...])).astype(BF16)


def _even_in(x, g, w_in, conv_w, cast0, cast1, tm, col):
    seq = x.shape[0]
    nconv = D_SHORTCONV // col
    nssm = D_SSM // col
    u_blk = 4 * nconv

    def wspec(k, alt0):
        return pl.BlockSpec((D_MODEL, col),
                            lambda i, j: (0, jnp.where(j < nconv, j + nconv * k, alt0 + j - nconv)))

    hold = lambda k: pl.BlockSpec((D_MODEL, col), lambda i, j: (0, jnp.minimum(j, nconv - 1) + nconv * k))
    ssm_step = lambda j: jnp.maximum(j - nconv, 0)
    nj = nconv + nssm
    cast_spec = lambda a: _cast_slice_spec(a, (seq // tm) * nj, lambda i, j: i * nj + j)
    return pl.pallas_call(
        functools.partial(_even_in_kernel, tm=tm, col=col),
        grid=(seq // tm, nj),
        in_specs=[
            pl.BlockSpec((tm, D_MODEL), lambda i, j: (i, 0)),
            pl.BlockSpec((1, D_MODEL), lambda i, j: (0, 0)),
            wspec(0, u_blk), wspec(1, u_blk + nssm), hold(2), hold(3),
            pl.BlockSpec((SHORTCONV_WIDTH, col), lambda i, j: (0, jnp.minimum(j, nconv - 1))),
            cast_spec(cast0), cast_spec(cast1),
        ],
        out_specs=[
            pl.BlockSpec((tm, col), lambda i, j: (i, jnp.minimum(j, nconv - 1))),
            pl.BlockSpec((col // 128, tm, 128), lambda i, j: (ssm_step(j), i, 0)),
            pl.BlockSpec((tm, col), lambda i, j: (i, ssm_step(j))),
            cast_spec(cast0), cast_spec(cast1),
        ],
        out_shape=[
            jax.ShapeDtypeStruct((seq, D_SHORTCONV), BF16),
            jax.ShapeDtypeStruct((SSM_SLABS, seq, 128), F32),
            jax.ShapeDtypeStruct((seq, D_SSM), BF16),
            jax.ShapeDtypeStruct(cast0.shape, BF16),
            jax.ShapeDtypeStruct(cast1.shape, BF16),
        ],
        scratch_shapes=[
            pltpu.VMEM((tm, D_MODEL), BF16),
            pltpu.VMEM((tm + 8, col), F32),
            pltpu.VMEM((nconv, 8, col), F32),
        ],
        compiler_params=_params(("arbitrary", "arbitrary")),
        name="even_in",
    )(x, g, w_in, w_in, w_in, w_in, conv_w, cast0, cast1)


def _s5_tables(lam_re, lam_im, log_dt, b_re, b_im, c_re, c_im, d_skip):
    hi = lax.Precision.HIGH
    t_, g_, n_, p_ = SSM_CHUNK, SSM_GROUPS, SSM_STATE, SSM_GROUP
    lam_re, lam_im = lam_re.astype(F32), lam_im.astype(F32)
    dt = jnp.exp(log_dt.astype(F32))[:, None]
    xr, xi = lam_re * dt, lam_im * dt
    k = jnp.arange(t_ + 1, dtype=F32)[:, None, None]
    mag = jnp.exp(xr * k)
    pr, pi = mag * jnp.cos(xi * k), mag * jnp.sin(xi * k)
    nr, ni = pr[1] - 1.0, pi[1]
    den = lam_re * lam_re + lam_im * lam_im
    fr = (nr * lam_re + ni * lam_im) / den
    fi = (ni * lam_re - nr * lam_im) / den
    b_re, b_im = b_re.astype(F32), b_im.astype(F32)
    bbr = fr[..., None] * b_re - fi[..., None] * b_im
    bbi = fr[..., None] * b_im + fi[..., None] * b_re
    c_re, c_im = c_re.astype(F32), c_im.astype(F32)

    cpr = c_re[None] * pr[:t_, :, None, :] - c_im[None] * pi[:t_, :, None, :]
    cpi = c_re[None] * pi[:t_, :, None, :] + c_im[None] * pr[:t_, :, None, :]
    kern = jnp.einsum('tgpn,gnq->tgpq', jnp.concatenate([cpr, -cpi], axis=-1),
                      jnp.concatenate([bbr, bbi], axis=1), precision=hi)
    skip = d_skip.astype(F32).reshape(g_, p_)[:, :, None] * jnp.eye(p_, dtype=F32)[None]
    kern = kern.at[0].add(skip)
    flat = t_ * p_
    last = jnp.transpose(kern[::-1], (1, 2, 0, 3)).reshape(g_, p_, flat)
    mt = jnp.stack([jnp.pad(last[:, :, (t_ - 1 - t) * p_:], ((0, 0), (0, 0), (0, (t_ - 1 - t) * p_)))
                    for t in range(t_)], axis=1).reshape(g_, flat, flat)

    prj, pij = pr[t_ - 1::-1][:t_], pi[t_ - 1::-1][:t_]
    wr = prj[..., None] * bbr[None] - pij[..., None] * bbi[None]
    wi = prj[..., None] * bbi[None] + pij[..., None] * bbr[None]
    half = (np.arange(2) == 0)[None, None, :, None, None]

    def pack_w(w):
        w = jnp.transpose(w.reshape(t_, g_ // 2, 2, n_, p_), (1, 0, 2, 4, 3))
        w = jnp.concatenate([jnp.where(half, w, 0.0), jnp.where(half, 0.0, w)], axis=-1)
        return w.reshape(g_ // 2, 2 * flat, 2 * n_)

    vr = c_re[None] * pr[1:, :, None, :] - c_im[None] * pi[1:, :, None, :]
    vi = c_re[None] * pi[1:, :, None, :] + c_im[None] * pr[1:, :, None, :]
    vr = jnp.transpose(vr, (1, 0, 2, 3)).reshape(g_ // 2, 2, flat, n_)
    vi = -jnp.transpose(vi, (1, 0, 2, 3)).reshape(g_ // 2, 2, flat, n_)
    zv = jnp.zeros((g_ // 2, flat, n_), F32)
    v_even = jnp.concatenate([vr[:, 0], zv, vi[:, 0], zv], axis=-1)
    v_odd = jnp.concatenate([zv, vr[:, 1], zv, vi[:, 1]], axis=-1)
    vt = jnp.stack([v_even, v_odd], axis=1).reshape(g_, flat, 4 * n_)
    a_re = pr[t_].reshape(1, g_ * n_)
    a_im = pi[t_].reshape(1, g_ * n_)
    return mt.astype(BF16), pack_w(wr).astype(BF16), pack_w(wi).astype(BF16), vt.astype(BF16), a_re, a_im


_NT = (((1,), (1,)), ((), ()))
_TN = (((0,), (0,)), ((), ()))


def _s5_kernel(u_ref, mt_ref, wre_ref, wim_ref, vt_ref, are_ref, aim_ref, y_ref,
               xt_sc, yt_sc, zre_sc, zim_sc, carry_sc, *, ct):
    i = pl.program_id(0)
    t_ = SSM_CHUNK

    @pl.when(i == 0)
    def _():
        carry_sc[...] = jnp.zeros_like(carry_sc)

    for s in range(SSM_SLABS):
        for j in range(t_):
            xt_sc[s, j] = u_ref[s, pl.ds(j, ct, stride=t_), :].T.astype(BF16)

    pairs_per_slab = 128 // (2 * SSM_GROUP)
    for gp in range(SSM_GROUPS // 2):
        s, k = gp // pairs_per_slab, gp % pairs_per_slab
        xp = xt_sc[s, :, 32 * k:32 * k + 32, :].reshape(2 * SSM_FLAT, ct)
        zre_sc[:, gp * 128:(gp + 1) * 128] = lax.dot_general(xp, wre_ref[gp], _TN, preferred_element_type=F32)
        zim_sc[:, gp * 128:(gp + 1) * 128] = lax.dot_general(xp, wim_ref[gp], _TN, preferred_element_type=F32)

    are = are_ref[...]
    aim = aim_ref[...]

    def step(r, carry):
        sre, sim = carry
        zr = zre_sc[pl.ds(r, 1), :]
        zi = zim_sc[pl.ds(r, 1), :]
        zre_sc[pl.ds(r, 1), :] = sre
        zim_sc[pl.ds(r, 1), :] = sim
        return are * sre - aim * sim + zr, are * sim + aim * sre + zi

    sre, sim = lax.fori_loop(0, ct, step, (carry_sc[0:1, :], carry_sc[1:2, :]), unroll=8)
    carry_sc[0:1, :] = sre
    carry_sc[1:2, :] = sim

    groups_per_slab = 128 // SSM_GROUP
    for g in range(SSM_GROUPS):
        gp, s, k = g // 2, g // groups_per_slab, g % groups_per_slab
        xg = xt_sc[s, :, 16 * k:16 * k + 16, :].reshape(SSM_FLAT, ct)
        sp = jnp.concatenate([zre_sc[:, gp * 128:(gp + 1) * 128],
                              zim_sc[:, gp * 128:(gp + 1) * 128]], axis=1).astype(BF16)
        yt = _dot(mt_ref[g], xg) + lax.dot_general(vt_ref[g], sp, _NT, preferred_element_type=F32)
        yt_sc[s, :, 16 * k:16 * k + 16, :] = yt.reshape(t_, SSM_GROUP, ct)

    for s in range(SSM_SLABS):
        for t in range(t_):
            y_ref[s, pl.ds(t, ct, stride=t_), :] = yt_sc[s, t].T


def _s5(u4, tables, ct):
    mt, wre, wim, vt, are, aim = tables
    _, seq, _ = u4.shape
    full = lambda a: pl.BlockSpec(a.shape, lambda i: (0,) * a.ndim)
    rows = pl.BlockSpec((SSM_SLABS, SSM_CHUNK * ct, 128), lambda i: (0, i, 0))
    nstate = SSM_GROUPS * SSM_STATE
    return pl.pallas_call(
        functools.partial(_s5_kernel, ct=ct),
        grid=(seq // (SSM_CHUNK * ct),),
        in_specs=[rows, full(mt), full(wre), full(wim), full(vt), full(are), full(aim)],
        out_specs=rows,
        out_shape=jax.ShapeDtypeStruct(u4.shape, F32),
        scratch_shapes=[pltpu.VMEM((SSM_SLABS, SSM_CHUNK, 128, ct), BF16),
                        pltpu.VMEM((SSM_SLABS, SSM_CHUNK, 128, ct), F32),
                        pltpu.VMEM((ct, nstate), F32),
                        pltpu.VMEM((ct, nstate), F32),
                        pltpu.VMEM((8, nstate), F32)],
        compiler_params=_params(("arbitrary",)),
        name="s5_ssm",
    )(u4, mt, wre, wim, vt, are, aim)


OUT_ROWS = 256


def _even_out_kernel(x_ref, ya_ref, ys_ref, szb_ref, wglu_ref, wout_ref, g_ref, c0_ref,
                     h_ref, hn_ref, c0b_ref, *, tm):
    c0b_ref[...] = c0_ref[...].astype(BF16)
    for r0 in range(0, tm, OUT_ROWS):
        rows = slice(r0, r0 + OUT_ROWS)
        gl = _gelu_tanh(jnp.concatenate([ys_ref[s, rows, :] for s in range(SSM_SLABS)], axis=1))
        yb = gl * _sigmoid(_dot(gl.astype(BF16), wglu_ref[...])) * szb_ref[rows, :].astype(F32)
        acc = _dot(ya_ref[rows, :], wout_ref[0:D_SHORTCONV, :])
        acc = acc + _dot(yb.astype(BF16), wout_ref[D_SHORTCONV:D_SHORTCONV + D_SSM, :])
        h = x_ref[rows, :] + acc
        h_ref[rows, :] = h
        hn_ref[rows, :] = _rms(h, g_ref[...]).astype(BF16)


def _even_out(x, ya, ys, szb, w_glu, w_out, g, cast0, tm):
    seq = x.shape[0]
    cast_spec = _cast_slice_spec(cast0, seq // tm, lambda i: i)
    row = lambda w: pl.BlockSpec((tm, w), lambda i: (i, 0))
    const = lambda a: pl.BlockSpec(a.shape, lambda i: (0, 0), pipeline_mode=pl.Buffered(1))
    return pl.pallas_call(
        functools.partial(_even_out_kernel, tm=tm),
        grid=(seq // tm,),
        in_specs=[row(D_MODEL), row(D_SHORTCONV),
                  pl.BlockSpec((SSM_SLABS, tm, 128), lambda i: (0, i, 0)), row(D_SSM),
                  const(w_glu), const(w_out), const(g), cast_spec],
        out_specs=[row(D_MODEL), row(D_MODEL), cast_spec],
        out_shape=[jax.ShapeDtypeStruct((seq, D_MODEL), F32),
                   jax.ShapeDtypeStruct((seq, D_MODEL), BF16),
                   jax.ShapeDtypeStruct(cast0.shape, BF16)],
        compiler_params=_params(("arbitrary",)),
        name="even_out",
    )(x, ya, ys, szb, w_glu, w_out, g, cast0)


ODD_GROUPS = 6


def _odd_in_kernel(hn_ref, wa_ref, wb_ref, o_ref, qkv_ref, *, tm, col):
    s = pl.program_id(1)
    nb = D_CONFORMER // col
    group = s // nb
    is_silu = (group == 1) | (group == 5)
    blocks = [slice(r0, r0 + IN_ROWS) for r0 in range(0, tm, IN_ROWS)]

    @pl.when(group == 0)
    def _():
        for rows in blocks:
            hn = hn_ref[rows, :]
            o_ref[rows, :] = (_dot(hn, wa_ref[...]) * _sigmoid(_dot(hn, wb_ref[...]))).astype(BF16)

    @pl.when(is_silu)
    def _():
        for rows in blocks:
            o_ref[rows, :] = _silu(_dot(hn_ref[rows, :], wa_ref[...])).astype(BF16)

    @pl.when((group > 0) & jnp.logical_not(is_silu))
    def _():
        scale = jnp.where(group == 2, Q_SCALE, 1.0).astype(F32)
        for rows in blocks:
            qkv_ref[rows, :] = _dot(hn_ref[rows, :], wa_ref[...]) * scale


def _odd_in(hn, w_in, tm, col):
    seq = hn.shape[0]
    nb = D_CONFORMER // col
    return pl.pallas_call(
        functools.partial(_odd_in_kernel, tm=tm, col=col),
        grid=(seq // tm, ODD_GROUPS * nb),
        in_specs=[pl.BlockSpec((tm, D_MODEL), lambda i, s: (i, 0)),
                  pl.BlockSpec((D_MODEL, col), lambda i, s: (0, jnp.where(s < nb, s, s + nb))),
                  pl.BlockSpec((D_MODEL, col), lambda i, s: (0, jnp.where(s < nb, s + nb, 2 * nb - 1)))],
        out_specs=[pl.BlockSpec((tm, col), lambda i, s: (i, jnp.where(s < 2 * nb, s, jnp.where(
                       s < 5 * nb, 2 * nb - 1, s - 3 * nb)))),
                   pl.BlockSpec((tm, col), lambda i, s: (i, jnp.clip(s - 2 * nb, 0, 3 * nb - 1)))],
        out_shape=[jax.ShapeDtypeStruct((seq, 3 * D_CONFORMER), BF16),
                   jax.ShapeDtypeStruct((seq, 3 * D_ATTN), F32)],
        compiler_params=_params(("arbitrary", "arbitrary")),
        name="odd_in",
    )(hn, w_in, w_in)


CONV_ROWS = 64


def _conformer_kernel(cur_ref, halo_ref, szc_ref, cw_ref, cb_ref, lg_ref, lb_ref, wpw_ref,
                      o_ref, buf_sc, conv_sc, *, tm):
    i = pl.program_id(0)
    nslab = D_CONFORMER // 128
    for c in range(nslab):
        lanes = slice(c * 128, (c + 1) * 128)
        buf_sc[c, 0:CONV_HALO, :] = jnp.where(i == 0, 0.0, halo_ref[:, lanes].astype(F32))
        buf_sc[c, CONV_HALO:CONV_HALO + tm, :] = cur_ref[:, lanes].astype(F32)
    shift = CONV_HALO - (CONFORMER_WIDTH - 1)

    def rows(rc, _):
        base = pl.multiple_of(rc * CONV_ROWS, CONV_ROWS)
        for c in range(nslab):
            lanes = slice(c * 128, (c + 1) * 128)
            acc = jnp.zeros((CONV_ROWS, 128), F32) + cb_ref[:, lanes]
            for k in range(CONFORMER_WIDTH):
                acc = acc + cw_ref[k:k + 1, lanes] * buf_sc[c, pl.ds(base + (shift + k), CONV_ROWS), :]
            conv_sc[pl.ds(base, CONV_ROWS), lanes] = acc
        return 0

    lax.fori_loop(0, tm // CONV_ROWS, rows, 0)
    for r0 in range(0, tm, OUT_ROWS):
        sub = slice(r0, r0 + OUT_ROWS)
        y = conv_sc[sub, :]
        mu = jnp.mean(y, axis=-1, keepdims=True)
        yc = y - mu
        var = jnp.mean(yc * yc, axis=-1, keepdims=True)
        ln = yc * lax.rsqrt(var + NORM_EPS) * lg_ref[...] + lb_ref[...]
        o_ref[sub, :] = (_dot(_silu(ln).astype(BF16), wpw_ref[...]) * szc_ref[sub, :].astype(F32)).astype(BF16)


def _conformer(proj, conv_w, conv_b, ln_g, ln_b, w_pw, tm):
    seq = proj.shape[0]
    per = tm // CONV_HALO
    const = lambda a: pl.BlockSpec(a.shape, lambda i: (0, 0))
    return pl.pallas_call(
        functools.partial(_conformer_kernel, tm=tm),
        grid=(seq // tm,),
        in_specs=[pl.BlockSpec((tm, D_CONFORMER), lambda i: (i, 0)),
                  pl.BlockSpec((CONV_HALO, D_CONFORMER), lambda i: (jnp.maximum(i * per - 1, 0), 0)),
                  pl.BlockSpec((tm, D_CONFORMER), lambda i: (i, 1)),
                  const(conv_w), const(conv_b), const(ln_g), const(ln_b), const(w_pw)],
        out_specs=pl.BlockSpec((tm, D_CONFORMER), lambda i: (i, 0)),
        out_shape=jax.ShapeDtypeStruct((seq, D_CONFORMER), BF16),
        scratch_shapes=[pltpu.VMEM((D_CONFORMER // 128, tm + CONV_HALO, 128), F32),
                        pltpu.VMEM((tm, D_CONFORMER), F32)],
        compiler_params=_params(("arbitrary",)),
        name="conformer",
    )(proj, proj, proj, conv_w, conv_b, ln_g, ln_b, w_pw)


def _t5_bucket(dist):
    max_exact = REL_BUCKETS // 2
    d_f = jnp.maximum(dist, 1).astype(F32)
    large = max_exact + (jnp.log(d_f / max_exact) / math.log(REL_MAX_DISTANCE / max_exact)
                         * (REL_BUCKETS - max_exact)).astype(jnp.int32)
    large = jnp.minimum(large, REL_BUCKETS - 1)
    return jnp.where(dist < max_exact, dist, large)


def _attn_bias(rel_bias):
    q = ATTN_BLOCK
    nh = rel_bias.shape[1]
    kj = np.arange(2 * q)[None, None, :]
    neg = lambda n: jnp.full((nh, n), NEG_INF, F32)
    tabs = []
    for window, dilation in DILATED_PATTERNS:
        steps = window // dilation
        assert steps == q
        rel = jnp.asarray(np.arange(steps + 1) * dilation, dtype=jnp.int32)
        by_rel = rel_bias.astype(F32)[_t5_bucket(rel)].T
        g = jnp.concatenate([by_rel[:, ::-1], neg(2 * q - 1)], axis=1)
        flat = jnp.tile(g, (1, q))[:, :q * (3 * q - 1)]
        tab = flat.reshape(nh, q, 3 * q - 1)[:, :, :2 * q]
        if dilation == 1:
            tab = tab.reshape(nh, q // MERGE_DIL, MERGE_DIL, 2 * q).transpose(0, 2, 1, 3).reshape(nh, q, 2 * q)
        tab = tab * LOG2E
        tabs.append(tab)
        tabs.append(jnp.where(kj >= q, tab, NEG_INF))
    return jnp.stack(tabs, axis=1)


def _attn_kernel(q_ref, k_ref, v_ref, szd_ref, bias_ref, o_ref,
                 onat, qm, km, vm, qs1, qs2, ks0, ks1, ks2, vs0, vs1, vs2,
                 o0, o1, o2, l0, l1, l2):
    i = pl.program_id(1)
    t = ATTN_TILE
    blk = ATTN_BLOCK
    first_tile = i == 0
    qf, kf, vf = q_ref, k_ref, v_ref
    qstreams = (None, qs1, qs2)
    kstreams = (ks0, ks1, ks2)
    vstreams = (vs0, vs1, vs2)
    md = MERGE_DIL
    mspan = t // md

    def stream(src_f, src_m, dil, r):
        n = t // dil
        if dil == md:
            x = src_f[pl.ds(r, n, stride=dil), :]
            src_m[r * n:(r + 1) * n, :] = x
            return x
        return src_m[pl.ds((r % md) * mspan + r // md, n, stride=dil // md), :]

    for pidx, (_, dil) in enumerate(DILATED_PATTERNS):
        n = t // dil
        per = blk + n
        for r in range(dil):
            base = r * per
            for src_f, src_m, dst in ((kf, km, kstreams[pidx]), (vf, vm, vstreams[pidx])):
                tail = dst[base + n:base + per, :]
                dst[base:base + blk, :] = jnp.where(first_tile, jnp.zeros_like(tail), tail)
                if dil == 1:
                    dst[base + blk:base + per, :] = src_f[...].astype(BF16)
                else:
                    dst[base + blk:base + per, :] = stream(src_f, src_m, dil, r).astype(BF16)
            if dil > 1:
                qstreams[pidx][r * n:(r + 1) * n, :] = stream(qf, qm, dil, r).astype(BF16)

    ones = jnp.ones((2 * blk, HEAD_DIM), BF16)
    first = first_tile.astype(jnp.int32)

    def block(pidx, dil, r, sb, out_sc, lse_sc):
        n = t // dil
        per = blk + n
        if dil == 1:
            qb = jnp.concatenate([qf[pl.ds(sb * blk + c, blk // md, stride=md), :] for c in range(md)],
                                 axis=0).astype(BF16)
        else:
            qb = qstreams[pidx][r * n + sb * blk:r * n + (sb + 1) * blk, :]
        krows = slice(r * per + sb * blk, r * per + (sb + 2) * blk)
        kb = kstreams[pidx][krows, :]
        vb = vstreams[pidx][krows, :]
        s = lax.dot_general(qb, kb, _NT, preferred_element_type=F32)
        s = s + bias_ref[0, 2 * pidx + first if sb == 0 else 2 * pidx]
        mx = jnp.max(s, axis=-1, keepdims=True)
        p = jnp.exp2(s - mx).astype(BF16)
        pv = _dot(p, jnp.concatenate([vb, ones], axis=1))
        den = pv[:, HEAD_DIM:]
        out = pv[:, :HEAD_DIM] / den
        lse = mx + jnp.log2(den)
        if dil == 1:
            piece = blk // md
            for c in range(md):
                rows = slice(c * mspan + sb * piece, c * mspan + (sb + 1) * piece)
                out_sc[rows, :] = out[c * piece:(c + 1) * piece, :]
                lse_sc[rows, :] = lse[c * piece:(c + 1) * piece, :]
        elif dil == md:
            rows = slice(r * mspan + sb * blk, r * mspan + (sb + 1) * blk)
            out_sc[rows, :] = out
            lse_sc[rows, :] = lse
        else:
            step = dil // md
            rows = pl.ds((r % md) * mspan + sb * blk * step + r // md, blk, stride=step)
            out_sc[rows, :] = out
            lse_sc[rows, :] = lse

    scratch = ((o0, l0), (o1, l1), (o2, l2))
    for pidx, (_, dil) in enumerate(DILATED_PATTERNS):
        for sb in range(t // (blk * dil)):
            for r in range(dil):
                block(pidx, dil, r, sb, *scratch[pidx])

    la, lb, lc = l0[...], l1[...], l2[...]
    lall = jnp.maximum(jnp.maximum(la, lb), lc)
    wa, wb, wc = jnp.exp2(la - lall), jnp.exp2(lb - lall), jnp.exp2(lc - lall)
    merged = (wa * o0[...] + wb * o1[...] + wc * o2[...]) / (wa + wb + wc)
    for c in range(md):
        onat[pl.ds(c, mspan, stride=md), :] = merged[c * mspan:(c + 1) * mspan, :]
    o_ref[...] = (onat[...] * szd_ref[...].astype(F32)).astype(BF16)


def _attention(qkv, proj, bias):
    seq = proj.shape[0]
    t = ATTN_TILE
    q0, k0, v0 = 0, N_HEADS, 2 * N_HEADS
    z0 = (2 * D_CONFORMER) // HEAD_DIM
    cur = lambda c0: pl.BlockSpec((t, HEAD_DIM), lambda h, i: (i, c0 + h))
    f32 = lambda rows: pltpu.VMEM((rows, HEAD_DIM), F32)
    bf16 = lambda rows: pltpu.VMEM((rows, HEAD_DIM), BF16)
    streams = [bf16(dil * (ATTN_BLOCK + t // dil)) for _, dil in DILATED_PATTERNS]
    return pl.pallas_call(
        _attn_kernel,
        grid=(N_HEADS, seq // t),
        in_specs=[cur(q0), cur(k0), cur(v0), cur(z0),
                  pl.BlockSpec((1,) + bias.shape[1:], lambda h, i: (h, 0, 0, 0))],
        out_specs=pl.BlockSpec((t, HEAD_DIM), lambda h, i: (i, h)),
        out_shape=jax.ShapeDtypeStruct((seq, D_ATTN), BF16),
        scratch_shapes=[f32(t)] * 4 + [bf16(t)] * 2 + streams + streams + [f32(t)] * 6,
        compiler_params=_params(("arbitrary", "arbitrary")),
        name="dilated_attn",
    )(qkv, qkv, qkv, proj, bias)


def _odd_out_kernel(h_ref, yc_ref, yd_ref, wout_ref, g_ref, o_ref, *, tm):
    for r0 in range(0, tm, OUT_ROWS):
        rows = slice(r0, r0 + OUT_ROWS)
        acc = _dot(yc_ref[rows, :], wout_ref[0:D_CONFORMER, :])
        acc = acc + _dot(yd_ref[rows, :], wout_ref[D_CONFORMER:D_CONFORMER + D_ATTN, :])
        o_ref[rows, :] = _rms(h_ref[rows, :] + acc, g_ref[...])


def _odd_out(h, yc, yd, w_out, g, tm):
    seq = h.shape[0]
    row = lambda w: pl.BlockSpec((tm, w), lambda i: (i, 0))
    const = lambda a: pl.BlockSpec(a.shape, lambda i: (0, 0), pipeline_mode=pl.Buffered(1))
    return pl.pallas_call(
        functools.partial(_odd_out_kernel, tm=tm),
        grid=(seq // tm,),
        in_specs=[row(D_MODEL), row(D_CONFORMER), row(D_ATTN), const(w_out), const(g)],
        out_specs=row(D_MODEL),
        out_shape=jax.ShapeDtypeStruct((seq, D_MODEL), F32),
        compiler_params=_params(("arbitrary",)),
        name="odd_out",
    )(h, yc, yd, w_out, g)


def _row_tile(seq, want):
    return min(seq, want)


def kernel(x, norm_g, final_norm_g, ev_w_in, ev_conv_w, s5_lam_re, s5_lam_im, s5_log_dt, s5_b_re, s5_b_im, s5_c_re, s5_c_im, s5_d, s5_w_glu, ev_w_out, od_w_in, cf_conv_w, cf_conv_b, cf_ln_g, cf_ln_b, cf_w_pw, od_w_out, rel_bias):
    bsz, seq, _ = x.shape
    assert bsz == 1 and seq % ATTN_TILE == 0
    x2 = x.reshape(seq, D_MODEL)
    g0 = norm_g[0].astype(F32).reshape(1, D_MODEL)
    g1 = norm_g[1].astype(F32).reshape(1, D_MODEL)
    gf = final_norm_g.astype(F32).reshape(1, D_MODEL)

    ya, u, szb, ev_w_out_b, od_w_out_b = _even_in(
        x2, g0, ev_w_in[0].astype(BF16), ev_conv_w[0].astype(F32), ev_w_out[0].astype(F32),
        od_w_out[0].astype(F32), _row_tile(seq, EVEN_IN_TILE[0]), EVEN_IN_TILE[1])
    tables = _s5_tables(s5_lam_re[0], s5_lam_im[0], s5_log_dt[0], s5_b_re[0], s5_b_im[0],
                        s5_c_re[0], s5_c_im[0], s5_d[0])
    ys = _s5(u, tables, min(seq // SSM_CHUNK, SSM_CHUNK_ROWS))
    h1, hn1, od_w_in_b = _even_out(x2, ya, ys, szb, s5_w_glu[0].astype(BF16), ev_w_out_b, g1,
                                   od_w_in[0].astype(F32), _row_tile(seq, EVEN_OUT_ROWS))

    proj, qkv = _odd_in(hn1, od_w_in_b, _row_tile(seq, ODD_IN_TILE[0]), ODD_IN_TILE[1])
    conv_w = jnp.concatenate([cf_conv_w[0].astype(F32), jnp.zeros((1, D_CONFORMER), F32)], axis=0)
    yc = _conformer(proj, conv_w, cf_conv_b[0].astype(F32).reshape(1, -1),
                    cf_ln_g[0].astype(F32).reshape(1, -1), cf_ln_b[0].astype(F32).reshape(1, -1),
                    cf_w_pw[0].astype(BF16), _row_tile(seq, CONFORMER_ROWS))
    yd = _attention(qkv, proj, _attn_bias(rel_bias))
    out = _odd_out(h1, yc, yd, od_w_out_b, gf, _row_tile(seq, ODD_OUT_ROWS))
    return out.reshape(bsz, seq, D_MODEL)
```

```python
import functools
import math

import numpy as np
import jax
import jax.numpy as jnp
from jax import lax
from jax.experimental import pallas as pl
from jax.experimental.pallas import tpu as pltpu

F32 = jnp.float32
BF16 = jnp.bfloat16

D_MODEL = 2048
D_SHORTCONV = 1536
SHORTCONV_WIDTH = 3
D_SSM = 512
SSM_GROUP = 16
SSM_GROUPS = 32
SSM_STATE = 64
D_CONFORMER = 1024
CONFORMER_WIDTH = 31
D_ATTN = 1024
HEAD_DIM = 128
N_HEADS = 8
DILATED_PATTERNS = ((128, 1), (512, 4), (2048, 16))
ATTN_BLOCK = 128
REL_BUCKETS = 32
REL_MAX_DISTANCE = 2048
NORM_EPS = 1e-6
NEG_INF = -1e30

IN_ROWS = 512
EVEN_ROWS = 256
SSM_CHUNK = 16
SSM_FLAT = SSM_CHUNK * SSM_GROUP
SSM_SLABS = D_SSM // 128
ATTN_TILE = 2048
LOG2E = math.log2(math.e)
Q_SCALE = HEAD_DIM ** -0.5 * LOG2E
MERGE_DIL = 4
CONV_HALO = 32
VMEM_LIMIT = 50 * 1024 * 1024
EVEN_IN_VMEM = 54 * 1024 * 1024

EVEN_IN_TILE = (1024, 512)
EVEN_OUT_ROWS = 512
ODD_IN_TILE = (2048, 512)
CONFORMER_ROWS = 1024
ODD_OUT_ROWS = 512
SSM_CHUNK_ROWS = 128


def _params(sem, vmem=VMEM_LIMIT):
    return pltpu.CompilerParams(dimension_semantics=sem, vmem_limit_bytes=vmem)


def _dot(a, b):
    return jnp.dot(a, b, preferred_element_type=F32)


def _sigmoid(z):
    return 1.0 / (1.0 + jnp.exp(-z))


def _silu(z):
    return z * _sigmoid(z)


def _gelu_tanh(x):
    return 0.5 * x * (1.0 + jnp.tanh(math.sqrt(2.0 / math.pi) * (x + 0.044715 * (x * x * x))))


def _cast_slice_spec(a, nsteps, step_of):
    rows = a.shape[0] // nsteps
    assert rows * nsteps == a.shape[0] and rows % 16 == 0, (a.shape, nsteps)
    return pl.BlockSpec((rows, a.shape[1]), lambda *idx: (step_of(*idx), 0))


def _rms(xf, g):
    ms = jnp.mean(xf * xf, axis=-1, keepdims=True)
    return xf * lax.rsqrt(ms + NORM_EPS) * g


def _even_in_kernel(x_ref, g_ref, wa_ref, wb_ref, wc_ref, wd_ref, cw_ref, c0_ref, c1_ref, c2_ref,
                    ya_ref, u_ref, szb_ref, c0b_ref, c1b_ref, c2b_ref, hn_sc, pbuf_sc, carry_sc, *, tm, col):
    i = pl.program_id(0)
    j = pl.program_id(1)
    nconv = D_SHORTCONV // col
    c0b_ref[...] = c0_ref[...].astype(BF16)
    c1b_ref[...] = c1_ref[...].astype(BF16)
    c2b_ref[...] = c2_ref[...].astype(BF16)

    def conv_step(first):
        @pl.when(i == 0)
        def _():
            carry_sc[j] = jnp.zeros((8, col), F32)

        pbuf_sc[0:8, :] = carry_sc[j]
        cw = cw_ref[...]
        for r0 in range(0, tm, EVEN_ROWS):
            rows = slice(r0, r0 + EVEN_ROWS)
            if first:
                hn = _rms(x_ref[rows, :], g_ref[...]).astype(BF16)
                hn_sc[rows, :] = hn
            else:
                hn = hn_sc[rows, :]
            p = _dot(hn, wb_ref[...]) * _dot(hn, wa_ref[...])
            pbuf_sc[8 + r0:8 + r0 + EVEN_ROWS, :] = p
            conv = (cw[2:3, :] * p + cw[1:2, :] * pbuf_sc[pl.ds(7 + r0, EVEN_ROWS), :]
                    + cw[0:1, :] * pbuf_sc[pl.ds(6 + r0, EVEN_ROWS), :])
            gb = _dot(hn, wc_ref[...])
            za = _dot(hn, wd_ref[...])
            ya_ref[rows, :] = (gb * conv * _silu(za)).astype(BF16)
        carry_sc[j] = pbuf_sc[tm:tm + 8, :]

    pl.when(j == 0)(functools.partial(conv_step, True))
    pl.when((j > 0) & (j < nconv))(functools.partial(conv_step, False))

    @pl.when(j >= nconv)
    def _():
        for r0 in range(0, tm, EVEN_ROWS):
            rows = slice(r0, r0 + EVEN_ROWS)
            hn = hn_sc[rows, :]
            u = _dot(hn, wa_ref[...])
            for s in range(col // 128):
                u_ref[s, rows, :] = u[:, s * 128:(s + 1) * 128]
            szb_ref[rows, :] = _silu(_dot(hn, wb_ref[...])).astype(BF16)


def _even_in(x, g, w_in, conv_w, casts, tm, col):
    seq = x.shape[0]
    nconv = D_SHORTCONV // col
    nssm = D_SSM // col
    u_blk = 4 * nconv

    def wspec(k, alt0):
        return pl.BlockSpec((D_MODEL, col),
                            lambda i, j: (0, jnp.where(j < nconv, j + nconv * k, alt0 + j - nconv)))

    hold = lambda k: pl.BlockSpec((D_MODEL, col), lambda i, j: (0, jnp.minimum(j, nconv - 1) + nconv * k))
    ssm_step = lambda j: jnp.maximum(j - nconv, 0)
    nj = nconv + nssm
    cast_spec = lambda a: _cast_slice_spec(a, (seq // tm) * nj, lambda i, j: i * nj + j)
    return pl.pallas_call(
        functools.partial(_even_in_kernel, tm=tm, col=col),
        grid=(seq // tm, nj),
        in_specs=[
            pl.BlockSpec((tm, D_MODEL), lambda i, j: (i, 0)),
            pl.BlockSpec((1, D_MODEL), lambda i, j: (0, 0)),
            wspec(0, u_blk), wspec(1, u_blk + nssm), hold(2), hold(3),
            pl.BlockSpec((SHORTCONV_WIDTH, col), lambda i, j: (0, jnp.minimum(j, nconv - 1))),
            *[cast_spec(a) for a in casts],
        ],
        out_specs=[
            pl.BlockSpec((tm, col), lambda i, j: (i, jnp.minimum(j, nconv - 1))),
            pl.BlockSpec((col // 128, tm, 128), lambda i, j: (ssm_step(j), i, 0)),
            pl.BlockSpec((tm, col), lambda i, j: (i, ssm_step(j))),
            *[cast_spec(a) for a in casts],
        ],
        out_shape=[
            jax.ShapeDtypeStruct((seq, D_SHORTCONV), BF16),
            jax.ShapeDtypeStruct((SSM_SLABS, seq, 128), F32),
            jax.ShapeDtypeStruct((seq, D_SSM), BF16),
            *[jax.ShapeDtypeStruct(a.shape, BF16) for a in casts],
        ],
        scratch_shapes=[
            pltpu.VMEM((tm, D_MODEL), BF16),
            pltpu.VMEM((tm + 8, col), F32),
            pltpu.VMEM((nconv, 8, col), F32),
        ],
        compiler_params=_params(("arbitrary", "arbitrary"), EVEN_IN_VMEM),
        name="even_in",
    )(x, g, w_in, w_in, w_in, w_in, conv_w, *casts)


def _s5_tables(lam_re, lam_im, log_dt, b_re, b_im, c_re, c_im, d_skip):
    hi = lax.Precision.HIGH
    t_, g_, n_, p_ = SSM_CHUNK, SSM_GROUPS, SSM_STATE, SSM_GROUP
    lam_re, lam_im = lam_re.astype(F32), lam_im.astype(F32)
    dt = jnp.exp(log_dt.astype(F32))[:, None]
    xr, xi = lam_re * dt, lam_im * dt
    k = jnp.arange(t_ + 1, dtype=F32)[:, None, None]
    mag = jnp.exp(xr * k)
    pr, pi = mag * jnp.cos(xi * k), mag * jnp.sin(xi * k)
    nr, ni = pr[1] - 1.0, pi[1]
    den = lam_re * lam_re + lam_im * lam_im
    fr = (nr * lam_re + ni * lam_im) / den
    fi = (ni * lam_re - nr * lam_im) / den
    b_re, b_im = b_re.astype(F32), b_im.astype(F32)
    bbr = fr[..., None] * b_re - fi[..., None] * b_im
    bbi = fr[..., None] * b_im + fi[..., None] * b_re
    c_re, c_im = c_re.astype(F32), c_im.astype(F32)

    cpr = c_re[None] * pr[:t_, :, None, :] - c_im[None] * pi[:t_, :, None, :]
    cpi = c_re[None] * pi[:t_, :, None, :] + c_im[None] * pr[:t_, :, None, :]
    kern = jnp.einsum('tgpn,gnq->tgpq', jnp.concatenate([cpr, -cpi], axis=-1),
                      jnp.concatenate([bbr, bbi], axis=1), precision=hi)
    skip = d_skip.astype(F32).reshape(g_, p_)[:, :, None] * jnp.eye(p_, dtype=F32)[None]
    kern = kern.at[0].add(skip)
    flat = t_ * p_
    last = jnp.transpose(kern[::-1], (1, 2, 0, 3)).reshape(g_, p_, flat)
    mt = jnp.stack([jnp.pad(last[:, :, (t_ - 1 - t) * p_:], ((0, 0), (0, 0), (0, (t_ - 1 - t) * p_)))
                    for t in range(t_)], axis=1).reshape(g_, flat, flat)

    prj, pij = pr[t_ - 1::-1][:t_], pi[t_ - 1::-1][:t_]
    wr = prj[..., None] * bbr[None] - pij[..., None] * bbi[None]
    wi = prj[..., None] * bbi[None] + pij[..., None] * bbr[None]
    half = (np.arange(2) == 0)[None, None, :, None, None]

    def pack_w(w):
        w = jnp.transpose(w.reshape(t_, g_ // 2, 2, n_, p_), (1, 0, 2, 4, 3))
        w = jnp.concatenate([jnp.where(half, w, 0.0), jnp.where(half, 0.0, w)], axis=-1)
        return w.reshape(g_ // 2, 2 * flat, 2 * n_)

    vr = c_re[None] * pr[1:, :, None, :] - c_im[None] * pi[1:, :, None, :]
    vi = c_re[None] * pi[1:, :, None, :] + c_im[None] * pr[1:, :, None, :]
    vr = jnp.transpose(vr, (1, 0, 2, 3)).reshape(g_ // 2, 2, flat, n_)
    vi = -jnp.transpose(vi, (1, 0, 2, 3)).reshape(g_ // 2, 2, flat, n_)
    zv = jnp.zeros((g_ // 2, flat, n_), F32)
    v_even = jnp.concatenate([vr[:, 0], zv, vi[:, 0], zv], axis=-1)
    v_odd = jnp.concatenate([zv, vr[:, 1], zv, vi[:, 1]], axis=-1)
    vt = jnp.stack([v_even, v_odd], axis=1).reshape(g_, flat, 4 * n_)
    a_re = pr[t_].reshape(1, g_ * n_)
    a_im = pi[t_].reshape(1, g_ * n_)
    return mt.astype(BF16), pack_w(wr).astype(BF16), pack_w(wi).astype(BF16), vt.astype(BF16), a_re, a_im


_NT = (((1,), (1,)), ((), ()))
_TN = (((0,), (0,)), ((), ()))


def _s5_kernel(u_ref, mt_ref, wre_ref, wim_ref, vt_ref, are_ref, aim_ref, y_ref,
               xt_sc, yt_sc, zre_sc, zim_sc, carry_sc, *, ct):
    i = pl.program_id(0)
    t_ = SSM_CHUNK

    @pl.when(i == 0)
    def _():
        carry_sc[...] = jnp.zeros_like(carry_sc)

    for s in range(SSM_SLABS):
        for j in range(t_):
            xt_sc[s, j] = u_ref[s, pl.ds(j, ct, stride=t_), :].T.astype(BF16)

    pairs_per_slab = 128 // (2 * SSM_GROUP)
    for gp in range(SSM_GROUPS // 2):
        s, k = gp // pairs_per_slab, gp % pairs_per_slab
        xp = xt_sc[s, :, 32 * k:32 * k + 32, :].reshape(2 * SSM_FLAT, ct)
        zre_sc[:, gp * 128:(gp + 1) * 128] = lax.dot_general(xp, wre_ref[gp], _TN, preferred_element_type=F32)
        zim_sc[:, gp * 128:(gp + 1) * 128] = lax.dot_general(xp, wim_ref[gp], _TN, preferred_element_type=F32)

    are = are_ref[...]
    aim = aim_ref[...]

    def step(r, carry):
        sre, sim = carry
        zr = zre_sc[pl.ds(r, 1), :]
        zi = zim_sc[pl.ds(r, 1), :]
        zre_sc[pl.ds(r, 1), :] = sre
        zim_sc[pl.ds(r, 1), :] = sim
        return are * sre - aim * sim + zr, are * sim + aim * sre + zi

    sre, sim = lax.fori_loop(0, ct, step, (carry_sc[0:1, :], carry_sc[1:2, :]), unroll=8)
    carry_sc[0:1, :] = sre
    carry_sc[1:2, :] = sim

    groups_per_slab = 128 // SSM_GROUP
    for g in range(SSM_GROUPS):
        gp, s, k = g // 2, g // groups_per_slab, g % groups_per_slab
        xg = xt_sc[s, :, 16 * k:16 * k + 16, :].reshape(SSM_FLAT, ct)
        sp = jnp.concatenate([zre_sc[:, gp * 128:(gp + 1) * 128],
                              zim_sc[:, gp * 128:(gp + 1) * 128]], axis=1).astype(BF16)
        yt = _dot(mt_ref[g], xg) + lax.dot_general(vt_ref[g], sp, _NT, preferred_element_type=F32)
        yt_sc[s, :, 16 * k:16 * k + 16, :] = yt.reshape(t_, SSM_GROUP, ct)

    for s in range(SSM_SLABS):
        for t in range(t_):
            y_ref[s, pl.ds(t, ct, stride=t_), :] = yt_sc[s, t].T


def _s5(u4, tables, ct):
    mt, wre, wim, vt, are, aim = tables
    _, seq, _ = u4.shape
    full = lambda a: pl.BlockSpec(a.shape, lambda i: (0,) * a.ndim)
    rows = pl.BlockSpec((SSM_SLABS, SSM_CHUNK * ct, 128), lambda i: (0, i, 0))
    nstate = SSM_GROUPS * SSM_STATE
    return pl.pallas_call(
        functools.partial(_s5_kernel, ct=ct),
        grid=(seq // (SSM_CHUNK * ct),),
        in_specs=[rows, full(mt), full(wre), full(wim), full(vt), full(are), full(aim)],
        out_specs=rows,
        out_shape=jax.ShapeDtypeStruct(u4.shape, F32),
        scratch_shapes=[pltpu.VMEM((SSM_SLABS, SSM_CHUNK, 128, ct), BF16),
                        pltpu.VMEM((SSM_SLABS, SSM_CHUNK, 128, ct), F32),
                        pltpu.VMEM((ct, nstate), F32),
                        pltpu.VMEM((ct, nstate), F32),
                        pltpu.VMEM((8, nstate), F32)],
        compiler_params=_params(("arbitrary",)),
        name="s5_ssm",
    )(u4, mt, wre, wim, vt, are, aim)


OUT_ROWS = 256


def _even_out_kernel(x_ref, ya_ref, ys_ref, szb_ref, wglu_ref, wout_ref, g_ref, h_ref, hn_ref, *, tm):
    for r0 in range(0, tm, OUT_ROWS):
        rows = slice(r0, r0 + OUT_ROWS)
        gl = _gelu_tanh(jnp.concatenate([ys_ref[s, rows, :] for s in range(SSM_SLABS)], axis=1))
        yb = gl * _sigmoid(_dot(gl.astype(BF16), wglu_ref[...])) * szb_ref[rows, :].astype(F32)
        acc = _dot(ya_ref[rows, :], wout_ref[0:D_SHORTCONV, :])
        acc = acc + _dot(yb.astype(BF16), wout_ref[D_SHORTCONV:D_SHORTCONV + D_SSM, :])
        h = x_ref[rows, :] + acc
        h_ref[rows, :] = h
        hn_ref[rows, :] = _rms(h, g_ref[...]).astype(BF16)


def _even_out(x, ya, ys, szb, w_glu, w_out, g, tm):
    seq = x.shape[0]
    row = lambda w: pl.BlockSpec((tm, w), lambda i: (i, 0))
    const = lambda a: pl.BlockSpec(a.shape, lambda i: (0, 0), pipeline_mode=pl.Buffered(1))
    return pl.pallas_call(
        functools.partial(_even_out_kernel, tm=tm),
        grid=(seq // tm,),
        in_specs=[row(D_MODEL), row(D_SHORTCONV),
                  pl.BlockSpec((SSM_SLABS, tm, 128), lambda i: (0, i, 0)), row(D_SSM),
                  const(w_glu), const(w_out), const(g)],
        out_specs=[row(D_MODEL), row(D_MODEL)],
        out_shape=[jax.ShapeDtypeStruct((seq, D_MODEL), F32),
                   jax.ShapeDtypeStruct((seq, D_MODEL), BF16)],
        compiler_params=_params(("arbitrary",)),
        name="even_out",
    )(x, ya, ys, szb, w_glu, w_out, g)


ODD_GROUPS = 6


def _odd_in_kernel(hn_ref, wa_ref, wb_ref, o_ref, qkv_ref, *, tm, col):
    s = pl.program_id(1)
    nb = D_CONFORMER // col
    group = s // nb
    is_silu = (group == 1) | (group == 5)
    blocks = [slice(r0, r0 + IN_ROWS) for r0 in range(0, tm, IN_ROWS)]

    @pl.when(group == 0)
    def _():
        for rows in blocks:
            hn = hn_ref[rows, :]
            o_ref[rows, :] = (_dot(hn, wa_ref[...]) * _sigmoid(_dot(hn, wb_ref[...]))).astype(BF16)

    @pl.when(is_silu)
    def _():
        for rows in blocks:
            o_ref[rows, :] = _silu(_dot(hn_ref[rows, :], wa_ref[...])).astype(BF16)

    @pl.when((group > 0) & jnp.logical_not(is_silu))
    def _():
        scale = jnp.where(group == 2, Q_SCALE, 1.0).astype(F32)
        for rows in blocks:
            qkv_ref[rows, :] = _dot(hn_ref[rows, :], wa_ref[...]) * scale


def _odd_in(hn, w_in, tm, col):
    seq = hn.shape[0]
    nb = D_CONFORMER // col
    return pl.pallas_call(
        functools.partial(_odd_in_kernel, tm=tm, col=col),
        grid=(seq // tm, ODD_GROUPS * nb),
        in_specs=[pl.BlockSpec((tm, D_MODEL), lambda i, s: (i, 0)),
                  pl.BlockSpec((D_MODEL, col), lambda i, s: (0, jnp.where(s < nb, s, s + nb))),
                  pl.BlockSpec((D_MODEL, col), lambda i, s: (0, jnp.where(s < nb, s + nb, 2 * nb - 1)))],
        out_specs=[pl.BlockSpec((tm, col), lambda i, s: (i, jnp.where(s < 2 * nb, s, jnp.where(
                       s < 5 * nb, 2 * nb - 1, s - 3 * nb)))),
                   pl.BlockSpec((tm, col), lambda i, s: (i, jnp.clip(s - 2 * nb, 0, 3 * nb - 1)))],
        out_shape=[jax.ShapeDtypeStruct((seq, 3 * D_CONFORMER), BF16),
                   jax.ShapeDtypeStruct((seq, 3 * D_ATTN), F32)],
        compiler_params=_params(("arbitrary", "arbitrary")),
        name="odd_in",
    )(hn, w_in, w_in)


CONV_ROWS = 64


def _conformer_kernel(cur_ref, halo_ref, szc_ref, cw_ref, cb_ref, lg_ref, lb_ref, wpw_ref,
                      o_ref, buf_sc, conv_sc, *, tm):
    i = pl.program_id(0)
    nslab = D_CONFORMER // 128
    for c in range(nslab):
        lanes = slice(c * 128, (c + 1) * 128)
        buf_sc[c, 0:CONV_HALO, :] = jnp.where(i == 0, 0.0, halo_ref[:, lanes].astype(F32))
        buf_sc[c, CONV_HALO:CONV_HALO + tm, :] = cur_ref[:, lanes].astype(F32)
    shift = CONV_HALO - (CONFORMER_WIDTH - 1)

    def rows(rc, _):
        base = pl.multiple_of(rc * CONV_ROWS, CONV_ROWS)
        for c in range(nslab):
            lanes = slice(c * 128, (c + 1) * 128)
            acc = jnp.zeros((CONV_ROWS, 128), F32) + cb_ref[:, lanes]
            for k in range(CONFORMER_WIDTH):
                acc = acc + cw_ref[k:k + 1, lanes] * buf_sc[c, pl.ds(base + (shift + k), CONV_ROWS), :]
            conv_sc[pl.ds(base, CONV_ROWS), lanes] = acc
        return 0

    lax.fori_loop(0, tm // CONV_ROWS, rows, 0)
    for r0 in range(0, tm, OUT_ROWS):
        sub = slice(r0, r0 + OUT_ROWS)
        y = conv_sc[sub, :]
        mu = jnp.mean(y, axis=-1, keepdims=True)
        yc = y - mu
        var = jnp.mean(yc * yc, axis=-1, keepdims=True)
        ln = yc * lax.rsqrt(var + NORM_EPS) * lg_ref[...] + lb_ref[...]
        o_ref[sub, :] = (_dot(_silu(ln).astype(BF16), wpw_ref[...]) * szc_ref[sub, :].astype(F32)).astype(BF16)


def _conformer(proj, conv_w, conv_b, ln_g, ln_b, w_pw, tm):
    seq = proj.shape[0]
    per = tm // CONV_HALO
    const = lambda a: pl.BlockSpec(a.shape, lambda i: (0, 0))
    return pl.pallas_call(
        functools.partial(_conformer_kernel, tm=tm),
        grid=(seq // tm,),
        in_specs=[pl.BlockSpec((tm, D_CONFORMER), lambda i: (i, 0)),
                  pl.BlockSpec((CONV_HALO, D_CONFORMER), lambda i: (jnp.maximum(i * per - 1, 0), 0)),
                  pl.BlockSpec((tm, D_CONFORMER), lambda i: (i, 1)),
                  const(conv_w), const(conv_b), const(ln_g), const(ln_b), const(w_pw)],
        out_specs=pl.BlockSpec((tm, D_CONFORMER), lambda i: (i, 0)),
        out_shape=jax.ShapeDtypeStruct((seq, D_CONFORMER), BF16),
        scratch_shapes=[pltpu.VMEM((D_CONFORMER // 128, tm + CONV_HALO, 128), F32),
                        pltpu.VMEM((tm, D_CONFORMER), F32)],
        compiler_params=_params(("arbitrary",)),
        name="conformer",
    )(proj, proj, proj, conv_w, conv_b, ln_g, ln_b, w_pw)


def _t5_bucket(dist):
    max_exact = REL_BUCKETS // 2
    d_f = jnp.maximum(dist, 1).astype(F32)
    large = max_exact + (jnp.log(d_f / max_exact) / math.log(REL_MAX_DISTANCE / max_exact)
                         * (REL_BUCKETS - max_exact)).astype(jnp.int32)
    large = jnp.minimum(large, REL_BUCKETS - 1)
    return jnp.where(dist < max_exact, dist, large)


def _attn_bias(rel_bias):
    q = ATTN_BLOCK
    nh = rel_bias.shape[1]
    kj = np.arange(2 * q)[None, None, :]
    neg = lambda n: jnp.full((nh, n), NEG_INF, F32)
    tabs = []
    for window, dilation in DILATED_PATTERNS:
        steps = window // dilation
        assert steps == q
        rel = jnp.asarray(np.arange(steps + 1) * dilation, dtype=jnp.int32)
        by_rel = rel_bias.astype(F32)[_t5_bucket(rel)].T
        g = jnp.concatenate([by_rel[:, ::-1], neg(2 * q - 1)], axis=1)
        flat = jnp.tile(g, (1, q))[:, :q * (3 * q - 1)]
        tab = flat.reshape(nh, q, 3 * q - 1)[:, :, :2 * q]
        if dilation == 1:
            tab = tab.reshape(nh, q // MERGE_DIL, MERGE_DIL, 2 * q).transpose(0, 2, 1, 3).reshape(nh, q, 2 * q)
        tab = tab * LOG2E
        tabs.append(tab)
        tabs.append(jnp.where(kj >= q, tab, NEG_INF))
    return jnp.stack(tabs, axis=1)


def _attn_kernel(q_ref, k_ref, v_ref, szd_ref, bias_ref, o_ref,
                 onat, qm, km, vm, qs1, qs2, ks0, ks1, ks2, vs0, vs1, vs2,
                 o0, o1, o2, l0, l1, l2):
    i = pl.program_id(1)
    t = ATTN_TILE
    blk = ATTN_BLOCK
    first_tile = i == 0
    qf, kf, vf = q_ref, k_ref, v_ref
    qstreams = (None, qs1, qs2)
    kstreams = (ks0, ks1, ks2)
    vstreams = (vs0, vs1, vs2)
    md = MERGE_DIL
    mspan = t // md

    def stream(src_f, src_m, dil, r):
        n = t // dil
        if dil == md:
            x = src_f[pl.ds(r, n, stride=dil), :]
            src_m[r * n:(r + 1) * n, :] = x
            return x
        return src_m[pl.ds((r % md) * mspan + r // md, n, stride=dil // md), :]

    for pidx, (_, dil) in enumerate(DILATED_PATTERNS):
        n = t // dil
        per = blk + n
        for r in range(dil):
            base = r * per
            for src_f, src_m, dst in ((kf, km, kstreams[pidx]), (vf, vm, vstreams[pidx])):
                tail = dst[base + n:base + per, :]
                dst[base:base + blk, :] = jnp.where(first_tile, jnp.zeros_like(tail), tail)
                if dil == 1:
                    dst[base + blk:base + per, :] = src_f[...].astype(BF16)
                else:
                    dst[base + blk:base + per, :] = stream(src_f, src_m, dil, r).astype(BF16)
            if dil > 1:
                qstreams[pidx][r * n:(r + 1) * n, :] = stream(qf, qm, dil, r).astype(BF16)

    ones = jnp.ones((2 * blk, HEAD_DIM), BF16)
    first = first_tile.astype(jnp.int32)

    def block(pidx, dil, r, sb, out_sc, lse_sc):
        n = t // dil
        per = blk + n
        if dil == 1:
            qb = jnp.concatenate([qf[pl.ds(sb * blk + c, blk // md, stride=md), :] for c in range(md)],
                                 axis=0).astype(BF16)
        else:
            qb = qstreams[pidx][r * n + sb * blk:r * n + (sb + 1) * blk, :]
        krows = slice(r * per + sb * blk, r * per + (sb + 2) * blk)
        kb = kstreams[pidx][krows, :]
        vb = vstreams[pidx][krows, :]
        s = lax.dot_general(qb, kb, _NT, preferred_element_type=F32)
        s = s + bias_ref[0, 2 * pidx + first if sb == 0 else 2 * pidx]
        mx = jnp.max(s, axis=-1, keepdims=True)
        p = jnp.exp2(s - mx).astype(BF16)
        pv = _dot(p, jnp.concatenate([vb, ones], axis=1))
        den = pv[:, HEAD_DIM:]
        out = pv[:, :HEAD_DIM] / den
        lse = mx + jnp.log2(den)
        if dil == 1:
            piece = blk // md
            for c in range(md):
                rows = slice(c * mspan + sb * piece, c * mspan + (sb + 1) * piece)
                out_sc[rows, :] = out[c * piece:(c + 1) * piece, :]
                lse_sc[rows, :] = lse[c * piece:(c + 1) * piece, :]
        elif dil == md:
            rows = slice(r * mspan + sb * blk, r * mspan + (sb + 1) * blk)
            out_sc[rows, :] = out
            lse_sc[rows, :] = lse
        else:
            step = dil // md
            rows = pl.ds((r % md) * mspan + sb * blk * step + r // md, blk, stride=step)
            out_sc[rows, :] = out
            lse_sc[rows, :] = lse

    scratch = ((o0, l0), (o1, l1), (o2, l2))
    for pidx, (_, dil) in enumerate(DILATED_PATTERNS):
        for sb in range(t // (blk * dil)):
            for r in range(dil):
                block(pidx, dil, r, sb, *scratch[pidx])

    la, lb, lc = l0[...], l1[...], l2[...]
    lall = jnp.maximum(jnp.maximum(la, lb), lc)
    wa, wb, wc = jnp.exp2(la - lall), jnp.exp2(lb - lall), jnp.exp2(lc - lall)
    merged = (wa * o0[...] + wb * o1[...] + wc * o2[...]) / (wa + wb + wc)
    for c in range(md):
        onat[pl.ds(c, mspan, stride=md), :] = merged[c * mspan:(c + 1) * mspan, :]
    o_ref[...] = (onat[...] * szd_ref[...].astype(F32)).astype(BF16)


def _attention(qkv, proj, bias):
    seq = proj.shape[0]
    t = ATTN_TILE
    q0, k0, v0 = 0, N_HEADS, 2 * N_HEADS
    z0 = (2 * D_CONFORMER) // HEAD_DIM
    cur = lambda c0: pl.BlockSpec((t, HEAD_DIM), lambda h, i: (i, c0 + h))
    f32 = lambda rows: pltpu.VMEM((rows, HEAD_DIM), F32)
    bf16 = lambda rows: pltpu.VMEM((rows, HEAD_DIM), BF16)
    streams = [bf16(dil * (ATTN_BLOCK + t // dil)) for _, dil in DILATED_PATTERNS]
    return pl.pallas_call(
        _attn_kernel,
        grid=(N_HEADS, seq // t),
        in_specs=[cur(q0), cur(k0), cur(v0), cur(z0),
                  pl.BlockSpec((1,) + bias.shape[1:], lambda h, i: (h, 0, 0, 0))],
        out_specs=pl.BlockSpec((t, HEAD_DIM), lambda h, i: (i, h)),
        out_shape=jax.ShapeDtypeStruct((seq, D_ATTN), BF16),
        scratch_shapes=[f32(t)] * 4 + [bf16(t)] * 2 + streams + streams + [f32(t)] * 6,
        compiler_params=_params(("arbitrary", "arbitrary")),
        name="dilated_attn",
    )(qkv, qkv, qkv, proj, bias)


def _odd_out_kernel(h_ref, yc_ref, yd_ref, wout_ref, g_ref, o_ref, *, tm):
    for r0 in range(0, tm, OUT_ROWS):
        rows = slice(r0, r0 + OUT_ROWS)
        acc = _dot(yc_ref[rows, :], wout_ref[0:D_CONFORMER, :])
        acc = acc + _dot(yd_ref[rows, :], wout_ref[D_CONFORMER:D_CONFORMER + D_ATTN, :])
        o_ref[rows, :] = _rms(h_ref[rows, :] + acc, g_ref[...])


def _odd_out(h, yc, yd, w_out, g, tm):
    seq = h.shape[0]
    row = lambda w: pl.BlockSpec((tm, w), lambda i: (i, 0))
    const = lambda a: pl.BlockSpec(a.shape, lambda i: (0, 0), pipeline_mode=pl.Buffered(1))
    return pl.pallas_call(
        functools.partial(_odd_out_kernel, tm=tm),
        grid=(seq // tm,),
        in_specs=[row(D_MODEL), row(D_CONFORMER), row(D_ATTN), const(w_out), const(g)],
        out_specs=row(D_MODEL),
        out_shape=jax.ShapeDtypeStruct((seq, D_MODEL), F32),
        compiler_params=_params(("arbitrary",)),
        name="odd_out",
    )(h, yc, yd, w_out, g)


def _row_tile(seq, want):
    return min(seq, want)


def kernel(x, norm_g, final_norm_g, ev_w_in, ev_conv_w, s5_lam_re, s5_lam_im, s5_log_dt, s5_b_re, s5_b_im, s5_c_re, s5_c_im, s5_d, s5_w_glu, ev_w_out, od_w_in, cf_conv_w, cf_conv_b, cf_ln_g, cf_ln_b, cf_w_pw, od_w_out, rel_bias):
    bsz, seq, _ = x.shape
    assert bsz == 1 and seq % ATTN_TILE == 0
    x2 = x.reshape(seq, D_MODEL)
    g0 = norm_g[0].astype(F32).reshape(1, D_MODEL)
    g1 = norm_g[1].astype(F32).reshape(1, D_MODEL)
    gf = final_norm_g.astype(F32).reshape(1, D_MODEL)

    ya, u, szb, ev_w_out_b, od_w_out_b, od_w_in_b = _even_in(
        x2, g0, ev_w_in[0].astype(BF16), ev_conv_w[0].astype(F32),
        (ev_w_out[0].astype(F32), od_w_out[0].astype(F32), od_w_in[0].astype(F32)),
        _row_tile(seq, EVEN_IN_TILE[0]), EVEN_IN_TILE[1])
    tables = _s5_tables(s5_lam_re[0], s5_lam_im[0], s5_log_dt[0], s5_b_re[0], s5_b_im[0],
                        s5_c_re[0], s5_c_im[0], s5_d[0])
    ys = _s5(u, tables, min(seq // SSM_CHUNK, SSM_CHUNK_ROWS))
    h1, hn1 = _even_out(x2, ya, ys, szb, s5_w_glu[0].astype(BF16), ev_w_out_b, g1,
                        _row_tile(seq, EVEN_OUT_ROWS))

    proj, qkv = _odd_in(hn1, od_w_in_b, _row_tile(seq, ODD_IN_TILE[0]), ODD_IN_TILE[1])
    conv_w = jnp.concatenate([cf_conv_w[0].astype(F32), jnp.zeros((1, D_CONFORMER), F32)], axis=0)
    yc = _conformer(proj, conv_w, cf_conv_b[0].astype(F32).reshape(1, -1),
                    cf_ln_g[0].astype(F32).reshape(1, -1), cf_ln_b[0].astype(F32).reshape(1, -1),
                    cf_w_pw[0].astype(BF16), _row_tile(seq, CONFORMER_ROWS))
    yd = _attention(qkv, proj, _attn_bias(rel_bias))
    out = _odd_out(h1, yc, yd, od_w_out_b, gf, _row_tile(seq, ODD_OUT_ROWS))
    return out.reshape(bsz, seq, D_MODEL)
```

```python
import functools
import math

import numpy as np
import jax
import jax.numpy as jnp
from jax import lax
from jax.experimental import pallas as pl
from jax.experimental.pallas import tpu as pltpu

F32 = jnp.float32
BF16 = jnp.bfloat16

D_MODEL = 2048
D_SHORTCONV = 1536
SHORTCONV_WIDTH = 3
D_SSM = 512
SSM_GROUP = 16
SSM_GROUPS = 32
SSM_STATE = 64
D_CONFORMER = 1024
CONFORMER_WIDTH = 31
D_ATTN = 1024
HEAD_DIM = 128
N_HEADS = 8
DILATED_PATTERNS = ((128, 1), (512, 4), (2048, 16))
ATTN_BLOCK = 128
REL_BUCKETS = 32
REL_MAX_DISTANCE = 2048
NORM_EPS = 1e-6
NEG_INF = -1e30

IN_ROWS = 256
EVEN_ROWS = 256
SSM_CHUNK = 16
SSM_FLAT = SSM_CHUNK * SSM_GROUP
SSM_SLABS = D_SSM // 128
ATTN_TILE = 2048
LOG2E = math.log2(math.e)
Q_SCALE = HEAD_DIM ** -0.5 * LOG2E
MERGE_DIL = 4
CONV_HALO = 32
VMEM_LIMIT = 50 * 1024 * 1024

EVEN_IN_TILE = (1024, 512)
EVEN_OUT_ROWS = 512
ODD_IN_TILE = (2048, 512)
CONFORMER_ROWS = 1024
ODD_OUT_ROWS = 512
SSM_CHUNK_ROWS = 128


def _params(sem, vmem=VMEM_LIMIT):
    return pltpu.CompilerParams(dimension_semantics=sem, vmem_limit_bytes=vmem)


def _dot(a, b):
    return jnp.dot(a, b, preferred_element_type=F32)


def _sigmoid(z):
    return 1.0 / (1.0 + jnp.exp(-z))


def _silu(z):
    return z * _sigmoid(z)


def _gelu_tanh(x):
    return 0.5 * x * (1.0 + jnp.tanh(math.sqrt(2.0 / math.pi) * (x + 0.044715 * (x * x * x))))


def _cast_slice_spec(a, nsteps, step_of):
    rows = a.shape[0] // nsteps
    assert rows * nsteps == a.shape[0] and rows % 16 == 0, (a.shape, nsteps)
    return pl.BlockSpec((rows, a.shape[1]), lambda *idx: (step_of(*idx), 0))


def _rms(xf, g):
    ms = jnp.mean(xf * xf, axis=-1, keepdims=True)
    return xf * lax.rsqrt(ms + NORM_EPS) * g


def _even_in_kernel(x_ref, g_ref, wa_ref, wb_ref, wc_ref, wd_ref, cw_ref, c0_ref, c1_ref,
                    ya_ref, u_ref, szb_ref, c0b_ref, c1b_ref, hn_sc, pbuf_sc, carry_sc, *, tm, col):
    i = pl.program_id(0)
    j = pl.program_id(1)
    nconv = D_SHORTCONV // col
    c0b_ref[...] = c0_ref[...].astype(BF16)
    c1b_ref[...] = c1_ref[...].astype(BF16)

    def conv_step(first):
        @pl.when(i == 0)
        def _():
            carry_sc[j] = jnp.zeros((8, col), F32)

        pbuf_sc[0:8, :] = carry_sc[j]
        cw = cw_ref[...]
        for r0 in range(0, tm, EVEN_ROWS):
            rows = slice(r0, r0 + EVEN_ROWS)
            if first:
                hn = _rms(x_ref[rows, :], g_ref[...]).astype(BF16)
                hn_sc[rows, :] = hn
            else:
                hn = hn_sc[rows, :]
            p = _dot(hn, wb_ref[...]) * _dot(hn, wa_ref[...])
            pbuf_sc[8 + r0:8 + r0 + EVEN_ROWS, :] = p
            conv = (cw[2:3, :] * p + cw[1:2, :] * pbuf_sc[pl.ds(7 + r0, EVEN_ROWS), :]
                    + cw[0:1, :] * pbuf_sc[pl.ds(6 + r0, EVEN_ROWS), :])
            gb = _dot(hn, wc_ref[...])
            za = _dot(hn, wd_ref[...])
            ya_ref[rows, :] = (gb * conv * _silu(za)).astype(BF16)
        carry_sc[j] = pbuf_sc[tm:tm + 8, :]

    pl.when(j == 0)(functools.partial(conv_step, True))
    pl.when((j > 0) & (j < nconv))(functools.partial(conv_step, False))

    @pl.when(j >= nconv)
    def _():
        for r0 in range(0, tm, EVEN_ROWS):
            rows = slice(r0, r0 + EVEN_ROWS)
            hn = hn_sc[rows, :]
            u = _dot(hn, wa_ref[...])
            for s in range(col // 128):
                u_ref[s, rows, :] = u[:, s * 128:(s + 1) * 128]
            szb_ref[rows, :] = _silu(_dot(hn, wb_ref[...])).astype(BF16)


def _even_in(x, g, w_in, conv_w, cast0, cast1, tm, col):
    seq = x.shape[0]
    nconv = D_SHORTCONV // col
    nssm = D_SSM // col
    u_blk = 4 * nconv

    def wspec(k, alt0):
        return pl.BlockSpec((D_MODEL, col),
                            lambda i, j: (0, jnp.where(j < nconv, j + nconv * k, alt0 + j - nconv)))

    hold = lambda k: pl.BlockSpec((D_MODEL, col), lambda i, j: (0, jnp.minimum(j, nconv - 1) + nconv * k))
    ssm_step = lambda j: jnp.maximum(j - nconv, 0)
    nj = nconv + nssm
    cast_spec = lambda a: _cast_slice_spec(a, (seq // tm) * nj, lambda i, j: i * nj + j)
    return pl.pallas_call(
        functools.partial(_even_in_kernel, tm=tm, col=col),
        grid=(seq // tm, nj),
        in_specs=[
            pl.BlockSpec((tm, D_MODEL), lambda i, j: (i, 0)),
            pl.BlockSpec((1, D_MODEL), lambda i, j: (0, 0)),
            wspec(0, u_blk), wspec(1, u_blk + nssm), hold(2), hold(3),
            pl.BlockSpec((SHORTCONV_WIDTH, col), lambda i, j: (0, jnp.minimum(j, nconv - 1))),
            cast_spec(cast0), cast_spec(cast1),
        ],
        out_specs=[
            pl.BlockSpec((tm, col), lambda i, j: (i, jnp.minimum(j, nconv - 1))),
            pl.BlockSpec((col // 128, tm, 128), lambda i, j: (ssm_step(j), i, 0)),
            pl.BlockSpec((tm, col), lambda i, j: (i, ssm_step(j))),
            cast_spec(cast0), cast_spec(cast1),
        ],
        out_shape=[
            jax.ShapeDtypeStruct((seq, D_SHORTCONV), BF16),
            jax.ShapeDtypeStruct((SSM_SLABS, seq, 128), F32),
            jax.ShapeDtypeStruct((seq, D_SSM), BF16),
            jax.ShapeDtypeStruct(cast0.shape, BF16),
            jax.ShapeDtypeStruct(cast1.shape, BF16),
        ],
        scratch_shapes=[
            pltpu.VMEM((tm, D_MODEL), BF16),
            pltpu.VMEM((tm + 8, col), F32),
            pltpu.VMEM((nconv, 8, col), F32),
        ],
        compiler_params=_params(("arbitrary", "arbitrary")),
        name="even_in",
    )(x, g, w_in, w_in, w_in, w_in, conv_w, cast0, cast1)


def _s5_tables(lam_re, lam_im, log_dt, b_re, b_im, c_re, c_im, d_skip):
    hi = lax.Precision.HIGH
    t_, g_, n_, p_ = SSM_CHUNK, SSM_GROUPS, SSM_STATE, SSM_GROUP
    lam_re, lam_im = lam_re.astype(F32), lam_im.astype(F32)
    dt = jnp.exp(log_dt.astype(F32))[:, None]
    xr, xi = lam_re * dt, lam_im * dt
    k = jnp.arange(t_ + 1, dtype=F32)[:, None, None]
    mag = jnp.exp(xr * k)
    pr, pi = mag * jnp.cos(xi * k), mag * jnp.sin(xi * k)
    nr, ni = pr[1] - 1.0, pi[1]
    den = lam_re * lam_re + lam_im * lam_im
    fr = (nr * lam_re + ni * lam_im) / den
    fi = (ni * lam_re - nr * lam_im) / den
    b_re, b_im = b_re.astype(F32), b_im.astype(F32)
    bbr = fr[..., None] * b_re - fi[..., None] * b_im
    bbi = fr[..., None] * b_im + fi[..., None] * b_re
    c_re, c_im = c_re.astype(F32), c_im.astype(F32)

    cpr = c_re[None] * pr[:t_, :, None, :] - c_im[None] * pi[:t_, :, None, :]
    cpi = c_re[None] * pi[:t_, :, None, :] + c_im[None] * pr[:t_, :, None, :]
    kern = jnp.einsum('tgpn,gnq->tgpq', jnp.concatenate([cpr, -cpi], axis=-1),
                      jnp.concatenate([bbr, bbi], axis=1), precision=hi)
    skip = d_skip.astype(F32).reshape(g_, p_)[:, :, None] * jnp.eye(p_, dtype=F32)[None]
    kern = kern.at[0].add(skip)
    flat = t_ * p_
    last = jnp.transpose(kern[::-1], (1, 2, 0, 3)).reshape(g_, p_, flat)
    mt = jnp.stack([jnp.pad(last[:, :, (t_ - 1 - t) * p_:], ((0, 0), (0, 0), (0, (t_ - 1 - t) * p_)))
                    for t in range(t_)], axis=1).reshape(g_, flat, flat)

    prj, pij = pr[t_ - 1::-1][:t_], pi[t_ - 1::-1][:t_]
    wr = prj[..., None] * bbr[None] - pij[..., None] * bbi[None]
    wi = prj[..., None] * bbi[None] + pij[..., None] * bbr[None]
    half = (np.arange(2) == 0)[None, None, :, None, None]

    def pack_w(w):
        w = jnp.transpose(w.reshape(t_, g_ // 2, 2, n_, p_), (1, 0, 2, 4, 3))
        w = jnp.concatenate([jnp.where(half, w, 0.0), jnp.where(half, 0.0, w)], axis=-1)
        return w.reshape(g_ // 2, 2 * flat, 2 * n_)

    vr = c_re[None] * pr[1:, :, None, :] - c_im[None] * pi[1:, :, None, :]
    vi = c_re[None] * pi[1:, :, None, :] + c_im[None] * pr[1:, :, None, :]
    vr = jnp.transpose(vr, (1, 0, 2, 3)).reshape(g_ // 2, 2, flat, n_)
    vi = -jnp.transpose(vi, (1, 0, 2, 3)).reshape(g_ // 2, 2, flat, n_)
    zv = jnp.zeros((g_ // 2, flat, n_), F32)
    v_even = jnp.concatenate([vr[:, 0], zv, vi[:, 0], zv], axis=-1)
    v_odd = jnp.concatenate([zv, vr[:, 1], zv, vi[:, 1]], axis=-1)
    vt = jnp.stack([v_even, v_odd], axis=1).reshape(g_, flat, 4 * n_)
    a_re = pr[t_].reshape(1, g_ * n_)
    a_im = pi[t_].reshape(1, g_ * n_)
    return mt.astype(BF16), pack_w(wr).astype(BF16), pack_w(wi).astype(BF16), vt.astype(BF16), a_re, a_im


_NT = (((1,), (1,)), ((), ()))
_TN = (((0,), (0,)), ((), ()))


def _s5_kernel(u_ref, mt_ref, wre_ref, wim_ref, vt_ref, are_ref, aim_ref, y_ref,
               xt_sc, yt_sc, zre_sc, zim_sc, carry_sc, *, ct):
    i = pl.program_id(0)
    t_ = SSM_CHUNK

    @pl.when(i == 0)
    def _():
        carry_sc[...] = jnp.zeros_like(carry_sc)

    for s in range(SSM_SLABS):
        for j in range(t_):
            xt_sc[s, j] = u_ref[s, pl.ds(j, ct, stride=t_), :].T.astype(BF16)

    pairs_per_slab = 128 // (2 * SSM_GROUP)
    for gp in range(SSM_GROUPS // 2):
        s, k = gp // pairs_per_slab, gp % pairs_per_slab
        xp = xt_sc[s, :, 32 * k:32 * k + 32, :].reshape(2 * SSM_FLAT, ct)
        zre_sc[:, gp * 128:(gp + 1) * 128] = lax.dot_general(xp, wre_ref[gp], _TN, preferred_element_type=F32)
        zim_sc[:, gp * 128:(gp + 1) * 128] = lax.dot_general(xp, wim_ref[gp], _TN, preferred_element_type=F32)

    are = are_ref[...]
    aim = aim_ref[...]

    def step(r, carry):
        sre, sim = carry
        zr = zre_sc[pl.ds(r, 1), :]
        zi = zim_sc[pl.ds(r, 1), :]
        zre_sc[pl.ds(r, 1), :] = sre
        zim_sc[pl.ds(r, 1), :] = sim
        return are * sre - aim * sim + zr, are * sim + aim * sre + zi

    sre, sim = lax.fori_loop(0, ct, step, (carry_sc[0:1, :], carry_sc[1:2, :]), unroll=8)
    carry_sc[0:1, :] = sre
    carry_sc[1:2, :] = sim

    groups_per_slab = 128 // SSM_GROUP
    for g in range(SSM_GROUPS):
        gp, s, k = g // 2, g // groups_per_slab, g % groups_per_slab
        xg = xt_sc[s, :, 16 * k:16 * k + 16, :].reshape(SSM_FLAT, ct)
        sp = jnp.concatenate([zre_sc[:, gp * 128:(gp + 1) * 128],
                              zim_sc[:, gp * 128:(gp + 1) * 128]], axis=1).astype(BF16)
        yt = _dot(mt_ref[g], xg) + lax.dot_general(vt_ref[g], sp, _NT, preferred_element_type=F32)
        yt_sc[s, :, 16 * k:16 * k + 16, :] = yt.reshape(t_, SSM_GROUP, ct)

    for s in range(SSM_SLABS):
        for t in range(t_):
            y_ref[s, pl.ds(t, ct, stride=t_), :] = yt_sc[s, t].T


def _s5(u4, tables, ct):
    mt, wre, wim, vt, are, aim = tables
    _, seq, _ = u4.shape
    full = lambda a: pl.BlockSpec(a.shape, lambda i: (0,) * a.ndim)
    rows = pl.BlockSpec((SSM_SLABS, SSM_CHUNK * ct, 128), lambda i: (0, i, 0))
    nstate = SSM_GROUPS * SSM_STATE
    return pl.pallas_call(
        functools.partial(_s5_kernel, ct=ct),
        grid=(seq // (SSM_CHUNK * ct),),
        in_specs=[rows, full(mt), full(wre), full(wim), full(vt), full(are), full(aim)],
        out_specs=rows,
        out_shape=jax.ShapeDtypeStruct(u4.shape, F32),
        scratch_shapes=[pltpu.VMEM((SSM_SLABS, SSM_CHUNK, 128, ct), BF16),
                        pltpu.VMEM((SSM_SLABS, SSM_CHUNK, 128, ct), F32),
                        pltpu.VMEM((ct, nstate), F32),
                        pltpu.VMEM((ct, nstate), F32),
                        pltpu.VMEM((8, nstate), F32)],
        compiler_params=_params(("arbitrary",)),
        name="s5_ssm",
    )(u4, mt, wre, wim, vt, are, aim)


OUT_ROWS = 256


def _even_out_kernel(x_ref, ya_ref, ys_ref, szb_ref, wglu_ref, wout_ref, g_ref, c0_ref,
                     h_ref, hn_ref, c0b_ref, *, tm):
    c0b_ref[...] = c0_ref[...].astype(BF16)
    for r0 in range(0, tm, OUT_ROWS):
        rows = slice(r0, r0 + OUT_ROWS)
        gl = _gelu_tanh(jnp.concatenate([ys_ref[s, rows, :] for s in range(SSM_SLABS)], axis=1))
        yb = gl * _sigmoid(_dot(gl.astype(BF16), wglu_ref[...])) * szb_ref[rows, :].astype(F32)
        acc = _dot(ya_ref[rows, :], wout_ref[0:D_SHORTCONV, :])
        acc = acc + _dot(yb.astype(BF16), wout_ref[D_SHORTCONV:D_SHORTCONV + D_SSM, :])
        h = x_ref[rows, :] + acc
        h_ref[rows, :] = h
        hn_ref[rows, :] = _rms(h, g_ref[...]).astype(BF16)


def _even_out(x, ya, ys, szb, w_glu, w_out, g, cast0, tm):
    seq = x.shape[0]
    cast_spec = _cast_slice_spec(cast0, seq // tm, lambda i: i)
    row = lambda w: pl.BlockSpec((tm, w), lambda i: (i, 0))
    const = lambda a: pl.BlockSpec(a.shape, lambda i: (0, 0), pipeline_mode=pl.Buffered(1))
    return pl.pallas_call(
        functools.partial(_even_out_kernel, tm=tm),
        grid=(seq // tm,),
        in_specs=[row(D_MODEL), row(D_SHORTCONV),
                  pl.BlockSpec((SSM_SLABS, tm, 128), lambda i: (0, i, 0)), row(D_SSM),
                  const(w_glu), const(w_out), const(g), cast_spec],
        out_specs=[row(D_MODEL), row(D_MODEL), cast_spec],
        out_shape=[jax.ShapeDtypeStruct((seq, D_MODEL), F32),
                   jax.ShapeDtypeStruct((seq, D_MODEL), BF16),
                   jax.ShapeDtypeStruct(cast0.shape, BF16)],
        compiler_params=_params(("arbitrary",)),
        name="even_out",
    )(x, ya, ys, szb, w_glu, w_out, g, cast0)


ODD_GROUPS = 6


def _odd_in_kernel(hn_ref, wa_ref, wb_ref, o_ref, qkv_ref, *, tm, col):
    s = pl.program_id(1)
    nb = D_CONFORMER // col
    group = s // nb
    is_silu = (group == 1) | (group == 5)
    blocks = [slice(r0, r0 + IN_ROWS) for r0 in range(0, tm, IN_ROWS)]

    @pl.when(group == 0)
    def _():
        for rows in blocks:
            hn = hn_ref[rows, :]
            o_ref[rows, :] = (_dot(hn, wa_ref[...]) * _sigmoid(_dot(hn, wb_ref[...]))).astype(BF16)

    @pl.when(is_silu)
    def _():
        for rows in blocks:
            o_ref[rows, :] = _silu(_dot(hn_ref[rows, :], wa_ref[...])).astype(BF16)

    @pl.when((group > 0) & jnp.logical_not(is_silu))
    def _():
        scale = jnp.where(group == 2, Q_SCALE, 1.0).astype(F32)
        for rows in blocks:
            qkv_ref[rows, :] = _dot(hn_ref[rows, :], wa_ref[...]) * scale


def _odd_in(hn, w_in, tm, col):
    seq = hn.shape[0]
    nb = D_CONFORMER // col
    return pl.pallas_call(
        functools.partial(_odd_in_kernel, tm=tm, col=col),
        grid=(seq // tm, ODD_GROUPS * nb),
        in_specs=[pl.BlockSpec((tm, D_MODEL), lambda i, s: (i, 0)),
                  pl.BlockSpec((D_MODEL, col), lambda i, s: (0, jnp.where(s < nb, s, s + nb))),
                  pl.BlockSpec((D_MODEL, col), lambda i, s: (0, jnp.where(s < nb, s + nb, 2 * nb - 1)))],
        out_specs=[pl.BlockSpec((tm, col), lambda i, s: (i, jnp.where(s < 2 * nb, s, jnp.where(
                       s < 5 * nb, 2 * nb - 1, s - 3 * nb)))),
                   pl.BlockSpec((tm, col), lambda i, s: (i, jnp.clip(s - 2 * nb, 0, 3 * nb - 1)))],
        out_shape=[jax.ShapeDtypeStruct((seq, 3 * D_CONFORMER), BF16),
                   jax.ShapeDtypeStruct((seq, 3 * D_ATTN), F32)],
        compiler_params=_params(("arbitrary", "arbitrary")),
        name="odd_in",
    )(hn, w_in, w_in)


CONV_ROWS = 64


def _conformer_kernel(cur_ref, halo_ref, szc_ref, cw_ref, cb_ref, lg_ref, lb_ref, wpw_ref,
                      o_ref, buf_sc, conv_sc, *, tm):
    i = pl.program_id(0)
    nslab = D_CONFORMER // 128
    for c in range(nslab):
        lanes = slice(c * 128, (c + 1) * 128)
        buf_sc[c, 0:CONV_HALO, :] = jnp.where(i == 0, 0.0, halo_ref[:, lanes].astype(F32))
        buf_sc[c, CONV_HALO:CONV_HALO + tm, :] = cur_ref[:, lanes].astype(F32)
    shift = CONV_HALO - (CONFORMER_WIDTH - 1)

    def rows(rc, _):
        base = pl.multiple_of(rc * CONV_ROWS, CONV_ROWS)
        for c in range(nslab):
            lanes = slice(c * 128, (c + 1) * 128)
            acc = jnp.zeros((CONV_ROWS, 128), F32) + cb_ref[:, lanes]
            for k in range(CONFORMER_WIDTH):
                acc = acc + cw_ref[k:k + 1, lanes] * buf_sc[c, pl.ds(base + (shift + k), CONV_ROWS), :]
            conv_sc[pl.ds(base, CONV_ROWS), lanes] = acc
        return 0

    lax.fori_loop(0, tm // CONV_ROWS, rows, 0)
    for r0 in range(0, tm, OUT_ROWS):
        sub = slice(r0, r0 + OUT_ROWS)
        y = conv_sc[sub, :]
        mu = jnp.mean(y, axis=-1, keepdims=True)
        yc = y - mu
        var = jnp.mean(yc * yc, axis=-1, keepdims=True)
        ln = yc * lax.rsqrt(var + NORM_EPS) * lg_ref[...] + lb_ref[...]
        o_ref[sub, :] = (_dot(_silu(ln).astype(BF16), wpw_ref[...]) * szc_ref[sub, :].astype(F32)).astype(BF16)


def _conformer(proj, conv_w, conv_b, ln_g, ln_b, w_pw, tm):
    seq = proj.shape[0]
    per = tm // CONV_HALO
    const = lambda a: pl.BlockSpec(a.shape, lambda i: (0, 0))
    return pl.pallas_call(
        functools.partial(_conformer_kernel, tm=tm),
        grid=(seq // tm,),
        in_specs=[pl.BlockSpec((tm, D_CONFORMER), lambda i: (i, 0)),
                  pl.BlockSpec((CONV_HALO, D_CONFORMER), lambda i: (jnp.maximum(i * per - 1, 0), 0)),
                  pl.BlockSpec((tm, D_CONFORMER), lambda i: (i, 1)),
                  const(conv_w), const(conv_b), const(ln_g), const(ln_b), const(w_pw)],
        out_specs=pl.BlockSpec((tm, D_CONFORMER), lambda i: (i, 0)),
        out_shape=jax.ShapeDtypeStruct((seq, D_CONFORMER), BF16),
        scratch_shapes=[pltpu.VMEM((D_CONFORMER // 128, tm + CONV_HALO, 128), F32),
                        pltpu.VMEM((tm, D_CONFORMER), F32)],
        compiler_params=_params(("arbitrary",)),
        name="conformer",
    )(proj, proj, proj, conv_w, conv_b, ln_g, ln_b, w_pw)


def _t5_bucket(dist):
    max_exact = REL_BUCKETS // 2
    d_f = jnp.maximum(dist, 1).astype(F32)
    large = max_exact + (jnp.log(d_f / max_exact) / math.log(REL_MAX_DISTANCE / max_exact)
                         * (REL_BUCKETS - max_exact)).astype(jnp.int32)
    large = jnp.minimum(large, REL_BUCKETS - 1)
    return jnp.where(dist < max_exact, dist, large)


def _attn_bias(rel_bias):
    q = ATTN_BLOCK
    nh = rel_bias.shape[1]
    kj = np.arange(2 * q)[None, None, :]
    neg = lambda n: jnp.full((nh, n), NEG_INF, F32)
    tabs = []
    for window, dilation in DILATED_PATTERNS:
        steps = window // dilation
        assert steps == q
        rel = jnp.asarray(np.arange(steps + 1) * dilation, dtype=jnp.int32)
        by_rel = rel_bias.astype(F32)[_t5_bucket(rel)].T
        g = jnp.concatenate([by_rel[:, ::-1], neg(2 * q - 1)], axis=1)
        flat = jnp.tile(g, (1, q))[:, :q * (3 * q - 1)]
        tab = flat.reshape(nh, q, 3 * q - 1)[:, :, :2 * q]
        if dilation == 1:
            tab = tab.reshape(nh, q // MERGE_DIL, MERGE_DIL, 2 * q).transpose(0, 2, 1, 3).reshape(nh, q, 2 * q)
        tab = tab * LOG2E
        tabs.append(tab)
        tabs.append(jnp.where(kj >= q, tab, NEG_INF))
    return jnp.stack(tabs, axis=1)


def _attn_kernel(q_ref, k_ref, v_ref, szd_ref, bias_ref, o_ref,
                 onat, qm, km, vm, qs1, qs2, ks0, ks1, ks2, vs0, vs1, vs2,
                 o0, o1, o2, l0, l1, l2):
    i = pl.program_id(1)
    t = ATTN_TILE
    blk = ATTN_BLOCK
    first_tile = i == 0
    qf, kf, vf = q_ref, k_ref, v_ref
    qstreams = (None, qs1, qs2)
    kstreams = (ks0, ks1, ks2)
    vstreams = (vs0, vs1, vs2)
    md = MERGE_DIL
    mspan = t // md

    def stream(src_f, src_m, dil, r):
        n = t // dil
        if dil == md:
            x = src_f[pl.ds(r, n, stride=dil), :]
            src_m[r * n:(r + 1) * n, :] = x
            return x
        return src_m[pl.ds((r % md) * mspan + r // md, n, stride=dil // md), :]

    for pidx, (_, dil) in enumerate(DILATED_PATTERNS):
        n = t // dil
        per = blk + n
        for r in range(dil):
            base = r * per
            for src_f, src_m, dst in ((kf, km, kstreams[pidx]), (vf, vm, vstreams[pidx])):
                tail = dst[base + n:base + per, :]
                dst[base:base + blk, :] = jnp.where(first_tile, jnp.zeros_like(tail), tail)
                if dil == 1:
                    dst[base + blk:base + per, :] = src_f[...].astype(BF16)
                else:
                    dst[base + blk:base + per, :] = stream(src_f, src_m, dil, r).astype(BF16)
            if dil > 1:
                qstreams[pidx][r * n:(r + 1) * n, :] = stream(qf, qm, dil, r).astype(BF16)

    ones = jnp.ones((2 * blk, HEAD_DIM), BF16)
    first = first_tile.astype(jnp.int32)

    def block(pidx, dil, r, sb, out_sc, lse_sc):
        n = t // dil
        per = blk + n
        if dil == 1:
            qb = jnp.concatenate([qf[pl.ds(sb * blk + c, blk // md, stride=md), :] for c in range(md)],
                                 axis=0).astype(BF16)
        else:
            qb = qstreams[pidx][r * n + sb * blk:r * n + (sb + 1) * blk, :]
        krows = slice(r * per + sb * blk, r * per + (sb + 2) * blk)
        kb = kstreams[pidx][krows, :]
        vb = vstreams[pidx][krows, :]
        s = lax.dot_general(qb, kb, _NT, preferred_element_type=F32)
        s = s + bias_ref[0, 2 * pidx + first if sb == 0 else 2 * pidx]
        mx = jnp.max(s, axis=-1, keepdims=True)
        p = jnp.exp2(s - mx).astype(BF16)
        pv = _dot(p, jnp.concatenate([vb, ones], axis=1))
        den = pv[:, HEAD_DIM:]
        out = pv[:, :HEAD_DIM] / den
        lse = mx + jnp.log2(den)
        if dil == 1:
            piece = blk // md
            for c in range(md):
                rows = slice(c * mspan + sb * piece, c * mspan + (sb + 1) * piece)
                out_sc[rows, :] = out[c * piece:(c + 1) * piece, :]
                lse_sc[rows, :] = lse[c * piece:(c + 1) * piece, :]
        elif dil == md:
            rows = slice(r * mspan + sb * blk, r * mspan + (sb + 1) * blk)
            out_sc[rows, :] = out
            lse_sc[rows, :] = lse
        else:
            step = dil // md
            rows = pl.ds((r % md) * mspan + sb * blk * step + r // md, blk, stride=step)
            out_sc[rows, :] = out
            lse_sc[rows, :] = lse

    scratch = ((o0, l0), (o1, l1), (o2, l2))
    for pidx, (_, dil) in enumerate(DILATED_PATTERNS):
        for sb in range(t // (blk * dil)):
            for r in range(dil):
                block(pidx, dil, r, sb, *scratch[pidx])

    la, lb, lc = l0[...], l1[...], l2[...]
    lall = jnp.maximum(jnp.maximum(la, lb), lc)
    wa, wb, wc = jnp.exp2(la - lall), jnp.exp2(lb - lall), jnp.exp2(lc - lall)
    merged = (wa * o0[...] + wb * o1[...] + wc * o2[...]) / (wa + wb + wc)
    for c in range(md):
        onat[pl.ds(c, mspan, stride=md), :] = merged[c * mspan:(c + 1) * mspan, :]
    o_ref[...] = (onat[...] * szd_ref[...].astype(F32)).astype(BF16)


def _attention(qkv, proj, bias):
    seq = proj.shape[0]
    t = ATTN_TILE
    q0, k0, v0 = 0, N_HEADS, 2 * N_HEADS
    z0 = (2 * D_CONFORMER) // HEAD_DIM
    cur = lambda c0: pl.BlockSpec((t, HEAD_DIM), lambda h, i: (i, c0 + h))
    f32 = lambda rows: pltpu.VMEM((rows, HEAD_DIM), F32)
    bf16 = lambda rows: pltpu.VMEM((rows, HEAD_DIM), BF16)
    streams = [bf16(dil * (ATTN_BLOCK + t // dil)) for _, dil in DILATED_PATTERNS]
    return pl.pallas_call(
        _attn_kernel,
        grid=(N_HEADS, seq // t),
        in_specs=[cur(q0), cur(k0), cur(v0), cur(z0),
                  pl.BlockSpec((1,) + bias.shape[1:], lambda h, i: (h, 0, 0, 0))],
        out_specs=pl.BlockSpec((t, HEAD_DIM), lambda h, i: (i, h)),
        out_shape=jax.ShapeDtypeStruct((seq, D_ATTN), BF16),
        scratch_shapes=[f32(t)] * 4 + [bf16(t)] * 2 + streams + streams + [f32(t)] * 6,
        compiler_params=_params(("arbitrary", "arbitrary")),
        name="dilated_attn",
    )(qkv, qkv, qkv, proj, bias)


def _odd_out_kernel(h_ref, yc_ref, yd_ref, wout_ref, g_ref, o_ref, *, tm):
    for r0 in range(0, tm, OUT_ROWS):
        rows = slice(r0, r0 + OUT_ROWS)
        acc = _dot(yc_ref[rows, :], wout_ref[0:D_CONFORMER, :])
        acc = acc + _dot(yd_ref[rows, :], wout_ref[D_CONFORMER:D_CONFORMER + D_ATTN, :])
        o_ref[rows, :] = _rms(h_ref[rows, :] + acc, g_ref[...])


def _odd_out(h, yc, yd, w_out, g, tm):
    seq = h.shape[0]
    row = lambda w: pl.BlockSpec((tm, w), lambda i: (i, 0))
    const = lambda a: pl.BlockSpec(a.shape, lambda i: (0, 0), pipeline_mode=pl.Buffered(1))
    return pl.pallas_call(
        functools.partial(_odd_out_kernel, tm=tm),
        grid=(seq // tm,),
        in_specs=[row(D_MODEL), row(D_CONFORMER), row(D_ATTN), const(w_out), const(g)],
        out_specs=row(D_MODEL),
        out_shape=jax.ShapeDtypeStruct((seq, D_MODEL), F32),
        compiler_params=_params(("arbitrary",)),
        name="odd_out",
    )(h, yc, yd, w_out, g)


def _row_tile(seq, want):
    return min(seq, want)


def kernel(x, norm_g, final_norm_g, ev_w_in, ev_conv_w, s5_lam_re, s5_lam_im, s5_log_dt, s5_b_re, s5_b_im, s5_c_re, s5_c_im, s5_d, s5_w_glu, ev_w_out, od_w_in, cf_conv_w, cf_conv_b, cf_ln_g, cf_ln_b, cf_w_pw, od_w_out, rel_bias):
    bsz, seq, _ = x.shape
    assert bsz == 1 and seq % ATTN_TILE == 0
    x2 = x.reshape(seq, D_MODEL)
    g0 = norm_g[0].astype(F32).reshape(1, D_MODEL)
    g1 = norm_g[1].astype(F32).reshape(1, D_MODEL)
    gf = final_norm_g.astype(F32).reshape(1, D_MODEL)

    ya, u, szb, ev_w_out_b, od_w_out_b = _even_in(
        x2, g0, ev_w_in[0].astype(BF16), ev_conv_w[0].astype(F32), ev_w_out[0].astype(F32),
        od_w_out[0].astype(F32), _row_tile(seq, EVEN_IN_TILE[0]), EVEN_IN_TILE[1])
    tables = _s5_tables(s5_lam_re[0], s5_lam_im[0], s5_log_dt[0], s5_b_re[0], s5_b_im[0],
                        s5_c_re[0], s5_c_im[0], s5_d[0])
    ys = _s5(u, tables, min(seq // SSM_CHUNK, SSM_CHUNK_ROWS))
    h1, hn1, od_w_in_b = _even_out(x2, ya, ys, szb, s5_w_glu[0].astype(BF16), ev_w_out_b, g1,
                                   od_w_in[0].astype(F32), _row_tile(seq, EVEN_OUT_ROWS))

    proj, qkv = _odd_in(hn1, od_w_in_b, _row_tile(seq, ODD_IN_TILE[0]), ODD_IN_TILE[1])
    conv_w = jnp.concatenate([cf_conv_w[0].astype(F32), jnp.zeros((1, D_CONFORMER), F32)], axis=0)
    yc = _conformer(proj, conv_w, cf_conv_b[0].astype(F32).reshape(1, -1),
                    cf_ln_g[0].astype(F32).reshape(1, -1), cf_ln_b[0].astype(F32).reshape(1, -1),
                    cf_w_pw[0].astype(BF16), _row_tile(seq, CONFORMER_ROWS))
    yd = _attention(qkv, proj, _attn_bias(rel_bias))
    out = _odd_out(h1, yc, yd, od_w_out_b, gf, _row_tile(seq, ODD_OUT_ROWS))
    return out.reshape(bsz, seq, D_MODEL)
```

```python
import functools
import math

import numpy as np
import jax
import jax.numpy as jnp
from jax import lax
from jax.experimental import pallas as pl
from jax.experimental.pallas import tpu as pltpu

F32 = jnp.float32
BF16 = jnp.bfloat16

D_MODEL = 2048
D_SHORTCONV = 1536
SHORTCONV_WIDTH = 3
D_SSM = 512
SSM_GROUP = 16
SSM_GROUPS = 32
SSM_STATE = 64
D_CONFORMER = 1024
CONFORMER_WIDTH = 31
D_ATTN = 1024
HEAD_DIM = 128
N_HEADS = 8
DILATED_PATTERNS = ((128, 1), (512, 4), (2048, 16))
ATTN_BLOCK = 128
REL_BUCKETS = 32
REL_MAX_DISTANCE = 2048
NORM_EPS = 1e-6
NEG_INF = -1e30

IN_ROWS = 256
EVEN_ROWS = 256
SSM_CHUNK = 16
SSM_FLAT = SSM_CHUNK * SSM_GROUP
SSM_SLABS = D_SSM // 128
ATTN_TILE = 4096
LOG2E = math.log2(math.e)
Q_SCALE = HEAD_DIM ** -0.5 * LOG2E
MERGE_DIL = 4
CONV_HALO = 32
VMEM_LIMIT = 50 * 1024 * 1024

EVEN_IN_TILE = (1024, 512)
EVEN_OUT_ROWS = 512
ODD_IN_TILE = (2048, 512)
CONFORMER_ROWS = 1024
ODD_OUT_ROWS = 512
SSM_CHUNK_ROWS = 128


def _params(sem, vmem=VMEM_LIMIT):
    return pltpu.CompilerParams(dimension_semantics=sem, vmem_limit_bytes=vmem)


def _dot(a, b):
    return jnp.dot(a, b, preferred_element_type=F32)


def _sigmoid(z):
    return 1.0 / (1.0 + jnp.exp(-z))


def _silu(z):
    return z * _sigmoid(z)


def _gelu_tanh(x):
    return 0.5 * x * (1.0 + jnp.tanh(math.sqrt(2.0 / math.pi) * (x + 0.044715 * (x * x * x))))


def _cast_slice_spec(a, nsteps, step_of):
    rows = a.shape[0] // nsteps
    assert rows * nsteps == a.shape[0] and rows % 16 == 0, (a.shape, nsteps)
    return pl.BlockSpec((rows, a.shape[1]), lambda *idx: (step_of(*idx), 0))


def _rms(xf, g):
    ms = jnp.mean(xf * xf, axis=-1, keepdims=True)
    return xf * lax.rsqrt(ms + NORM_EPS) * g


def _even_in_kernel(x_ref, g_ref, wa_ref, wb_ref, wc_ref, wd_ref, cw_ref, c0_ref, c1_ref,
                    ya_ref, u_ref, szb_ref, c0b_ref, c1b_ref, hn_sc, pbuf_sc, carry_sc, *, tm, col):
    i = pl.program_id(0)
    j = pl.program_id(1)
    nconv = D_SHORTCONV // col
    c0b_ref[...] = c0_ref[...].astype(BF16)
    c1b_ref[...] = c1_ref[...].astype(BF16)

    def conv_step(first):
        @pl.when(i == 0)
        def _():
            carry_sc[j] = jnp.zeros((8, col), F32)

        pbuf_sc[0:8, :] = carry_sc[j]
        cw = cw_ref[...]
        for r0 in range(0, tm, EVEN_ROWS):
            rows = slice(r0, r0 + EVEN_ROWS)
            if first:
                hn = _rms(x_ref[rows, :], g_ref[...]).astype(BF16)
                hn_sc[rows, :] = hn
            else:
                hn = hn_sc[rows, :]
            p = _dot(hn, wb_ref[...]) * _dot(hn, wa_ref[...])
            pbuf_sc[8 + r0:8 + r0 + EVEN_ROWS, :] = p
            conv = (cw[2:3, :] * p + cw[1:2, :] * pbuf_sc[pl.ds(7 + r0, EVEN_ROWS), :]
                    + cw[0:1, :] * pbuf_sc[pl.ds(6 + r0, EVEN_ROWS), :])
            gb = _dot(hn, wc_ref[...])
            za = _dot(hn, wd_ref[...])
            ya_ref[rows, :] = (gb * conv * _silu(za)).astype(BF16)
        carry_sc[j] = pbuf_sc[tm:tm + 8, :]

    pl.when(j == 0)(functools.partial(conv_step, True))
    pl.when((j > 0) & (j < nconv))(functools.partial(conv_step, False))

    @pl.when(j >= nconv)
    def _():
        for r0 in range(0, tm, EVEN_ROWS):
            rows = slice(r0, r0 + EVEN_ROWS)
            hn = hn_sc[rows, :]
            u = _dot(hn, wa_ref[...])
            for s in range(col // 128):
                u_ref[s, rows, :] = u[:, s * 128:(s + 1) * 128]
            szb_ref[rows, :] = _silu(_dot(hn, wb_ref[...])).astype(BF16)


def _even_in(x, g, w_in, conv_w, cast0, cast1, tm, col):
    seq = x.shape[0]
    nconv = D_SHORTCONV // col
    nssm = D_SSM // col
    u_blk = 4 * nconv

    def wspec(k, alt0):
        return pl.BlockSpec((D_MODEL, col),
                            lambda i, j: (0, jnp.where(j < nconv, j + nconv * k, alt0 + j - nconv)))

    hold = lambda k: pl.BlockSpec((D_MODEL, col), lambda i, j: (0, jnp.minimum(j, nconv - 1) + nconv * k))
    ssm_step = lambda j: jnp.maximum(j - nconv, 0)
    nj = nconv + nssm
    cast_spec = lambda a: _cast_slice_spec(a, (seq // tm) * nj, lambda i, j: i * nj + j)
    return pl.pallas_call(
        functools.partial(_even_in_kernel, tm=tm, col=col),
        grid=(seq // tm, nj),
        in_specs=[
            pl.BlockSpec((tm, D_MODEL), lambda i, j: (i, 0)),
            pl.BlockSpec((1, D_MODEL), lambda i, j: (0, 0)),
            wspec(0, u_blk), wspec(1, u_blk + nssm), hold(2), hold(3),
            pl.BlockSpec((SHORTCONV_WIDTH, col), lambda i, j: (0, jnp.minimum(j, nconv - 1))),
            cast_spec(cast0), cast_spec(cast1),
        ],
        out_specs=[
            pl.BlockSpec((tm, col), lambda i, j: (i, jnp.minimum(j, nconv - 1))),
            pl.BlockSpec((col // 128, tm, 128), lambda i, j: (ssm_step(j), i, 0)),
            pl.BlockSpec((tm, col), lambda i, j: (i, ssm_step(j))),
            cast_spec(cast0), cast_spec(cast1),
        ],
        out_shape=[
            jax.ShapeDtypeStruct((seq, D_SHORTCONV), BF16),
            jax.ShapeDtypeStruct((SSM_SLABS, seq, 128), F32),
            jax.ShapeDtypeStruct((seq, D_SSM), BF16),
            jax.ShapeDtypeStruct(cast0.shape, BF16),
            jax.ShapeDtypeStruct(cast1.shape, BF16),
        ],
        scratch_shapes=[
            pltpu.VMEM((tm, D_MODEL), BF16),
            pltpu.VMEM((tm + 8, col), F32),
            pltpu.VMEM((nconv, 8, col), F32),
        ],
        compiler_params=_params(("arbitrary", "arbitrary")),
        name="even_in",
    )(x, g, w_in, w_in, w_in, w_in, conv_w, cast0, cast1)


def _s5_tables(lam_re, lam_im, log_dt, b_re, b_im, c_re, c_im, d_skip):
    hi = lax.Precision.HIGH
    t_, g_, n_, p_ = SSM_CHUNK, SSM_GROUPS, SSM_STATE, SSM_GROUP
    lam_re, lam_im = lam_re.astype(F32), lam_im.astype(F32)
    dt = jnp.exp(log_dt.astype(F32))[:, None]
    xr, xi = lam_re * dt, lam_im * dt
    k = jnp.arange(t_ + 1, dtype=F32)[:, None, None]
    mag = jnp.exp(xr * k)
    pr, pi = mag * jnp.cos(xi * k), mag * jnp.sin(xi * k)
    nr, ni = pr[1] - 1.0, pi[1]
    den = lam_re * lam_re + lam_im * lam_im
    fr = (nr * lam_re + ni * lam_im) / den
    fi = (ni * lam_re - nr * lam_im) / den
    b_re, b_im = b_re.astype(F32), b_im.astype(F32)
    bbr = fr[..., None] * b_re - fi[..., None] * b_im
    bbi = fr[..., None] * b_im + fi[..., None] * b_re
    c_re, c_im = c_re.astype(F32), c_im.astype(F32)

    cpr = c_re[None] * pr[:t_, :, None, :] - c_im[None] * pi[:t_, :, None, :]
    cpi = c_re[None] * pi[:t_, :, None, :] + c_im[None] * pr[:t_, :, None, :]
    kern = jnp.einsum('tgpn,gnq->tgpq', jnp.concatenate([cpr, -cpi], axis=-1),
                      jnp.concatenate([bbr, bbi], axis=1), precision=hi)
    skip = d_skip.astype(F32).reshape(g_, p_)[:, :, None] * jnp.eye(p_, dtype=F32)[None]
    kern = kern.at[0].add(skip)
    flat = t_ * p_
    last = jnp.transpose(kern[::-1], (1, 2, 0, 3)).reshape(g_, p_, flat)
    mt = jnp.stack([jnp.pad(last[:, :, (t_ - 1 - t) * p_:], ((0, 0), (0, 0), (0, (t_ - 1 - t) * p_)))
                    for t in range(t_)], axis=1).reshape(g_, flat, flat)

    prj, pij = pr[t_ - 1::-1][:t_], pi[t_ - 1::-1][:t_]
    wr = prj[..., None] * bbr[None] - pij[..., None] * bbi[None]
    wi = prj[..., None] * bbi[None] + pij[..., None] * bbr[None]
    half = (np.arange(2) == 0)[None, None, :, None, None]

    def pack_w(w):
        w = jnp.transpose(w.reshape(t_, g_ // 2, 2, n_, p_), (1, 0, 2, 4, 3))
        w = jnp.concatenate([jnp.where(half, w, 0.0), jnp.where(half, 0.0, w)], axis=-1)
        return w.reshape(g_ // 2, 2 * flat, 2 * n_)

    vr = c_re[None] * pr[1:, :, None, :] - c_im[None] * pi[1:, :, None, :]
    vi = c_re[None] * pi[1:, :, None, :] + c_im[None] * pr[1:, :, None, :]
    vr = jnp.transpose(vr, (1, 0, 2, 3)).reshape(g_ // 2, 2, flat, n_)
    vi = -jnp.transpose(vi, (1, 0, 2, 3)).reshape(g_ // 2, 2, flat, n_)
    zv = jnp.zeros((g_ // 2, flat, n_), F32)
    v_even = jnp.concatenate([vr[:, 0], zv, vi[:, 0], zv], axis=-1)
    v_odd = jnp.concatenate([zv, vr[:, 1], zv, vi[:, 1]], axis=-1)
    vt = jnp.stack([v_even, v_odd], axis=1).reshape(g_, flat, 4 * n_)
    a_re = pr[t_].reshape(1, g_ * n_)
    a_im = pi[t_].reshape(1, g_ * n_)
    return mt.astype(BF16), pack_w(wr).astype(BF16), pack_w(wi).astype(BF16), vt.astype(BF16), a_re, a_im


_NT = (((1,), (1,)), ((), ()))
_TN = (((0,), (0,)), ((), ()))


def _s5_kernel(u_ref, mt_ref, wre_ref, wim_ref, vt_ref, are_ref, aim_ref, y_ref,
               xt_sc, yt_sc, zre_sc, zim_sc, carry_sc, *, ct):
    i = pl.program_id(0)
    t_ = SSM_CHUNK

    @pl.when(i == 0)
    def _():
        carry_sc[...] = jnp.zeros_like(carry_sc)

    for s in range(SSM_SLABS):
        for j in range(t_):
            xt_sc[s, j] = u_ref[s, pl.ds(j, ct, stride=t_), :].T.astype(BF16)

    pairs_per_slab = 128 // (2 * SSM_GROUP)
    for gp in range(SSM_GROUPS // 2):
        s, k = gp // pairs_per_slab, gp % pairs_per_slab
        xp = xt_sc[s, :, 32 * k:32 * k + 32, :].reshape(2 * SSM_FLAT, ct)
        zre_sc[:, gp * 128:(gp + 1) * 128] = lax.dot_general(xp, wre_ref[gp], _TN, preferred_element_type=F32)
        zim_sc[:, gp * 128:(gp + 1) * 128] = lax.dot_general(xp, wim_ref[gp], _TN, preferred_element_type=F32)

    are = are_ref[...]
    aim = aim_ref[...]

    def step(r, carry):
        sre, sim = carry
        zr = zre_sc[pl.ds(r, 1), :]
        zi = zim_sc[pl.ds(r, 1), :]
        zre_sc[pl.ds(r, 1), :] = sre
        zim_sc[pl.ds(r, 1), :] = sim
        return are * sre - aim * sim + zr, are * sim + aim * sre + zi

    sre, sim = lax.fori_loop(0, ct, step, (carry_sc[0:1, :], carry_sc[1:2, :]), unroll=8)
    carry_sc[0:1, :] = sre
    carry_sc[1:2, :] = sim

    groups_per_slab = 128 // SSM_GROUP
    for g in range(SSM_GROUPS):
        gp, s, k = g // 2, g // groups_per_slab, g % groups_per_slab
        xg = xt_sc[s, :, 16 * k:16 * k + 16, :].reshape(SSM_FLAT, ct)
        sp = jnp.concatenate([zre_sc[:, gp * 128:(gp + 1) * 128],
                              zim_sc[:, gp * 128:(gp + 1) * 128]], axis=1).astype(BF16)
        yt = _dot(mt_ref[g], xg) + lax.dot_general(vt_ref[g], sp, _NT, preferred_element_type=F32)
        yt_sc[s, :, 16 * k:16 * k + 16, :] = yt.reshape(t_, SSM_GROUP, ct)

    for s in range(SSM_SLABS):
        for t in range(t_):
            y_ref[s, pl.ds(t, ct, stride=t_), :] = yt_sc[s, t].T


def _s5(u4, tables, ct):
    mt, wre, wim, vt, are, aim = tables
    _, seq, _ = u4.shape
    full = lambda a: pl.BlockSpec(a.shape, lambda i: (0,) * a.ndim)
    rows = pl.BlockSpec((SSM_SLABS, SSM_CHUNK * ct, 128), lambda i: (0, i, 0))
    nstate = SSM_GROUPS * SSM_STATE
    return pl.pallas_call(
        functools.partial(_s5_kernel, ct=ct),
        grid=(seq // (SSM_CHUNK * ct),),
        in_specs=[rows, full(mt), full(wre), full(wim), full(vt), full(are), full(aim)],
        out_specs=rows,
        out_shape=jax.ShapeDtypeStruct(u4.shape, F32),
        scratch_shapes=[pltpu.VMEM((SSM_SLABS, SSM_CHUNK, 128, ct), BF16),
                        pltpu.VMEM((SSM_SLABS, SSM_CHUNK, 128, ct), F32),
                        pltpu.VMEM((ct, nstate), F32),
                        pltpu.VMEM((ct, nstate), F32),
                        pltpu.VMEM((8, nstate), F32)],
        compiler_params=_params(("arbitrary",)),
        name="s5_ssm",
    )(u4, mt, wre, wim, vt, are, aim)


OUT_ROWS = 256


def _even_out_kernel(x_ref, ya_ref, ys_ref, szb_ref, wglu_ref, wout_ref, g_ref, c0_ref,
                     h_ref, hn_ref, c0b_ref, *, tm):
    c0b_ref[...] = c0_ref[...].astype(BF16)
    for r0 in range(0, tm, OUT_ROWS):
        rows = slice(r0, r0 + OUT_ROWS)
        gl = _gelu_tanh(jnp.concatenate([ys_ref[s, rows, :] for s in range(SSM_SLABS)], axis=1))
        yb = gl * _sigmoid(_dot(gl.astype(BF16), wglu_ref[...])) * szb_ref[rows, :].astype(F32)
        acc = _dot(ya_ref[rows, :], wout_ref[0:D_SHORTCONV, :])
        acc = acc + _dot(yb.astype(BF16), wout_ref[D_SHORTCONV:D_SHORTCONV + D_SSM, :])
        h = x_ref[rows, :] + acc
        h_ref[rows, :] = h
        hn_ref[rows, :] = _rms(h, g_ref[...]).astype(BF16)


def _even_out(x, ya, ys, szb, w_glu, w_out, g, cast0, tm):
    seq = x.shape[0]
    cast_spec = _cast_slice_spec(cast0, seq // tm, lambda i: i)
    row = lambda w: pl.BlockSpec((tm, w), lambda i: (i, 0))
    const = lambda a: pl.BlockSpec(a.shape, lambda i: (0, 0), pipeline_mode=pl.Buffered(1))
    return pl.pallas_call(
        functools.partial(_even_out_kernel, tm=tm),
        grid=(seq // tm,),
        in_specs=[row(D_MODEL), row(D_SHORTCONV),
                  pl.BlockSpec((SSM_SLABS, tm, 128), lambda i: (0, i, 0)), row(D_SSM),
                  const(w_glu), const(w_out), const(g), cast_spec],
        out_specs=[row(D_MODEL), row(D_MODEL), cast_spec],
        out_shape=[jax.ShapeDtypeStruct((seq, D_MODEL), F32),
                   jax.ShapeDtypeStruct((seq, D_MODEL), BF16),
                   jax.ShapeDtypeStruct(cast0.shape, BF16)],
        compiler_params=_params(("arbitrary",)),
        name="even_out",
    )(x, ya, ys, szb, w_glu, w_out, g, cast0)


ODD_GROUPS = 6


def _odd_in_kernel(hn_ref, wa_ref, wb_ref, o_ref, qkv_ref, *, tm, col):
    s = pl.program_id(1)
    nb = D_CONFORMER // col
    group = s // nb
    is_silu = (group == 1) | (group == 5)
    blocks = [slice(r0, r0 + IN_ROWS) for r0 in range(0, tm, IN_ROWS)]

    @pl.when(group == 0)
    def _():
        for rows in blocks:
            hn = hn_ref[rows, :]
            o_ref[rows, :] = (_dot(hn, wa_ref[...]) * _sigmoid(_dot(hn, wb_ref[...]))).astype(BF16)

    @pl.when(is_silu)
    def _():
        for rows in blocks:
            o_ref[rows, :] = _silu(_dot(hn_ref[rows, :], wa_ref[...])).astype(BF16)

    @pl.when((group > 0) & jnp.logical_not(is_silu))
    def _():
        scale = jnp.where(group == 2, Q_SCALE, 1.0).astype(F32)
        for rows in blocks:
            qkv_ref[rows, :] = _dot(hn_ref[rows, :], wa_ref[...]) * scale


def _odd_in(hn, w_in, tm, col):
    seq = hn.shape[0]
    nb = D_CONFORMER // col
    return pl.pallas_call(
        functools.partial(_odd_in_kernel, tm=tm, col=col),
        grid=(seq // tm, ODD_GROUPS * nb),
        in_specs=[pl.BlockSpec((tm, D_MODEL), lambda i, s: (i, 0)),
                  pl.BlockSpec((D_MODEL, col), lambda i, s: (0, jnp.where(s < nb, s, s + nb))),
                  pl.BlockSpec((D_MODEL, col), lambda i, s: (0, jnp.where(s < nb, s + nb, 2 * nb - 1)))],
        out_specs=[pl.BlockSpec((tm, col), lambda i, s: (i, jnp.where(s < 2 * nb, s, jnp.where(
                       s < 5 * nb, 2 * nb - 1, s - 3 * nb)))),
                   pl.BlockSpec((tm, col), lambda i, s: (i, jnp.clip(s - 2 * nb, 0, 3 * nb - 1)))],
        out_shape=[jax.ShapeDtypeStruct((seq, 3 * D_CONFORMER), BF16),
                   jax.ShapeDtypeStruct((seq, 3 * D_ATTN), F32)],
        compiler_params=_params(("arbitrary", "arbitrary")),
        name="odd_in",
    )(hn, w_in, w_in)


CONV_ROWS = 256


def _conformer_kernel(cur_ref, halo_ref, szc_ref, cw_ref, cb_ref, lg_ref, lb_ref, wpw_ref,
                      o_ref, buf_sc, conv_sc, *, tm):
    i = pl.program_id(0)
    nslab = D_CONFORMER // 128
    for c in range(nslab):
        lanes = slice(c * 128, (c + 1) * 128)
        buf_sc[c, 0:CONV_HALO, :] = jnp.where(i == 0, 0.0, halo_ref[:, lanes].astype(F32))
        buf_sc[c, CONV_HALO:CONV_HALO + tm, :] = cur_ref[:, lanes].astype(F32)
    shift = CONV_HALO - (CONFORMER_WIDTH - 1)

    def rows(rc, _):
        base = pl.multiple_of(rc * CONV_ROWS, CONV_ROWS)
        for c in range(nslab):
            lanes = slice(c * 128, (c + 1) * 128)
            acc = jnp.zeros((CONV_ROWS, 128), F32) + cb_ref[:, lanes]
            for k in range(CONFORMER_WIDTH):
                acc = acc + cw_ref[k:k + 1, lanes] * buf_sc[c, pl.ds(base + (shift + k), CONV_ROWS), :]
            conv_sc[pl.ds(base, CONV_ROWS), lanes] = acc
        return 0

    lax.fori_loop(0, tm // CONV_ROWS, rows, 0)
    for r0 in range(0, tm, OUT_ROWS):
        sub = slice(r0, r0 + OUT_ROWS)
        y = conv_sc[sub, :]
        mu = jnp.mean(y, axis=-1, keepdims=True)
        yc = y - mu
        var = jnp.mean(yc * yc, axis=-1, keepdims=True)
        ln = yc * lax.rsqrt(var + NORM_EPS) * lg_ref[...] + lb_ref[...]
        o_ref[sub, :] = (_dot(_silu(ln).astype(BF16), wpw_ref[...]) * szc_ref[sub, :].astype(F32)).astype(BF16)


def _conformer(proj, conv_w, conv_b, ln_g, ln_b, w_pw, tm):
    seq = proj.shape[0]
    per = tm // CONV_HALO
    const = lambda a: pl.BlockSpec(a.shape, lambda i: (0, 0))
    return pl.pallas_call(
        functools.partial(_conformer_kernel, tm=tm),
        grid=(seq // tm,),
        in_specs=[pl.BlockSpec((tm, D_CONFORMER), lambda i: (i, 0)),
                  pl.BlockSpec((CONV_HALO, D_CONFORMER), lambda i: (jnp.maximum(i * per - 1, 0), 0)),
                  pl.BlockSpec((tm, D_CONFORMER), lambda i: (i, 1)),
                  const(conv_w), const(conv_b), const(ln_g), const(ln_b), const(w_pw)],
        out_specs=pl.BlockSpec((tm, D_CONFORMER), lambda i: (i, 0)),
        out_shape=jax.ShapeDtypeStruct((seq, D_CONFORMER), BF16),
        scratch_shapes=[pltpu.VMEM((D_CONFORMER // 128, tm + CONV_HALO, 128), F32),
                        pltpu.VMEM((tm, D_CONFORMER), F32)],
        compiler_params=_params(("arbitrary",)),
        name="conformer",
    )(proj, proj, proj, conv_w, conv_b, ln_g, ln_b, w_pw)


def _t5_bucket(dist):
    max_exact = REL_BUCKETS // 2
    d_f = jnp.maximum(dist, 1).astype(F32)
    large = max_exact + (jnp.log(d_f / max_exact) / math.log(REL_MAX_DISTANCE / max_exact)
                         * (REL_BUCKETS - max_exact)).astype(jnp.int32)
    large = jnp.minimum(large, REL_BUCKETS - 1)
    return jnp.where(dist < max_exact, dist, large)


def _attn_bias(rel_bias):
    q = ATTN_BLOCK
    nh = rel_bias.shape[1]
    kj = np.arange(2 * q)[None, None, :]
    neg = lambda n: jnp.full((nh, n), NEG_INF, F32)
    tabs = []
    for window, dilation in DILATED_PATTERNS:
        steps = window // dilation
        assert steps == q
        rel = jnp.asarray(np.arange(steps + 1) * dilation, dtype=jnp.int32)
        by_rel = rel_bias.astype(F32)[_t5_bucket(rel)].T
        g = jnp.concatenate([by_rel[:, ::-1], neg(2 * q - 1)], axis=1)
        flat = jnp.tile(g, (1, q))[:, :q * (3 * q - 1)]
        tab = flat.reshape(nh, q, 3 * q - 1)[:, :, :2 * q]
        if dilation == 1:
            tab = tab.reshape(nh, q // MERGE_DIL, MERGE_DIL, 2 * q).transpose(0, 2, 1, 3).reshape(nh, q, 2 * q)
        tab = tab * LOG2E
        tabs.append(tab)
        tabs.append(jnp.where(kj >= q, tab, NEG_INF))
    return jnp.stack(tabs, axis=1)


def _attn_kernel(q_ref, k_ref, v_ref, szd_ref, bias_ref, o_ref,
                 onat, qm, km, vm, qs1, qs2, ks0, ks1, ks2, vs0, vs1, vs2,
                 o0, o1, o2, l0, l1, l2):
    i = pl.program_id(1)
    t = ATTN_TILE
    blk = ATTN_BLOCK
    first_tile = i == 0
    qf, kf, vf = q_ref, k_ref, v_ref
    qstreams = (None, qs1, qs2)
    kstreams = (ks0, ks1, ks2)
    vstreams = (vs0, vs1, vs2)
    md = MERGE_DIL
    mspan = t // md

    def stream(src_f, src_m, dil, r):
        n = t // dil
        if dil == md:
            x = src_f[pl.ds(r, n, stride=dil), :]
            src_m[r * n:(r + 1) * n, :] = x
            return x
        return src_m[pl.ds((r % md) * mspan + r // md, n, stride=dil // md), :]

    for pidx, (_, dil) in enumerate(DILATED_PATTERNS):
        n = t // dil
        per = blk + n
        for r in range(dil):
            base = r * per
            for src_f, src_m, dst in ((kf, km, kstreams[pidx]), (vf, vm, vstreams[pidx])):
                tail = dst[base + n:base + per, :]
                dst[base:base + blk, :] = jnp.where(first_tile, jnp.zeros_like(tail), tail)
                if dil == 1:
                    dst[base + blk:base + per, :] = src_f[...].astype(BF16)
                else:
                    dst[base + blk:base + per, :] = stream(src_f, src_m, dil, r).astype(BF16)
            if dil > 1:
                qstreams[pidx][r * n:(r + 1) * n, :] = stream(qf, qm, dil, r).astype(BF16)

    ones = jnp.ones((2 * blk, HEAD_DIM), BF16)
    first = first_tile.astype(jnp.int32)

    def block(pidx, dil, r, sb, out_sc, lse_sc):
        n = t // dil
        per = blk + n
        if dil == 1:
            qb = jnp.concatenate([qf[pl.ds(sb * blk + c, blk // md, stride=md), :] for c in range(md)],
                                 axis=0).astype(BF16)
        else:
            qb = qstreams[pidx][r * n + sb * blk:r * n + (sb + 1) * blk, :]
        krows = slice(r * per + sb * blk, r * per + (sb + 2) * blk)
        kb = kstreams[pidx][krows, :]
        vb = vstreams[pidx][krows, :]
        s = lax.dot_general(qb, kb, _NT, preferred_element_type=F32)
        s = s + bias_ref[0, 2 * pidx + first if sb == 0 else 2 * pidx]
        mx = jnp.max(s, axis=-1, keepdims=True)
        p = jnp.exp2(s - mx).astype(BF16)
        pv = _dot(p, jnp.concatenate([vb, ones], axis=1))
        den = pv[:, HEAD_DIM:]
        out = pv[:, :HEAD_DIM] / den
        lse = mx + jnp.log2(den)
        if dil == 1:
            piece = blk // md
            for c in range(md):
                rows = slice(c * mspan + sb * piece, c * mspan + (sb + 1) * piece)
                out_sc[rows, :] = out[c * piece:(c + 1) * piece, :]
                lse_sc[rows, :] = lse[c * piece:(c + 1) * piece, :]
        elif dil == md:
            rows = slice(r * mspan + sb * blk, r * mspan + (sb + 1) * blk)
            out_sc[rows, :] = out
            lse_sc[rows, :] = lse
        else:
            step = dil // md
            rows = pl.ds((r % md) * mspan + sb * blk * step + r // md, blk, stride=step)
            out_sc[rows, :] = out
            lse_sc[rows, :] = lse

    scratch = ((o0, l0), (o1, l1), (o2, l2))
    for pidx, (_, dil) in enumerate(DILATED_PATTERNS):
        for sb in range(t // (blk * dil)):
            for r in range(dil):
                block(pidx, dil, r, sb, *scratch[pidx])

    la, lb, lc = l0[...], l1[...], l2[...]
    lall = jnp.maximum(jnp.maximum(la, lb), lc)
    wa, wb, wc = jnp.exp2(la - lall), jnp.exp2(lb - lall), jnp.exp2(lc - lall)
    merged = (wa * o0[...] + wb * o1[...] + wc * o2[...]) / (wa + wb + wc)
    for c in range(md):
        onat[pl.ds(c, mspan, stride=md), :] = merged[c * mspan:(c + 1) * mspan, :]
    o_ref[...] = (onat[...] * szd_ref[...].astype(F32)).astype(BF16)


def _attention(qkv, proj, bias):
    seq = proj.shape[0]
    t = ATTN_TILE
    q0, k0, v0 = 0, N_HEADS, 2 * N_HEADS
    z0 = (2 * D_CONFORMER) // HEAD_DIM
    cur = lambda c0: pl.BlockSpec((t, HEAD_DIM), lambda h, i: (i, c0 + h))
    f32 = lambda rows: pltpu.VMEM((rows, HEAD_DIM), F32)
    bf16 = lambda rows: pltpu.VMEM((rows, HEAD_DIM), BF16)
    streams = [bf16(dil * (ATTN_BLOCK + t // dil)) for _, dil in DILATED_PATTERNS]
    return pl.pallas_call(
        _attn_kernel,
        grid=(N_HEADS, seq // t),
        in_specs=[cur(q0), cur(k0), cur(v0), cur(z0),
                  pl.BlockSpec((1,) + bias.shape[1:], lambda h, i: (h, 0, 0, 0))],
        out_specs=pl.BlockSpec((t, HEAD_DIM), lambda h, i: (i, h)),
        out_shape=jax.ShapeDtypeStruct((seq, D_ATTN), BF16),
        scratch_shapes=[f32(t)] * 4 + [bf16(t)] * 2 + streams + streams + [f32(t)] * 6,
        compiler_params=_params(("arbitrary", "arbitrary")),
        name="dilated_attn",
    )(qkv, qkv, qkv, proj, bias)


def _odd_out_kernel(h_ref, yc_ref, yd_ref, wout_ref, g_ref, o_ref, *, tm):
    for r0 in range(0, tm, OUT_ROWS):
        rows = slice(r0, r0 + OUT_ROWS)
        acc = _dot(yc_ref[rows, :], wout_ref[0:D_CONFORMER, :])
        acc = acc + _dot(yd_ref[rows, :], wout_ref[D_CONFORMER:D_CONFORMER + D_ATTN, :])
        o_ref[rows, :] = _rms(h_ref[rows, :] + acc, g_ref[...])


def _odd_out(h, yc, yd, w_out, g, tm):
    seq = h.shape[0]
    row = lambda w: pl.BlockSpec((tm, w), lambda i: (i, 0))
    const = lambda a: pl.BlockSpec(a.shape, lambda i: (0, 0), pipeline_mode=pl.Buffered(1))
    return pl.pallas_call(
        functools.partial(_odd_out_kernel, tm=tm),
        grid=(seq // tm,),
        in_specs=[row(D_MODEL), row(D_CONFORMER), row(D_ATTN), const(w_out), const(g)],
        out_specs=row(D_MODEL),
        out_shape=jax.ShapeDtypeStruct((seq, D_MODEL), F32),
        compiler_params=_params(("arbitrary",)),
        name="odd_out",
    )(h, yc, yd, w_out, g)


def _row_tile(seq, want):
    return min(seq, want)


def kernel(x, norm_g, final_norm_g, ev_w_in, ev_conv_w, s5_lam_re, s5_lam_im, s5_log_dt, s5_b_re, s5_b_im, s5_c_re, s5_c_im, s5_d, s5_w_glu, ev_w_out, od_w_in, cf_conv_w, cf_conv_b, cf_ln_g, cf_ln_b, cf_w_pw, od_w_out, rel_bias):
    bsz, seq, _ = x.shape
    assert bsz == 1 and seq % ATTN_TILE == 0
    x2 = x.reshape(seq, D_MODEL)
    g0 = norm_g[0].astype(F32).reshape(1, D_MODEL)
    g1 = norm_g[1].astype(F32).reshape(1, D_MODEL)
    gf = final_norm_g.astype(F32).reshape(1, D_MODEL)

    ya, u, szb, ev_w_out_b, od_w_out_b = _even_in(
        x2, g0, ev_w_in[0].astype(BF16), ev_conv_w[0].astype(F32), ev_w_out[0].astype(F32),
        od_w_out[0].astype(F32), _row_tile(seq, EVEN_IN_TILE[0]), EVEN_IN_TILE[1])
    tables = _s5_tables(s5_lam_re[0], s5_lam_im[0], s5_log_dt[0], s5_b_re[0], s5_b_im[0],
                        s5_c_re[0], s5_c_im[0], s5_d[0])
    ys = _s5(u, tables, min(seq // SSM_CHUNK, SSM_CHUNK_ROWS))
    h1, hn1, od_w_in_b = _even_out(x2, ya, ys, szb, s5_w_glu[0].astype(BF16), ev_w_out_b, g1,
                                   od_w_in[0].astype(F32), _row_tile(seq, EVEN_OUT_ROWS))

    proj, qkv = _odd_in(hn1, od_w_in_b, _row_tile(seq, ODD_IN_TILE[0]), ODD_IN_TILE[1])
    conv_w = jnp.concatenate([cf_conv_w[0].astype(F32), jnp.zeros((1, D_CONFORMER), F32)], axis=0)
    yc = _conformer(proj, conv_w, cf_conv_b[0].astype(F32).reshape(1, -1),
                    cf_ln_g[0].astype(F32).reshape(1, -1), cf_ln_b[0].astype(F32).reshape(1, -1),
                    cf_w_pw[0].astype(BF16), _row_tile(seq, CONFORMER_ROWS))
    yd = _attention(qkv, proj, _attn_bias(rel_bias))
    out = _odd_out(h1, yc, yd, od_w_out_b, gf, _row_tile(seq, ODD_OUT_ROWS))
    return out.reshape(bsz, seq, D_MODEL)
```

```python
import functools
import math

import numpy as np
import jax
import jax.numpy as jnp
from jax import lax
from jax.experimental import pallas as pl
from jax.experimental.pallas import tpu as pltpu

F32 = jnp.float32
BF16 = jnp.bfloat16

D_MODEL = 2048
D_SHORTCONV = 1536
SHORTCONV_WIDTH = 3
D_SSM = 512
SSM_GROUP = 16
SSM_GROUPS = 32
SSM_STATE = 64
D_CONFORMER = 1024
CONFORMER_WIDTH = 31
D_ATTN = 1024
HEAD_DIM = 128
N_HEADS = 8
DILATED_PATTERNS = ((128, 1), (512, 4), (2048, 16))
ATTN_BLOCK = 128
REL_BUCKETS = 32
REL_MAX_DISTANCE = 2048
NORM_EPS = 1e-6
NEG_INF = -1e30

IN_ROWS = 256
EVEN_ROWS = 256
SSM_CHUNK = 16
SSM_FLAT = SSM_CHUNK * SSM_GROUP
SSM_SLABS = D_SSM // 128
ATTN_TILE = 4096
LOG2E = math.log2(math.e)
Q_SCALE = HEAD_DIM ** -0.5 * LOG2E
MERGE_DIL = 4
CONV_HALO = 32
VMEM_LIMIT = 50 * 1024 * 1024

EVEN_IN_TILE = (1024, 512)
EVEN_OUT_ROWS = 512
ODD_IN_TILE = (2048, 512)
CONFORMER_ROWS = 2048
ODD_OUT_ROWS = 512
SSM_CHUNK_ROWS = 128


def _params(sem, vmem=VMEM_LIMIT):
    return pltpu.CompilerParams(dimension_semantics=sem, vmem_limit_bytes=vmem)


def _dot(a, b):
    return jnp.dot(a, b, preferred_element_type=F32)


def _sigmoid(z):
    return 1.0 / (1.0 + jnp.exp(-z))


def _silu(z):
    return z * _sigmoid(z)


def _gelu_tanh(x):
    return 0.5 * x * (1.0 + jnp.tanh(math.sqrt(2.0 / math.pi) * (x + 0.044715 * (x * x * x))))


def _cast_slice_spec(a, nsteps, step_of):
    rows = a.shape[0] // nsteps
    assert rows * nsteps == a.shape[0] and rows % 16 == 0, (a.shape, nsteps)
    return pl.BlockSpec((rows, a.shape[1]), lambda *idx: (step_of(*idx), 0))


def _rms(xf, g):
    ms = jnp.mean(xf * xf, axis=-1, keepdims=True)
    return xf * lax.rsqrt(ms + NORM_EPS) * g


def _even_in_kernel(x_ref, g_ref, wa_ref, wb_ref, wc_ref, wd_ref, cw_ref, c0_ref, c1_ref,
                    ya_ref, u_ref, szb_ref, c0b_ref, c1b_ref, hn_sc, pbuf_sc, carry_sc, *, tm, col):
    i = pl.program_id(0)
    j = pl.program_id(1)
    nconv = D_SHORTCONV // col
    c0b_ref[...] = c0_ref[...].astype(BF16)
    c1b_ref[...] = c1_ref[...].astype(BF16)

    def conv_step(first):
        @pl.when(i == 0)
        def _():
            carry_sc[j] = jnp.zeros((8, col), F32)

        pbuf_sc[0:8, :] = carry_sc[j]
        cw = cw_ref[...]
        for r0 in range(0, tm, EVEN_ROWS):
            rows = slice(r0, r0 + EVEN_ROWS)
            if first:
                hn = _rms(x_ref[rows, :], g_ref[...]).astype(BF16)
                hn_sc[rows, :] = hn
            else:
                hn = hn_sc[rows, :]
            p = _dot(hn, wb_ref[...]) * _dot(hn, wa_ref[...])
            pbuf_sc[8 + r0:8 + r0 + EVEN_ROWS, :] = p
            conv = (cw[2:3, :] * p + cw[1:2, :] * pbuf_sc[pl.ds(7 + r0, EVEN_ROWS), :]
                    + cw[0:1, :] * pbuf_sc[pl.ds(6 + r0, EVEN_ROWS), :])
            gb = _dot(hn, wc_ref[...])
            za = _dot(hn, wd_ref[...])
            ya_ref[rows, :] = (gb * conv * _silu(za)).astype(BF16)
        carry_sc[j] = pbuf_sc[tm:tm + 8, :]

    pl.when(j == 0)(functools.partial(conv_step, True))
    pl.when((j > 0) & (j < nconv))(functools.partial(conv_step, False))

    @pl.when(j >= nconv)
    def _():
        for r0 in range(0, tm, EVEN_ROWS):
            rows = slice(r0, r0 + EVEN_ROWS)
            hn = hn_sc[rows, :]
            u = _dot(hn, wa_ref[...])
            for s in range(col // 128):
                u_ref[s, rows, :] = u[:, s * 128:(s + 1) * 128]
            szb_ref[rows, :] = _silu(_dot(hn, wb_ref[...])).astype(BF16)


def _even_in(x, g, w_in, conv_w, cast0, cast1, tm, col):
    seq = x.shape[0]
    nconv = D_SHORTCONV // col
    nssm = D_SSM // col
    u_blk = 4 * nconv

    def wspec(k, alt0):
        return pl.BlockSpec((D_MODEL, col),
                            lambda i, j: (0, jnp.where(j < nconv, j + nconv * k, alt0 + j - nconv)))

    hold = lambda k: pl.BlockSpec((D_MODEL, col), lambda i, j: (0, jnp.minimum(j, nconv - 1) + nconv * k))
    ssm_step = lambda j: jnp.maximum(j - nconv, 0)
    nj = nconv + nssm
    cast_spec = lambda a: _cast_slice_spec(a, (seq // tm) * nj, lambda i, j: i * nj + j)
    return pl.pallas_call(
        functools.partial(_even_in_kernel, tm=tm, col=col),
        grid=(seq // tm, nj),
        in_specs=[
            pl.BlockSpec((tm, D_MODEL), lambda i, j: (i, 0)),
            pl.BlockSpec((1, D_MODEL), lambda i, j: (0, 0)),
            wspec(0, u_blk), wspec(1, u_blk + nssm), hold(2), hold(3),
            pl.BlockSpec((SHORTCONV_WIDTH, col), lambda i, j: (0, jnp.minimum(j, nconv - 1))),
            cast_spec(cast0), cast_spec(cast1),
        ],
        out_specs=[
            pl.BlockSpec((tm, col), lambda i, j: (i, jnp.minimum(j, nconv - 1))),
            pl.BlockSpec((col // 128, tm, 128), lambda i, j: (ssm_step(j), i, 0)),
            pl.BlockSpec((tm, col), lambda i, j: (i, ssm_step(j))),
            cast_spec(cast0), cast_spec(cast1),
        ],
        out_shape=[
            jax.ShapeDtypeStruct((seq, D_SHORTCONV), BF16),
            jax.ShapeDtypeStruct((SSM_SLABS, seq, 128), F32),
            jax.ShapeDtypeStruct((seq, D_SSM), BF16),
            jax.ShapeDtypeStruct(cast0.shape, BF16),
            jax.ShapeDtypeStruct(cast1.shape, BF16),
        ],
        scratch_shapes=[
            pltpu.VMEM((tm, D_MODEL), BF16),
            pltpu.VMEM((tm + 8, col), F32),
            pltpu.VMEM((nconv, 8, col), F32),
        ],
        compiler_params=_params(("arbitrary", "arbitrary")),
        name="even_in",
    )(x, g, w_in, w_in, w_in, w_in, conv_w, cast0, cast1)


def _s5_tables(lam_re, lam_im, log_dt, b_re, b_im, c_re, c_im, d_skip):
    hi = lax.Precision.HIGH
    t_, g_, n_, p_ = SSM_CHUNK, SSM_GROUPS, SSM_STATE, SSM_GROUP
    lam_re, lam_im = lam_re.astype(F32), lam_im.astype(F32)
    dt = jnp.exp(log_dt.astype(F32))[:, None]
    xr, xi = lam_re * dt, lam_im * dt
    k = jnp.arange(t_ + 1, dtype=F32)[:, None, None]
    mag = jnp.exp(xr * k)
    pr, pi = mag * jnp.cos(xi * k), mag * jnp.sin(xi * k)
    nr, ni = pr[1] - 1.0, pi[1]
    den = lam_re * lam_re + lam_im * lam_im
    fr = (nr * lam_re + ni * lam_im) / den
    fi = (ni * lam_re - nr * lam_im) / den
    b_re, b_im = b_re.astype(F32), b_im.astype(F32)
    bbr = fr[..., None] * b_re - fi[..., None] * b_im
    bbi = fr[..., None] * b_im + fi[..., None] * b_re
    c_re, c_im = c_re.astype(F32), c_im.astype(F32)

    cpr = c_re[None] * pr[:t_, :, None, :] - c_im[None] * pi[:t_, :, None, :]
    cpi = c_re[None] * pi[:t_, :, None, :] + c_im[None] * pr[:t_, :, None, :]
    kern = jnp.einsum('tgpn,gnq->tgpq', jnp.concatenate([cpr, -cpi], axis=-1),
                      jnp.concatenate([bbr, bbi], axis=1), precision=hi)
    skip = d_skip.astype(F32).reshape(g_, p_)[:, :, None] * jnp.eye(p_, dtype=F32)[None]
    kern = kern.at[0].add(skip)
    flat = t_ * p_
    last = jnp.transpose(kern[::-1], (1, 2, 0, 3)).reshape(g_, p_, flat)
    mt = jnp.stack([jnp.pad(last[:, :, (t_ - 1 - t) * p_:], ((0, 0), (0, 0), (0, (t_ - 1 - t) * p_)))
                    for t in range(t_)], axis=1).reshape(g_, flat, flat)

    prj, pij = pr[t_ - 1::-1][:t_], pi[t_ - 1::-1][:t_]
    wr = prj[..., None] * bbr[None] - pij[..., None] * bbi[None]
    wi = prj[..., None] * bbi[None] + pij[..., None] * bbr[None]
    half = (np.arange(2) == 0)[None, None, :, None, None]

    def pack_w(w):
        w = jnp.transpose(w.reshape(t_, g_ // 2, 2, n_, p_), (1, 0, 2, 4, 3))
        w = jnp.concatenate([jnp.where(half, w, 0.0), jnp.where(half, 0.0, w)], axis=-1)
        return w.reshape(g_ // 2, 2 * flat, 2 * n_)

    vr = c_re[None] * pr[1:, :, None, :] - c_im[None] * pi[1:, :, None, :]
    vi = c_re[None] * pi[1:, :, None, :] + c_im[None] * pr[1:, :, None, :]
    vr = jnp.transpose(vr, (1, 0, 2, 3)).reshape(g_ // 2, 2, flat, n_)
    vi = -jnp.transpose(vi, (1, 0, 2, 3)).reshape(g_ // 2, 2, flat, n_)
    zv = jnp.zeros((g_ // 2, flat, n_), F32)
    v_even = jnp.concatenate([vr[:, 0], zv, vi[:, 0], zv], axis=-1)
    v_odd = jnp.concatenate([zv, vr[:, 1], zv, vi[:, 1]], axis=-1)
    vt = jnp.stack([v_even, v_odd], axis=1).reshape(g_, flat, 4 * n_)
    a_re = pr[t_].reshape(1, g_ * n_)
    a_im = pi[t_].reshape(1, g_ * n_)
    return mt.astype(BF16), pack_w(wr).astype(BF16), pack_w(wi).astype(BF16), vt.astype(BF16), a_re, a_im


_NT = (((1,), (1,)), ((), ()))
_TN = (((0,), (0,)), ((), ()))


def _s5_kernel(u_ref, mt_ref, wre_ref, wim_ref, vt_ref, are_ref, aim_ref, y_ref,
               xt_sc, yt_sc, zre_sc, zim_sc, carry_sc, *, ct):
    i = pl.program_id(0)
    t_ = SSM_CHUNK

    @pl.when(i == 0)
    def _():
        carry_sc[...] = jnp.zeros_like(carry_sc)

    for s in range(SSM_SLABS):
        for j in range(t_):
            xt_sc[s, j] = u_ref[s, pl.ds(j, ct, stride=t_), :].T.astype(BF16)

    pairs_per_slab = 128 // (2 * SSM_GROUP)
    for gp in range(SSM_GROUPS // 2):
        s, k = gp // pairs_per_slab, gp % pairs_per_slab
        xp = xt_sc[s, :, 32 * k:32 * k + 32, :].reshape(2 * SSM_FLAT, ct)
        zre_sc[:, gp * 128:(gp + 1) * 128] = lax.dot_general(xp, wre_ref[gp], _TN, preferred_element_type=F32)
        zim_sc[:, gp * 128:(gp + 1) * 128] = lax.dot_general(xp, wim_ref[gp], _TN, preferred_element_type=F32)

    are = are_ref[...]
    aim = aim_ref[...]

    def step(r, carry):
        sre, sim = carry
        zr = zre_sc[pl.ds(r, 1), :]
        zi = zim_sc[pl.ds(r, 1), :]
        zre_sc[pl.ds(r, 1), :] = sre
        zim_sc[pl.ds(r, 1), :] = sim
        return are * sre - aim * sim + zr, are * sim + aim * sre + zi

    sre, sim = lax.fori_loop(0, ct, step, (carry_sc[0:1, :], carry_sc[1:2, :]), unroll=8)
    carry_sc[0:1, :] = sre
    carry_sc[1:2, :] = sim

    groups_per_slab = 128 // SSM_GROUP
    for g in range(SSM_GROUPS):
        gp, s, k = g // 2, g // groups_per_slab, g % groups_per_slab
        xg = xt_sc[s, :, 16 * k:16 * k + 16, :].reshape(SSM_FLAT, ct)
        sp = jnp.concatenate([zre_sc[:, gp * 128:(gp + 1) * 128],
                              zim_sc[:, gp * 128:(gp + 1) * 128]], axis=1).astype(BF16)
        yt = _dot(mt_ref[g], xg) + lax.dot_general(vt_ref[g], sp, _NT, preferred_element_type=F32)
        yt_sc[s, :, 16 * k:16 * k + 16, :] = yt.reshape(t_, SSM_GROUP, ct)

    for s in range(SSM_SLABS):
        for t in range(t_):
            y_ref[s, pl.ds(t, ct, stride=t_), :] = yt_sc[s, t].T


def _s5(u4, tables, ct):
    mt, wre, wim, vt, are, aim = tables
    _, seq, _ = u4.shape
    full = lambda a: pl.BlockSpec(a.shape, lambda i: (0,) * a.ndim)
    rows = pl.BlockSpec((SSM_SLABS, SSM_CHUNK * ct, 128), lambda i: (0, i, 0))
    nstate = SSM_GROUPS * SSM_STATE
    return pl.pallas_call(
        functools.partial(_s5_kernel, ct=ct),
        grid=(seq // (SSM_CHUNK * ct),),
        in_specs=[rows, full(mt), full(wre), full(wim), full(vt), full(are), full(aim)],
        out_specs=rows,
        out_shape=jax.ShapeDtypeStruct(u4.shape, F32),
        scratch_shapes=[pltpu.VMEM((SSM_SLABS, SSM_CHUNK, 128, ct), BF16),
                        pltpu.VMEM((SSM_SLABS, SSM_CHUNK, 128, ct), F32),
                        pltpu.VMEM((ct, nstate), F32),
                        pltpu.VMEM((ct, nstate), F32),
                        pltpu.VMEM((8, nstate), F32)],
        compiler_params=_params(("arbitrary",)),
        name="s5_ssm",
    )(u4, mt, wre, wim, vt, are, aim)


OUT_ROWS = 256


def _even_out_kernel(x_ref, ya_ref, ys_ref, szb_ref, wglu_ref, wout_ref, g_ref, c0_ref,
                     h_ref, hn_ref, c0b_ref, *, tm):
    c0b_ref[...] = c0_ref[...].astype(BF16)
    for r0 in range(0, tm, OUT_ROWS):
        rows = slice(r0, r0 + OUT_ROWS)
        gl = _gelu_tanh(jnp.concatenate([ys_ref[s, rows, :] for s in range(SSM_SLABS)], axis=1))
        yb = gl * _sigmoid(_dot(gl.astype(BF16), wglu_ref[...])) * szb_ref[rows, :].astype(F32)
        acc = _dot(ya_ref[rows, :], wout_ref[0:D_SHORTCONV, :])
        acc = acc + _dot(yb.astype(BF16), wout_ref[D_SHORTCONV:D_SHORTCONV + D_SSM, :])
        h = x_ref[rows, :] + acc
        h_ref[rows, :] = h
        hn_ref[rows, :] = _rms(h, g_ref[...]).astype(BF16)


def _even_out(x, ya, ys, szb, w_glu, w_out, g, cast0, tm):
    seq = x.shape[0]
    cast_spec = _cast_slice_spec(cast0, seq // tm, lambda i: i)
    row = lambda w: pl.BlockSpec((tm, w), lambda i: (i, 0))
    const = lambda a: pl.BlockSpec(a.shape, lambda i: (0, 0), pipeline_mode=pl.Buffered(1))
    return pl.pallas_call(
        functools.partial(_even_out_kernel, tm=tm),
        grid=(seq // tm,),
        in_specs=[row(D_MODEL), row(D_SHORTCONV),
                  pl.BlockSpec((SSM_SLABS, tm, 128), lambda i: (0, i, 0)), row(D_SSM),
                  const(w_glu), const(w_out), const(g), cast_spec],
        out_specs=[row(D_MODEL), row(D_MODEL), cast_spec],
        out_shape=[jax.ShapeDtypeStruct((seq, D_MODEL), F32),
                   jax.ShapeDtypeStruct((seq, D_MODEL), BF16),
                   jax.ShapeDtypeStruct(cast0.shape, BF16)],
        compiler_params=_params(("arbitrary",)),
        name="even_out",
    )(x, ya, ys, szb, w_glu, w_out, g, cast0)


ODD_GROUPS = 6


def _odd_in_kernel(hn_ref, wa_ref, wb_ref, o_ref, qkv_ref, *, tm, col):
    s = pl.program_id(1)
    nb = D_CONFORMER // col
    group = s // nb
    is_silu = (group == 1) | (group == 5)
    blocks = [slice(r0, r0 + IN_ROWS) for r0 in range(0, tm, IN_ROWS)]

    @pl.when(group == 0)
    def _():
        for rows in blocks:
            hn = hn_ref[rows, :]
            o_ref[rows, :] = (_dot(hn, wa_ref[...]) * _sigmoid(_dot(hn, wb_ref[...]))).astype(BF16)

    @pl.when(is_silu)
    def _():
        for rows in blocks:
            o_ref[rows, :] = _silu(_dot(hn_ref[rows, :], wa_ref[...])).astype(BF16)

    @pl.when((group > 0) & jnp.logical_not(is_silu))
    def _():
        scale = jnp.where(group == 2, Q_SCALE, 1.0).astype(F32)
        for rows in blocks:
            qkv_ref[rows, :] = _dot(hn_ref[rows, :], wa_ref[...]) * scale


def _odd_in(hn, w_in, tm, col):
    seq = hn.shape[0]
    nb = D_CONFORMER // col
    return pl.pallas_call(
        functools.partial(_odd_in_kernel, tm=tm, col=col),
        grid=(seq // tm, ODD_GROUPS * nb),
        in_specs=[pl.BlockSpec((tm, D_MODEL), lambda i, s: (i, 0)),
                  pl.BlockSpec((D_MODEL, col), lambda i, s: (0, jnp.where(s < nb, s, s + nb))),
                  pl.BlockSpec((D_MODEL, col), lambda i, s: (0, jnp.where(s < nb, s + nb, 2 * nb - 1)))],
        out_specs=[pl.BlockSpec((tm, col), lambda i, s: (i, jnp.where(s < 2 * nb, s, jnp.where(
                       s < 5 * nb, 2 * nb - 1, s - 3 * nb)))),
                   pl.BlockSpec((tm, col), lambda i, s: (i, jnp.clip(s - 2 * nb, 0, 3 * nb - 1)))],
        out_shape=[jax.ShapeDtypeStruct((seq, 3 * D_CONFORMER), BF16),
                   jax.ShapeDtypeStruct((seq, 3 * D_ATTN), F32)],
        compiler_params=_params(("arbitrary", "arbitrary")),
        name="odd_in",
    )(hn, w_in, w_in)


CONV_ROWS = 256


def _conformer_kernel(cur_ref, halo_ref, szc_ref, cw_ref, cb_ref, lg_ref, lb_ref, wpw_ref,
                      o_ref, buf_sc, conv_sc, *, tm):
    i = pl.program_id(0)
    nslab = D_CONFORMER // 128
    for c in range(nslab):
        lanes = slice(c * 128, (c + 1) * 128)
        buf_sc[c, 0:CONV_HALO, :] = jnp.where(i == 0, 0.0, halo_ref[:, lanes].astype(F32))
        buf_sc[c, CONV_HALO:CONV_HALO + tm, :] = cur_ref[:, lanes].astype(F32)
    shift = CONV_HALO - (CONFORMER_WIDTH - 1)

    def rows(rc, _):
        base = pl.multiple_of(rc * CONV_ROWS, CONV_ROWS)
        for c in range(nslab):
            lanes = slice(c * 128, (c + 1) * 128)
            acc = jnp.zeros((CONV_ROWS, 128), F32) + cb_ref[:, lanes]
            for k in range(CONFORMER_WIDTH):
                acc = acc + cw_ref[k:k + 1, lanes] * buf_sc[c, pl.ds(base + (shift + k), CONV_ROWS), :]
            conv_sc[pl.ds(base, CONV_ROWS), lanes] = acc
        return 0

    lax.fori_loop(0, tm // CONV_ROWS, rows, 0)
    for r0 in range(0, tm, OUT_ROWS):
        sub = slice(r0, r0 + OUT_ROWS)
        y = conv_sc[sub, :]
        mu = jnp.mean(y, axis=-1, keepdims=True)
        yc = y - mu
        var = jnp.mean(yc * yc, axis=-1, keepdims=True)
        ln = yc * lax.rsqrt(var + NORM_EPS) * lg_ref[...] + lb_ref[...]
        o_ref[sub, :] = (_dot(_silu(ln).astype(BF16), wpw_ref[...]) * szc_ref[sub, :].astype(F32)).astype(BF16)


def _conformer(proj, conv_w, conv_b, ln_g, ln_b, w_pw, tm):
    seq = proj.shape[0]
    per = tm // CONV_HALO
    const = lambda a: pl.BlockSpec(a.shape, lambda i: (0, 0))
    return pl.pallas_call(
        functools.partial(_conformer_kernel, tm=tm),
        grid=(seq // tm,),
        in_specs=[pl.BlockSpec((tm, D_CONFORMER), lambda i: (i, 0)),
                  pl.BlockSpec((CONV_HALO, D_CONFORMER), lambda i: (jnp.maximum(i * per - 1, 0), 0)),
                  pl.BlockSpec((tm, D_CONFORMER), lambda i: (i, 1)),
                  const(conv_w), const(conv_b), const(ln_g), const(ln_b), const(w_pw)],
        out_specs=pl.BlockSpec((tm, D_CONFORMER), lambda i: (i, 0)),
        out_shape=jax.ShapeDtypeStruct((seq, D_CONFORMER), BF16),
        scratch_shapes=[pltpu.VMEM((D_CONFORMER // 128, tm + CONV_HALO, 128), F32),
                        pltpu.VMEM((tm, D_CONFORMER), F32)],
        compiler_params=_params(("arbitrary",)),
        name="conformer",
    )(proj, proj, proj, conv_w, conv_b, ln_g, ln_b, w_pw)


def _t5_bucket(dist):
    max_exact = REL_BUCKETS // 2
    d_f = jnp.maximum(dist, 1).astype(F32)
    large = max_exact + (jnp.log(d_f / max_exact) / math.log(REL_MAX_DISTANCE / max_exact)
                         * (REL_BUCKETS - max_exact)).astype(jnp.int32)
    large = jnp.minimum(large, REL_BUCKETS - 1)
    return jnp.where(dist < max_exact, dist, large)


def _attn_bias(rel_bias):
    q = ATTN_BLOCK
    nh = rel_bias.shape[1]
    kj = np.arange(2 * q)[None, None, :]
    neg = lambda n: jnp.full((nh, n), NEG_INF, F32)
    tabs = []
    for window, dilation in DILATED_PATTERNS:
        steps = window // dilation
        assert steps == q
        rel = jnp.asarray(np.arange(steps + 1) * dilation, dtype=jnp.int32)
        by_rel = rel_bias.astype(F32)[_t5_bucket(rel)].T
        g = jnp.concatenate([by_rel[:, ::-1], neg(2 * q - 1)], axis=1)
        flat = jnp.tile(g, (1, q))[:, :q * (3 * q - 1)]
        tab = flat.reshape(nh, q, 3 * q - 1)[:, :, :2 * q]
        if dilation == 1:
            tab = tab.reshape(nh, q // MERGE_DIL, MERGE_DIL, 2 * q).transpose(0, 2, 1, 3).reshape(nh, q, 2 * q)
        tab = tab * LOG2E
        tabs.append(tab)
        tabs.append(jnp.where(kj >= q, tab, NEG_INF))
    return jnp.stack(tabs, axis=1)


def _attn_kernel(q_ref, k_ref, v_ref, szd_ref, bias_ref, o_ref,
                 onat, qm, km, vm, qs1, qs2, ks0, ks1, ks2, vs0, vs1, vs2,
                 o0, o1, o2, l0, l1, l2):
    i = pl.program_id(1)
    t = ATTN_TILE
    blk = ATTN_BLOCK
    first_tile = i == 0
    qf, kf, vf = q_ref, k_ref, v_ref
    qstreams = (None, qs1, qs2)
    kstreams = (ks0, ks1, ks2)
    vstreams = (vs0, vs1, vs2)
    md = MERGE_DIL
    mspan = t // md

    def stream(src_f, src_m, dil, r):
        n = t // dil
        if dil == md:
            x = src_f[pl.ds(r, n, stride=dil), :]
            src_m[r * n:(r + 1) * n, :] = x
            return x
        return src_m[pl.ds((r % md) * mspan + r // md, n, stride=dil // md), :]

    for pidx, (_, dil) in enumerate(DILATED_PATTERNS):
        n = t // dil
        per = blk + n
        for r in range(dil):
            base = r * per
            for src_f, src_m, dst in ((kf, km, kstreams[pidx]), (vf, vm, vstreams[pidx])):
                tail = dst[base + n:base + per, :]
                dst[base:base + blk, :] = jnp.where(first_tile, jnp.zeros_like(tail), tail)
                if dil == 1:
                    dst[base + blk:base + per, :] = src_f[...].astype(BF16)
                else:
                    dst[base + blk:base + per, :] = stream(src_f, src_m, dil, r).astype(BF16)
            if dil > 1:
                qstreams[pidx][r * n:(r + 1) * n, :] = stream(qf, qm, dil, r).astype(BF16)

    ones = jnp.ones((2 * blk, HEAD_DIM), BF16)
    first = first_tile.astype(jnp.int32)

    def block(pidx, dil, r, sb, out_sc, lse_sc):
        n = t // dil
        per = blk + n
        if dil == 1:
            qb = jnp.concatenate([qf[pl.ds(sb * blk + c, blk // md, stride=md), :] for c in range(md)],
                                 axis=0).astype(BF16)
        else:
            qb = qstreams[pidx][r * n + sb * blk:r * n + (sb + 1) * blk, :]
        krows = slice(r * per + sb * blk, r * per + (sb + 2) * blk)
        kb = kstreams[pidx][krows, :]
        vb = vstreams[pidx][krows, :]
        s = lax.dot_general(qb, kb, _NT, preferred_element_type=F32)
        s = s + bias_ref[0, 2 * pidx + first if sb == 0 else 2 * pidx]
        mx = jnp.max(s, axis=-1, keepdims=True)
        p = jnp.exp2(s - mx).astype(BF16)
        pv = _dot(p, jnp.concatenate([vb, ones], axis=1))
        den = pv[:, HEAD_DIM:]
        out = pv[:, :HEAD_DIM] / den
        lse = mx + jnp.log2(den)
        if dil == 1:
            piece = blk // md
            for c in range(md):
                rows = slice(c * mspan + sb * piece, c * mspan + (sb + 1) * piece)
                out_sc[rows, :] = out[c * piece:(c + 1) * piece, :]
                lse_sc[rows, :] = lse[c * piece:(c + 1) * piece, :]
        elif dil == md:
            rows = slice(r * mspan + sb * blk, r * mspan + (sb + 1) * blk)
            out_sc[rows, :] = out
            lse_sc[rows, :] = lse
        else:
            step = dil // md
            rows = pl.ds((r % md) * mspan + sb * blk * step + r // md, blk, stride=step)
            out_sc[rows, :] = out
            lse_sc[rows, :] = lse

    scratch = ((o0, l0), (o1, l1), (o2, l2))
    for pidx, (_, dil) in enumerate(DILATED_PATTERNS):
        for sb in range(t // (blk * dil)):
            for r in range(dil):
                block(pidx, dil, r, sb, *scratch[pidx])

    la, lb, lc = l0[...], l1[...], l2[...]
    lall = jnp.maximum(jnp.maximum(la, lb), lc)
    wa, wb, wc = jnp.exp2(la - lall), jnp.exp2(lb - lall), jnp.exp2(lc - lall)
    merged = (wa * o0[...] + wb * o1[...] + wc * o2[...]) / (wa + wb + wc)
    for c in range(md):
        onat[pl.ds(c, mspan, stride=md), :] = merged[c * mspan:(c + 1) * mspan, :]
    o_ref[...] = (onat[...] * szd_ref[...].astype(F32)).astype(BF16)


def _attention(qkv, proj, bias):
    seq = proj.shape[0]
    t = ATTN_TILE
    q0, k0, v0 = 0, N_HEADS, 2 * N_HEADS
    z0 = (2 * D_CONFORMER) // HEAD_DIM
    cur = lambda c0: pl.BlockSpec((t, HEAD_DIM), lambda h, i: (i, c0 + h))
    f32 = lambda rows: pltpu.VMEM((rows, HEAD_DIM), F32)
    bf16 = lambda rows: pltpu.VMEM((rows, HEAD_DIM), BF16)
    streams = [bf16(dil * (ATTN_BLOCK + t // dil)) for _, dil in DILATED_PATTERNS]
    return pl.pallas_call(
        _attn_kernel,
        grid=(N_HEADS, seq // t),
        in_specs=[cur(q0), cur(k0), cur(v0), cur(z0),
                  pl.BlockSpec((1,) + bias.shape[1:], lambda h, i: (h, 0, 0, 0))],
        out_specs=pl.BlockSpec((t, HEAD_DIM), lambda h, i: (i, h)),
        out_shape=jax.ShapeDtypeStruct((seq, D_ATTN), BF16),
        scratch_shapes=[f32(t)] * 4 + [bf16(t)] * 2 + streams + streams + [f32(t)] * 6,
        compiler_params=_params(("arbitrary", "arbitrary")),
        name="dilated_attn",
    )(qkv, qkv, qkv, proj, bias)


def _odd_out_kernel(h_ref, yc_ref, yd_ref, wout_ref, g_ref, o_ref, *, tm):
    for r0 in range(0, tm, OUT_ROWS):
        rows = slice(r0, r0 + OUT_ROWS)
        acc = _dot(yc_ref[rows, :], wout_ref[0:D_CONFORMER, :])
        acc = acc + _dot(yd_ref[rows, :], wout_ref[D_CONFORMER:D_CONFORMER + D_ATTN, :])
        o_ref[rows, :] = _rms(h_ref[rows, :] + acc, g_ref[...])


def _odd_out(h, yc, yd, w_out, g, tm):
    seq = h.shape[0]
    row = lambda w: pl.BlockSpec((tm, w), lambda i: (i, 0))
    const = lambda a: pl.BlockSpec(a.shape, lambda i: (0, 0), pipeline_mode=pl.Buffered(1))
    return pl.pallas_call(
        functools.partial(_odd_out_kernel, tm=tm),
        grid=(seq // tm,),
        in_specs=[row(D_MODEL), row(D_CONFORMER), row(D_ATTN), const(w_out), const(g)],
        out_specs=row(D_MODEL),
        out_shape=jax.ShapeDtypeStruct((seq, D_MODEL), F32),
        compiler_params=_params(("arbitrary",)),
        name="odd_out",
    )(h, yc, yd, w_out, g)


def _row_tile(seq, want):
    return min(seq, want)


def kernel(x, norm_g, final_norm_g, ev_w_in, ev_conv_w, s5_lam_re, s5_lam_im, s5_log_dt, s5_b_re, s5_b_im, s5_c_re, s5_c_im, s5_d, s5_w_glu, ev_w_out, od_w_in, cf_conv_w, cf_conv_b, cf_ln_g, cf_ln_b, cf_w_pw, od_w_out, rel_bias):
    bsz, seq, _ = x.shape
    assert bsz == 1 and seq % ATTN_TILE == 0
    x2 = x.reshape(seq, D_MODEL)
    g0 = norm_g[0].astype(F32).reshape(1, D_MODEL)
    g1 = norm_g[1].astype(F32).reshape(1, D_MODEL)
    gf = final_norm_g.astype(F32).reshape(1, D_MODEL)

    ya, u, szb, ev_w_out_b, od_w_out_b = _even_in(
        x2, g0, ev_w_in[0].astype(BF16), ev_conv_w[0].astype(F32), ev_w_out[0].astype(F32),
        od_w_out[0].astype(F32), _row_tile(seq, EVEN_IN_TILE[0]), EVEN_IN_TILE[1])
    tables = _s5_tables(s5_lam_re[0], s5_lam_im[0], s5_log_dt[0], s5_b_re[0], s5_b_im[0],
                        s5_c_re[0], s5_c_im[0], s5_d[0])
    ys = _s5(u, tables, min(seq // SSM_CHUNK, SSM_CHUNK_ROWS))
    h1, hn1, od_w_in_b = _even_out(x2, ya, ys, szb, s5_w_glu[0].astype(BF16), ev_w_out_b, g1,
                                   od_w_in[0].astype(F32), _row_tile(seq, EVEN_OUT_ROWS))

    proj, qkv = _odd_in(hn1, od_w_in_b, _row_tile(seq, ODD_IN_TILE[0]), ODD_IN_TILE[1])
    conv_w = jnp.concatenate([cf_conv_w[0].astype(F32), jnp.zeros((1, D_CONFORMER), F32)], axis=0)
    yc = _conformer(proj, conv_w, cf_conv_b[0].astype(F32).reshape(1, -1),
                    cf_ln_g[0].astype(F32).reshape(1, -1), cf_ln_b[0].astype(F32).reshape(1, -1),
                    cf_w_pw[0].astype(BF16), _row_tile(seq, CONFORMER_ROWS))
    yd = _attention(qkv, proj, _attn_bias(rel_bias))
    out = _odd_out(h1, yc, yd, od_w_out_b, gf, _row_tile(seq, ODD_OUT_ROWS))
    return out.reshape(bsz, seq, D_MODEL)
```

```python
import functools
import math

import numpy as np
import jax
import jax.numpy as jnp
from jax import lax
from jax.experimental import pallas as pl
from jax.experimental.pallas import tpu as pltpu

F32 = jnp.float32
BF16 = jnp.bfloat16

D_MODEL = 2048
D_SHORTCONV = 1536
SHORTCONV_WIDTH = 3
D_SSM = 512
SSM_GROUP = 16
SSM_GROUPS = 32
SSM_STATE = 64
D_CONFORMER = 1024
CONFORMER_WIDTH = 31
D_ATTN = 1024
HEAD_DIM = 128
N_HEADS = 8
DILATED_PATTERNS = ((128, 1), (512, 4), (2048, 16))
ATTN_BLOCK = 128
REL_BUCKETS = 32
REL_MAX_DISTANCE = 2048
NORM_EPS = 1e-6
NEG_INF = -1e30

IN_ROWS = 128
EVEN_ROWS = 256
SSM_CHUNK = 16
SSM_FLAT = SSM_CHUNK * SSM_GROUP
SSM_SLABS = D_SSM // 128
ATTN_TILE = 4096
LOG2E = math.log2(math.e)
Q_SCALE = HEAD_DIM ** -0.5 * LOG2E
MERGE_DIL = 4
CONV_HALO = 32
VMEM_LIMIT = 50 * 1024 * 1024

EVEN_IN_TILE = (1024, 512)
EVEN_OUT_ROWS = 512
ODD_IN_TILE = (2048, 512)
CONFORMER_ROWS = 1024
ODD_OUT_ROWS = 512
SSM_CHUNK_ROWS = 128


def _params(sem, vmem=VMEM_LIMIT):
    return pltpu.CompilerParams(dimension_semantics=sem, vmem_limit_bytes=vmem)


def _dot(a, b):
    return jnp.dot(a, b, preferred_element_type=F32)


def _sigmoid(z):
    return 1.0 / (1.0 + jnp.exp(-z))


def _silu(z):
    return z * _sigmoid(z)


def _gelu_tanh(x):
    return 0.5 * x * (1.0 + jnp.tanh(math.sqrt(2.0 / math.pi) * (x + 0.044715 * (x * x * x))))


def _cast_slice_spec(a, nsteps, step_of):
    rows = a.shape[0] // nsteps
    assert rows * nsteps == a.shape[0] and rows % 16 == 0, (a.shape, nsteps)
    return pl.BlockSpec((rows, a.shape[1]), lambda *idx: (step_of(*idx), 0))


def _rms(xf, g):
    ms = jnp.mean(xf * xf, axis=-1, keepdims=True)
    return xf * lax.rsqrt(ms + NORM_EPS) * g


def _even_in_kernel(x_ref, g_ref, wa_ref, wb_ref, wc_ref, wd_ref, cw_ref, c0_ref, c1_ref,
                    ya_ref, u_ref, szb_ref, c0b_ref, c1b_ref, hn_sc, pbuf_sc, carry_sc, *, tm, col):
    i = pl.program_id(0)
    j = pl.program_id(1)
    nconv = D_SHORTCONV // col
    c0b_ref[...] = c0_ref[...].astype(BF16)
    c1b_ref[...] = c1_ref[...].astype(BF16)

    def conv_step(first):
        @pl.when(i == 0)
        def _():
            carry_sc[j] = jnp.zeros((8, col), F32)

        pbuf_sc[0:8, :] = carry_sc[j]
        cw = cw_ref[...]
        for r0 in range(0, tm, EVEN_ROWS):
            rows = slice(r0, r0 + EVEN_ROWS)
            if first:
                hn = _rms(x_ref[rows, :], g_ref[...]).astype(BF16)
                hn_sc[rows, :] = hn
            else:
                hn = hn_sc[rows, :]
            p = _dot(hn, wb_ref[...]) * _dot(hn, wa_ref[...])
            pbuf_sc[8 + r0:8 + r0 + EVEN_ROWS, :] = p
            conv = (cw[2:3, :] * p + cw[1:2, :] * pbuf_sc[pl.ds(7 + r0, EVEN_ROWS), :]
                    + cw[0:1, :] * pbuf_sc[pl.ds(6 + r0, EVEN_ROWS), :])
            gb = _dot(hn, wc_ref[...])
            za = _dot(hn, wd_ref[...])
            ya_ref[rows, :] = (gb * conv * _silu(za)).astype(BF16)
        carry_sc[j] = pbuf_sc[tm:tm + 8, :]

    pl.when(j == 0)(functools.partial(conv_step, True))
    pl.when((j > 0) & (j < nconv))(functools.partial(conv_step, False))

    @pl.when(j >= nconv)
    def _():
        for r0 in range(0, tm, EVEN_ROWS):
            rows = slice(r0, r0 + EVEN_ROWS)
            hn = hn_sc[rows, :]
            u = _dot(hn, wa_ref[...])
            for s in range(col // 128):
                u_ref[s, rows, :] = u[:, s * 128:(s + 1) * 128]
            szb_ref[rows, :] = _silu(_dot(hn, wb_ref[...])).astype(BF16)


def _even_in(x, g, w_in, conv_w, cast0, cast1, tm, col):
    seq = x.shape[0]
    nconv = D_SHORTCONV // col
    nssm = D_SSM // col
    u_blk = 4 * nconv

    def wspec(k, alt0):
        return pl.BlockSpec((D_MODEL, col),
                            lambda i, j: (0, jnp.where(j < nconv, j + nconv * k, alt0 + j - nconv)))

    hold = lambda k: pl.BlockSpec((D_MODEL, col), lambda i, j: (0, jnp.minimum(j, nconv - 1) + nconv * k))
    ssm_step = lambda j: jnp.maximum(j - nconv, 0)
    nj = nconv + nssm
    cast_spec = lambda a: _cast_slice_spec(a, (seq // tm) * nj, lambda i, j: i * nj + j)
    return pl.pallas_call(
        functools.partial(_even_in_kernel, tm=tm, col=col),
        grid=(seq // tm, nj),
        in_specs=[
            pl.BlockSpec((tm, D_MODEL), lambda i, j: (i, 0)),
            pl.BlockSpec((1, D_MODEL), lambda i, j: (0, 0)),
            wspec(0, u_blk), wspec(1, u_blk + nssm), hold(2), hold(3),
            pl.BlockSpec((SHORTCONV_WIDTH, col), lambda i, j: (0, jnp.minimum(j, nconv - 1))),
            cast_spec(cast0), cast_spec(cast1),
        ],
        out_specs=[
            pl.BlockSpec((tm, col), lambda i, j: (i, jnp.minimum(j, nconv - 1))),
            pl.BlockSpec((col // 128, tm, 128), lambda i, j: (ssm_step(j), i, 0)),
            pl.BlockSpec((tm, col), lambda i, j: (i, ssm_step(j))),
            cast_spec(cast0), cast_spec(cast1),
        ],
        out_shape=[
            jax.ShapeDtypeStruct((seq, D_SHORTCONV), BF16),
            jax.ShapeDtypeStruct((SSM_SLABS, seq, 128), F32),
            jax.ShapeDtypeStruct((seq, D_SSM), BF16),
            jax.ShapeDtypeStruct(cast0.shape, BF16),
            jax.ShapeDtypeStruct(cast1.shape, BF16),
        ],
        scratch_shapes=[
            pltpu.VMEM((tm, D_MODEL), BF16),
            pltpu.VMEM((tm + 8, col), F32),
            pltpu.VMEM((nconv, 8, col), F32),
        ],
        compiler_params=_params(("arbitrary", "arbitrary")),
        name="even_in",
    )(x, g, w_in, w_in, w_in, w_in, conv_w, cast0, cast1)


def _s5_tables(lam_re, lam_im, log_dt, b_re, b_im, c_re, c_im, d_skip):
    hi = lax.Precision.HIGH
    t_, g_, n_, p_ = SSM_CHUNK, SSM_GROUPS, SSM_STATE, SSM_GROUP
    lam_re, lam_im = lam_re.astype(F32), lam_im.astype(F32)
    dt = jnp.exp(log_dt.astype(F32))[:, None]
    xr, xi = lam_re * dt, lam_im * dt
    k = jnp.arange(t_ + 1, dtype=F32)[:, None, None]
    mag = jnp.exp(xr * k)
    pr, pi = mag * jnp.cos(xi * k), mag * jnp.sin(xi * k)
    nr, ni = pr[1] - 1.0, pi[1]
    den = lam_re * lam_re + lam_im * lam_im
    fr = (nr * lam_re + ni * lam_im) / den
    fi = (ni * lam_re - nr * lam_im) / den
    b_re, b_im = b_re.astype(F32), b_im.astype(F32)
    bbr = fr[..., None] * b_re - fi[..., None] * b_im
    bbi = fr[..., None] * b_im + fi[..., None] * b_re
    c_re, c_im = c_re.astype(F32), c_im.astype(F32)

    cpr = c_re[None] * pr[:t_, :, None, :] - c_im[None] * pi[:t_, :, None, :]
    cpi = c_re[None] * pi[:t_, :, None, :] + c_im[None] * pr[:t_, :, None, :]
    kern = jnp.einsum('tgpn,gnq->tgpq', jnp.concatenate([cpr, -cpi], axis=-1),
                      jnp.concatenate([bbr, bbi], axis=1), precision=hi)
    skip = d_skip.astype(F32).reshape(g_, p_)[:, :, None] * jnp.eye(p_, dtype=F32)[None]
    kern = kern.at[0].add(skip)
    flat = t_ * p_
    last = jnp.transpose(kern[::-1], (1, 2, 0, 3)).reshape(g_, p_, flat)
    mt = jnp.stack([jnp.pad(last[:, :, (t_ - 1 - t) * p_:], ((0, 0), (0, 0), (0, (t_ - 1 - t) * p_)))
                    for t in range(t_)], axis=1).reshape(g_, flat, flat)

    prj, pij = pr[t_ - 1::-1][:t_], pi[t_ - 1::-1][:t_]
    wr = prj[..., None] * bbr[None] - pij[..., None] * bbi[None]
    wi = prj[..., None] * bbi[None] + pij[..., None] * bbr[None]
    half = (np.arange(2) == 0)[None, None, :, None, None]

    def pack_w(w):
        w = jnp.transpose(w.reshape(t_, g_ // 2, 2, n_, p_), (1, 0, 2, 4, 3))
        w = jnp.concatenate([jnp.where(half, w, 0.0), jnp.where(half, 0.0, w)], axis=-1)
        return w.reshape(g_ // 2, 2 * flat, 2 * n_)

    vr = c_re[None] * pr[1:, :, None, :] - c_im[None] * pi[1:, :, None, :]
    vi = c_re[None] * pi[1:, :, None, :] + c_im[None] * pr[1:, :, None, :]
    vr = jnp.transpose(vr, (1, 0, 2, 3)).reshape(g_ // 2, 2, flat, n_)
    vi = -jnp.transpose(vi, (1, 0, 2, 3)).reshape(g_ // 2, 2, flat, n_)
    zv = jnp.zeros((g_ // 2, flat, n_), F32)
    v_even = jnp.concatenate([vr[:, 0], zv, vi[:, 0], zv], axis=-1)
    v_odd = jnp.concatenate([zv, vr[:, 1], zv, vi[:, 1]], axis=-1)
    vt = jnp.stack([v_even, v_odd], axis=1).reshape(g_, flat, 4 * n_)
    a_re = pr[t_].reshape(1, g_ * n_)
    a_im = pi[t_].reshape(1, g_ * n_)
    return mt.astype(BF16), pack_w(wr).astype(BF16), pack_w(wi).astype(BF16), vt.astype(BF16), a_re, a_im


_NT = (((1,), (1,)), ((), ()))
_TN = (((0,), (0,)), ((), ()))


def _s5_kernel(u_ref, mt_ref, wre_ref, wim_ref, vt_ref, are_ref, aim_ref, y_ref,
               xt_sc, yt_sc, zre_sc, zim_sc, carry_sc, *, ct):
    i = pl.program_id(0)
    t_ = SSM_CHUNK

    @pl.when(i == 0)
    def _():
        carry_sc[...] = jnp.zeros_like(carry_sc)

    for s in range(SSM_SLABS):
        for j in range(t_):
            xt_sc[s, j] = u_ref[s, pl.ds(j, ct, stride=t_), :].T.astype(BF16)

    pairs_per_slab = 128 // (2 * SSM_GROUP)
    for gp in range(SSM_GROUPS // 2):
        s, k = gp // pairs_per_slab, gp % pairs_per_slab
        xp = xt_sc[s, :, 32 * k:32 * k + 32, :].reshape(2 * SSM_FLAT, ct)
        zre_sc[:, gp * 128:(gp + 1) * 128] = lax.dot_general(xp, wre_ref[gp], _TN, preferred_element_type=F32)
        zim_sc[:, gp * 128:(gp + 1) * 128] = lax.dot_general(xp, wim_ref[gp], _TN, preferred_element_type=F32)

    are = are_ref[...]
    aim = aim_ref[...]

    def step(r, carry):
        sre, sim = carry
        zr = zre_sc[pl.ds(r, 1), :]
        zi = zim_sc[pl.ds(r, 1), :]
        zre_sc[pl.ds(r, 1), :] = sre
        zim_sc[pl.ds(r, 1), :] = sim
        return are * sre - aim * sim + zr, are * sim + aim * sre + zi

    sre, sim = lax.fori_loop(0, ct, step, (carry_sc[0:1, :], carry_sc[1:2, :]), unroll=8)
    carry_sc[0:1, :] = sre
    carry_sc[1:2, :] = sim

    groups_per_slab = 128 // SSM_GROUP
    for g in range(SSM_GROUPS):
        gp, s, k = g // 2, g // groups_per_slab, g % groups_per_slab
        xg = xt_sc[s, :, 16 * k:16 * k + 16, :].reshape(SSM_FLAT, ct)
        sp = jnp.concatenate([zre_sc[:, gp * 128:(gp + 1) * 128],
                              zim_sc[:, gp * 128:(gp + 1) * 128]], axis=1).astype(BF16)
        yt = _dot(mt_ref[g], xg) + lax.dot_general(vt_ref[g], sp, _NT, preferred_element_type=F32)
        yt_sc[s, :, 16 * k:16 * k + 16, :] = yt.reshape(t_, SSM_GROUP, ct)

    for s in range(SSM_SLABS):
        for t in range(t_):
            y_ref[s, pl.ds(t, ct, stride=t_), :] = yt_sc[s, t].T


def _s5(u4, tables, ct):
    mt, wre, wim, vt, are, aim = tables
    _, seq, _ = u4.shape
    full = lambda a: pl.BlockSpec(a.shape, lambda i: (0,) * a.ndim)
    rows = pl.BlockSpec((SSM_SLABS, SSM_CHUNK * ct, 128), lambda i: (0, i, 0))
    nstate = SSM_GROUPS * SSM_STATE
    return pl.pallas_call(
        functools.partial(_s5_kernel, ct=ct),
        grid=(seq // (SSM_CHUNK * ct),),
        in_specs=[rows, full(mt), full(wre), full(wim), full(vt), full(are), full(aim)],
        out_specs=rows,
        out_shape=jax.ShapeDtypeStruct(u4.shape, F32),
        scratch_shapes=[pltpu.VMEM((SSM_SLABS, SSM_CHUNK, 128, ct), BF16),
                        pltpu.VMEM((SSM_SLABS, SSM_CHUNK, 128, ct), F32),
                        pltpu.VMEM((ct, nstate), F32),
                        pltpu.VMEM((ct, nstate), F32),
                        pltpu.VMEM((8, nstate), F32)],
        compiler_params=_params(("arbitrary",)),
        name="s5_ssm",
    )(u4, mt, wre, wim, vt, are, aim)


OUT_ROWS = 256


def _even_out_kernel(x_ref, ya_ref, ys_ref, szb_ref, wglu_ref, wout_ref, g_ref, c0_ref,
                     h_ref, hn_ref, c0b_ref, *, tm):
    c0b_ref[...] = c0_ref[...].astype(BF16)
    for r0 in range(0, tm, OUT_ROWS):
        rows = slice(r0, r0 + OUT_ROWS)
        gl = _gelu_tanh(jnp.concatenate([ys_ref[s, rows, :] for s in range(SSM_SLABS)], axis=1))
        yb = gl * _sigmoid(_dot(gl.astype(BF16), wglu_ref[...])) * szb_ref[rows, :].astype(F32)
        acc = _dot(ya_ref[rows, :], wout_ref[0:D_SHORTCONV, :])
        acc = acc + _dot(yb.astype(BF16), wout_ref[D_SHORTCONV:D_SHORTCONV + D_SSM, :])
        h = x_ref[rows, :] + acc
        h_ref[rows, :] = h
        hn_ref[rows, :] = _rms(h, g_ref[...]).astype(BF16)


def _even_out(x, ya, ys, szb, w_glu, w_out, g, cast0, tm):
    seq = x.shape[0]
    cast_spec = _cast_slice_spec(cast0, seq // tm, lambda i: i)
    row = lambda w: pl.BlockSpec((tm, w), lambda i: (i, 0))
    const = lambda a: pl.BlockSpec(a.shape, lambda i: (0, 0), pipeline_mode=pl.Buffered(1))
    return pl.pallas_call(
        functools.partial(_even_out_kernel, tm=tm),
        grid=(seq // tm,),
        in_specs=[row(D_MODEL), row(D_SHORTCONV),
                  pl.BlockSpec((SSM_SLABS, tm, 128), lambda i: (0, i, 0)), row(D_SSM),
                  const(w_glu), const(w_out), const(g), cast_spec],
        out_specs=[row(D_MODEL), row(D_MODEL), cast_spec],
        out_shape=[jax.ShapeDtypeStruct((seq, D_MODEL), F32),
                   jax.ShapeDtypeStruct((seq, D_MODEL), BF16),
                   jax.ShapeDtypeStruct(cast0.shape, BF16)],
        compiler_params=_params(("arbitrary",)),
        name="even_out",
    )(x, ya, ys, szb, w_glu, w_out, g, cast0)


ODD_GROUPS = 6


def _odd_in_kernel(hn_ref, wa_ref, wb_ref, o_ref, qkv_ref, *, tm, col):
    s = pl.program_id(1)
    nb = D_CONFORMER // col
    group = s // nb
    is_silu = (group == 1) | (group == 5)
    blocks = [slice(r0, r0 + IN_ROWS) for r0 in range(0, tm, IN_ROWS)]

    @pl.when(group == 0)
    def _():
        for rows in blocks:
            hn = hn_ref[rows, :]
            o_ref[rows, :] = (_dot(hn, wa_ref[...]) * _sigmoid(_dot(hn, wb_ref[...]))).astype(BF16)

    @pl.when(is_silu)
    def _():
        for rows in blocks:
            o_ref[rows, :] = _silu(_dot(hn_ref[rows, :], wa_ref[...])).astype(BF16)

    @pl.when((group > 0) & jnp.logical_not(is_silu))
    def _():
        scale = jnp.where(group == 2, Q_SCALE, 1.0).astype(F32)
        for rows in blocks:
            qkv_ref[rows, :] = _dot(hn_ref[rows, :], wa_ref[...]) * scale


def _odd_in(hn, w_in, tm, col):
    seq = hn.shape[0]
    nb = D_CONFORMER // col
    return pl.pallas_call(
        functools.partial(_odd_in_kernel, tm=tm, col=col),
        grid=(seq // tm, ODD_GROUPS * nb),
        in_specs=[pl.BlockSpec((tm, D_MODEL), lambda i, s: (i, 0)),
                  pl.BlockSpec((D_MODEL, col), lambda i, s: (0, jnp.where(s < nb, s, s + nb))),
                  pl.BlockSpec((D_MODEL, col), lambda i, s: (0, jnp.where(s < nb, s + nb, 2 * nb - 1)))],
        out_specs=[pl.BlockSpec((tm, col), lambda i, s: (i, jnp.where(s < 2 * nb, s, jnp.where(
                       s < 5 * nb, 2 * nb - 1, s - 3 * nb)))),
                   pl.BlockSpec((tm, col), lambda i, s: (i, jnp.clip(s - 2 * nb, 0, 3 * nb - 1)))],
        out_shape=[jax.ShapeDtypeStruct((seq, 3 * D_CONFORMER), BF16),
                   jax.ShapeDtypeStruct((seq, 3 * D_ATTN), F32)],
        compiler_params=_params(("arbitrary", "arbitrary")),
        name="odd_in",
    )(hn, w_in, w_in)


CONV_ROWS = 256


def _conformer_kernel(cur_ref, halo_ref, szc_ref, cw_ref, cb_ref, lg_ref, lb_ref, wpw_ref,
                      o_ref, buf_sc, conv_sc, *, tm):
    i = pl.program_id(0)
    nslab = D_CONFORMER // 128
    for c in range(nslab):
        lanes = slice(c * 128, (c + 1) * 128)
        buf_sc[c, 0:CONV_HALO, :] = jnp.where(i == 0, 0.0, halo_ref[:, lanes].astype(F32))
        buf_sc[c, CONV_HALO:CONV_HALO + tm, :] = cur_ref[:, lanes].astype(F32)
    shift = CONV_HALO - (CONFORMER_WIDTH - 1)

    def rows(rc, _):
        base = pl.multiple_of(rc * CONV_ROWS, CONV_ROWS)
        for c in range(nslab):
            lanes = slice(c * 128, (c + 1) * 128)
            acc = jnp.zeros((CONV_ROWS, 128), F32) + cb_ref[:, lanes]
            for k in range(CONFORMER_WIDTH):
                acc = acc + cw_ref[k:k + 1, lanes] * buf_sc[c, pl.ds(base + (shift + k), CONV_ROWS), :]
            conv_sc[pl.ds(base, CONV_ROWS), lanes] = acc
        return 0

    lax.fori_loop(0, tm // CONV_ROWS, rows, 0)
    for r0 in range(0, tm, OUT_ROWS):
        sub = slice(r0, r0 + OUT_ROWS)
        y = conv_sc[sub, :]
        mu = jnp.mean(y, axis=-1, keepdims=True)
        yc = y - mu
        var = jnp.mean(yc * yc, axis=-1, keepdims=True)
        ln = yc * lax.rsqrt(var + NORM_EPS) * lg_ref[...] + lb_ref[...]
        o_ref[sub, :] = (_dot(_silu(ln).astype(BF16), wpw_ref[...]) * szc_ref[sub, :].astype(F32)).astype(BF16)


def _conformer(proj, conv_w, conv_b, ln_g, ln_b, w_pw, tm):
    seq = proj.shape[0]
    per = tm // CONV_HALO
    const = lambda a: pl.BlockSpec(a.shape, lambda i: (0, 0))
    return pl.pallas_call(
        functools.partial(_conformer_kernel, tm=tm),
        grid=(seq // tm,),
        in_specs=[pl.BlockSpec((tm, D_CONFORMER), lambda i: (i, 0)),
                  pl.BlockSpec((CONV_HALO, D_CONFORMER), lambda i: (jnp.maximum(i * per - 1, 0), 0)),
                  pl.BlockSpec((tm, D_CONFORMER), lambda i: (i, 1)),
                  const(conv_w), const(conv_b), const(ln_g), const(ln_b), const(w_pw)],
        out_specs=pl.BlockSpec((tm, D_CONFORMER), lambda i: (i, 0)),
        out_shape=jax.ShapeDtypeStruct((seq, D_CONFORMER), BF16),
        scratch_shapes=[pltpu.VMEM((D_CONFORMER // 128, tm + CONV_HALO, 128), F32),
                        pltpu.VMEM((tm, D_CONFORMER), F32)],
        compiler_params=_params(("arbitrary",)),
        name="conformer",
    )(proj, proj, proj, conv_w, conv_b, ln_g, ln_b, w_pw)


def _t5_bucket(dist):
    max_exact = REL_BUCKETS // 2
    d_f = jnp.maximum(dist, 1).astype(F32)
    large = max_exact + (jnp.log(d_f / max_exact) / math.log(REL_MAX_DISTANCE / max_exact)
                         * (REL_BUCKETS - max_exact)).astype(jnp.int32)
    large = jnp.minimum(large, REL_BUCKETS - 1)
    return jnp.where(dist < max_exact, dist, large)


def _attn_bias(rel_bias):
    q = ATTN_BLOCK
    nh = rel_bias.shape[1]
    kj = np.arange(2 * q)[None, None, :]
    neg = lambda n: jnp.full((nh, n), NEG_INF, F32)
    tabs = []
    for window, dilation in DILATED_PATTERNS:
        steps = window // dilation
        assert steps == q
        rel = jnp.asarray(np.arange(steps + 1) * dilation, dtype=jnp.int32)
        by_rel = rel_bias.astype(F32)[_t5_bucket(rel)].T
        g = jnp.concatenate([by_rel[:, ::-1], neg(2 * q - 1)], axis=1)
        flat = jnp.tile(g, (1, q))[:, :q * (3 * q - 1)]
        tab = flat.reshape(nh, q, 3 * q - 1)[:, :, :2 * q]
        if dilation == 1:
            tab = tab.reshape(nh, q // MERGE_DIL, MERGE_DIL, 2 * q).transpose(0, 2, 1, 3).reshape(nh, q, 2 * q)
        tab = tab * LOG2E
        tabs.append(tab)
        tabs.append(jnp.where(kj >= q, tab, NEG_INF))
    return jnp.stack(tabs, axis=1)


def _attn_kernel(q_ref, k_ref, v_ref, szd_ref, bias_ref, o_ref,
                 onat, qm, km, vm, qs1, qs2, ks0, ks1, ks2, vs0, vs1, vs2,
                 o0, o1, o2, l0, l1, l2):
    i = pl.program_id(1)
    t = ATTN_TILE
    blk = ATTN_BLOCK
    first_tile = i == 0
    qf, kf, vf = q_ref, k_ref, v_ref
    qstreams = (None, qs1, qs2)
    kstreams = (ks0, ks1, ks2)
    vstreams = (vs0, vs1, vs2)
    md = MERGE_DIL
    mspan = t // md

    def stream(src_f, src_m, dil, r):
        n = t // dil
        if dil == md:
            x = src_f[pl.ds(r, n, stride=dil), :]
            src_m[r * n:(r + 1) * n, :] = x
            return x
        return src_m[pl.ds((r % md) * mspan + r // md, n, stride=dil // md), :]

    for pidx, (_, dil) in enumerate(DILATED_PATTERNS):
        n = t // dil
        per = blk + n
        for r in range(dil):
            base = r * per
            for src_f, src_m, dst in ((kf, km, kstreams[pidx]), (vf, vm, vstreams[pidx])):
                tail = dst[base + n:base + per, :]
                dst[base:base + blk, :] = jnp.where(first_tile, jnp.zeros_like(tail), tail)
                if dil == 1:
                    dst[base + blk:base + per, :] = src_f[...].astype(BF16)
                else:
                    dst[base + blk:base + per, :] = stream(src_f, src_m, dil, r).astype(BF16)
            if dil > 1:
                qstreams[pidx][r * n:(r + 1) * n, :] = stream(qf, qm, dil, r).astype(BF16)

    ones = jnp.ones((2 * blk, HEAD_DIM), BF16)
    first = first_tile.astype(jnp.int32)

    def block(pidx, dil, r, sb, out_sc, lse_sc):
        n = t // dil
        per = blk + n
        if dil == 1:
            qb = jnp.concatenate([qf[pl.ds(sb * blk + c, blk // md, stride=md), :] for c in range(md)],
                                 axis=0).astype(BF16)
        else:
            qb = qstreams[pidx][r * n + sb * blk:r * n + (sb + 1) * blk, :]
        krows = slice(r * per + sb * blk, r * per + (sb + 2) * blk)
        kb = kstreams[pidx][krows, :]
        vb = vstreams[pidx][krows, :]
        s = lax.dot_general(qb, kb, _NT, preferred_element_type=F32)
        s = s + bias_ref[0, 2 * pidx + first if sb == 0 else 2 * pidx]
        mx = jnp.max(s, axis=-1, keepdims=True)
        p = jnp.exp2(s - mx).astype(BF16)
        pv = _dot(p, jnp.concatenate([vb, ones], axis=1))
        den = pv[:, HEAD_DIM:]
        out = pv[:, :HEAD_DIM] / den
        lse = mx + jnp.log2(den)
        if dil == 1:
            piece = blk // md
            for c in range(md):
                rows = slice(c * mspan + sb * piece, c * mspan + (sb + 1) * piece)
                out_sc[rows, :] = out[c * piece:(c + 1) * piece, :]
                lse_sc[rows, :] = lse[c * piece:(c + 1) * piece, :]
        elif dil == md:
            rows = slice(r * mspan + sb * blk, r * mspan + (sb + 1) * blk)
            out_sc[rows, :] = out
            lse_sc[rows, :] = lse
        else:
            step = dil // md
            rows = pl.ds((r % md) * mspan + sb * blk * step + r // md, blk, stride=step)
            out_sc[rows, :] = out
            lse_sc[rows, :] = lse

    scratch = ((o0, l0), (o1, l1), (o2, l2))
    for pidx, (_, dil) in enumerate(DILATED_PATTERNS):
        for sb in range(t // (blk * dil)):
            for r in range(dil):
                block(pidx, dil, r, sb, *scratch[pidx])

    la, lb, lc = l0[...], l1[...], l2[...]
    lall = jnp.maximum(jnp.maximum(la, lb), lc)
    wa, wb, wc = jnp.exp2(la - lall), jnp.exp2(lb - lall), jnp.exp2(lc - lall)
    merged = (wa * o0[...] + wb * o1[...] + wc * o2[...]) / (wa + wb + wc)
    for c in range(md):
        onat[pl.ds(c, mspan, stride=md), :] = merged[c * mspan:(c + 1) * mspan, :]
    o_ref[...] = (onat[...] * szd_ref[...].astype(F32)).astype(BF16)


def _attention(qkv, proj, bias):
    seq = proj.shape[0]
    t = ATTN_TILE
    q0, k0, v0 = 0, N_HEADS, 2 * N_HEADS
    z0 = (2 * D_CONFORMER) // HEAD_DIM
    cur = lambda c0: pl.BlockSpec((t, HEAD_DIM), lambda h, i: (i, c0 + h))
    f32 = lambda rows: pltpu.VMEM((rows, HEAD_DIM), F32)
    bf16 = lambda rows: pltpu.VMEM((rows, HEAD_DIM), BF16)
    streams = [bf16(dil * (ATTN_BLOCK + t // dil)) for _, dil in DILATED_PATTERNS]
    return pl.pallas_call(
        _attn_kernel,
        grid=(N_HEADS, seq // t),
        in_specs=[cur(q0), cur(k0), cur(v0), cur(z0),
                  pl.BlockSpec((1,) + bias.shape[1:], lambda h, i: (h, 0, 0, 0))],
        out_specs=pl.BlockSpec((t, HEAD_DIM), lambda h, i: (i, h)),
        out_shape=jax.ShapeDtypeStruct((seq, D_ATTN), BF16),
        scratch_shapes=[f32(t)] * 4 + [bf16(t)] * 2 + streams + streams + [f32(t)] * 6,
        compiler_params=_params(("arbitrary", "arbitrary")),
        name="dilated_attn",
    )(qkv, qkv, qkv, proj, bias)


def _odd_out_kernel(h_ref, yc_ref, yd_ref, wout_ref, g_ref, o_ref, *, tm):
    for r0 in range(0, tm, OUT_ROWS):
        rows = slice(r0, r0 + OUT_ROWS)
        acc = _dot(yc_ref[rows, :], wout_ref[0:D_CONFORMER, :])
        acc = acc + _dot(yd_ref[rows, :], wout_ref[D_CONFORMER:D_CONFORMER + D_ATTN, :])
        o_ref[rows, :] = _rms(h_ref[rows, :] + acc, g_ref[...])


def _odd_out(h, yc, yd, w_out, g, tm):
    seq = h.shape[0]
    row = lambda w: pl.BlockSpec((tm, w), lambda i: (i, 0))
    const = lambda a: pl.BlockSpec(a.shape, lambda i: (0, 0), pipeline_mode=pl.Buffered(1))
    return pl.pallas_call(
        functools.partial(_odd_out_kernel, tm=tm),
        grid=(seq // tm,),
        in_specs=[row(D_MODEL), row(D_CONFORMER), row(D_ATTN), const(w_out), const(g)],
        out_specs=row(D_MODEL),
        out_shape=jax.ShapeDtypeStruct((seq, D_MODEL), F32),
        compiler_params=_params(("arbitrary",)),
        name="odd_out",
    )(h, yc, yd, w_out, g)


def _row_tile(seq, want):
    return min(seq, want)


def kernel(x, norm_g, final_norm_g, ev_w_in, ev_conv_w, s5_lam_re, s5_lam_im, s5_log_dt, s5_b_re, s5_b_im, s5_c_re, s5_c_im, s5_d, s5_w_glu, ev_w_out, od_w_in, cf_conv_w, cf_conv_b, cf_ln_g, cf_ln_b, cf_w_pw, od_w_out, rel_bias):
    bsz, seq, _ = x.shape
    assert bsz == 1 and seq % ATTN_TILE == 0
    x2 = x.reshape(seq, D_MODEL)
    g0 = norm_g[0].astype(F32).reshape(1, D_MODEL)
    g1 = norm_g[1].astype(F32).reshape(1, D_MODEL)
    gf = final_norm_g.astype(F32).reshape(1, D_MODEL)

    ya, u, szb, ev_w_out_b, od_w_out_b = _even_in(
        x2, g0, ev_w_in[0].astype(BF16), ev_conv_w[0].astype(F32), ev_w_out[0].astype(F32),
        od_w_out[0].astype(F32), _row_tile(seq, EVEN_IN_TILE[0]), EVEN_IN_TILE[1])
    tables = _s5_tables(s5_lam_re[0], s5_lam_im[0], s5_log_dt[0], s5_b_re[0], s5_b_im[0],
                        s5_c_re[0], s5_c_im[0], s5_d[0])
    ys = _s5(u, tables, min(seq // SSM_CHUNK, SSM_CHUNK_ROWS))
    h1, hn1, od_w_in_b = _even_out(x2, ya, ys, szb, s5_w_glu[0].astype(BF16), ev_w_out_b, g1,
                                   od_w_in[0].astype(F32), _row_tile(seq, EVEN_OUT_ROWS))

    proj, qkv = _odd_in(hn1, od_w_in_b, _row_tile(seq, ODD_IN_TILE[0]), ODD_IN_TILE[1])
    conv_w = jnp.concatenate([cf_conv_w[0].astype(F32), jnp.zeros((1, D_CONFORMER), F32)], axis=0)
    yc = _conformer(proj, conv_w, cf_conv_b[0].astype(F32).reshape(1, -1),
                    cf_ln_g[0].astype(F32).reshape(1, -1), cf_ln_b[0].astype(F32).reshape(1, -1),
                    cf_w_pw[0].astype(BF16), _row_tile(seq, CONFORMER_ROWS))
    yd = _attention(qkv, proj, _attn_bias(rel_bias))
    out = _odd_out(h1, yc, yd, od_w_out_b, gf, _row_tile(seq, ODD_OUT_ROWS))
    return out.reshape(bsz, seq, D_MODEL)
```

```python
import functools
import math

import numpy as np
import jax
import jax.numpy as jnp
from jax import lax
from jax.experimental import pallas as pl
from jax.experimental.pallas import tpu as pltpu

F32 = jnp.float32
BF16 = jnp.bfloat16

D_MODEL = 2048
D_SHORTCONV = 1536
SHORTCONV_WIDTH = 3
D_SSM = 512
SSM_GROUP = 16
SSM_GROUPS = 32
SSM_STATE = 64
D_CONFORMER = 1024
CONFORMER_WIDTH = 31
D_ATTN = 1024
HEAD_DIM = 128
N_HEADS = 8
DILATED_PATTERNS = ((128, 1), (512, 4), (2048, 16))
ATTN_BLOCK = 128
REL_BUCKETS = 32
REL_MAX_DISTANCE = 2048
NORM_EPS = 1e-6
NEG_INF = -1e30

IN_ROWS = 128
EVEN_ROWS = 256
SSM_CHUNK = 16
SSM_FLAT = SSM_CHUNK * SSM_GROUP
SSM_SLABS = D_SSM // 128
ATTN_TILE = 4096
LOG2E = math.log2(math.e)
Q_SCALE = HEAD_DIM ** -0.5 * LOG2E
MERGE_DIL = 4
CONV_HALO = 32
VMEM_LIMIT = 50 * 1024 * 1024

EVEN_IN_TILE = (1024, 512)
EVEN_OUT_ROWS = 512
ODD_IN_TILE = (2048, 512)
CONFORMER_ROWS = 1024
ODD_OUT_ROWS = 512
SSM_CHUNK_ROWS = 128


def _params(sem, vmem=VMEM_LIMIT):
    return pltpu.CompilerParams(dimension_semantics=sem, vmem_limit_bytes=vmem)


def _dot(a, b):
    return jnp.dot(a, b, preferred_element_type=F32)


def _sigmoid(z):
    return 1.0 / (1.0 + jnp.exp(-z))


def _silu(z):
    return z * _sigmoid(z)


def _gelu_tanh(x):
    return 0.5 * x * (1.0 + jnp.tanh(math.sqrt(2.0 / math.pi) * (x + 0.044715 * (x * x * x))))


def _cast_slice_spec(a, nsteps, step_of):
    rows = a.shape[0] // nsteps
    assert rows * nsteps == a.shape[0] and rows % 16 == 0, (a.shape, nsteps)
    return pl.BlockSpec((rows, a.shape[1]), lambda *idx: (step_of(*idx), 0))


def _rms(xf, g):
    ms = jnp.mean(xf * xf, axis=-1, keepdims=True)
    return xf * lax.rsqrt(ms + NORM_EPS) * g


def _even_in_kernel(x_ref, g_ref, wa_ref, wb_ref, wc_ref, wd_ref, cw_ref, c0_ref, c1_ref,
                    ya_ref, u_ref, szb_ref, c0b_ref, c1b_ref, hn_sc, pbuf_sc, carry_sc, *, tm, col):
    i = pl.program_id(0)
    j = pl.program_id(1)
    nconv = D_SHORTCONV // col
    c0b_ref[...] = c0_ref[...].astype(BF16)
    c1b_ref[...] = c1_ref[...].astype(BF16)

    def conv_step(first):
        @pl.when(i == 0)
        def _():
            carry_sc[j] = jnp.zeros((8, col), F32)

        pbuf_sc[0:8, :] = carry_sc[j]
        cw = cw_ref[...]
        for r0 in range(0, tm, EVEN_ROWS):
            rows = slice(r0, r0 + EVEN_ROWS)
            if first:
                hn = _rms(x_ref[rows, :], g_ref[...]).astype(BF16)
                hn_sc[rows, :] = hn
            else:
                hn = hn_sc[rows, :]
            p = _dot(hn, wb_ref[...]) * _dot(hn, wa_ref[...])
            pbuf_sc[8 + r0:8 + r0 + EVEN_ROWS, :] = p
            conv = (cw[2:3, :] * p + cw[1:2, :] * pbuf_sc[pl.ds(7 + r0, EVEN_ROWS), :]
                    + cw[0:1, :] * pbuf_sc[pl.ds(6 + r0, EVEN_ROWS), :])
            gb = _dot(hn, wc_ref[...])
            za = _dot(hn, wd_ref[...])
            ya_ref[rows, :] = (gb * conv * _silu(za)).astype(BF16)
        carry_sc[j] = pbuf_sc[tm:tm + 8, :]

    pl.when(j == 0)(functools.partial(conv_step, True))
    pl.when((j > 0) & (j < nconv))(functools.partial(conv_step, False))

    @pl.when(j >= nconv)
    def _():
        for r0 in range(0, tm, EVEN_ROWS):
            rows = slice(r0, r0 + EVEN_ROWS)
            hn = hn_sc[rows, :]
            u = _dot(hn, wa_ref[...])
            for s in range(col // 128):
                u_ref[s, rows, :] = u[:, s * 128:(s + 1) * 128]
            szb_ref[rows, :] = _silu(_dot(hn, wb_ref[...])).astype(BF16)


def _even_in(x, g, w_in, conv_w, cast0, cast1, tm, col):
    seq = x.shape[0]
    nconv = D_SHORTCONV // col
    nssm = D_SSM // col
    u_blk = 4 * nconv

    def wspec(k, alt0):
        return pl.BlockSpec((D_MODEL, col),
                            lambda i, j: (0, jnp.where(j < nconv, j + nconv * k, alt0 + j - nconv)))

    hold = lambda k: pl.BlockSpec((D_MODEL, col), lambda i, j: (0, jnp.minimum(j, nconv - 1) + nconv * k))
    ssm_step = lambda j: jnp.maximum(j - nconv, 0)
    nj = nconv + nssm
    cast_spec = lambda a: _cast_slice_spec(a, (seq // tm) * nj, lambda i, j: i * nj + j)
    return pl.pallas_call(
        functools.partial(_even_in_kernel, tm=tm, col=col),
        grid=(seq // tm, nj),
        in_specs=[
            pl.BlockSpec((tm, D_MODEL), lambda i, j: (i, 0)),
            pl.BlockSpec((1, D_MODEL), lambda i, j: (0, 0)),
            wspec(0, u_blk), wspec(1, u_blk + nssm), hold(2), hold(3),
            pl.BlockSpec((SHORTCONV_WIDTH, col), lambda i, j: (0, jnp.minimum(j, nconv - 1))),
            cast_spec(cast0), cast_spec(cast1),
        ],
        out_specs=[
            pl.BlockSpec((tm, col), lambda i, j: (i, jnp.minimum(j, nconv - 1))),
            pl.BlockSpec((col // 128, tm, 128), lambda i, j: (ssm_step(j), i, 0)),
            pl.BlockSpec((tm, col), lambda i, j: (i, ssm_step(j))),
            cast_spec(cast0), cast_spec(cast1),
        ],
        out_shape=[
            jax.ShapeDtypeStruct((seq, D_SHORTCONV), BF16),
            jax.ShapeDtypeStruct((SSM_SLABS, seq, 128), F32),
            jax.ShapeDtypeStruct((seq, D_SSM), BF16),
            jax.ShapeDtypeStruct(cast0.shape, BF16),
            jax.ShapeDtypeStruct(cast1.shape, BF16),
        ],
        scratch_shapes=[
            pltpu.VMEM((tm, D_MODEL), BF16),
            pltpu.VMEM((tm + 8, col), F32),
            pltpu.VMEM((nconv, 8, col), F32),
        ],
        compiler_params=_params(("arbitrary", "arbitrary")),
        name="even_in",
    )(x, g, w_in, w_in, w_in, w_in, conv_w, cast0, cast1)


def _s5_tables(lam_re, lam_im, log_dt, b_re, b_im, c_re, c_im, d_skip):
    hi = lax.Precision.HIGH
    t_, g_, n_, p_ = SSM_CHUNK, SSM_GROUPS, SSM_STATE, SSM_GROUP
    lam_re, lam_im = lam_re.astype(F32), lam_im.astype(F32)
    dt = jnp.exp(log_dt.astype(F32))[:, None]
    xr, xi = lam_re * dt, lam_im * dt
    k = jnp.arange(t_ + 1, dtype=F32)[:, None, None]
    mag = jnp.exp(xr * k)
    pr, pi = mag * jnp.cos(xi * k), mag * jnp.sin(xi * k)
    nr, ni = pr[1] - 1.0, pi[1]
    den = lam_re * lam_re + lam_im * lam_im
    fr = (nr * lam_re + ni * lam_im) / den
    fi = (ni * lam_re - nr * lam_im) / den
    b_re, b_im = b_re.astype(F32), b_im.astype(F32)
    bbr = fr[..., None] * b_re - fi[..., None] * b_im
    bbi = fr[..., None] * b_im + fi[..., None] * b_re
    c_re, c_im = c_re.astype(F32), c_im.astype(F32)

    cpr = c_re[None] * pr[:t_, :, None, :] - c_im[None] * pi[:t_, :, None, :]
    cpi = c_re[None] * pi[:t_, :, None, :] + c_im[None] * pr[:t_, :, None, :]
    kern = jnp.einsum('tgpn,gnq->tgpq', jnp.concatenate([cpr, -cpi], axis=-1),
                      jnp.concatenate([bbr, bbi], axis=1), precision=hi)
    skip = d_skip.astype(F32).reshape(g_, p_)[:, :, None] * jnp.eye(p_, dtype=F32)[None]
    kern = kern.at[0].add(skip)
    flat = t_ * p_
    last = jnp.transpose(kern[::-1], (1, 2, 0, 3)).reshape(g_, p_, flat)
    mt = jnp.stack([jnp.pad(last[:, :, (t_ - 1 - t) * p_:], ((0, 0), (0, 0), (0, (t_ - 1 - t) * p_)))
                    for t in range(t_)], axis=1).reshape(g_, flat, flat)

    prj, pij = pr[t_ - 1::-1][:t_], pi[t_ - 1::-1][:t_]
    wr = prj[..., None] * bbr[None] - pij[..., None] * bbi[None]
    wi = prj[..., None] * bbi[None] + pij[..., None] * bbr[None]
    half = (np.arange(2) == 0)[None, None, :, None, None]

    def pack_w(w):
        w = jnp.transpose(w.reshape(t_, g_ // 2, 2, n_, p_), (1, 0, 2, 4, 3))
        w = jnp.concatenate([jnp.where(half, w, 0.0), jnp.where(half, 0.0, w)], axis=-1)
        return w.reshape(g_ // 2, 2 * flat, 2 * n_)

    vr = c_re[None] * pr[1:, :, None, :] - c_im[None] * pi[1:, :, None, :]
    vi = c_re[None] * pi[1:, :, None, :] + c_im[None] * pr[1:, :, None, :]
    vr = jnp.transpose(vr, (1, 0, 2, 3)).reshape(g_ // 2, 2, flat, n_)
    vi = -jnp.transpose(vi, (1, 0, 2, 3)).reshape(g_ // 2, 2, flat, n_)
    zv = jnp.zeros((g_ // 2, flat, n_), F32)
    v_even = jnp.concatenate([vr[:, 0], zv, vi[:, 0], zv], axis=-1)
    v_odd = jnp.concatenate([zv, vr[:, 1], zv, vi[:, 1]], axis=-1)
    vt = jnp.stack([v_even, v_odd], axis=1).reshape(g_, flat, 4 * n_)
    a_re = pr[t_].reshape(1, g_ * n_)
    a_im = pi[t_].reshape(1, g_ * n_)
    return mt.astype(BF16), pack_w(wr).astype(BF16), pack_w(wi).astype(BF16), vt.astype(BF16), a_re, a_im


_NT = (((1,), (1,)), ((), ()))
_TN = (((0,), (0,)), ((), ()))


def _s5_kernel(u_ref, mt_ref, wre_ref, wim_ref, vt_ref, are_ref, aim_ref, y_ref,
               xt_sc, yt_sc, zre_sc, zim_sc, carry_sc, *, ct):
    i = pl.program_id(0)
    t_ = SSM_CHUNK

    @pl.when(i == 0)
    def _():
        carry_sc[...] = jnp.zeros_like(carry_sc)

    for s in range(SSM_SLABS):
        for j in range(t_):
            xt_sc[s, j] = u_ref[s, pl.ds(j, ct, stride=t_), :].T.astype(BF16)

    pairs_per_slab = 128 // (2 * SSM_GROUP)
    for gp in range(SSM_GROUPS // 2):
        s, k = gp // pairs_per_slab, gp % pairs_per_slab
        xp = xt_sc[s, :, 32 * k:32 * k + 32, :].reshape(2 * SSM_FLAT, ct)
        zre_sc[:, gp * 128:(gp + 1) * 128] = lax.dot_general(xp, wre_ref[gp], _TN, preferred_element_type=F32)
        zim_sc[:, gp * 128:(gp + 1) * 128] = lax.dot_general(xp, wim_ref[gp], _TN, preferred_element_type=F32)

    are = are_ref[...]
    aim = aim_ref[...]

    def step(r, carry):
        sre, sim = carry
        zr = zre_sc[pl.ds(r, 1), :]
        zi = zim_sc[pl.ds(r, 1), :]
        zre_sc[pl.ds(r, 1), :] = sre
        zim_sc[pl.ds(r, 1), :] = sim
        return are * sre - aim * sim + zr, are * sim + aim * sre + zi

    sre, sim = lax.fori_loop(0, ct, step, (carry_sc[0:1, :], carry_sc[1:2, :]), unroll=8)
    carry_sc[0:1, :] = sre
    carry_sc[1:2, :] = sim

    groups_per_slab = 128 // SSM_GROUP
    for g in range(SSM_GROUPS):
        gp, s, k = g // 2, g // groups_per_slab, g % groups_per_slab
        xg = xt_sc[s, :, 16 * k:16 * k + 16, :].reshape(SSM_FLAT, ct)
        sp = jnp.concatenate([zre_sc[:, gp * 128:(gp + 1) * 128],
                              zim_sc[:, gp * 128:(gp + 1) * 128]], axis=1).astype(BF16)
        yt = _dot(mt_ref[g], xg) + lax.dot_general(vt_ref[g], sp, _NT, preferred_element_type=F32)
        yt_sc[s, :, 16 * k:16 * k + 16, :] = yt.reshape(t_, SSM_GROUP, ct)

    for s in range(SSM_SLABS):
        for t in range(t_):
            y_ref[s, pl.ds(t, ct, stride=t_), :] = yt_sc[s, t].T


def _s5(u4, tables, ct):
    mt, wre, wim, vt, are, aim = tables
    _, seq, _ = u4.shape
    full = lambda a: pl.BlockSpec(a.shape, lambda i: (0,) * a.ndim)
    rows = pl.BlockSpec((SSM_SLABS, SSM_CHUNK * ct, 128), lambda i: (0, i, 0))
    nstate = SSM_GROUPS * SSM_STATE
    return pl.pallas_call(
        functools.partial(_s5_kernel, ct=ct),
        grid=(seq // (SSM_CHUNK * ct),),
        in_specs=[rows, full(mt), full(wre), full(wim), full(vt), full(are), full(aim)],
        out_specs=rows,
        out_shape=jax.ShapeDtypeStruct(u4.shape, F32),
        scratch_shapes=[pltpu.VMEM((SSM_SLABS, SSM_CHUNK, 128, ct), BF16),
                        pltpu.VMEM((SSM_SLABS, SSM_CHUNK, 128, ct), F32),
                        pltpu.VMEM((ct, nstate), F32),
                        pltpu.VMEM((ct, nstate), F32),
                        pltpu.VMEM((8, nstate), F32)],
        compiler_params=_params(("arbitrary",)),
        name="s5_ssm",
    )(u4, mt, wre, wim, vt, are, aim)


OUT_ROWS = 256


X_RING = 3


def _even_out_kernel(x_hbm, ya_ref, ys_ref, szb_ref, wglu_ref, wout_ref, g_ref, c0_ref,
                     h_ref, hn_ref, c0b_ref, xbuf, xsem, *, tm):
    i = pl.program_id(0)
    n = pl.num_programs(0)

    def x_copy(step):
        slot = step % X_RING
        return pltpu.make_async_copy(x_hbm.at[pl.ds(step * tm, tm), :], xbuf.at[slot], xsem.at[slot])

    @pl.when(i == 0)
    def _():
        x_copy(0).start()

        @pl.when(n > 1)
        def _():
            x_copy(1).start()

    @pl.when(i + 2 < n)
    def _():
        x_copy(i + 2).start()

    x_copy(i).wait()
    x_ref = xbuf.at[i % X_RING]
    c0b_ref[...] = c0_ref[...].astype(BF16)
    for r0 in range(0, tm, OUT_ROWS):
        rows = slice(r0, r0 + OUT_ROWS)
        gl = _gelu_tanh(jnp.concatenate([ys_ref[s, rows, :] for s in range(SSM_SLABS)], axis=1))
        yb = gl * _sigmoid(_dot(gl.astype(BF16), wglu_ref[...])) * szb_ref[rows, :].astype(F32)
        acc = _dot(ya_ref[rows, :], wout_ref[0:D_SHORTCONV, :])
        acc = acc + _dot(yb.astype(BF16), wout_ref[D_SHORTCONV:D_SHORTCONV + D_SSM, :])
        h = x_ref[rows, :] + acc
        h_ref[rows, :] = h
        hn_ref[rows, :] = _rms(h, g_ref[...]).astype(BF16)


def _even_out(x, ya, ys, szb, w_glu, w_out, g, cast0, tm):
    seq = x.shape[0]
    cast_spec = _cast_slice_spec(cast0, seq // tm, lambda i: i)
    row = lambda w: pl.BlockSpec((tm, w), lambda i: (i, 0))
    const = lambda a: pl.BlockSpec(a.shape, lambda i: (0, 0), pipeline_mode=pl.Buffered(1))
    return pl.pallas_call(
        functools.partial(_even_out_kernel, tm=tm),
        grid=(seq // tm,),
        in_specs=[pl.BlockSpec(memory_space=pl.ANY), row(D_SHORTCONV),
                  pl.BlockSpec((SSM_SLABS, tm, 128), lambda i: (0, i, 0)), row(D_SSM),
                  const(w_glu), const(w_out), const(g), cast_spec],
        out_specs=[row(D_MODEL), row(D_MODEL), cast_spec],
        out_shape=[jax.ShapeDtypeStruct((seq, D_MODEL), F32),
                   jax.ShapeDtypeStruct((seq, D_MODEL), BF16),
                   jax.ShapeDtypeStruct(cast0.shape, BF16)],
        scratch_shapes=[pltpu.VMEM((X_RING, tm, D_MODEL), F32), pltpu.SemaphoreType.DMA((X_RING,))],
        compiler_params=_params(("arbitrary",)),
        name="even_out",
    )(x, ya, ys, szb, w_glu, w_out, g, cast0)


ODD_GROUPS = 6


def _odd_in_kernel(hn_ref, wa_ref, wb_ref, o_ref, qkv_ref, *, tm, col):
    s = pl.program_id(1)
    nb = D_CONFORMER // col
    group = s // nb
    is_silu = (group == 1) | (group == 5)
    blocks = [slice(r0, r0 + IN_ROWS) for r0 in range(0, tm, IN_ROWS)]

    @pl.when(group == 0)
    def _():
        for rows in blocks:
            hn = hn_ref[rows, :]
            o_ref[rows, :] = (_dot(hn, wa_ref[...]) * _sigmoid(_dot(hn, wb_ref[...]))).astype(BF16)

    @pl.when(is_silu)
    def _():
        for rows in blocks:
            o_ref[rows, :] = _silu(_dot(hn_ref[rows, :], wa_ref[...])).astype(BF16)

    @pl.when((group > 0) & jnp.logical_not(is_silu))
    def _():
        scale = jnp.where(group == 2, Q_SCALE, 1.0).astype(F32)
        for rows in blocks:
            qkv_ref[rows, :] = _dot(hn_ref[rows, :], wa_ref[...]) * scale


def _odd_in(hn, w_in, tm, col):
    seq = hn.shape[0]
    nb = D_CONFORMER // col
    return pl.pallas_call(
        functools.partial(_odd_in_kernel, tm=tm, col=col),
        grid=(seq // tm, ODD_GROUPS * nb),
        in_specs=[pl.BlockSpec((tm, D_MODEL), lambda i, s: (i, 0)),
                  pl.BlockSpec((D_MODEL, col), lambda i, s: (0, jnp.where(s < nb, s, s + nb))),
                  pl.BlockSpec((D_MODEL, col), lambda i, s: (0, jnp.where(s < nb, s + nb, 2 * nb - 1)))],
        out_specs=[pl.BlockSpec((tm, col), lambda i, s: (i, jnp.where(s < 2 * nb, s, jnp.where(
                       s < 5 * nb, 2 * nb - 1, s - 3 * nb)))),
                   pl.BlockSpec((tm, col), lambda i, s: (i, jnp.clip(s - 2 * nb, 0, 3 * nb - 1)))],
        out_shape=[jax.ShapeDtypeStruct((seq, 3 * D_CONFORMER), BF16),
                   jax.ShapeDtypeStruct((seq, 3 * D_ATTN), F32)],
        compiler_params=_params(("arbitrary", "arbitrary")),
        name="odd_in",
    )(hn, w_in, w_in)


CONV_ROWS = 256


def _conformer_kernel(cur_ref, halo_ref, szc_ref, cw_ref, cb_ref, lg_ref, lb_ref, wpw_ref,
                      o_ref, buf_sc, conv_sc, *, tm):
    i = pl.program_id(0)
    nslab = D_CONFORMER // 128
    for c in range(nslab):
        lanes = slice(c * 128, (c + 1) * 128)
        buf_sc[c, 0:CONV_HALO, :] = jnp.where(i == 0, 0.0, halo_ref[:, lanes].astype(F32))
        buf_sc[c, CONV_HALO:CONV_HALO + tm, :] = cur_ref[:, lanes].astype(F32)
    shift = CONV_HALO - (CONFORMER_WIDTH - 1)

    def rows(rc, _):
        base = pl.multiple_of(rc * CONV_ROWS, CONV_ROWS)
        for c in range(nslab):
            lanes = slice(c * 128, (c + 1) * 128)
            acc = jnp.zeros((CONV_ROWS, 128), F32) + cb_ref[:, lanes]
            for k in range(CONFORMER_WIDTH):
                acc = acc + cw_ref[k:k + 1, lanes] * buf_sc[c, pl.ds(base + (shift + k), CONV_ROWS), :]
            conv_sc[pl.ds(base, CONV_ROWS), lanes] = acc
        return 0

    lax.fori_loop(0, tm // CONV_ROWS, rows, 0)
    for r0 in range(0, tm, OUT_ROWS):
        sub = slice(r0, r0 + OUT_ROWS)
        y = conv_sc[sub, :]
        mu = jnp.mean(y, axis=-1, keepdims=True)
        yc = y - mu
        var = jnp.mean(yc * yc, axis=-1, keepdims=True)
        ln = yc * lax.rsqrt(var + NORM_EPS) * lg_ref[...] + lb_ref[...]
        o_ref[sub, :] = (_dot(_silu(ln).astype(BF16), wpw_ref[...]) * szc_ref[sub, :].astype(F32)).astype(BF16)


def _conformer(proj, conv_w, conv_b, ln_g, ln_b, w_pw, tm):
    seq = proj.shape[0]
    per = tm // CONV_HALO
    const = lambda a: pl.BlockSpec(a.shape, lambda i: (0, 0))
    return pl.pallas_call(
        functools.partial(_conformer_kernel, tm=tm),
        grid=(seq // tm,),
        in_specs=[pl.BlockSpec((tm, D_CONFORMER), lambda i: (i, 0)),
                  pl.BlockSpec((CONV_HALO, D_CONFORMER), lambda i: (jnp.maximum(i * per - 1, 0), 0)),
                  pl.BlockSpec((tm, D_CONFORMER), lambda i: (i, 1)),
                  const(conv_w), const(conv_b), const(ln_g), const(ln_b), const(w_pw)],
        out_specs=pl.BlockSpec((tm, D_CONFORMER), lambda i: (i, 0)),
        out_shape=jax.ShapeDtypeStruct((seq, D_CONFORMER), BF16),
        scratch_shapes=[pltpu.VMEM((D_CONFORMER // 128, tm + CONV_HALO, 128), F32),
                        pltpu.VMEM((tm, D_CONFORMER), F32)],
        compiler_params=_params(("arbitrary",)),
        name="conformer",
    )(proj, proj, proj, conv_w, conv_b, ln_g, ln_b, w_pw)


def _t5_bucket(dist):
    max_exact = REL_BUCKETS // 2
    d_f = jnp.maximum(dist, 1).astype(F32)
    large = max_exact + (jnp.log(d_f / max_exact) / math.log(REL_MAX_DISTANCE / max_exact)
                         * (REL_BUCKETS - max_exact)).astype(jnp.int32)
    large = jnp.minimum(large, REL_BUCKETS - 1)
    return jnp.where(dist < max_exact, dist, large)


def _attn_bias(rel_bias):
    q = ATTN_BLOCK
    nh = rel_bias.shape[1]
    kj = np.arange(2 * q)[None, None, :]
    neg = lambda n: jnp.full((nh, n), NEG_INF, F32)
    tabs = []
    for window, dilation in DILATED_PATTERNS:
        steps = window // dilation
        assert steps == q
        rel = jnp.asarray(np.arange(steps + 1) * dilation, dtype=jnp.int32)
        by_rel = rel_bias.astype(F32)[_t5_bucket(rel)].T
        g = jnp.concatenate([by_rel[:, ::-1], neg(2 * q - 1)], axis=1)
        flat = jnp.tile(g, (1, q))[:, :q * (3 * q - 1)]
        tab = flat.reshape(nh, q, 3 * q - 1)[:, :, :2 * q]
        if dilation == 1:
            tab = tab.reshape(nh, q // MERGE_DIL, MERGE_DIL, 2 * q).transpose(0, 2, 1, 3).reshape(nh, q, 2 * q)
        tab = tab * LOG2E
        tabs.append(tab)
        tabs.append(jnp.where(kj >= q, tab, NEG_INF))
    return jnp.stack(tabs, axis=1)


def _attn_kernel(q_ref, k_ref, v_ref, szd_ref, bias_ref, o_ref,
                 onat, qm, km, vm, qs1, qs2, ks0, ks1, ks2, vs0, vs1, vs2,
                 o0, o1, o2, l0, l1, l2):
    i = pl.program_id(1)
    t = ATTN_TILE
    blk = ATTN_BLOCK
    first_tile = i == 0
    qf, kf, vf = q_ref, k_ref, v_ref
    qstreams = (None, qs1, qs2)
    kstreams = (ks0, ks1, ks2)
    vstreams = (vs0, vs1, vs2)
    md = MERGE_DIL
    mspan = t // md

    def stream(src_f, src_m, dil, r):
        n = t // dil
        if dil == md:
            x = src_f[pl.ds(r, n, stride=dil), :]
            src_m[r * n:(r + 1) * n, :] = x
            return x
        return src_m[pl.ds((r % md) * mspan + r // md, n, stride=dil // md), :]

    for pidx, (_, dil) in enumerate(DILATED_PATTERNS):
        n = t // dil
        per = blk + n
        for r in range(dil):
            base = r * per
            for src_f, src_m, dst in ((kf, km, kstreams[pidx]), (vf, vm, vstreams[pidx])):
                tail = dst[base + n:base + per, :]
                dst[base:base + blk, :] = jnp.where(first_tile, jnp.zeros_like(tail), tail)
                if dil == 1:
                    dst[base + blk:base + per, :] = src_f[...].astype(BF16)
                else:
                    dst[base + blk:base + per, :] = stream(src_f, src_m, dil, r).astype(BF16)
            if dil > 1:
                qstreams[pidx][r * n:(r + 1) * n, :] = stream(qf, qm, dil, r).astype(BF16)

    ones = jnp.ones((2 * blk, HEAD_DIM), BF16)
    first = first_tile.astype(jnp.int32)

    def block(pidx, dil, r, sb, out_sc, lse_sc):
        n = t // dil
        per = blk + n
        if dil == 1:
            qb = jnp.concatenate([qf[pl.ds(sb * blk + c, blk // md, stride=md), :] for c in range(md)],
                                 axis=0).astype(BF16)
        else:
            qb = qstreams[pidx][r * n + sb * blk:r * n + (sb + 1) * blk, :]
        krows = slice(r * per + sb * blk, r * per + (sb + 2) * blk)
        kb = kstreams[pidx][krows, :]
        vb = vstreams[pidx][krows, :]
        s = lax.dot_general(qb, kb, _NT, preferred_element_type=F32)
        s = s + bias_ref[0, 2 * pidx + first if sb == 0 else 2 * pidx]
        mx = jnp.max(s, axis=-1, keepdims=True)
        p = jnp.exp2(s - mx).astype(BF16)
        pv = _dot(p, jnp.concatenate([vb, ones], axis=1))
        den = pv[:, HEAD_DIM:]
        out = pv[:, :HEAD_DIM] / den
        lse = mx + jnp.log2(den)
        if dil == 1:
            piece = blk // md
            for c in range(md):
                rows = slice(c * mspan + sb * piece, c * mspan + (sb + 1) * piece)
                out_sc[rows, :] = out[c * piece:(c + 1) * piece, :]
                lse_sc[rows, :] = lse[c * piece:(c + 1) * piece, :]
        elif dil == md:
            rows = slice(r * mspan + sb * blk, r * mspan + (sb + 1) * blk)
            out_sc[rows, :] = out
            lse_sc[rows, :] = lse
        else:
            step = dil // md
            rows = pl.ds((r % md) * mspan + sb * blk * step + r // md, blk, stride=step)
            out_sc[rows, :] = out
            lse_sc[rows, :] = lse

    scratch = ((o0, l0), (o1, l1), (o2, l2))
    for pidx, (_, dil) in enumerate(DILATED_PATTERNS):
        for sb in range(t // (blk * dil)):
            for r in range(dil):
                block(pidx, dil, r, sb, *scratch[pidx])

    la, lb, lc = l0[...], l1[...], l2[...]
    lall = jnp.maximum(jnp.maximum(la, lb), lc)
    wa, wb, wc = jnp.exp2(la - lall), jnp.exp2(lb - lall), jnp.exp2(lc - lall)
    merged = (wa * o0[...] + wb * o1[...] + wc * o2[...]) / (wa + wb + wc)
    for c in range(md):
        onat[pl.ds(c, mspan, stride=md), :] = merged[c * mspan:(c + 1) * mspan, :]
    o_ref[...] = (onat[...] * szd_ref[...].astype(F32)).astype(BF16)


def _attention(qkv, proj, bias):
    seq = proj.shape[0]
    t = ATTN_TILE
    q0, k0, v0 = 0, N_HEADS, 2 * N_HEADS
    z0 = (2 * D_CONFORMER) // HEAD_DIM
    cur = lambda c0: pl.BlockSpec((t, HEAD_DIM), lambda h, i: (i, c0 + h))
    f32 = lambda rows: pltpu.VMEM((rows, HEAD_DIM), F32)
    bf16 = lambda rows: pltpu.VMEM((rows, HEAD_DIM), BF16)
    streams = [bf16(dil * (ATTN_BLOCK + t // dil)) for _, dil in DILATED_PATTERNS]
    return pl.pallas_call(
        _attn_kernel,
        grid=(N_HEADS, seq // t),
        in_specs=[cur(q0), cur(k0), cur(v0), cur(z0),
                  pl.BlockSpec((1,) + bias.shape[1:], lambda h, i: (h, 0, 0, 0))],
        out_specs=pl.BlockSpec((t, HEAD_DIM), lambda h, i: (i, h)),
        out_shape=jax.ShapeDtypeStruct((seq, D_ATTN), BF16),
        scratch_shapes=[f32(t)] * 4 + [bf16(t)] * 2 + streams + streams + [f32(t)] * 6,
        compiler_params=_params(("arbitrary", "arbitrary")),
        name="dilated_attn",
    )(qkv, qkv, qkv, proj, bias)


def _odd_out_kernel(h_ref, yc_ref, yd_ref, wout_ref, g_ref, o_ref, *, tm):
    for r0 in range(0, tm, OUT_ROWS):
        rows = slice(r0, r0 + OUT_ROWS)
        acc = _dot(yc_ref[rows, :], wout_ref[0:D_CONFORMER, :])
        acc = acc + _dot(yd_ref[rows, :], wout_ref[D_CONFORMER:D_CONFORMER + D_ATTN, :])
        o_ref[rows, :] = _rms(h_ref[rows, :] + acc, g_ref[...])


def _odd_out(h, yc, yd, w_out, g, tm):
    seq = h.shape[0]
    row = lambda w: pl.BlockSpec((tm, w), lambda i: (i, 0))
    const = lambda a: pl.BlockSpec(a.shape, lambda i: (0, 0), pipeline_mode=pl.Buffered(1))
    return pl.pallas_call(
        functools.partial(_odd_out_kernel, tm=tm),
        grid=(seq // tm,),
        in_specs=[row(D_MODEL), row(D_CONFORMER), row(D_ATTN), const(w_out), const(g)],
        out_specs=row(D_MODEL),
        out_shape=jax.ShapeDtypeStruct((seq, D_MODEL), F32),
        compiler_params=_params(("arbitrary",)),
        name="odd_out",
    )(h, yc, yd, w_out, g)


def _row_tile(seq, want):
    return min(seq, want)


def kernel(x, norm_g, final_norm_g, ev_w_in, ev_conv_w, s5_lam_re, s5_lam_im, s5_log_dt, s5_b_re, s5_b_im, s5_c_re, s5_c_im, s5_d, s5_w_glu, ev_w_out, od_w_in, cf_conv_w, cf_conv_b, cf_ln_g, cf_ln_b, cf_w_pw, od_w_out, rel_bias):
    bsz, seq, _ = x.shape
    assert bsz == 1 and seq % ATTN_TILE == 0
    x2 = x.reshape(seq, D_MODEL)
    g0 = norm_g[0].astype(F32).reshape(1, D_MODEL)
    g1 = norm_g[1].astype(F32).reshape(1, D_MODEL)
    gf = final_norm_g.astype(F32).reshape(1, D_MODEL)

    ya, u, szb, ev_w_out_b, od_w_out_b = _even_in(
        x2, g0, ev_w_in[0].astype(BF16), ev_conv_w[0].astype(F32), ev_w_out[0].astype(F32),
        od_w_out[0].astype(F32), _row_tile(seq, EVEN_IN_TILE[0]), EVEN_IN_TILE[1])
    tables = _s5_tables(s5_lam_re[0], s5_lam_im[0], s5_log_dt[0], s5_b_re[0], s5_b_im[0],
                        s5_c_re[0], s5_c_im[0], s5_d[0])
    ys = _s5(u, tables, min(seq // SSM_CHUNK, SSM_CHUNK_ROWS))
    h1, hn1, od_w_in_b = _even_out(x2, ya, ys, szb, s5_w_glu[0].astype(BF16), ev_w_out_b, g1,
                                   od_w_in[0].astype(F32), _row_tile(seq, EVEN_OUT_ROWS))

    proj, qkv = _odd_in(hn1, od_w_in_b, _row_tile(seq, ODD_IN_TILE[0]), ODD_IN_TILE[1])
    conv_w = jnp.concatenate([cf_conv_w[0].astype(F32), jnp.zeros((1, D_CONFORMER), F32)], axis=0)
    yc = _conformer(proj, conv_w, cf_conv_b[0].astype(F32).reshape(1, -1),
                    cf_ln_g[0].astype(F32).reshape(1, -1), cf_ln_b[0].astype(F32).reshape(1, -1),
                    cf_w_pw[0].astype(BF16), _row_tile(seq, CONFORMER_ROWS))
    yd = _attention(qkv, proj, _attn_bias(rel_bias))
    out = _odd_out(h1, yc, yd, od_w_out_b, gf, _row_tile(seq, ODD_OUT_ROWS))
    return out.reshape(bsz, seq, D_MODEL)
```

```python
import functools
import math

import numpy as np
import jax
import jax.numpy as jnp
from jax import lax
from jax.experimental import pallas as pl
from jax.experimental.pallas import tpu as pltpu

F32 = jnp.float32
BF16 = jnp.bfloat16

D_MODEL = 2048
D_SHORTCONV = 1536
SHORTCONV_WIDTH = 3
D_SSM = 512
SSM_GROUP = 16
SSM_GROUPS = 32
SSM_STATE = 64
D_CONFORMER = 1024
CONFORMER_WIDTH = 31
D_ATTN = 1024
HEAD_DIM = 128
N_HEADS = 8
DILATED_PATTERNS = ((128, 1), (512, 4), (2048, 16))
ATTN_BLOCK = 128
REL_BUCKETS = 32
REL_MAX_DISTANCE = 2048
NORM_EPS = 1e-6
NEG_INF = -1e30

IN_ROWS = 128
EVEN_ROWS = 256
SSM_CHUNK = 16
SSM_FLAT = SSM_CHUNK * SSM_GROUP
SSM_SLABS = D_SSM // 128
ATTN_TILE = 4096
LOG2E = math.log2(math.e)
Q_SCALE = HEAD_DIM ** -0.5 * LOG2E
MERGE_DIL = 4
CONV_HALO = 32
VMEM_LIMIT = 50 * 1024 * 1024

EVEN_IN_TILE = (1024, 512)
EVEN_OUT_ROWS = 512
ODD_IN_TILE = (2048, 512)
CONFORMER_ROWS = 1024
ODD_OUT_ROWS = 512
SSM_CHUNK_ROWS = 128


def _params(sem, vmem=VMEM_LIMIT):
    return pltpu.CompilerParams(dimension_semantics=sem, vmem_limit_bytes=vmem)


def _dot(a, b):
    return jnp.dot(a, b, preferred_element_type=F32)


def _sigmoid(z):
    return 1.0 / (1.0 + jnp.exp(-z))


def _silu(z):
    return z * _sigmoid(z)


def _gelu_tanh(x):
    return 0.5 * x * (1.0 + jnp.tanh(math.sqrt(2.0 / math.pi) * (x + 0.044715 * (x * x * x))))


def _cast_slice_spec(a, nsteps, step_of):
    rows = a.shape[0] // nsteps
    assert rows * nsteps == a.shape[0] and rows % 16 == 0, (a.shape, nsteps)
    return pl.BlockSpec((rows, a.shape[1]), lambda *idx: (step_of(*idx), 0))


def _rms(xf, g):
    ms = jnp.mean(xf * xf, axis=-1, keepdims=True)
    return xf * lax.rsqrt(ms + NORM_EPS) * g


def _even_in_kernel(x_ref, g_ref, wa_ref, wb_ref, wc_ref, wd_ref, cw_ref, c0_ref, c1_ref,
                    ya_ref, u_ref, szb_ref, c0b_ref, c1b_ref, hn_sc, pbuf_sc, carry_sc, *, tm, col):
    i = pl.program_id(0)
    j = pl.program_id(1)
    nconv = D_SHORTCONV // col
    c0b_ref[...] = c0_ref[...].astype(BF16)
    c1b_ref[...] = c1_ref[...].astype(BF16)

    def conv_step(first):
        @pl.when(i == 0)
        def _():
            carry_sc[j] = jnp.zeros((8, col), F32)

        pbuf_sc[0:8, :] = carry_sc[j]
        cw = cw_ref[...]
        for r0 in range(0, tm, EVEN_ROWS):
            rows = slice(r0, r0 + EVEN_ROWS)
            if first:
                hn = _rms(x_ref[rows, :], g_ref[...]).astype(BF16)
                hn_sc[rows, :] = hn
            else:
                hn = hn_sc[rows, :]
            p = _dot(hn, wb_ref[...]) * _dot(hn, wa_ref[...])
            pbuf_sc[8 + r0:8 + r0 + EVEN_ROWS, :] = p
            conv = (cw[2:3, :] * p + cw[1:2, :] * pbuf_sc[pl.ds(7 + r0, EVEN_ROWS), :]
                    + cw[0:1, :] * pbuf_sc[pl.ds(6 + r0, EVEN_ROWS), :])
            gb = _dot(hn, wc_ref[...])
            za = _dot(hn, wd_ref[...])
            ya_ref[rows, :] = (gb * conv * _silu(za)).astype(BF16)
        carry_sc[j] = pbuf_sc[tm:tm + 8, :]

    pl.when(j == 0)(functools.partial(conv_step, True))
    pl.when((j > 0) & (j < nconv))(functools.partial(conv_step, False))

    @pl.when(j >= nconv)
    def _():
        for r0 in range(0, tm, EVEN_ROWS):
            rows = slice(r0, r0 + EVEN_ROWS)
            hn = hn_sc[rows, :]
            u = _dot(hn, wa_ref[...])
            for s in range(col // 128):
                u_ref[s, rows, :] = u[:, s * 128:(s + 1) * 128]
            szb_ref[rows, :] = _silu(_dot(hn, wb_ref[...])).astype(BF16)


def _even_in(x, g, w_in, conv_w, cast0, cast1, tm, col):
    seq = x.shape[0]
    nconv = D_SHORTCONV // col
    nssm = D_SSM // col
    u_blk = 4 * nconv

    def wspec(k, alt0):
        return pl.BlockSpec((D_MODEL, col),
                            lambda i, j: (0, jnp.where(j < nconv, j + nconv * k, alt0 + j - nconv)))

    hold = lambda k: pl.BlockSpec((D_MODEL, col), lambda i, j: (0, jnp.minimum(j, nconv - 1) + nconv * k))
    ssm_step = lambda j: jnp.maximum(j - nconv, 0)
    nj = nconv + nssm
    cast_spec = lambda a: _cast_slice_spec(a, (seq // tm) * nj, lambda i, j: i * nj + j)
    return pl.pallas_call(
        functools.partial(_even_in_kernel, tm=tm, col=col),
        grid=(seq // tm, nj),
        in_specs=[
            pl.BlockSpec((tm, D_MODEL), lambda i, j: (i, 0)),
            pl.BlockSpec((1, D_MODEL), lambda i, j: (0, 0)),
            wspec(0, u_blk), wspec(1, u_blk + nssm), hold(2), hold(3),
            pl.BlockSpec((SHORTCONV_WIDTH, col), lambda i, j: (0, jnp.minimum(j, nconv - 1))),
            cast_spec(cast0), cast_spec(cast1),
        ],
        out_specs=[
            pl.BlockSpec((tm, col), lambda i, j: (i, jnp.minimum(j, nconv - 1))),
            pl.BlockSpec((col // 128, tm, 128), lambda i, j: (ssm_step(j), i, 0)),
            pl.BlockSpec((tm, col), lambda i, j: (i, ssm_step(j))),
            cast_spec(cast0), cast_spec(cast1),
        ],
        out_shape=[
            jax.ShapeDtypeStruct((seq, D_SHORTCONV), BF16),
            jax.ShapeDtypeStruct((SSM_SLABS, seq, 128), F32),
            jax.ShapeDtypeStruct((seq, D_SSM), BF16),
            jax.ShapeDtypeStruct(cast0.shape, BF16),
            jax.ShapeDtypeStruct(cast1.shape, BF16),
        ],
        scratch_shapes=[
            pltpu.VMEM((tm, D_MODEL), BF16),
            pltpu.VMEM((tm + 8, col), F32),
            pltpu.VMEM((nconv, 8, col), F32),
        ],
        compiler_params=_params(("arbitrary", "arbitrary")),
        name="even_in",
    )(x, g, w_in, w_in, w_in, w_in, conv_w, cast0, cast1)


def _s5_tables(lam_re, lam_im, log_dt, b_re, b_im, c_re, c_im, d_skip):
    hi = lax.Precision.HIGH
    t_, g_, n_, p_ = SSM_CHUNK, SSM_GROUPS, SSM_STATE, SSM_GROUP
    lam_re, lam_im = lam_re.astype(F32), lam_im.astype(F32)
    dt = jnp.exp(log_dt.astype(F32))[:, None]
    xr, xi = lam_re * dt, lam_im * dt
    k = jnp.arange(t_ + 1, dtype=F32)[:, None, None]
    mag = jnp.exp(xr * k)
    pr, pi = mag * jnp.cos(xi * k), mag * jnp.sin(xi * k)
    nr, ni = pr[1] - 1.0, pi[1]
    den = lam_re * lam_re + lam_im * lam_im
    fr = (nr * lam_re + ni * lam_im) / den
    fi = (ni * lam_re - nr * lam_im) / den
    b_re, b_im = b_re.astype(F32), b_im.astype(F32)
    bbr = fr[..., None] * b_re - fi[..., None] * b_im
    bbi = fr[..., None] * b_im + fi[..., None] * b_re
    c_re, c_im = c_re.astype(F32), c_im.astype(F32)

    cpr = c_re[None] * pr[:t_, :, None, :] - c_im[None] * pi[:t_, :, None, :]
    cpi = c_re[None] * pi[:t_, :, None, :] + c_im[None] * pr[:t_, :, None, :]
    kern = jnp.einsum('tgpn,gnq->tgpq', jnp.concatenate([cpr, -cpi], axis=-1),
                      jnp.concatenate([bbr, bbi], axis=1), precision=hi)
    skip = d_skip.astype(F32).reshape(g_, p_)[:, :, None] * jnp.eye(p_, dtype=F32)[None]
    kern = kern.at[0].add(skip)
    flat = t_ * p_
    last = jnp.transpose(kern[::-1], (1, 2, 0, 3)).reshape(g_, p_, flat)
    mt = jnp.stack([jnp.pad(last[:, :, (t_ - 1 - t) * p_:], ((0, 0), (0, 0), (0, (t_ - 1 - t) * p_)))
                    for t in range(t_)], axis=1).reshape(g_, flat, flat)

    prj, pij = pr[t_ - 1::-1][:t_], pi[t_ - 1::-1][:t_]
    wr = prj[..., None] * bbr[None] - pij[..., None] * bbi[None]
    wi = prj[..., None] * bbi[None] + pij[..., None] * bbr[None]
    half = (np.arange(2) == 0)[None, None, :, None, None]

    def pack_w(w):
        w = jnp.transpose(w.reshape(t_, g_ // 2, 2, n_, p_), (1, 0, 2, 4, 3))
        w = jnp.concatenate([jnp.where(half, w, 0.0), jnp.where(half, 0.0, w)], axis=-1)
        return w.reshape(g_ // 2, 2 * flat, 2 * n_)

    vr = c_re[None] * pr[1:, :, None, :] - c_im[None] * pi[1:, :, None, :]
    vi = c_re[None] * pi[1:, :, None, :] + c_im[None] * pr[1:, :, None, :]
    vr = jnp.transpose(vr, (1, 0, 2, 3)).reshape(g_ // 2, 2, flat, n_)
    vi = -jnp.transpose(vi, (1, 0, 2, 3)).reshape(g_ // 2, 2, flat, n_)
    zv = jnp.zeros((g_ // 2, flat, n_), F32)
    v_even = jnp.concatenate([vr[:, 0], zv, vi[:, 0], zv], axis=-1)
    v_odd = jnp.concatenate([zv, vr[:, 1], zv, vi[:, 1]], axis=-1)
    vt = jnp.stack([v_even, v_odd], axis=1).reshape(g_, flat, 4 * n_)
    a_re = pr[t_].reshape(1, g_ * n_)
    a_im = pi[t_].reshape(1, g_ * n_)
    return mt.astype(BF16), pack_w(wr).astype(BF16), pack_w(wi).astype(BF16), vt.astype(BF16), a_re, a_im


_NT = (((1,), (1,)), ((), ()))
_TN = (((0,), (0,)), ((), ()))


def _s5_kernel(u_ref, mt_ref, wre_ref, wim_ref, vt_ref, are_ref, aim_ref, y_ref,
               xt_sc, yt_sc, zre_sc, zim_sc, carry_sc, *, ct):
    i = pl.program_id(0)
    t_ = SSM_CHUNK

    @pl.when(i == 0)
    def _():
        carry_sc[...] = jnp.zeros_like(carry_sc)

    for s in range(SSM_SLABS):
        for j in range(t_):
            xt_sc[s, j] = u_ref[s, pl.ds(j, ct, stride=t_), :].T.astype(BF16)

    pairs_per_slab = 128 // (2 * SSM_GROUP)
    for gp in range(SSM_GROUPS // 2):
        s, k = gp // pairs_per_slab, gp % pairs_per_slab
        xp = xt_sc[s, :, 32 * k:32 * k + 32, :].reshape(2 * SSM_FLAT, ct)
        zre_sc[:, gp * 128:(gp + 1) * 128] = lax.dot_general(xp, wre_ref[gp], _TN, preferred_element_type=F32)
        zim_sc[:, gp * 128:(gp + 1) * 128] = lax.dot_general(xp, wim_ref[gp], _TN, preferred_element_type=F32)

    are = are_ref[...]
    aim = aim_ref[...]

    def step(r, carry):
        sre, sim = carry
        zr = zre_sc[pl.ds(r, 1), :]
        zi = zim_sc[pl.ds(r, 1), :]
        zre_sc[pl.ds(r, 1), :] = sre
        zim_sc[pl.ds(r, 1), :] = sim
        return are * sre - aim * sim + zr, are * sim + aim * sre + zi

    sre, sim = lax.fori_loop(0, ct, step, (carry_sc[0:1, :], carry_sc[1:2, :]), unroll=8)
    carry_sc[0:1, :] = sre
    carry_sc[1:2, :] = sim

    groups_per_slab = 128 // SSM_GROUP
    for g in range(SSM_GROUPS):
        gp, s, k = g // 2, g // groups_per_slab, g % groups_per_slab
        xg = xt_sc[s, :, 16 * k:16 * k + 16, :].reshape(SSM_FLAT, ct)
        sp = jnp.concatenate([zre_sc[:, gp * 128:(gp + 1) * 128],
                              zim_sc[:, gp * 128:(gp + 1) * 128]], axis=1).astype(BF16)
        yt = _dot(mt_ref[g], xg) + lax.dot_general(vt_ref[g], sp, _NT, preferred_element_type=F32)
        yt_sc[s, :, 16 * k:16 * k + 16, :] = yt.reshape(t_, SSM_GROUP, ct)

    for s in range(SSM_SLABS):
        for t in range(t_):
            y_ref[s, pl.ds(t, ct, stride=t_), :] = yt_sc[s, t].T


def _s5(u4, tables, ct):
    mt, wre, wim, vt, are, aim = tables
    _, seq, _ = u4.shape
    full = lambda a: pl.BlockSpec(a.shape, lambda i: (0,) * a.ndim)
    rows = pl.BlockSpec((SSM_SLABS, SSM_CHUNK * ct, 128), lambda i: (0, i, 0))
    nstate = SSM_GROUPS * SSM_STATE
    return pl.pallas_call(
        functools.partial(_s5_kernel, ct=ct),
        grid=(seq // (SSM_CHUNK * ct),),
        in_specs=[rows, full(mt), full(wre), full(wim), full(vt), full(are), full(aim)],
        out_specs=rows,
        out_shape=jax.ShapeDtypeStruct(u4.shape, F32),
        scratch_shapes=[pltpu.VMEM((SSM_SLABS, SSM_CHUNK, 128, ct), BF16),
                        pltpu.VMEM((SSM_SLABS, SSM_CHUNK, 128, ct), F32),
                        pltpu.VMEM((ct, nstate), F32),
                        pltpu.VMEM((ct, nstate), F32),
                        pltpu.VMEM((8, nstate), F32)],
        compiler_params=_params(("arbitrary",)),
        name="s5_ssm",
    )(u4, mt, wre, wim, vt, are, aim)


OUT_ROWS = 256


X_RING = 3


def _ring_fetch(src_hbm, buf, sem, tm):
    i = pl.program_id(0)
    n = pl.num_programs(0)

    def copy(step):
        slot = step % X_RING
        return pltpu.make_async_copy(src_hbm.at[pl.ds(step * tm, tm), :], buf.at[slot], sem.at[slot])

    @pl.when(i == 0)
    def _():
        copy(0).start()

        @pl.when(n > 1)
        def _():
            copy(1).start()

    @pl.when(i + 2 < n)
    def _():
        copy(i + 2).start()

    copy(i).wait()
    return buf.at[i % X_RING]


def _even_out_kernel(x_hbm, ya_ref, ys_ref, szb_ref, wglu_ref, wout_ref, g_ref, c0_ref,
                     h_ref, hn_ref, c0b_ref, xbuf, xsem, *, tm):
    x_ref = _ring_fetch(x_hbm, xbuf, xsem, tm)
    c0b_ref[...] = c0_ref[...].astype(BF16)
    for r0 in range(0, tm, OUT_ROWS):
        rows = slice(r0, r0 + OUT_ROWS)
        gl = _gelu_tanh(jnp.concatenate([ys_ref[s, rows, :] for s in range(SSM_SLABS)], axis=1))
        yb = gl * _sigmoid(_dot(gl.astype(BF16), wglu_ref[...])) * szb_ref[rows, :].astype(F32)
        acc = _dot(ya_ref[rows, :], wout_ref[0:D_SHORTCONV, :])
        acc = acc + _dot(yb.astype(BF16), wout_ref[D_SHORTCONV:D_SHORTCONV + D_SSM, :])
        h = x_ref[rows, :] + acc
        h_ref[rows, :] = h
        hn_ref[rows, :] = _rms(h, g_ref[...]).astype(BF16)


def _even_out(x, ya, ys, szb, w_glu, w_out, g, cast0, tm):
    seq = x.shape[0]
    cast_spec = _cast_slice_spec(cast0, seq // tm, lambda i: i)
    row = lambda w: pl.BlockSpec((tm, w), lambda i: (i, 0))
    const = lambda a: pl.BlockSpec(a.shape, lambda i: (0, 0), pipeline_mode=pl.Buffered(1))
    return pl.pallas_call(
        functools.partial(_even_out_kernel, tm=tm),
        grid=(seq // tm,),
        in_specs=[pl.BlockSpec(memory_space=pl.ANY), row(D_SHORTCONV),
                  pl.BlockSpec((SSM_SLABS, tm, 128), lambda i: (0, i, 0)), row(D_SSM),
                  const(w_glu), const(w_out), const(g), cast_spec],
        out_specs=[row(D_MODEL), row(D_MODEL), cast_spec],
        out_shape=[jax.ShapeDtypeStruct((seq, D_MODEL), F32),
                   jax.ShapeDtypeStruct((seq, D_MODEL), BF16),
                   jax.ShapeDtypeStruct(cast0.shape, BF16)],
        scratch_shapes=[pltpu.VMEM((X_RING, tm, D_MODEL), F32), pltpu.SemaphoreType.DMA((X_RING,))],
        compiler_params=_params(("arbitrary",)),
        name="even_out",
    )(x, ya, ys, szb, w_glu, w_out, g, cast0)


ODD_GROUPS = 6


def _odd_in_kernel(hn_ref, wa_ref, wb_ref, o_ref, qkv_ref, *, tm, col):
    s = pl.program_id(1)
    nb = D_CONFORMER // col
    group = s // nb
    is_silu = (group == 1) | (group == 5)
    blocks = [slice(r0, r0 + IN_ROWS) for r0 in range(0, tm, IN_ROWS)]

    @pl.when(group == 0)
    def _():
        for rows in blocks:
            hn = hn_ref[rows, :]
            o_ref[rows, :] = (_dot(hn, wa_ref[...]) * _sigmoid(_dot(hn, wb_ref[...]))).astype(BF16)

    @pl.when(is_silu)
    def _():
        for rows in blocks:
            o_ref[rows, :] = _silu(_dot(hn_ref[rows, :], wa_ref[...])).astype(BF16)

    @pl.when((group > 0) & jnp.logical_not(is_silu))
    def _():
        scale = jnp.where(group == 2, Q_SCALE, 1.0).astype(F32)
        for rows in blocks:
            qkv_ref[rows, :] = _dot(hn_ref[rows, :], wa_ref[...]) * scale


def _odd_in(hn, w_in, tm, col):
    seq = hn.shape[0]
    nb = D_CONFORMER // col
    return pl.pallas_call(
        functools.partial(_odd_in_kernel, tm=tm, col=col),
        grid=(seq // tm, ODD_GROUPS * nb),
        in_specs=[pl.BlockSpec((tm, D_MODEL), lambda i, s: (i, 0)),
                  pl.BlockSpec((D_MODEL, col), lambda i, s: (0, jnp.where(s < nb, s, s + nb))),
                  pl.BlockSpec((D_MODEL, col), lambda i, s: (0, jnp.where(s < nb, s + nb, 2 * nb - 1)))],
        out_specs=[pl.BlockSpec((tm, col), lambda i, s: (i, jnp.where(s < 2 * nb, s, jnp.where(
                       s < 5 * nb, 2 * nb - 1, s - 3 * nb)))),
                   pl.BlockSpec((tm, col), lambda i, s: (i, jnp.clip(s - 2 * nb, 0, 3 * nb - 1)))],
        out_shape=[jax.ShapeDtypeStruct((seq, 3 * D_CONFORMER), BF16),
                   jax.ShapeDtypeStruct((seq, 3 * D_ATTN), F32)],
        compiler_params=_params(("arbitrary", "arbitrary")),
        name="odd_in",
    )(hn, w_in, w_in)


CONV_ROWS = 256


def _conformer_kernel(cur_ref, halo_ref, szc_ref, cw_ref, cb_ref, lg_ref, lb_ref, wpw_ref,
                      o_ref, buf_sc, conv_sc, *, tm):
    i = pl.program_id(0)
    nslab = D_CONFORMER // 128
    for c in range(nslab):
        lanes = slice(c * 128, (c + 1) * 128)
        buf_sc[c, 0:CONV_HALO, :] = jnp.where(i == 0, 0.0, halo_ref[:, lanes].astype(F32))
        buf_sc[c, CONV_HALO:CONV_HALO + tm, :] = cur_ref[:, lanes].astype(F32)
    shift = CONV_HALO - (CONFORMER_WIDTH - 1)

    def rows(rc, _):
        base = pl.multiple_of(rc * CONV_ROWS, CONV_ROWS)
        for c in range(nslab):
            lanes = slice(c * 128, (c + 1) * 128)
            acc = jnp.zeros((CONV_ROWS, 128), F32) + cb_ref[:, lanes]
            for k in range(CONFORMER_WIDTH):
                acc = acc + cw_ref[k:k + 1, lanes] * buf_sc[c, pl.ds(base + (shift + k), CONV_ROWS), :]
            conv_sc[pl.ds(base, CONV_ROWS), lanes] = acc
        return 0

    lax.fori_loop(0, tm // CONV_ROWS, rows, 0)
    for r0 in range(0, tm, OUT_ROWS):
        sub = slice(r0, r0 + OUT_ROWS)
        y = conv_sc[sub, :]
        mu = jnp.mean(y, axis=-1, keepdims=True)
        yc = y - mu
        var = jnp.mean(yc * yc, axis=-1, keepdims=True)
        ln = yc * lax.rsqrt(var + NORM_EPS) * lg_ref[...] + lb_ref[...]
        o_ref[sub, :] = (_dot(_silu(ln).astype(BF16), wpw_ref[...]) * szc_ref[sub, :].astype(F32)).astype(BF16)


def _conformer(proj, conv_w, conv_b, ln_g, ln_b, w_pw, tm):
    seq = proj.shape[0]
    per = tm // CONV_HALO
    const = lambda a: pl.BlockSpec(a.shape, lambda i: (0, 0))
    return pl.pallas_call(
        functools.partial(_conformer_kernel, tm=tm),
        grid=(seq // tm,),
        in_specs=[pl.BlockSpec((tm, D_CONFORMER), lambda i: (i, 0)),
                  pl.BlockSpec((CONV_HALO, D_CONFORMER), lambda i: (jnp.maximum(i * per - 1, 0), 0)),
                  pl.BlockSpec((tm, D_CONFORMER), lambda i: (i, 1)),
                  const(conv_w), const(conv_b), const(ln_g), const(ln_b), const(w_pw)],
        out_specs=pl.BlockSpec((tm, D_CONFORMER), lambda i: (i, 0)),
        out_shape=jax.ShapeDtypeStruct((seq, D_CONFORMER), BF16),
        scratch_shapes=[pltpu.VMEM((D_CONFORMER // 128, tm + CONV_HALO, 128), F32),
                        pltpu.VMEM((tm, D_CONFORMER), F32)],
        compiler_params=_params(("arbitrary",)),
        name="conformer",
    )(proj, proj, proj, conv_w, conv_b, ln_g, ln_b, w_pw)


def _t5_bucket(dist):
    max_exact = REL_BUCKETS // 2
    d_f = jnp.maximum(dist, 1).astype(F32)
    large = max_exact + (jnp.log(d_f / max_exact) / math.log(REL_MAX_DISTANCE / max_exact)
                         * (REL_BUCKETS - max_exact)).astype(jnp.int32)
    large = jnp.minimum(large, REL_BUCKETS - 1)
    return jnp.where(dist < max_exact, dist, large)


def _attn_bias(rel_bias):
    q = ATTN_BLOCK
    nh = rel_bias.shape[1]
    kj = np.arange(2 * q)[None, None, :]
    neg = lambda n: jnp.full((nh, n), NEG_INF, F32)
    tabs = []
    for window, dilation in DILATED_PATTERNS:
        steps = window // dilation
        assert steps == q
        rel = jnp.asarray(np.arange(steps + 1) * dilation, dtype=jnp.int32)
        by_rel = rel_bias.astype(F32)[_t5_bucket(rel)].T
        g = jnp.concatenate([by_rel[:, ::-1], neg(2 * q - 1)], axis=1)
        flat = jnp.tile(g, (1, q))[:, :q * (3 * q - 1)]
        tab = flat.reshape(nh, q, 3 * q - 1)[:, :, :2 * q]
        if dilation == 1:
            tab = tab.reshape(nh, q // MERGE_DIL, MERGE_DIL, 2 * q).transpose(0, 2, 1, 3).reshape(nh, q, 2 * q)
        tab = tab * LOG2E
        tabs.append(tab)
        tabs.append(jnp.where(kj >= q, tab, NEG_INF))
    return jnp.stack(tabs, axis=1)


def _attn_kernel(q_ref, k_ref, v_ref, szd_ref, bias_ref, o_ref,
                 onat, qm, km, vm, qs1, qs2, ks0, ks1, ks2, vs0, vs1, vs2,
                 o0, o1, o2, l0, l1, l2):
    i = pl.program_id(1)
    t = ATTN_TILE
    blk = ATTN_BLOCK
    first_tile = i == 0
    qf, kf, vf = q_ref, k_ref, v_ref
    qstreams = (None, qs1, qs2)
    kstreams = (ks0, ks1, ks2)
    vstreams = (vs0, vs1, vs2)
    md = MERGE_DIL
    mspan = t // md

    def stream(src_f, src_m, dil, r):
        n = t // dil
        if dil == md:
            x = src_f[pl.ds(r, n, stride=dil), :]
            src_m[r * n:(r + 1) * n, :] = x
            return x
        return src_m[pl.ds((r % md) * mspan + r // md, n, stride=dil // md), :]

    for pidx, (_, dil) in enumerate(DILATED_PATTERNS):
        n = t // dil
        per = blk + n
        for r in range(dil):
            base = r * per
            for src_f, src_m, dst in ((kf, km, kstreams[pidx]), (vf, vm, vstreams[pidx])):
                tail = dst[base + n:base + per, :]
                dst[base:base + blk, :] = jnp.where(first_tile, jnp.zeros_like(tail), tail)
                if dil == 1:
                    dst[base + blk:base + per, :] = src_f[...].astype(BF16)
                else:
                    dst[base + blk:base + per, :] = stream(src_f, src_m, dil, r).astype(BF16)
            if dil > 1:
                qstreams[pidx][r * n:(r + 1) * n, :] = stream(qf, qm, dil, r).astype(BF16)

    ones = jnp.ones((2 * blk, HEAD_DIM), BF16)
    first = first_tile.astype(jnp.int32)

    def block(pidx, dil, r, sb, out_sc, lse_sc):
        n = t // dil
        per = blk + n
        if dil == 1:
            qb = jnp.concatenate([qf[pl.ds(sb * blk + c, blk // md, stride=md), :] for c in range(md)],
                                 axis=0).astype(BF16)
        else:
            qb = qstreams[pidx][r * n + sb * blk:r * n + (sb + 1) * blk, :]
        krows = slice(r * per + sb * blk, r * per + (sb + 2) * blk)
        kb = kstreams[pidx][krows, :]
        vb = vstreams[pidx][krows, :]
        s = lax.dot_general(qb, kb, _NT, preferred_element_type=F32)
        s = s + bias_ref[0, 2 * pidx + first if sb == 0 else 2 * pidx]
        mx = jnp.max(s, axis=-1, keepdims=True)
        p = jnp.exp2(s - mx).astype(BF16)
        pv = _dot(p, jnp.concatenate([vb, ones], axis=1))
        den = pv[:, HEAD_DIM:]
        out = pv[:, :HEAD_DIM] / den
        lse = mx + jnp.log2(den)
        if dil == 1:
            piece = blk // md
            for c in range(md):
                rows = slice(c * mspan + sb * piece, c * mspan + (sb + 1) * piece)
                out_sc[rows, :] = out[c * piece:(c + 1) * piece, :]
                lse_sc[rows, :] = lse[c * piece:(c + 1) * piece, :]
        elif dil == md:
            rows = slice(r * mspan + sb * blk, r * mspan + (sb + 1) * blk)
            out_sc[rows, :] = out
            lse_sc[rows, :] = lse
        else:
            step = dil // md
            rows = pl.ds((r % md) * mspan + sb * blk * step + r // md, blk, stride=step)
            out_sc[rows, :] = out
            lse_sc[rows, :] = lse

    scratch = ((o0, l0), (o1, l1), (o2, l2))
    for pidx, (_, dil) in enumerate(DILATED_PATTERNS):
        for sb in range(t // (blk * dil)):
            for r in range(dil):
                block(pidx, dil, r, sb, *scratch[pidx])

    la, lb, lc = l0[...], l1[...], l2[...]
    lall = jnp.maximum(jnp.maximum(la, lb), lc)
    wa, wb, wc = jnp.exp2(la - lall), jnp.exp2(lb - lall), jnp.exp2(lc - lall)
    merged = (wa * o0[...] + wb * o1[...] + wc * o2[...]) / (wa + wb + wc)
    for c in range(md):
        onat[pl.ds(c, mspan, stride=md), :] = merged[c * mspan:(c + 1) * mspan, :]
    o_ref[...] = (onat[...] * szd_ref[...].astype(F32)).astype(BF16)


def _attention(qkv, proj, bias):
    seq = proj.shape[0]
    t = ATTN_TILE
    q0, k0, v0 = 0, N_HEADS, 2 * N_HEADS
    z0 = (2 * D_CONFORMER) // HEAD_DIM
    cur = lambda c0: pl.BlockSpec((t, HEAD_DIM), lambda h, i: (i, c0 + h))
    f32 = lambda rows: pltpu.VMEM((rows, HEAD_DIM), F32)
    bf16 = lambda rows: pltpu.VMEM((rows, HEAD_DIM), BF16)
    streams = [bf16(dil * (ATTN_BLOCK + t // dil)) for _, dil in DILATED_PATTERNS]
    return pl.pallas_call(
        _attn_kernel,
        grid=(N_HEADS, seq // t),
        in_specs=[cur(q0), cur(k0), cur(v0), cur(z0),
                  pl.BlockSpec((1,) + bias.shape[1:], lambda h, i: (h, 0, 0, 0))],
        out_specs=pl.BlockSpec((t, HEAD_DIM), lambda h, i: (i, h)),
        out_shape=jax.ShapeDtypeStruct((seq, D_ATTN), BF16),
        scratch_shapes=[f32(t)] * 4 + [bf16(t)] * 2 + streams + streams + [f32(t)] * 6,
        compiler_params=_params(("arbitrary", "arbitrary")),
        name="dilated_attn",
    )(qkv, qkv, qkv, proj, bias)


def _odd_out_kernel(h_hbm, yc_ref, yd_ref, wout_ref, g_ref, o_ref, hbuf, hsem, *, tm):
    h_ref = _ring_fetch(h_hbm, hbuf, hsem, tm)
    for r0 in range(0, tm, OUT_ROWS):
        rows = slice(r0, r0 + OUT_ROWS)
        acc = _dot(yc_ref[rows, :], wout_ref[0:D_CONFORMER, :])
        acc = acc + _dot(yd_ref[rows, :], wout_ref[D_CONFORMER:D_CONFORMER + D_ATTN, :])
        o_ref[rows, :] = _rms(h_ref[rows, :] + acc, g_ref[...])


def _odd_out(h, yc, yd, w_out, g, tm):
    seq = h.shape[0]
    row = lambda w: pl.BlockSpec((tm, w), lambda i: (i, 0))
    const = lambda a: pl.BlockSpec(a.shape, lambda i: (0, 0), pipeline_mode=pl.Buffered(1))
    return pl.pallas_call(
        functools.partial(_odd_out_kernel, tm=tm),
        grid=(seq // tm,),
        in_specs=[pl.BlockSpec(memory_space=pl.ANY), row(D_CONFORMER), row(D_ATTN), const(w_out), const(g)],
        out_specs=row(D_MODEL),
        out_shape=jax.ShapeDtypeStruct((seq, D_MODEL), F32),
        scratch_shapes=[pltpu.VMEM((X_RING, tm, D_MODEL), F32), pltpu.SemaphoreType.DMA((X_RING,))],
        compiler_params=_params(("arbitrary",)),
        name="odd_out",
    )(h, yc, yd, w_out, g)


def _row_tile(seq, want):
    return min(seq, want)


def kernel(x, norm_g, final_norm_g, ev_w_in, ev_conv_w, s5_lam_re, s5_lam_im, s5_log_dt, s5_b_re, s5_b_im, s5_c_re, s5_c_im, s5_d, s5_w_glu, ev_w_out, od_w_in, cf_conv_w, cf_conv_b, cf_ln_g, cf_ln_b, cf_w_pw, od_w_out, rel_bias):
    bsz, seq, _ = x.shape
    assert bsz == 1 and seq % ATTN_TILE == 0
    x2 = x.reshape(seq, D_MODEL)
    g0 = norm_g[0].astype(F32).reshape(1, D_MODEL)
    g1 = norm_g[1].astype(F32).reshape(1, D_MODEL)
    gf = final_norm_g.astype(F32).reshape(1, D_MODEL)

    ya, u, szb, ev_w_out_b, od_w_out_b = _even_in(
        x2, g0, ev_w_in[0].astype(BF16), ev_conv_w[0].astype(F32), ev_w_out[0].astype(F32),
        od_w_out[0].astype(F32), _row_tile(seq, EVEN_IN_TILE[0]), EVEN_IN_TILE[1])
    tables = _s5_tables(s5_lam_re[0], s5_lam_im[0], s5_log_dt[0], s5_b_re[0], s5_b_im[0],
                        s5_c_re[0], s5_c_im[0], s5_d[0])
    ys = _s5(u, tables, min(seq // SSM_CHUNK, SSM_CHUNK_ROWS))
    h1, hn1, od_w_in_b = _even_out(x2, ya, ys, szb, s5_w_glu[0].astype(BF16), ev_w_out_b, g1,
                                   od_w_in[0].astype(F32), _row_tile(seq, EVEN_OUT_ROWS))

    proj, qkv = _odd_in(hn1, od_w_in_b, _row_tile(seq, ODD_IN_TILE[0]), ODD_IN_TILE[1])
    conv_w = jnp.concatenate([cf_conv_w[0].astype(F32), jnp.zeros((1, D_CONFORMER), F32)], axis=0)
    yc = _conformer(proj, conv_w, cf_conv_b[0].astype(F32).reshape(1, -1),
                    cf_ln_g[0].astype(F32).reshape(1, -1), cf_ln_b[0].astype(F32).reshape(1, -1),
                    cf_w_pw[0].astype(BF16), _row_tile(seq, CONFORMER_ROWS))
    yd = _attention(qkv, proj, _attn_bias(rel_bias))
    out = _odd_out(h1, yc, yd, od_w_out_b, gf, _row_tile(seq, ODD_OUT_ROWS))
    return out.reshape(bsz, seq, D_MODEL)
```
